```python
import math
import jax
import jax.numpy as jnp
from jax import lax
import numpy as np

D_MODEL = 1024
BATCH = 4
SEQ = 4096
DEPTH = 2
DEC_BATCH = 32
DEC_SEQ = 4
PAST_LEN = 8192
PAGE_SIZE = 128

HEAD_DIM = 64
N_EVEN = (DEPTH + 1) // 2
N_ODD = DEPTH // 2
NORM_EPS = 1e-6
ROPE_THETA = 10000.0
RWKV_HEADS = D_MODEL // (2 * HEAD_DIM)
RWKV_DIM = RWKV_HEADS * HEAD_DIM
DECAY_LORA = 64
AAA_LORA = 64
GATE_LORA = 128
RWKV_COLS = 3 * RWKV_DIM + DECAY_LORA + AAA_LORA + GATE_LORA
RWKV_GN_EPS = 64e-5
NSA_HEADS = D_MODEL // (2 * HEAD_DIM)
NSA_KV_HEADS = 2
NSA_DIM = NSA_HEADS * HEAD_DIM
CMP_STRIDE = 16
CMP_LEN = 2 * CMP_STRIDE
SEL_BLOCK = 64
SEL_TOPN = 16
WINDOW = 512
NSA_QBLK = 64
FORCE_BONUS = 100.0
NSA_COLS = NSA_DIM + 6 * NSA_KV_HEADS * HEAD_DIM + 3 * NSA_HEADS
EVEN_COLS = RWKV_COLS + NSA_COLS
MOBA_HEADS = D_MODEL // HEAD_DIM
MOBA_KV_HEADS = 4
MOBA_DIM = MOBA_HEADS * HEAD_DIM
MOBA_BLOCK = 256
MOBA_TOPK = 3
MOBA_QBLK = 32
ODD_COLS = MOBA_DIM + 2 * MOBA_KV_HEADS * HEAD_DIM
D_FF = -(-8 * D_MODEL // (3 * 256)) * 256

kernel_name = 'rwkv7_nsa_moba_hybrid_step'


def rms_norm(x, g):
    xf = x.astype(jnp.float32)
    y = xf * lax.rsqrt(jnp.mean(xf * xf, axis=-1, keepdims=True) + NORM_EPS)
    return (y * g.astype(jnp.float32)).astype(x.dtype)


def rope(x, pos):
    half = HEAD_DIM // 2
    inv = ROPE_THETA ** (-jnp.arange(half, dtype=jnp.float32) / half)
    ang = pos.astype(jnp.float32)[:, None] * inv[None, :]
    cos = jnp.cos(ang)[None, :, None, :]
    sin = jnp.sin(ang)[None, :, None, :]
    xf = x.astype(jnp.float32)
    x1, x2 = xf[..., :half], xf[..., half:]
    return jnp.concatenate([x1 * cos - x2 * sin, x2 * cos + x1 * sin], axis=-1).astype(x.dtype)


def masked_softmax(s, mask):
    s = jnp.where(mask, s.astype(jnp.float32), -jnp.inf)
    m = jnp.max(s, axis=-1, keepdims=True)
    e = jnp.exp(s - jnp.where(jnp.isfinite(m), m, 0.0))
    return e / jnp.maximum(jnp.sum(e, axis=-1, keepdims=True), 1e-30)


def swiglu(x, wg, wu, wd):
    return (jax.nn.silu(x @ wg) * (x @ wu)) @ wd


def even_project(xn, pos, w_in, gate_b):
    B, T, _ = xn.shape
    proj = xn @ w_in
    rw = proj[..., :RWKV_COLS]
    o = RWKV_COLS
    q = proj[..., o:o + NSA_DIM].reshape(B, T, NSA_HEADS, HEAD_DIM)
    o += NSA_DIM
    kv = proj[..., o:o + 6 * NSA_KV_HEADS * HEAD_DIM].reshape(B, T, 3, 2, NSA_KV_HEADS, HEAD_DIM)
    o += 6 * NSA_KV_HEADS * HEAD_DIM
    gates = jax.nn.sigmoid(proj[..., o:] + gate_b).reshape(B, T, NSA_HEADS, 3)
    k = rope(kv[:, :, :, 0].reshape(B, T, 3 * NSA_KV_HEADS, HEAD_DIM), pos).reshape(B, T, 3, NSA_KV_HEADS, HEAD_DIM)
    kv = jnp.stack([k, kv[:, :, :, 1]], axis=3).reshape(B, T, 6, NSA_KV_HEADS, HEAD_DIM)
    return rw, rope(q, pos), kv, gates


def rwkv_mix(rw, shift_prev, wkv0, p):
    mu, w0, w_up, a0, a_up, g_up, k_k, k_a, r_k, ln_g, ln_b = p
    B, T, _ = rw.shape
    f32 = jnp.float32
    prev = jnp.concatenate([shift_prev[:, None].astype(rw.dtype), rw[:, :-1]], axis=1)
    xm = rw + (prev - rw) * mu
    r, k, v, xw, xa, xg = jnp.split(xm, [RWKV_DIM, 2 * RWKV_DIM, 3 * RWKV_DIM, 3 * RWKV_DIM + DECAY_LORA, 3 * RWKV_DIM + DECAY_LORA + AAA_LORA], axis=-1)

    def heads(t):
        return t.astype(f32).reshape(B, T, RWKV_HEADS, HEAD_DIM)

    w = jnp.exp(-math.exp(-0.5) * jax.nn.sigmoid((w0 + jnp.tanh(xw) @ w_up).astype(f32)))
    a = jax.nn.sigmoid((a0 + xa @ a_up).astype(f32))
    g = jax.nn.sigmoid(xg) @ g_up
    kk = heads(k * k_k)
    kk = kk / jnp.maximum(jnp.sqrt(jnp.sum(kk * kk, axis=-1, keepdims=True)), 1e-12)
    k = k.astype(f32) * (1.0 + (a - 1.0) * k_a)
    r, w, k, v, a = heads(r), heads(w), heads(k), heads(v), heads(a)

    def step(S, inp):
        r_t, w_t, k_t, v_t, kk_t, a_t = inp
        sk = jnp.einsum('bhij,bhj->bhi', S, kk_t)
        S = S * w_t[:, :, None, :] - sk[..., None] * (kk_t * a_t)[:, :, None, :] + v_t[..., None] * k_t[:, :, None, :]
        return S, jnp.einsum('bhij,bhj->bhi', S, r_t)

    xs = tuple(jnp.moveaxis(t, 1, 0) for t in (r, w, k, v, kk, a))
    wkv, y = lax.scan(step, wkv0.astype(f32), xs)
    y = jnp.moveaxis(y, 0, 1)
    m = jnp.mean(y, axis=-1, keepdims=True)
    var = jnp.mean(jnp.square(y - m), axis=-1, keepdims=True)
    y = ((y - m) * lax.rsqrt(var + RWKV_GN_EPS)).reshape(B, T, RWKV_DIM) * ln_g + ln_b
    bonus = (jnp.sum(r * k * r_k, axis=-1, keepdims=True) * v).reshape(B, T, RWKV_DIM)
    out = (y + bonus) * g
    return out.astype(rw.dtype), rw[:, -1], wkv.astype(wkv0.dtype)


def nsa_compress(x, w1, pe, w2):
    B, L, G, dh = x.shape
    nch = L // CMP_STRIDE
    ch = x[:, :nch * CMP_STRIDE].reshape(B, nch, CMP_STRIDE, G, dh)
    h_a = jnp.einsum('bncgd,cde->bnge', ch, w1[:CMP_STRIDE])
    h_b = jnp.einsum('bncgd,cde->bnge', ch, w1[CMP_STRIDE:])
    bias = jnp.einsum('cd,cde->e', pe, w1)
    h = jax.nn.gelu(h_a[:, :-1] + h_b[:, 1:] + bias)
    return jnp.einsum('bnge,ef->bngf', h, w2)


def cmp_sel_overlap(nc, nsel):
    i = jnp.arange(nc, dtype=jnp.int32)[:, None]
    j = jnp.arange(nsel, dtype=jnp.int32)[None, :]
    lo = jnp.maximum(i * CMP_STRIDE, j * SEL_BLOCK)
    hi = jnp.minimum(i * CMP_STRIDE + CMP_LEN, (j + 1) * SEL_BLOCK)
    return jnp.clip(hi - lo, 0).astype(jnp.float32) / CMP_LEN


def nsa_prepare(kv4, cw):
    w1, pe, w2 = cw
    B, L = kv4.shape[:2]
    kc = nsa_compress(kv4[:, :, 0], w1[0], pe[0], w2[0]).transpose(0, 2, 1, 3)
    vc = nsa_compress(kv4[:, :, 1], w1[1], pe[1], w2[1]).transpose(0, 2, 1, 3)
    nsel = -(-L // SEL_BLOCK)
    sel = jnp.pad(kv4[:, :, 2:4], ((0, 0), (0, nsel * SEL_BLOCK - L), (0, 0), (0, 0), (0, 0)))
    sel = sel.reshape(B, nsel, SEL_BLOCK, 2, NSA_KV_HEADS, HEAD_DIM).transpose(3, 0, 4, 1, 2, 5)
    return kc, vc, sel[0], sel[1]


def nsa_block(q, q_pos, kc, vc, ks, vs, kw, vw, w_pos, gates):
    B, T = q.shape[:2]
    G, R = NSA_KV_HEADS, NSA_HEADS // NSA_KV_HEADS
    nc, nsel = kc.shape[2], ks.shape[2]
    qg = (q * HEAD_DIM ** -0.5).reshape(B, T, G, R, HEAD_DIM).transpose(0, 2, 3, 1, 4)
    t = q_pos[:, None]
    c_end = jnp.arange(nc, dtype=jnp.int32) * CMP_STRIDE + (CMP_LEN - 1)
    p_c = masked_softmax(jnp.einsum('bgrtd,bgnd->bgrtn', qg, kc), c_end[None, :] <= t)
    o_c = jnp.einsum('bgrtn,bgnd->bgrtd', p_c.astype(vc.dtype), vc)
    imp = jnp.einsum('bgrtn,nj->bgtj', p_c, cmp_sel_overlap(nc, nsel))
    blk = jnp.arange(nsel, dtype=jnp.int32)[None, :]
    cur = t // SEL_BLOCK
    forced = ((blk == cur) | (blk == cur - 1) | (blk == 0)).astype(jnp.float32)
    score = jnp.where(blk <= cur, imp + FORCE_BONUS * forced, -jnp.inf)
    n_top = min(SEL_TOPN, nsel)
    top_s, idx = lax.top_k(score, n_top)
    b_ix = jnp.arange(B)[:, None, None, None]
    g_ix = jnp.arange(G)[None, :, None, None]
    kg = ks[b_ix, g_ix, idx].reshape(B, G, T, n_top * SEL_BLOCK, HEAD_DIM)
    vg = vs[b_ix, g_ix, idx].reshape(B, G, T, n_top * SEL_BLOCK, HEAD_DIM)
    kpos = (idx[..., None] * SEL_BLOCK + jnp.arange(SEL_BLOCK, dtype=jnp.int32)).reshape(B, G, T, n_top * SEL_BLOCK)
    m_sel = jnp.repeat(jnp.isfinite(top_s), SEL_BLOCK, axis=-1) & (kpos <= t)
    p_s = masked_softmax(jnp.einsum('bgrtd,bgtsd->bgrts', qg, kg), m_sel[:, :, None])
    o_s = jnp.einsum('bgrts,bgtsd->bgrtd', p_s.astype(vg.dtype), vg)
    wp = w_pos[None, :]
    m_w = (wp <= t) & (wp > t - WINDOW) & (wp >= 0)
    p_w = masked_softmax(jnp.einsum('bgrtd,bgwd->bgrtw', qg, kw), m_w)
    o_w = jnp.einsum('bgrtw,bgwd->bgrtd', p_w.astype(vw.dtype), vw)
    gt = gates.reshape(B, T, G, R, 3).transpose(0, 2, 3, 1, 4).astype(jnp.float32)
    o = gt[..., 0:1] * o_c + gt[..., 1:2] * o_s + gt[..., 2:3] * o_w
    return o.transpose(0, 3, 1, 2, 4).reshape(B, T, NSA_DIM).astype(q.dtype)


def nsa_prompt(q, kv, gates, cw):
    B, T = q.shape[:2]
    kc, vc, ks, vs = nsa_prepare(kv[:, :, :4], cw)
    win = jnp.pad(kv[:, :, 4:6], ((0, 0), (WINDOW, 0), (0, 0), (0, 0), (0, 0))).transpose(2, 0, 3, 1, 4)

    def block(bi):
        q0 = bi * NSA_QBLK
        kw = lax.dynamic_slice_in_dim(win, q0, WINDOW + NSA_QBLK, axis=3)
        return nsa_block(lax.dynamic_slice_in_dim(q, q0, NSA_QBLK, axis=1),
                         q0 + jnp.arange(NSA_QBLK, dtype=jnp.int32), kc, vc, ks, vs, kw[0], kw[1],
                         q0 - WINDOW + jnp.arange(WINDOW + NSA_QBLK, dtype=jnp.int32),
                         lax.dynamic_slice_in_dim(gates, q0, NSA_QBLK, axis=1))

    out = lax.map(block, jnp.arange(T // NSA_QBLK, dtype=jnp.int32))
    return out.transpose(1, 0, 2, 3).reshape(B, T, NSA_DIM)


def nsa_sample(q, kv, gates, pos, cache, page_table, win_buf, past_len, cw):
    Bs, T = q.shape[:2]
    past = cache[page_table].reshape(Bs, past_len, 4, NSA_KV_HEADS, HEAD_DIM).astype(kv.dtype)
    kc, vc, ks, vs = nsa_prepare(jnp.concatenate([past, kv[:, :, :4]], axis=1), cw)
    nw = win_buf.shape[1]
    win = jnp.concatenate([win_buf.astype(kv.dtype), kv[:, :, 4:6]], axis=1)
    wt = win.transpose(2, 0, 3, 1, 4)
    wpos = past_len - nw + jnp.arange(nw + T, dtype=jnp.int32)
    out = nsa_block(q, pos, kc, vc, ks, vs, wt[0], wt[1], wpos, gates)
    return out, win[:, T:]


def odd_project(xn, pos, w_in):
    B, T, _ = xn.shape
    proj = xn @ w_in
    q = rope(proj[..., :MOBA_DIM].reshape(B, T, MOBA_HEADS, HEAD_DIM), pos)
    kv = proj[..., MOBA_DIM:].reshape(B, T, 2, MOBA_KV_HEADS, HEAD_DIM)
    kv = jnp.stack([rope(kv[:, :, 0], pos), kv[:, :, 1]], axis=2)
    return q, kv


def moba_prepare(kv):
    B, L = kv.shape[:2]
    nb = -(-L // MOBA_BLOCK)
    kvp = jnp.pad(kv, ((0, 0), (0, nb * MOBA_BLOCK - L), (0, 0), (0, 0), (0, 0)))
    kvp = kvp.reshape(B, nb, MOBA_BLOCK, 2, MOBA_KV_HEADS, HEAD_DIM).transpose(3, 0, 4, 1, 2, 5)
    means = jnp.mean(kvp[0].astype(jnp.float32), axis=3)
    return kvp[0], kvp[1], means


def moba_block(q, q_pos, kb, vb, means):
    B, T = q.shape[:2]
    G, R = MOBA_KV_HEADS, MOBA_HEADS // MOBA_KV_HEADS
    nb = kb.shape[2]
    qg = (q * HEAD_DIM ** -0.5).reshape(B, T, G, R, HEAD_DIM).transpose(0, 2, 3, 1, 4)
    t = q_pos[:, None]
    cur = t // MOBA_BLOCK
    gate = jnp.einsum('bgrtd,bgnd->bgrtn', qg.astype(jnp.float32), means)
    gate = jnp.where(jnp.arange(nb, dtype=jnp.int32)[None, :] < cur, gate, -jnp.inf)
    n_top = min(MOBA_TOPK, nb)
    top_s, idx = lax.top_k(gate, n_top)
    idx = jnp.concatenate([idx, jnp.broadcast_to(cur[None, None, None], (B, G, R, T, 1)).astype(idx.dtype)], axis=-1)
    ok = jnp.concatenate([jnp.isfinite(top_s), jnp.ones((B, G, R, T, 1), dtype=bool)], axis=-1)
    n_slot = n_top + 1
    b_ix = jnp.arange(B)[:, None, None, None]
    g_ix = jnp.arange(G)[None, :, None, None]
    s = jnp.concatenate([jnp.einsum('bgrtd,bgrtkd->bgrtk', qg, kb[b_ix, g_ix, idx[..., j]]) for j in range(n_slot)], axis=-1)
    kpos = (idx[..., None] * MOBA_BLOCK + jnp.arange(MOBA_BLOCK, dtype=jnp.int32)).reshape(B, G, R, T, n_slot * MOBA_BLOCK)
    mask = jnp.repeat(ok, MOBA_BLOCK, axis=-1) & (kpos <= t)
    p = masked_softmax(s, mask).reshape(B, G, R, T, n_slot, MOBA_BLOCK)
    o = jnp.einsum('bgrtk,bgrtkd->bgrtd', p[..., 0, :].astype(vb.dtype), vb[b_ix, g_ix, idx[..., 0]])
    for j in range(1, n_slot):
        o = o + jnp.einsum('bgrtk,bgrtkd->bgrtd', p[..., j, :].astype(vb.dtype), vb[b_ix, g_ix, idx[..., j]])
    return o.transpose(0, 3, 1, 2, 4).reshape(B, T, MOBA_DIM).astype(q.dtype)


def moba_prompt(q, kv):
    B, T = q.shape[:2]
    kb, vb, means = moba_prepare(kv)

    def block(bi):
        q0 = bi * MOBA_QBLK
        return moba_block(lax.dynamic_slice_in_dim(q, q0, MOBA_QBLK, axis=1),
                          q0 + jnp.arange(MOBA_QBLK, dtype=jnp.int32), kb, vb, means)

    out = lax.map(block, jnp.arange(T // MOBA_QBLK, dtype=jnp.int32))
    return out.transpose(1, 0, 2, 3).reshape(B, T, MOBA_DIM)


def moba_sample(q, kv, pos, cache, page_table, past_len):
    Bs = q.shape[0]
    past = cache[page_table].reshape(Bs, past_len, 2, MOBA_KV_HEADS, HEAD_DIM).astype(kv.dtype)
    kb, vb, means = moba_prepare(jnp.concatenate([past, kv], axis=1))
    return moba_block(q, pos, kb, vb, means)


def setup_inputs(seed: int = 0) -> dict:
    key = jax.random.key(seed)
    keys = jax.random.split(key, 40)

    def nrm(i, shape, scale):
        return jax.random.normal(keys[i], shape, jnp.float32) * scale

    n_pages = PAST_LEN // PAGE_SIZE
    n_used = DEC_BATCH * n_pages
    n_phys = n_used + n_used // 4
    win_buf = min(WINDOW, PAST_LEN)
    page_table = jax.random.permutation(keys[0], n_phys)[:n_used].reshape(DEC_BATCH, n_pages).astype(jnp.int32)
    return {
        'x_prompt': nrm(1, (BATCH, SEQ, D_MODEL), 1.0),
        'x_sample': nrm(2, (DEC_BATCH, DEC_SEQ, D_MODEL), 1.0),
        'cache_nsa_kv': nrm(3, (N_EVEN, n_phys, PAGE_SIZE, 4, NSA_KV_HEADS, HEAD_DIM), 1.0),
        'cache_moba_kv': nrm(4, (N_ODD, n_phys, PAGE_SIZE, 2, MOBA_KV_HEADS, HEAD_DIM), 1.0),
        'state_win_kv': nrm(5, (N_EVEN, DEC_BATCH, win_buf, 2, NSA_KV_HEADS, HEAD_DIM), 1.0),
        'state_wkv': nrm(6, (N_EVEN, DEC_BATCH, RWKV_HEADS, HEAD_DIM, HEAD_DIM), 0.5),
        'state_shift': nrm(7, (N_EVEN, DEC_BATCH, RWKV_COLS), 1.0),
        'page_table': page_table,
        'norm_mix': 1.0 + nrm(8, (DEPTH, D_MODEL), 0.1),
        'norm_ffn': 1.0 + nrm(9, (DEPTH, D_MODEL), 0.1),
        'norm_final': 1.0 + nrm(10, (D_MODEL,), 0.1),
        'even_w_in': nrm(11, (N_EVEN, D_MODEL, EVEN_COLS), D_MODEL ** -0.5),
        'even_w_out': nrm(12, (N_EVEN, RWKV_DIM + NSA_DIM, D_MODEL), (RWKV_DIM + NSA_DIM) ** -0.5),
        'rwkv_mu': jax.random.uniform(keys[13], (N_EVEN, RWKV_COLS), jnp.float32),
        'rwkv_w0': nrm(14, (N_EVEN, RWKV_DIM), 0.5),
        'rwkv_w_up': nrm(15, (N_EVEN, DECAY_LORA, RWKV_DIM), 0.1),
        'rwkv_a0': nrm(16, (N_EVEN, RWKV_DIM), 0.5),
        'rwkv_a_up': nrm(17, (N_EVEN, AAA_LORA, RWKV_DIM), 0.1),
        'rwkv_g_up': nrm(18, (N_EVEN, GATE_LORA, RWKV_DIM), GATE_LORA ** -0.5),
        'rwkv_k_k': 0.85 + nrm(19, (N_EVEN, RWKV_DIM), 0.05),
        'rwkv_k_a': 1.0 + nrm(20, (N_EVEN, RWKV_DIM), 0.05),
        'rwkv_r_k': nrm(21, (N_EVEN, RWKV_HEADS, HEAD_DIM), 0.1),
        'rwkv_ln_g': 1.0 + nrm(22, (N_EVEN, RWKV_DIM), 0.1),
        'rwkv_ln_b': nrm(23, (N_EVEN, RWKV_DIM), 0.02),
        'nsa_gate_b': nrm(24, (N_EVEN, 3 * NSA_HEADS), 0.1),
        'nsa_cmp_w1': nrm(25, (N_EVEN, 2, CMP_LEN, HEAD_DIM, HEAD_DIM), (CMP_LEN * HEAD_DIM) ** -0.5),
        'nsa_cmp_pe': nrm(26, (N_EVEN, 2, CMP_LEN, HEAD_DIM), 0.1),
        'nsa_cmp_w2': nrm(27, (N_EVEN, 2, HEAD_DIM, HEAD_DIM), HEAD_DIM ** -0.5),
        'odd_w_in': nrm(28, (N_ODD, D_MODEL, ODD_COLS), D_MODEL ** -0.5),
        'odd_w_out': nrm(29, (N_ODD, MOBA_DIM, D_MODEL), MOBA_DIM ** -0.5),
        'ffn_w_gate': nrm(30, (DEPTH, D_MODEL, D_FF), D_MODEL ** -0.5),
        'ffn_w_up': nrm(31, (DEPTH, D_MODEL, D_FF), D_MODEL ** -0.5),
        'ffn_w_down': nrm(32, (DEPTH, D_FF, D_MODEL), D_FF ** -0.5),
    }


def reference(x_prompt, x_sample, cache_nsa_kv, cache_moba_kv, state_win_kv, state_wkv, state_shift, page_table,
              norm_mix, norm_ffn, norm_final, even_w_in, even_w_out, rwkv_mu, rwkv_w0, rwkv_w_up, rwkv_a0,
              rwkv_a_up, rwkv_g_up, rwkv_k_k, rwkv_k_a, rwkv_r_k, rwkv_ln_g, rwkv_ln_b, nsa_gate_b,
              nsa_cmp_w1, nsa_cmp_pe, nsa_cmp_w2, odd_w_in, odd_w_out, ffn_w_gate, ffn_w_up, ffn_w_down):
    B, T = x_prompt.shape[:2]
    Bs, Ts = x_sample.shape[:2]
    past_len = page_table.shape[1] * cache_nsa_kv.shape[2]
    pos_p = jnp.arange(T, dtype=jnp.int32)
    pos_s = past_len + jnp.arange(Ts, dtype=jnp.int32)
    hp, hs = x_prompt, x_sample
    nsa_p, nsa_s, moba_p, moba_s = [], [], [], []
    win_p, win_s, wkv_p, wkv_s, sh_p, sh_s = [], [], [], [], [], []
    for layer in range(DEPTH):
        i = layer // 2
        xp = rms_norm(hp, norm_mix[layer])
        xs = rms_norm(hs, norm_mix[layer])
        if layer % 2 == 0:
            rp = (rwkv_mu[i], rwkv_w0[i], rwkv_w_up[i], rwkv_a0[i], rwkv_a_up[i], rwkv_g_up[i],
                  rwkv_k_k[i], rwkv_k_a[i], rwkv_r_k[i], rwkv_ln_g[i], rwkv_ln_b[i])
            cw = (nsa_cmp_w1[i], nsa_cmp_pe[i], nsa_cmp_w2[i])
            rw, q, kv, gates = even_project(xp, pos_p, even_w_in[i], nsa_gate_b[i])
            a_out, shift_new, wkv_new = rwkv_mix(rw, jnp.zeros((B, RWKV_COLS), rw.dtype),
                                                 jnp.zeros((B, RWKV_HEADS, HEAD_DIM, HEAD_DIM), rw.dtype), rp)
            b_out = nsa_prompt(q, kv, gates, cw)
            hp = hp + jnp.concatenate([a_out, b_out], axis=-1) @ even_w_out[i]
            nsa_p.append(kv[:, :, :4])
            win_p.append(kv[:, T - min(WINDOW, T):, 4:6])
            wkv_p.append(wkv_new)
            sh_p.append(shift_new)
            rw, q, kv, gates = even_project(xs, pos_s, even_w_in[i], nsa_gate_b[i])
            a_out, shift_new, wkv_new = rwkv_mix(rw, state_shift[i], state_wkv[i], rp)
            b_out, win_new = nsa_sample(q, kv, gates, pos_s, cache_nsa_kv[i], page_table, state_win_kv[i], past_len, cw)
            hs = hs + jnp.concatenate([a_out, b_out], axis=-1) @ even_w_out[i]
            nsa_s.append(kv[:, :, :4])
            win_s.append(win_new)
            wkv_s.append(wkv_new)
            sh_s.append(shift_new)
        else:
            q, kv = odd_project(xp, pos_p, odd_w_in[i])
            hp = hp + moba_prompt(q, kv) @ odd_w_out[i]
            moba_p.append(kv)
            q, kv = odd_project(xs, pos_s, odd_w_in[i])
            hs = hs + moba_sample(q, kv, pos_s, cache_moba_kv[i], page_table, past_len) @ odd_w_out[i]
            moba_s.append(kv)
        hp = hp + swiglu(rms_norm(hp, norm_ffn[layer]), ffn_w_gate[layer], ffn_w_up[layer], ffn_w_down[layer])
        hs = hs + swiglu(rms_norm(hs, norm_ffn[layer]), ffn_w_gate[layer], ffn_w_up[layer], ffn_w_down[layer])
    y_prompt = rms_norm(hp, norm_final)
    y_sample = rms_norm(hs, norm_final)
    return (y_prompt, y_sample, jnp.stack(nsa_p), jnp.stack(nsa_s), jnp.stack(moba_p), jnp.stack(moba_s),
            jnp.stack(win_p), jnp.stack(win_s), jnp.stack(wkv_p), jnp.stack(wkv_s), jnp.stack(sh_p), jnp.stack(sh_s))
```

```python
import functools
import math

import jax
import jax.numpy as jnp
from jax import lax
from jax.experimental import pallas as pl
from jax.experimental.pallas import tpu as pltpu

F32 = jnp.float32
BF16 = jnp.bfloat16
HIGHEST = lax.Precision.HIGHEST

HEAD_DIM = 64
NORM_EPS = 1e-6
ROPE_THETA = 10000.0
DECAY_LORA = 64
AAA_LORA = 64
GATE_LORA = 128
RWKV_GN_EPS = 64e-5
NSA_KV_HEADS = 2
CMP_STRIDE = 16
CMP_LEN = 2 * CMP_STRIDE
SEL_BLOCK = 64
SEL_TOPN = 16
WINDOW = 512
FORCE_BONUS = 100.0
MOBA_KV_HEADS = 4
MOBA_BLOCK = 256
MOBA_TOPK = 3

LANES = 128
SUBLANES = 8
VMEM_LIMIT = 56 * 1024 * 1024

ROW_TILE = 512
FFN_ROW_TILE = 1024
FFN_COL_TILE = 256
ATT_Q_TILE = 128
ATT_K_TILE = 256
SCAN_CHUNK = 64
SAMPLE_Q_PAD = 8
PAGES_PER_STEP = 8

NT = (((1,), (1,)), ((), ()))
TN = (((0,), (0,)), ((), ()))
NEG_INF = float("-inf")


def _round_up(x, m):
    return -(-x // m) * m


def _mm(a, b, dims=None, exact=False):
    if dims is None:
        dims = (((a.ndim - 1,), (0,)), ((), ()))
    if exact:
        return lax.dot_general(a.astype(F32), b.astype(F32), dims, precision=HIGHEST,
                               preferred_element_type=F32)
    return lax.dot_general(a.astype(BF16), b.astype(BF16), dims, preferred_element_type=F32)


def _params(*sem):
    return pltpu.CompilerParams(dimension_semantics=sem, vmem_limit_bytes=VMEM_LIMIT)


def _rms_norm(x, g):
    return x * lax.rsqrt(jnp.mean(x * x, axis=-1, keepdims=True) + NORM_EPS) * g


def _rope_chunk(x, cos, sin):
    lane = lax.broadcasted_iota(jnp.int32, x.shape, 1)
    half = HEAD_DIM // 2
    partner = jnp.where((lane % HEAD_DIM) < half, pltpu.roll(x, LANES - half, 1), pltpu.roll(x, half, 1))
    return x * cos + partner * sin


def _rank_desc(score, ncols):
    lane = lax.broadcasted_iota(jnp.int32, score.shape, 1)
    rank = jnp.zeros(score.shape, F32)
    for j in range(ncols):
        col = score[:, j:j + 1]
        rank = rank + jnp.where(col > score, 1.0, 0.0) + jnp.where(col == score, jnp.where(lane > j, 1.0, 0.0), 0.0)
    return rank


def _proj_kernel(spec, x_ref, g_ref, cos_ref, sin_ref, *refs):
    nseg = len(spec)
    w_refs = refs[:nseg]
    nbias = sum(1 for s in spec if s[2])
    b_refs = list(refs[nseg:nseg + nbias])
    o_refs = refs[nseg + nbias:]
    xn = _rms_norm(x_ref[...], g_ref[...]).astype(BF16)
    for (ncols, rope_chunks, sig), w_ref, o_ref in zip(spec, w_refs, o_refs):
        y = jnp.dot(xn, w_ref[...], preferred_element_type=F32)
        if sig:
            y = jax.nn.sigmoid(y + b_refs.pop(0)[...])
        if rope_chunks:
            cos = cos_ref[...]
            sin = sin_ref[...]
            for c in range(ncols // LANES):
                yc = y[:, c * LANES:(c + 1) * LANES]
                if c in rope_chunks:
                    yc = _rope_chunk(yc, cos, sin)
                o_ref[:, c * LANES:(c + 1) * LANES] = yc
        else:
            o_ref[...] = y


def _project(x, g, cos_tab, sin_tab, weights, biases, spec, name):
    M, D = x.shape
    tm = min(ROW_TILE, M)
    ntab = cos_tab.shape[0] // tm
    in_specs = [pl.BlockSpec((tm, D), lambda i: (i, 0)),
                pl.BlockSpec((1, D), lambda i: (0, 0)),
                pl.BlockSpec((tm, LANES), lambda i: (i % ntab, 0)),
                pl.BlockSpec((tm, LANES), lambda i: (i % ntab, 0))]
    in_specs += [pl.BlockSpec(w.shape, lambda i: (0, 0)) for w in weights]
    in_specs += [pl.BlockSpec(b.shape, lambda i: (0, 0)) for b in biases]
    out_shape = [jax.ShapeDtypeStruct((M, s[0]), F32) for s in spec]
    out_specs = [pl.BlockSpec((tm, s[0]), lambda i: (i, 0)) for s in spec]
    return pl.pallas_call(
        functools.partial(_proj_kernel, spec), grid=(M // tm,), in_specs=in_specs, out_specs=out_specs,
        out_shape=out_shape, compiler_params=_params("parallel"), name=name,
    )(x, g.reshape(1, D), cos_tab, sin_tab, *weights, *biases)


def _rope_tables(pos):
    half = HEAD_DIM // 2
    inv = ROPE_THETA ** (-jnp.arange(half, dtype=F32) / half)
    ang = pos.astype(F32)[:, None] * inv[None, :]
    cos = jnp.cos(ang)
    sin = jnp.sin(ang)
    cos_t = jnp.tile(cos, (1, LANES // half))
    sin_t = jnp.tile(jnp.concatenate([-sin, sin], axis=1), (1, LANES // HEAD_DIM))
    return cos_t, sin_t


def _even_out_kernel(res_ref, y_ref, g_ref, oc_ref, os_ref, ow_ref, gt_ref, ec_ref, es_ref, ew_ref,
                     w1_ref, w2_ref, o_ref):
    a = y_ref[...] * g_ref[...]
    gt = gt_ref[...]
    b = (_mm(gt, ec_ref[...], exact=True) * oc_ref[...] + _mm(gt, es_ref[...], exact=True) * os_ref[...]
         + _mm(gt, ew_ref[...], exact=True) * ow_ref[...])
    o_ref[...] = res_ref[...] + _mm(a, w1_ref[...]) + _mm(b, w2_ref[...])


def _even_out(res, y, g, o_c, o_s, o_w, gates, w_out):
    M, D = res.shape
    n_rw = y.shape[1]
    n_nsa = o_c.shape[1]
    tm = min(ROW_TILE, M)
    heads = n_nsa // HEAD_DIM
    col = jnp.arange(n_nsa)[None, :] // HEAD_DIM
    row = jnp.arange(LANES)[:, None]
    expand = [(row == col * 3 + br).astype(F32) for br in range(3)]
    del heads
    row_spec = lambda n: pl.BlockSpec((tm, n), lambda i: (i, 0))
    full = lambda a: pl.BlockSpec(a.shape, lambda i: (0, 0))
    w1 = w_out[:n_rw].astype(BF16)
    w2 = w_out[n_rw:].astype(BF16)
    return pl.pallas_call(
        _even_out_kernel, grid=(M // tm,),
        in_specs=[row_spec(D), row_spec(n_rw), row_spec(n_rw), row_spec(n_nsa), row_spec(n_nsa), row_spec(n_nsa),
                  row_spec(LANES), full(expand[0]), full(expand[1]), full(expand[2]), full(w1), full(w2)],
        out_specs=row_spec(D), out_shape=jax.ShapeDtypeStruct((M, D), F32),
        compiler_params=_params("parallel"), name="even_out",
    )(res, y, g, o_c, o_s, o_w, gates, *expand, w1, w2)


def _odd_out_kernel(res_ref, a_ref, w_ref, o_ref):
    o_ref[...] = res_ref[...] + _mm(a_ref[...], w_ref[...])


def _odd_out(res, a, w_out):
    M, D = res.shape
    tm = min(ROW_TILE, M)
    w = w_out.astype(BF16)
    return pl.pallas_call(
        _odd_out_kernel, grid=(M // tm,),
        in_specs=[pl.BlockSpec((tm, D), lambda i: (i, 0)), pl.BlockSpec((tm, a.shape[1]), lambda i: (i, 0)),
                  pl.BlockSpec(w.shape, lambda i: (0, 0))],
        out_specs=pl.BlockSpec((tm, D), lambda i: (i, 0)), out_shape=jax.ShapeDtypeStruct((M, D), F32),
        compiler_params=_params("parallel"), name="odd_out",
    )(res, a, w)


def _ffn_kernel(final_norm, x_ref, g_ref, wg_ref, wu_ref, wd_ref, gf_ref, o_ref, xn_scr, acc_scr):
    j = pl.program_id(1)

    @pl.when(j == 0)
    def _():
        xn_scr[...] = _rms_norm(x_ref[...], g_ref[...]).astype(BF16)
        acc_scr[...] = jnp.zeros(acc_scr.shape, F32)

    xn = xn_scr[...]
    h = jax.nn.silu(jnp.dot(xn, wg_ref[...], preferred_element_type=F32)) * jnp.dot(
        xn, wu_ref[...], preferred_element_type=F32)
    acc_scr[...] += _mm(h, wd_ref[...])

    @pl.when(j == pl.num_programs(1) - 1)
    def _():
        y = x_ref[...] + acc_scr[...]
        if final_norm:
            y = _rms_norm(y, gf_ref[...])
        o_ref[...] = y


def _ffn(x, g, wg, wu, wd, g_final=None):
    M, D = x.shape
    F = wg.shape[1]
    tm = min(FFN_ROW_TILE, M)
    tf = FFN_COL_TILE
    final_norm = g_final is not None
    gf = (g_final if final_norm else g).reshape(1, D)
    return pl.pallas_call(
        functools.partial(_ffn_kernel, final_norm), grid=(M // tm, F // tf),
        in_specs=[pl.BlockSpec((tm, D), lambda i, j: (i, 0)), pl.BlockSpec((1, D), lambda i, j: (0, 0)),
                  pl.BlockSpec((D, tf), lambda i, j: (0, j)), pl.BlockSpec((D, tf), lambda i, j: (0, j)),
                  pl.BlockSpec((tf, D), lambda i, j: (j, 0)), pl.BlockSpec((1, D), lambda i, j: (0, 0))],
        out_specs=pl.BlockSpec((tm, D), lambda i, j: (i, 0)), out_shape=jax.ShapeDtypeStruct((M, D), F32),
        scratch_shapes=[pltpu.VMEM((tm, D), BF16), pltpu.VMEM((tm, D), F32)],
        compiler_params=_params("parallel", "arbitrary"), name="ffn",
    )(x, g.reshape(1, D), wg.astype(BF16), wu.astype(BF16), wd.astype(BF16), gf)


def _rwkv_pre_kernel(n_dim, rw_ref, prev_ref, mu_ref, w0_ref, a0_ref, kk_ref, ka_ref, wup_ref, aup_ref, gup_ref,
                     hsum_ref, r_out, lw_out, k_out, v_out, kk_out, b_out, g_out):
    rw = rw_ref[...]
    xm = rw + (prev_ref[...] - rw) * mu_ref[...]
    r = xm[:, :n_dim]
    k = xm[:, n_dim:2 * n_dim]
    v = xm[:, 2 * n_dim:3 * n_dim]
    lora = xm[:, 3 * n_dim:3 * n_dim + DECAY_LORA + AAA_LORA]
    xg = xm[:, 3 * n_dim + DECAY_LORA + AAA_LORA:]
    lw = -math.exp(-0.5) * jax.nn.sigmoid(w0_ref[...] + _mm(jnp.tanh(lora), wup_ref[...], exact=True))
    a = jax.nn.sigmoid(a0_ref[...] + _mm(lora, aup_ref[...], exact=True))
    g = _mm(jax.nn.sigmoid(xg), gup_ref[...])
    kk = k * kk_ref[...]
    norm = jnp.sqrt(_mm(kk * kk, hsum_ref[...], exact=True))
    kk = kk / jnp.maximum(norm, 1e-12)
    r_out[...] = r
    lw_out[...] = lw
    k_out[...] = k * (1.0 + (a - 1.0) * ka_ref[...])
    v_out[...] = v
    kk_out[...] = kk
    b_out[...] = kk * a
    g_out[...] = g


def _rwkv_pre(rw, prev, mu, w0, w_up, a0, a_up, g_up, k_k, k_a):
    M, ncols = rw.shape
    n_dim = w0.shape[0]
    tm = min(ROW_TILE, M)
    zeros = jnp.zeros((AAA_LORA, n_dim), F32)
    wup_pad = jnp.concatenate([w_up, zeros], axis=0)
    aup_pad = jnp.concatenate([jnp.zeros((DECAY_LORA, n_dim), F32), a_up], axis=0)
    head = jnp.arange(n_dim) // HEAD_DIM
    hsum = (head[:, None] == head[None, :]).astype(F32)
    vec = lambda a: a.reshape(1, -1)
    row = lambda n: pl.BlockSpec((tm, n), lambda i: (i, 0))
    full = lambda a: pl.BlockSpec(a.shape, lambda i: (0, 0))
    ins = [rw, prev, vec(mu), vec(w0), vec(a0), vec(k_k), vec(k_a), wup_pad, aup_pad, g_up, hsum]
    return pl.pallas_call(
        functools.partial(_rwkv_pre_kernel, n_dim), grid=(M // tm,),
        in_specs=[row(ncols), row(ncols)] + [full(a) for a in ins[2:]],
        out_specs=[row(n_dim)] * 7, out_shape=[jax.ShapeDtypeStruct((M, n_dim), F32)] * 7,
        compiler_params=_params("parallel"), name="rwkv_pre",
    )(*ins)


def _rwkv_scan_kernel(n_heads, chunk, r_ref, lw_ref, k_ref, v_ref, kk_ref, b_ref, s0_ref, rk_ref, lng_ref, lnb_ref,
                      y_ref, s_out_ref, s_scr):
    c = pl.program_id(1)

    @pl.when(c == 0)
    def _():
        s_scr[...] = s0_ref[...]

    C = chunk
    ti = lax.broadcasted_iota(jnp.int32, (C, C), 0)
    si = lax.broadcasted_iota(jnp.int32, (C, C), 1)
    incl = jnp.where(si <= ti, 1.0, 0.0)
    n_double = max(1, math.ceil(math.log2(C)))
    for h in range(n_heads):
        r = r_ref[h]
        lw = lw_ref[h]
        k = k_ref[h]
        v = v_ref[h]
        kk = kk_ref[h]
        b = b_ref[h]
        s0 = s_scr[h]
        cum = _mm(incl, lw, exact=True)
        cum_end = cum[C - 1:C, :]
        kkw = kk * jnp.exp(cum - lw)
        rwc = r * jnp.exp(cum)
        w_inv = jnp.exp(-cum)
        kd = k * w_inv
        bd = b * w_inv
        w_end = jnp.exp(cum_end - cum)
        a_ub = jnp.where(si < ti, _mm(kkw, bd, NT, exact=True), 0.0)
        a_vk = jnp.where(si < ti, _mm(kkw, kd, NT, exact=True), 0.0)
        b_rb = jnp.where(si <= ti, _mm(rwc, bd, NT, exact=True), 0.0)
        b_rk = jnp.where(si <= ti, _mm(rwc, kd, NT, exact=True), 0.0)
        u = _mm(kkw, s0, NT, exact=True) + _mm(a_vk, v, exact=True)
        p = -a_ub
        u = u + _mm(p, u, exact=True)
        for _ in range(n_double - 1):
            p = _mm(p, p, exact=True)
            u = u + _mm(p, u, exact=True)
        y = _mm(rwc, s0, NT, exact=True) + _mm(b_rk, v, exact=True) - _mm(b_rb, u, exact=True)
        s_new = (s0 * jnp.exp(cum_end) + _mm(v, k * w_end, TN, exact=True) - _mm(u, b * w_end, TN, exact=True))
        s_scr[h] = s_new
        mean = jnp.mean(y, axis=-1, keepdims=True)
        var = jnp.mean(jnp.square(y - mean), axis=-1, keepdims=True)
        yn = (y - mean) * lax.rsqrt(var + RWKV_GN_EPS) * lng_ref[h:h + 1, :] + lnb_ref[h:h + 1, :]
        bonus = jnp.sum(r * k * rk_ref[h:h + 1, :], axis=-1, keepdims=True) * v
        y_ref[h] = yn + bonus

    @pl.when(c == pl.num_programs(1) - 1)
    def _():
        s_out_ref[...] = s_scr[...]


def _rwkv_scan(r, lw, k, v, kk, b, s0, r_k, ln_g, ln_b, chunk):
    n_bh, T, dh = r.shape
    H = r_k.shape[0]
    seq = pl.BlockSpec((H, chunk, dh), lambda i, c: (i, c, 0))
    state = pl.BlockSpec((H, dh, dh), lambda i, c: (i, 0, 0))
    par = pl.BlockSpec((H, dh), lambda i, c: (0, 0))
    return pl.pallas_call(
        functools.partial(_rwkv_scan_kernel, H, chunk), grid=(n_bh // H, T // chunk),
        in_specs=[seq] * 6 + [state, par, par, par], out_specs=[seq, state],
        out_shape=[jax.ShapeDtypeStruct((n_bh, T, dh), F32), jax.ShapeDtypeStruct((n_bh, dh, dh), F32)],
        scratch_shapes=[pltpu.VMEM((H, dh, dh), F32)],
        compiler_params=_params("parallel", "arbitrary"), name="rwkv_scan",
    )(r, lw, k, v, kk, b, s0, r_k, ln_g.reshape(H, dh), ln_b.reshape(H, dh))


def _rwkv_mix(rw, shift_prev, wkv0, p, chunk):
    mu, w0, w_up, a0, a_up, g_up, k_k, k_a, r_k, ln_g, ln_b = p
    B, T, ncols = rw.shape
    H = r_k.shape[0]
    n_dim = H * HEAD_DIM
    prev = jnp.concatenate([shift_prev[:, None], rw[:, :-1]], axis=1)
    outs = _rwkv_pre(rw.reshape(B * T, ncols), prev.reshape(B * T, ncols), mu, w0, w_up, a0, a_up, g_up, k_k, k_a)
    g = outs[6]
    Tp = _round_up(T, chunk)

    def heads(t):
        t = t.reshape(B, T, H, HEAD_DIM).transpose(0, 2, 1, 3)
        t = jnp.pad(t, ((0, 0), (0, 0), (0, Tp - T), (0, 0)))
        return t.reshape(B * H, Tp, HEAD_DIM)

    y, s_new = _rwkv_scan(*[heads(t) for t in outs[:6]], wkv0.reshape(B * H, HEAD_DIM, HEAD_DIM), r_k, ln_g, ln_b,
                          chunk)
    y = y.reshape(B, H, Tp, HEAD_DIM)[:, :, :T].transpose(0, 2, 1, 3).reshape(B * T, n_dim)
    return y, g, s_new.reshape(B, H, HEAD_DIM, HEAD_DIM)


def _compress_kernel(x_ref, w1_ref, pe_ref, w2_ref, o_ref):
    w1 = w1_ref[...]
    bias = _mm(pe_ref[...], w1)[0:1, :]
    h = jax.nn.gelu(_mm(x_ref[...], w1) + bias)
    o_ref[...] = _mm(h, w2_ref[...])


def _nsa_compress(kv_cmp, w1, pe, w2):
    _, BG, L, dh = kv_cmp.shape
    nch = L // CMP_STRIDE
    x = kv_cmp[:, :, :nch * CMP_STRIDE].reshape(2, BG, nch, CMP_STRIDE * dh)
    xw = jnp.concatenate([x[:, :, :-1], x[:, :, 1:]], axis=-1)
    ncp = _round_up(nch - 1, LANES)
    xw = jnp.pad(xw, ((0, 0), (0, 0), (0, ncp - (nch - 1)), (0, 0)))
    kdim = CMP_LEN * dh
    pe8 = jnp.pad(pe.reshape(2, 1, kdim), ((0, 0), (0, SUBLANES - 1), (0, 0)))
    return pl.pallas_call(
        _compress_kernel, grid=(2, BG),
        in_specs=[pl.BlockSpec((None, None, ncp, kdim), lambda s, i: (s, i, 0, 0)),
                  pl.BlockSpec((None, kdim, dh), lambda s, i: (s, 0, 0)),
                  pl.BlockSpec((None, SUBLANES, kdim), lambda s, i: (s, 0, 0)),
                  pl.BlockSpec((None, dh, dh), lambda s, i: (s, 0, 0))],
        out_specs=pl.BlockSpec((None, None, ncp, dh), lambda s, i: (s, i, 0, 0)),
        out_shape=jax.ShapeDtypeStruct((2, BG, ncp, dh), F32),
        compiler_params=_params("parallel", "parallel"), name="nsa_compress",
    )(xw, w1.reshape(2, kdim, dh), pe8, w2)


def _cmp_select_kernel(tq, n_rep, q_base, n_sel, q_ref, kc_ref, vc_ref, ov_ref, oc_ref, sel_ref):
    qi = pl.program_id(1)
    rows = n_rep * tq
    ncp = kc_ref.shape[0]
    nsp = ov_ref.shape[1]
    q = q_ref[...] * HEAD_DIM ** -0.5
    s = _mm(q, kc_ref[...], NT)
    t_row = q_base + qi * tq + lax.broadcasted_iota(jnp.int32, (rows, 1), 0) % tq
    c_end = lax.broadcasted_iota(jnp.int32, (1, ncp), 1) * CMP_STRIDE + (CMP_LEN - 1)
    s = jnp.where(c_end <= t_row, s, NEG_INF)
    m = jnp.max(s, axis=-1, keepdims=True)
    e = jnp.exp(s - jnp.where(m == NEG_INF, 0.0, m))
    p = e / jnp.maximum(jnp.sum(e, axis=-1, keepdims=True), 1e-30)
    oc_ref[...] = _mm(p, vc_ref[...])
    p_sum = p[0:tq]
    for r in range(1, n_rep):
        p_sum = p_sum + p[r * tq:(r + 1) * tq]
    imp = _mm(p_sum, ov_ref[...], exact=True)
    t_q = q_base + qi * tq + lax.broadcasted_iota(jnp.int32, (tq, 1), 0)
    cur = t_q // SEL_BLOCK
    blk = lax.broadcasted_iota(jnp.int32, (tq, nsp), 1)
    forced = jnp.where(blk == cur, 1.0, jnp.where(blk == cur - 1, 1.0, jnp.where(blk == 0, 1.0, 0.0)))
    score = jnp.where(blk <= cur, imp + FORCE_BONUS * forced, NEG_INF)
    rank = _rank_desc(score, n_sel)
    sel_ref[...] = jnp.where(blk <= cur, jnp.where(rank < min(SEL_TOPN, n_sel), 1.0, 0.0), 0.0)


def _cmp_select(q, kvc, overlap, tq, q_base, n_sel):
    BG, nq, rows, dh = q.shape
    ncp, nsp = overlap.shape
    n_rep = rows // tq
    qspec = pl.BlockSpec((None, None, rows, dh), lambda i, j: (i, j, 0, 0))
    return pl.pallas_call(
        functools.partial(_cmp_select_kernel, tq, n_rep, q_base, n_sel), grid=(BG, nq),
        in_specs=[qspec, pl.BlockSpec((None, None, ncp, dh), lambda i, j: (0, i, 0, 0)),
                  pl.BlockSpec((None, None, ncp, dh), lambda i, j: (1, i, 0, 0)),
                  pl.BlockSpec((ncp, nsp), lambda i, j: (0, 0))],
        out_specs=[qspec, pl.BlockSpec((None, None, tq, nsp), lambda i, j: (i, j, 0, 0))],
        out_shape=[jax.ShapeDtypeStruct((BG, nq, rows, dh), F32), jax.ShapeDtypeStruct((BG, nq, tq, nsp), F32)],
        compiler_params=_params("parallel", "parallel"), name="nsa_cmp_select",
    )(q, kvc, kvc, overlap)


def _overlap_matrix(nc, nsel, ncp, nsp):
    i = jnp.arange(nc, dtype=jnp.int32)[:, None]
    j = jnp.arange(nsel, dtype=jnp.int32)[None, :]
    lo = jnp.maximum(i * CMP_STRIDE, j * SEL_BLOCK)
    hi = jnp.minimum(i * CMP_STRIDE + CMP_LEN, (j + 1) * SEL_BLOCK)
    ov = jnp.clip(hi - lo, 0).astype(F32) / CMP_LEN
    return jnp.pad(ov, ((0, ncp - nc), (0, nsp - nsel)))


def _flash_kernel(mode, tq, n_rep, tk, q_base, k_base, n_blk, *refs):
    if mode == "sel":
        q_ref, k_ref, v_ref, sel_ref, o_ref, m_scr, l_scr, acc_scr = refs
    elif mode == "win":
        q_ref, k_ref, v_ref, o_ref, m_scr, l_scr, acc_scr = refs
    else:
        q_ref, k_ref, v_ref, o_ref, m_scr, l_scr, acc_scr, mean_scr = refs
    qi = pl.program_id(1)
    rows = n_rep * tq
    q0 = q_base + qi * tq
    t_row = q0 + lax.broadcasted_iota(jnp.int32, (rows, 1), 0) % tq
    q = q_ref[...] * HEAD_DIM ** -0.5
    m_scr[...] = jnp.full(m_scr.shape, NEG_INF, F32)
    l_scr[...] = jnp.zeros(l_scr.shape, F32)
    acc_scr[...] = jnp.zeros(acc_scr.shape, F32)

    block = None
    if mode == "sel":
        block, blk_mask = SEL_BLOCK, sel_ref[...].astype(BF16)
    elif mode == "moba":
        block = MOBA_BLOCK
        nbp = mean_scr.shape[0]

        @pl.when(qi == 0)
        def _():
            mean_scr[...] = jnp.zeros(mean_scr.shape, F32)
            for n in range(n_blk):
                mean_scr[n:n + 1, :] = jnp.sum(k_ref[n * block:(n + 1) * block, :], axis=0, keepdims=True) / block

        gate = _mm(q, mean_scr[...], NT, exact=True)
        blk = lax.broadcasted_iota(jnp.int32, (rows, nbp), 1)
        cur = t_row // block
        gate = jnp.where(blk < cur, gate, NEG_INF)
        rank = _rank_desc(gate, n_blk)
        picked = jnp.where(blk < cur, jnp.where(rank < min(MOBA_TOPK, n_blk), 1.0, 0.0), 0.0)
        blk_mask = jnp.where(blk == cur, 1.0, picked).astype(BF16)

    lo = 0
    if mode == "win":
        lo = jnp.maximum(q0 - (WINDOW - 1) - k_base, 0) // tk
    hi = (q0 + tq - 1 - k_base) // tk + 1

    def body(ki, carry):
        k0 = pl.multiple_of(ki * tk, tk)
        kt = k_ref[pl.ds(k0, tk), :]
        vt = v_ref[pl.ds(k0, tk), :]
        s = _mm(q, kt, NT)
        kpos = k_base + k0 + lax.broadcasted_iota(jnp.int32, (1, tk), 1)
        s = jnp.where(kpos <= t_row, s, NEG_INF)
        if mode == "win":
            s = jnp.where(kpos > t_row - WINDOW, s, NEG_INF)
        else:
            nbp_ = blk_mask.shape[1]
            e_row = lax.broadcasted_iota(jnp.int32, (nbp_, tk), 0)
            e_col = lax.broadcasted_iota(jnp.int32, (nbp_, tk), 1)
            expand = jnp.where(e_row == ki * (tk // block) + e_col // block, 1.0, 0.0).astype(BF16)
            allowed = jnp.dot(blk_mask, expand, preferred_element_type=F32)
            if mode == "sel":
                allowed = jnp.concatenate([allowed] * n_rep, axis=0)
            s = jnp.where(allowed > 0.5, s, NEG_INF)
        m_prev = m_scr[...]
        m_new = jnp.maximum(m_prev, jnp.max(s, axis=-1, keepdims=True))
        m_safe = jnp.where(m_new == NEG_INF, 0.0, m_new)
        alpha = jnp.exp(m_prev - m_safe)
        p = jnp.exp(s - m_safe)
        l_scr[...] = alpha * l_scr[...] + jnp.sum(p, axis=-1, keepdims=True)
        acc_scr[...] = alpha * acc_scr[...] + _mm(p, vt)
        m_scr[...] = m_new
        return carry

    lax.fori_loop(lo, hi, body, 0)
    o_ref[...] = acc_scr[...] / jnp.maximum(l_scr[...], 1e-30)


def _flash(mode, q, k, v, sel, tq, q_base, k_base, n_blk):
    BG, nq, rows, dh = q.shape
    Lp = k.shape[1]
    n_rep = rows // tq
    tk = ATT_K_TILE
    assert Lp % tk == 0 and (q_base + nq * tq - 1 - k_base) // tk < Lp // tk
    qspec = pl.BlockSpec((None, None, rows, dh), lambda i, j: (i, j, 0, 0))
    kspec = pl.BlockSpec((None, Lp, dh), lambda i, j: (i, 0, 0))
    ins, in_specs = [q, k, v], [qspec, kspec, kspec]
    scratch = [pltpu.VMEM((rows, 1), F32), pltpu.VMEM((rows, 1), F32), pltpu.VMEM((rows, dh), F32)]
    if mode == "sel":
        nsp = sel.shape[-1]
        ins.append(sel)
        in_specs.append(pl.BlockSpec((None, None, tq, nsp), lambda i, j: (i, j, 0, 0)))
    if mode == "moba":
        scratch.append(pltpu.VMEM((_round_up(n_blk, LANES), dh), F32))
    return pl.pallas_call(
        functools.partial(_flash_kernel, mode, tq, n_rep, tk, q_base, k_base, n_blk), grid=(BG, nq),
        in_specs=in_specs, out_specs=qspec, out_shape=jax.ShapeDtypeStruct((BG, nq, rows, dh), F32),
        scratch_shapes=scratch, compiler_params=_params("parallel", "arbitrary"), name="flash_" + mode,
    )(*ins)


def _group_queries(q, n_groups, tq):
    B, T, n = q.shape
    R = n // HEAD_DIM // n_groups
    q = q.reshape(B, T // tq, tq, n_groups, R, HEAD_DIM).transpose(0, 3, 1, 4, 2, 5)
    return q.reshape(B * n_groups, T // tq, R * tq, HEAD_DIM)


def _ungroup(o, B, n_groups, tq):
    BG, nq, rows, dh = o.shape
    R = rows // tq
    o = o.reshape(B, n_groups, nq, R, tq, dh).transpose(0, 2, 4, 1, 3, 5)
    return o.reshape(B, nq * tq, n_groups * R * dh)


def _head_major(x, Lp):
    B, L, G, dh = x.shape
    x = jnp.pad(x.transpose(0, 2, 1, 3), ((0, 0), (0, 0), (0, Lp - L), (0, 0)))
    return x.reshape(B * G, Lp, dh)


def _nsa_attend(q, kv4, win_kv, cw, tq, q_base, win_base):
    B, Tq, _ = q.shape
    L = kv4.shape[1]
    G = NSA_KV_HEADS
    w1, pe, w2 = cw
    qg = _group_queries(q, G, tq)
    cmp_rows = jnp.stack([kv4[:, :, 0], kv4[:, :, 1]]).transpose(0, 1, 3, 2, 4).reshape(2, B * G, L, HEAD_DIM)
    kvc = _nsa_compress(cmp_rows, w1, pe, w2)
    nc = L // CMP_STRIDE - 1
    n_sel = -(-L // SEL_BLOCK)
    ncp = kvc.shape[2]
    nsp = _round_up(n_sel, LANES)
    o_c, sel = _cmp_select(qg, kvc, _overlap_matrix(nc, n_sel, ncp, nsp), tq, q_base, n_sel)
    Lp = _round_up(max(L, q_base + Tq), ATT_K_TILE)
    o_s = _flash("sel", qg, _head_major(kv4[:, :, 2], Lp), _head_major(kv4[:, :, 3], Lp), sel, tq, q_base, 0, n_sel)
    Lw = win_kv.shape[1]
    Lwp = _round_up(max(Lw, q_base + Tq - win_base), ATT_K_TILE)
    o_w = _flash("win", qg, _head_major(win_kv[:, :, 0], Lwp), _head_major(win_kv[:, :, 1], Lwp), None, tq, q_base,
                 win_base, 0)
    return tuple(_ungroup(o, B, G, tq) for o in (o_c, o_s, o_w))


def _moba_attend(q, kv, tq, q_base):
    B, Tq, _ = q.shape
    L = kv.shape[1]
    G = MOBA_KV_HEADS
    n_blk = -(-max(L, q_base + Tq) // MOBA_BLOCK)
    Lp = n_blk * MOBA_BLOCK
    qg = _group_queries(q, G, tq)
    o = _flash("moba", qg, _head_major(kv[:, :, 0], Lp), _head_major(kv[:, :, 1], Lp), None, tq, q_base, 0, n_blk)
    return _ungroup(o, B, G, tq)


def _gather_kernel(pt_ref, *refs):
    del pt_ref
    o_ref = refs[-1]
    rows = refs[0].shape[0]
    for j, src in enumerate(refs[:-1]):
        o_ref[j * rows:(j + 1) * rows, :] = src[...]


def _gather_pages(cache, page_table):
    _, page, W = cache.shape
    Bs, n_pages = page_table.shape
    pps = math.gcd(PAGES_PER_STEP, n_pages)

    def in_map(j):
        return lambda b, p, pt: (pt[b, p * pps + j], 0, 0)

    grid_spec = pltpu.PrefetchScalarGridSpec(
        num_scalar_prefetch=1, grid=(Bs, n_pages // pps),
        in_specs=[pl.BlockSpec((None, page, W), in_map(j)) for j in range(pps)],
        out_specs=pl.BlockSpec((None, pps * page, W), lambda b, p, pt: (b, p, 0)))
    return pl.pallas_call(
        _gather_kernel, grid_spec=grid_spec, out_shape=jax.ShapeDtypeStruct((Bs, n_pages * page, W), F32),
        compiler_params=_params("parallel", "parallel"), name="gather_pages",
    )(page_table, *([cache] * pps))


def _pad_queries(x, tq):
    return jnp.pad(x, ((0, 0), (0, tq - x.shape[1]), (0, 0)))


def kernel(x_prompt, x_sample, cache_nsa_kv, cache_moba_kv, state_win_kv, state_wkv, state_shift, page_table, norm_mix, norm_ffn, norm_final, even_w_in, even_w_out, rwkv_mu, rwkv_w0, rwkv_w_up, rwkv_a0, rwkv_a_up, rwkv_g_up, rwkv_k_k, rwkv_k_a, rwkv_r_k, rwkv_ln_g, rwkv_ln_b, nsa_gate_b, nsa_cmp_w1, nsa_cmp_pe, nsa_cmp_w2, odd_w_in, odd_w_out, ffn_w_gate, ffn_w_up, ffn_w_down):
    B, T, D = x_prompt.shape
    Bs, Ts, _ = x_sample.shape
    depth = norm_mix.shape[0]
    page = cache_nsa_kv.shape[2]
    past_len = page_table.shape[1] * page
    rwkv_dim = rwkv_w0.shape[1]
    rwkv_cols = rwkv_mu.shape[1]
    nsa_heads = nsa_gate_b.shape[1] // 3
    nsa_dim = nsa_heads * HEAD_DIM
    nsa_kv_cols = 6 * NSA_KV_HEADS * HEAD_DIM
    moba_kv_cols = 2 * MOBA_KV_HEADS * HEAD_DIM
    moba_dim = odd_w_in.shape[2] - moba_kv_cols
    tq_p = min(ATT_Q_TILE, T)
    tq_s = SAMPLE_Q_PAD
    n_win = state_win_kv.shape[2]

    cos_p, sin_p = _rope_tables(jnp.arange(T, dtype=jnp.int32))
    pos_s = past_len + jnp.arange(Ts, dtype=jnp.int32)
    cos_s, sin_s = _rope_tables(jnp.tile(pos_s, Bs))

    even_spec = ((rwkv_cols, (), False), (nsa_dim, tuple(range(nsa_dim // LANES)), False),
                 (nsa_kv_cols, tuple(range(0, nsa_kv_cols // LANES, 2)), False), (LANES, (), True))
    k_chunks = MOBA_KV_HEADS * HEAD_DIM // LANES
    odd_spec = ((moba_dim, tuple(range(moba_dim // LANES)), False), (moba_kv_cols, tuple(range(k_chunks)), False))

    hp = x_prompt.reshape(B * T, D)
    hs = x_sample.reshape(Bs * Ts, D)
    nsa_p, nsa_s, moba_p, moba_s = [], [], [], []
    win_p, win_s, wkv_p, wkv_s, sh_p, sh_s = [], [], [], [], [], []
    for layer in range(depth):
        i = layer // 2
        if layer % 2 == 0:
            w_in = even_w_in[i].astype(BF16)
            o = rwkv_cols
            n_gate = 3 * nsa_heads
            weights = [w_in[:, :o], w_in[:, o:o + nsa_dim], w_in[:, o + nsa_dim:o + nsa_dim + nsa_kv_cols],
                       jnp.pad(w_in[:, o + nsa_dim + nsa_kv_cols:], ((0, 0), (0, LANES - n_gate)))]
            gate_b = jnp.pad(nsa_gate_b[i], (0, LANES - n_gate)).reshape(1, LANES)
            rp = (rwkv_mu[i], rwkv_w0[i], rwkv_w_up[i], rwkv_a0[i], rwkv_a_up[i], rwkv_g_up[i],
                  rwkv_k_k[i], rwkv_k_a[i], rwkv_r_k[i], rwkv_ln_g[i], rwkv_ln_b[i])
            cw = (nsa_cmp_w1[i], nsa_cmp_pe[i], nsa_cmp_w2[i])

            rw, q, kv, gates = _project(hp, norm_mix[layer], cos_p, sin_p, weights, [gate_b], even_spec, "even_proj")
            rw3 = rw.reshape(B, T, rwkv_cols)
            kv6 = kv.reshape(B, T, 6, NSA_KV_HEADS, HEAD_DIM)
            y, g, wkv_new = _rwkv_mix(rw3, jnp.zeros((B, rwkv_cols), F32),
                                      jnp.zeros((B, rwkv_dim // HEAD_DIM, HEAD_DIM, HEAD_DIM), F32), rp, SCAN_CHUNK)
            o_c, o_s, o_w = _nsa_attend(q.reshape(B, T, nsa_dim), kv6[:, :, :4], kv6[:, :, 4:6], cw, tq_p, 0, 0)
            hp = _even_out(hp, y, g, o_c.reshape(B * T, nsa_dim), o_s.reshape(B * T, nsa_dim),
                           o_w.reshape(B * T, nsa_dim), gates, even_w_out[i])
            nsa_p.append(kv6[:, :, :4])
            win_p.append(kv6[:, T - min(WINDOW, T):, 4:6])
            wkv_p.append(wkv_new)
            sh_p.append(rw3[:, -1])

            rw, q, kv, gates = _project(hs, norm_mix[layer], cos_s, sin_s, weights, [gate_b], even_spec, "even_proj_s")
            rw3 = rw.reshape(Bs, Ts, rwkv_cols)
            kv6 = kv.reshape(Bs, Ts, 6, NSA_KV_HEADS, HEAD_DIM)
            y, g, wkv_new = _rwkv_mix(rw3, state_shift[i], state_wkv[i], rp, SUBLANES)
            past = _gather_pages(cache_nsa_kv[i].reshape(-1, page, 4 * NSA_KV_HEADS * HEAD_DIM), page_table)
            kv4 = jnp.concatenate([past.reshape(Bs, past_len, 4, NSA_KV_HEADS, HEAD_DIM), kv6[:, :, :4]], axis=1)
            win = jnp.concatenate([state_win_kv[i], kv6[:, :, 4:6]], axis=1)
            qs = _pad_queries(q.reshape(Bs, Ts, nsa_dim), tq_s)
            outs = _nsa_attend(qs, kv4, win, cw, tq_s, past_len, past_len - n_win)
            o_c, o_s, o_w = [t[:, :Ts].reshape(Bs * Ts, nsa_dim) for t in outs]
            hs = _even_out(hs, y, g, o_c, o_s, o_w, gates, even_w_out[i])
            nsa_s.append(kv6[:, :, :4])
            win_s.append(win[:, Ts:])
            wkv_s.append(wkv_new)
            sh_s.append(rw3[:, -1])
        else:
            w_in = odd_w_in[i].astype(BF16)
            weights = [w_in[:, :moba_dim], w_in[:, moba_dim:]]
            q, kv = _project(hp, norm_mix[layer], cos_p, sin_p, weights, [], odd_spec, "odd_proj")
            kv2 = kv.reshape(B, T, 2, MOBA_KV_HEADS, HEAD_DIM)
            a = _moba_attend(q.reshape(B, T, moba_dim), kv2, tq_p, 0)
            hp = _odd_out(hp, a.reshape(B * T, moba_dim), odd_w_out[i])
            moba_p.append(kv2)

            q, kv = _project(hs, norm_mix[layer], cos_s, sin_s, weights, [], odd_spec, "odd_proj_s")
            kv2 = kv.reshape(Bs, Ts, 2, MOBA_KV_HEADS, HEAD_DIM)
            past = _gather_pages(cache_moba_kv[i].reshape(-1, page, moba_kv_cols), page_table)
            kv_all = jnp.concatenate([past.reshape(Bs, past_len, 2, MOBA_KV_HEADS, HEAD_DIM), kv2], axis=1)
            a = _moba_attend(_pad_queries(q.reshape(Bs, Ts, moba_dim), tq_s), kv_all, tq_s, past_len)
            hs = _odd_out(hs, a[:, :Ts].reshape(Bs * Ts, moba_dim), odd_w_out[i])
            moba_s.append(kv2)
        g_final = norm_final if layer == depth - 1 else None
        hp = _ffn(hp, norm_ffn[layer], ffn_w_gate[layer], ffn_w_up[layer], ffn_w_down[layer], g_final)
        hs = _ffn(hs, norm_ffn[layer], ffn_w_gate[layer], ffn_w_up[layer], ffn_w_down[layer], g_final)
    return (hp.reshape(B, T, D), hs.reshape(Bs, Ts, D), jnp.stack(nsa_p), jnp.stack(nsa_s), jnp.stack(moba_p),
            jnp.stack(moba_s), jnp.stack(win_p), jnp.stack(win_s), jnp.stack(wkv_p), jnp.stack(wkv_s),
            jnp.stack(sh_p), jnp.stack(sh_s))
```

```python
import functools
import math

import jax
import jax.numpy as jnp
from jax import lax
from jax.experimental import pallas as pl
from jax.experimental.pallas import tpu as pltpu

F32 = jnp.float32
BF16 = jnp.bfloat16
HIGHEST = lax.Precision.HIGHEST

HEAD_DIM = 64
NORM_EPS = 1e-6
ROPE_THETA = 10000.0
DECAY_LORA = 64
AAA_LORA = 64
GATE_LORA = 128
RWKV_GN_EPS = 64e-5
NSA_KV_HEADS = 2
CMP_STRIDE = 16
CMP_LEN = 2 * CMP_STRIDE
SEL_BLOCK = 64
SEL_TOPN = 16
WINDOW = 512
FORCE_BONUS = 100.0
MOBA_KV_HEADS = 4
MOBA_BLOCK = 256
MOBA_TOPK = 3

LANES = 128
SUBLANES = 8
VMEM_LIMIT = 56 * 1024 * 1024

ROW_TILE = 512
FFN_ROW_TILE = 1024
FFN_COL_TILE = 256
ATT_Q_TILE = 128
ATT_K_TILE = 512
SCAN_CHUNK = 64
SAMPLE_Q_PAD = 32
PAGES_PER_STEP = 8

NT = (((1,), (1,)), ((), ()))
TN = (((0,), (0,)), ((), ()))
NEG_INF = float("-inf")
NEG_BIG = -1e30
V_ROWS = HEAD_DIM + 16


def _round_up(x, m):
    return -(-x // m) * m


def _mm(a, b, dims=None, exact=False):
    if dims is None:
        dims = (((a.ndim - 1,), (0,)), ((), ()))
    if exact:
        return lax.dot_general(a.astype(F32), b.astype(F32), dims, precision=HIGHEST,
                               preferred_element_type=F32)
    return lax.dot_general(a.astype(BF16), b.astype(BF16), dims, preferred_element_type=F32)


def _params(*sem):
    return pltpu.CompilerParams(dimension_semantics=sem, vmem_limit_bytes=VMEM_LIMIT)


def _rms_norm(x, g):
    return x * lax.rsqrt(jnp.mean(x * x, axis=-1, keepdims=True) + NORM_EPS) * g


def _rope_chunk(x, cos, sin):
    lane = lax.broadcasted_iota(jnp.int32, x.shape, 1)
    half = HEAD_DIM // 2
    partner = jnp.where((lane % HEAD_DIM) < half, pltpu.roll(x, LANES - half, 1), pltpu.roll(x, half, 1))
    return x * cos + partner * sin


def _rank_rows(score, blk, ncand):
    rank = jnp.zeros(score.shape, F32)
    for j in range(ncand):
        row = score[j:j + 1, :]
        rank = rank + jnp.where(row > score, 1.0, 0.0) + jnp.where(row == score, jnp.where(blk > j, 1.0, 0.0), 0.0)
    return rank


def _proj_kernel(spec, x_ref, g_ref, cos_ref, sin_ref, *refs):
    nseg = len(spec)
    w_refs = refs[:nseg]
    nbias = sum(1 for s in spec if s[2])
    b_refs = list(refs[nseg:nseg + nbias])
    o_refs = refs[nseg + nbias:]
    xn = _rms_norm(x_ref[...], g_ref[...]).astype(BF16)
    for (ncols, rope_chunks, sig), w_ref, o_ref in zip(spec, w_refs, o_refs):
        y = jnp.dot(xn, w_ref[...], preferred_element_type=F32)
        if sig:
            y = jax.nn.sigmoid(y + b_refs.pop(0)[...])
        if rope_chunks:
            cos = cos_ref[...]
            sin = sin_ref[...]
            for c in range(ncols // LANES):
                yc = y[:, c * LANES:(c + 1) * LANES]
                if c in rope_chunks:
                    yc = _rope_chunk(yc, cos, sin)
                o_ref[:, c * LANES:(c + 1) * LANES] = yc
        else:
            o_ref[...] = y


def _project(x, g, cos_tab, sin_tab, weights, biases, spec, name):
    M, D = x.shape
    tm = min(ROW_TILE, M)
    ntab = cos_tab.shape[0] // tm
    in_specs = [pl.BlockSpec((tm, D), lambda i: (i, 0)),
                pl.BlockSpec((1, D), lambda i: (0, 0)),
                pl.BlockSpec((tm, LANES), lambda i: (i % ntab, 0)),
                pl.BlockSpec((tm, LANES), lambda i: (i % ntab, 0))]
    in_specs += [pl.BlockSpec(w.shape, lambda i: (0, 0)) for w in weights]
    in_specs += [pl.BlockSpec(b.shape, lambda i: (0, 0)) for b in biases]
    out_shape = [jax.ShapeDtypeStruct((M, s[0]), F32) for s in spec]
    out_specs = [pl.BlockSpec((tm, s[0]), lambda i: (i, 0)) for s in spec]
    return pl.pallas_call(
        functools.partial(_proj_kernel, spec), grid=(M // tm,), in_specs=in_specs, out_specs=out_specs,
        out_shape=out_shape, compiler_params=_params("parallel"), name=name,
    )(x, g.reshape(1, D), cos_tab, sin_tab, *weights, *biases)


def _rope_tables(pos):
    half = HEAD_DIM // 2
    inv = ROPE_THETA ** (-jnp.arange(half, dtype=F32) / half)
    ang = pos.astype(F32)[:, None] * inv[None, :]
    cos = jnp.cos(ang)
    sin = jnp.sin(ang)
    cos_t = jnp.tile(cos, (1, LANES // half))
    sin_t = jnp.tile(jnp.concatenate([-sin, sin], axis=1), (1, LANES // HEAD_DIM))
    return cos_t, sin_t


def _even_out_kernel(res_ref, y_ref, g_ref, oc_ref, os_ref, ow_ref, gt_ref, ec_ref, es_ref, ew_ref,
                     w1_ref, w2_ref, o_ref):
    a = y_ref[...] * g_ref[...]
    gt = gt_ref[...]
    b = (_mm(gt, ec_ref[...], exact=True) * oc_ref[...] + _mm(gt, es_ref[...], exact=True) * os_ref[...]
         + _mm(gt, ew_ref[...], exact=True) * ow_ref[...])
    o_ref[...] = res_ref[...] + _mm(a, w1_ref[...]) + _mm(b, w2_ref[...])


def _even_out(res, y, g, o_c, o_s, o_w, gates, w_out):
    M, D = res.shape
    n_rw = y.shape[1]
    n_nsa = o_c.shape[1]
    tm = min(ROW_TILE, M)
    heads = n_nsa // HEAD_DIM
    col = jnp.arange(n_nsa)[None, :] // HEAD_DIM
    row = jnp.arange(LANES)[:, None]
    expand = [(row == col * 3 + br).astype(F32) for br in range(3)]
    del heads
    row_spec = lambda n: pl.BlockSpec((tm, n), lambda i: (i, 0))
    full = lambda a: pl.BlockSpec(a.shape, lambda i: (0, 0))
    w1 = w_out[:n_rw].astype(BF16)
    w2 = w_out[n_rw:].astype(BF16)
    return pl.pallas_call(
        _even_out_kernel, grid=(M // tm,),
        in_specs=[row_spec(D), row_spec(n_rw), row_spec(n_rw), row_spec(n_nsa), row_spec(n_nsa), row_spec(n_nsa),
                  row_spec(LANES), full(expand[0]), full(expand[1]), full(expand[2]), full(w1), full(w2)],
        out_specs=row_spec(D), out_shape=jax.ShapeDtypeStruct((M, D), F32),
        compiler_params=_params("parallel"), name="even_out",
    )(res, y, g, o_c, o_s, o_w, gates, *expand, w1, w2)


def _odd_out_kernel(res_ref, a_ref, w_ref, o_ref):
    o_ref[...] = res_ref[...] + _mm(a_ref[...], w_ref[...])


def _odd_out(res, a, w_out):
    M, D = res.shape
    tm = min(ROW_TILE, M)
    w = w_out.astype(BF16)
    return pl.pallas_call(
        _odd_out_kernel, grid=(M // tm,),
        in_specs=[pl.BlockSpec((tm, D), lambda i: (i, 0)), pl.BlockSpec((tm, a.shape[1]), lambda i: (i, 0)),
                  pl.BlockSpec(w.shape, lambda i: (0, 0))],
        out_specs=pl.BlockSpec((tm, D), lambda i: (i, 0)), out_shape=jax.ShapeDtypeStruct((M, D), F32),
        compiler_params=_params("parallel"), name="odd_out",
    )(res, a, w)


def _ffn_kernel(final_norm, x_ref, g_ref, wg_ref, wu_ref, wd_ref, gf_ref, o_ref, xn_scr, acc_scr):
    j = pl.program_id(1)

    @pl.when(j == 0)
    def _():
        xn_scr[...] = _rms_norm(x_ref[...], g_ref[...]).astype(BF16)
        acc_scr[...] = jnp.zeros(acc_scr.shape, F32)

    xn = xn_scr[...]
    h = jax.nn.silu(jnp.dot(xn, wg_ref[...], preferred_element_type=F32)) * jnp.dot(
        xn, wu_ref[...], preferred_element_type=F32)
    acc_scr[...] += _mm(h, wd_ref[...])

    @pl.when(j == pl.num_programs(1) - 1)
    def _():
        y = x_ref[...] + acc_scr[...]
        if final_norm:
            y = _rms_norm(y, gf_ref[...])
        o_ref[...] = y


def _ffn(x, g, wg, wu, wd, g_final=None):
    M, D = x.shape
    F = wg.shape[1]
    tm = min(FFN_ROW_TILE, M)
    tf = FFN_COL_TILE
    final_norm = g_final is not None
    gf = (g_final if final_norm else g).reshape(1, D)
    return pl.pallas_call(
        functools.partial(_ffn_kernel, final_norm), grid=(M // tm, F // tf),
        in_specs=[pl.BlockSpec((tm, D), lambda i, j: (i, 0)), pl.BlockSpec((1, D), lambda i, j: (0, 0)),
                  pl.BlockSpec((D, tf), lambda i, j: (0, j)), pl.BlockSpec((D, tf), lambda i, j: (0, j)),
                  pl.BlockSpec((tf, D), lambda i, j: (j, 0)), pl.BlockSpec((1, D), lambda i, j: (0, 0))],
        out_specs=pl.BlockSpec((tm, D), lambda i, j: (i, 0)), out_shape=jax.ShapeDtypeStruct((M, D), F32),
        scratch_shapes=[pltpu.VMEM((tm, D), BF16), pltpu.VMEM((tm, D), F32)],
        compiler_params=_params("parallel", "arbitrary"), name="ffn",
    )(x, g.reshape(1, D), wg.astype(BF16), wu.astype(BF16), wd.astype(BF16), gf)


def _rwkv_pre_kernel(n_dim, rw_ref, prev_ref, mu_ref, w0_ref, a0_ref, kk_ref, ka_ref, wup_ref, aup_ref, gup_ref,
                     hsum_ref, r_out, lw_out, k_out, v_out, kk_out, b_out, g_out):
    rw = rw_ref[...]
    xm = rw + (prev_ref[...] - rw) * mu_ref[...]
    r = xm[:, :n_dim]
    k = xm[:, n_dim:2 * n_dim]
    v = xm[:, 2 * n_dim:3 * n_dim]
    lora = xm[:, 3 * n_dim:3 * n_dim + DECAY_LORA + AAA_LORA]
    xg = xm[:, 3 * n_dim + DECAY_LORA + AAA_LORA:]
    lw = -math.exp(-0.5) * jax.nn.sigmoid(w0_ref[...] + _mm(jnp.tanh(lora), wup_ref[...], exact=True))
    a = jax.nn.sigmoid(a0_ref[...] + _mm(lora, aup_ref[...], exact=True))
    g = _mm(jax.nn.sigmoid(xg), gup_ref[...])
    kk = k * kk_ref[...]
    norm = jnp.sqrt(_mm(kk * kk, hsum_ref[...], exact=True))
    kk = kk / jnp.maximum(norm, 1e-12)
    r_out[...] = r
    lw_out[...] = lw
    k_out[...] = k * (1.0 + (a - 1.0) * ka_ref[...])
    v_out[...] = v
    kk_out[...] = kk
    b_out[...] = kk * a
    g_out[...] = g


def _rwkv_pre(rw, prev, mu, w0, w_up, a0, a_up, g_up, k_k, k_a):
    M, ncols = rw.shape
    n_dim = w0.shape[0]
    tm = min(ROW_TILE, M)
    zeros = jnp.zeros((AAA_LORA, n_dim), F32)
    wup_pad = jnp.concatenate([w_up, zeros], axis=0)
    aup_pad = jnp.concatenate([jnp.zeros((DECAY_LORA, n_dim), F32), a_up], axis=0)
    head = jnp.arange(n_dim) // HEAD_DIM
    hsum = (head[:, None] == head[None, :]).astype(F32)
    vec = lambda a: a.reshape(1, -1)
    row = lambda n: pl.BlockSpec((tm, n), lambda i: (i, 0))
    full = lambda a: pl.BlockSpec(a.shape, lambda i: (0, 0))
    ins = [rw, prev, vec(mu), vec(w0), vec(a0), vec(k_k), vec(k_a), wup_pad, aup_pad, g_up, hsum]
    return pl.pallas_call(
        functools.partial(_rwkv_pre_kernel, n_dim), grid=(M // tm,),
        in_specs=[row(ncols), row(ncols)] + [full(a) for a in ins[2:]],
        out_specs=[row(n_dim)] * 7, out_shape=[jax.ShapeDtypeStruct((M, n_dim), F32)] * 7,
        compiler_params=_params("parallel"), name="rwkv_pre",
    )(*ins)


def _rwkv_scan_kernel(n_heads, chunk, r_ref, lw_ref, k_ref, v_ref, kk_ref, b_ref, s0_ref, rk_ref, lng_ref, lnb_ref,
                      y_ref, s_out_ref, s_scr):
    c = pl.program_id(1)

    @pl.when(c == 0)
    def _():
        s_scr[...] = s0_ref[...]

    C = chunk
    ti = lax.broadcasted_iota(jnp.int32, (C, C), 0)
    si = lax.broadcasted_iota(jnp.int32, (C, C), 1)
    incl = jnp.where(si <= ti, 1.0, 0.0)
    n_double = max(1, math.ceil(math.log2(C)))
    for h in range(n_heads):
        r = r_ref[h]
        lw = lw_ref[h]
        k = k_ref[h]
        v = v_ref[h]
        kk = kk_ref[h]
        b = b_ref[h]
        s0 = s_scr[h]
        cum = _mm(incl, lw, exact=True)
        cum_end = cum[C - 1:C, :]
        kkw = kk * jnp.exp(cum - lw)
        rwc = r * jnp.exp(cum)
        w_inv = jnp.exp(-cum)
        kd = k * w_inv
        bd = b * w_inv
        w_end = jnp.exp(cum_end - cum)
        a_ub = jnp.where(si < ti, _mm(kkw, bd, NT, exact=True), 0.0)
        a_vk = jnp.where(si < ti, _mm(kkw, kd, NT, exact=True), 0.0)
        b_rb = jnp.where(si <= ti, _mm(rwc, bd, NT, exact=True), 0.0)
        b_rk = jnp.where(si <= ti, _mm(rwc, kd, NT, exact=True), 0.0)
        u = _mm(kkw, s0, NT, exact=True) + _mm(a_vk, v, exact=True)
        p = -a_ub
        u = u + _mm(p, u, exact=True)
        for _ in range(n_double - 1):
            p = _mm(p, p, exact=True)
            u = u + _mm(p, u, exact=True)
        y = _mm(rwc, s0, NT, exact=True) + _mm(b_rk, v, exact=True) - _mm(b_rb, u, exact=True)
        s_new = (s0 * jnp.exp(cum_end) + _mm(v, k * w_end, TN, exact=True) - _mm(u, b * w_end, TN, exact=True))
        s_scr[h] = s_new
        mean = jnp.mean(y, axis=-1, keepdims=True)
        var = jnp.mean(jnp.square(y - mean), axis=-1, keepdims=True)
        yn = (y - mean) * lax.rsqrt(var + RWKV_GN_EPS) * lng_ref[h:h + 1, :] + lnb_ref[h:h + 1, :]
        bonus = jnp.sum(r * k * rk_ref[h:h + 1, :], axis=-1, keepdims=True) * v
        y_ref[h] = yn + bonus

    @pl.when(c == pl.num_programs(1) - 1)
    def _():
        s_out_ref[...] = s_scr[...]


def _rwkv_scan(r, lw, k, v, kk, b, s0, r_k, ln_g, ln_b, chunk):
    n_bh, T, dh = r.shape
    H = r_k.shape[0]
    seq = pl.BlockSpec((H, chunk, dh), lambda i, c: (i, c, 0))
    state = pl.BlockSpec((H, dh, dh), lambda i, c: (i, 0, 0))
    par = pl.BlockSpec((H, dh), lambda i, c: (0, 0))
    return pl.pallas_call(
        functools.partial(_rwkv_scan_kernel, H, chunk), grid=(n_bh // H, T // chunk),
        in_specs=[seq] * 6 + [state, par, par, par], out_specs=[seq, state],
        out_shape=[jax.ShapeDtypeStruct((n_bh, T, dh), F32), jax.ShapeDtypeStruct((n_bh, dh, dh), F32)],
        scratch_shapes=[pltpu.VMEM((H, dh, dh), F32)],
        compiler_params=_params("parallel", "arbitrary"), name="rwkv_scan",
    )(r, lw, k, v, kk, b, s0, r_k, ln_g.reshape(H, dh), ln_b.reshape(H, dh))


def _rwkv_mix(rw, shift_prev, wkv0, p, chunk):
    mu, w0, w_up, a0, a_up, g_up, k_k, k_a, r_k, ln_g, ln_b = p
    B, T, ncols = rw.shape
    H = r_k.shape[0]
    n_dim = H * HEAD_DIM
    prev = jnp.concatenate([shift_prev[:, None], rw[:, :-1]], axis=1)
    outs = _rwkv_pre(rw.reshape(B * T, ncols), prev.reshape(B * T, ncols), mu, w0, w_up, a0, a_up, g_up, k_k, k_a)
    g = outs[6]
    Tp = _round_up(T, chunk)

    def heads(t):
        t = t.reshape(B, T, H, HEAD_DIM).transpose(0, 2, 1, 3)
        t = jnp.pad(t, ((0, 0), (0, 0), (0, Tp - T), (0, 0)))
        return t.reshape(B * H, Tp, HEAD_DIM)

    y, s_new = _rwkv_scan(*[heads(t) for t in outs[:6]], wkv0.reshape(B * H, HEAD_DIM, HEAD_DIM), r_k, ln_g, ln_b,
                          chunk)
    y = y.reshape(B, H, Tp, HEAD_DIM)[:, :, :T].transpose(0, 2, 1, 3).reshape(B * T, n_dim)
    return y, g, s_new.reshape(B, H, HEAD_DIM, HEAD_DIM)


def _compress_kernel(x_ref, w1_ref, pe_ref, w2_ref, o_ref):
    w1 = w1_ref[...]
    bias = _mm(pe_ref[...], w1)[0:1, :]
    h = jax.nn.gelu(_mm(x_ref[...], w1) + bias)
    o_ref[...] = _mm(h, w2_ref[...])


def _nsa_compress(kv_cmp, w1, pe, w2):
    _, BG, L, dh = kv_cmp.shape
    nch = L // CMP_STRIDE
    x = kv_cmp[:, :, :nch * CMP_STRIDE].reshape(2, BG, nch, CMP_STRIDE * dh)
    xw = jnp.concatenate([x[:, :, :-1], x[:, :, 1:]], axis=-1)
    ncp = _round_up(nch - 1, LANES)
    xw = jnp.pad(xw, ((0, 0), (0, 0), (0, ncp - (nch - 1)), (0, 0)))
    kdim = CMP_LEN * dh
    pe8 = jnp.pad(pe.reshape(2, 1, kdim), ((0, 0), (0, SUBLANES - 1), (0, 0)))
    return pl.pallas_call(
        _compress_kernel, grid=(2, BG),
        in_specs=[pl.BlockSpec((None, None, ncp, kdim), lambda s, i: (s, i, 0, 0)),
                  pl.BlockSpec((None, kdim, dh), lambda s, i: (s, 0, 0)),
                  pl.BlockSpec((None, SUBLANES, kdim), lambda s, i: (s, 0, 0)),
                  pl.BlockSpec((None, dh, dh), lambda s, i: (s, 0, 0))],
        out_specs=pl.BlockSpec((None, None, ncp, dh), lambda s, i: (s, i, 0, 0)),
        out_shape=jax.ShapeDtypeStruct((2, BG, ncp, dh), F32),
        compiler_params=_params("parallel", "parallel"), name="nsa_compress",
    )(xw, w1.reshape(2, kdim, dh), pe8, w2)


def _cmp_select_kernel(tq, n_rep, q_base, n_sel, q_ref, kc_ref, vct_ref, ovt_ref, oc_ref, bias_ref):
    qi = pl.program_id(1)
    rows = n_rep * tq
    ncp = kc_ref.shape[0]
    nr = ovt_ref.shape[0]
    nbp = bias_ref.shape[0]
    q = q_ref[...] * HEAD_DIM ** -0.5
    s = _mm(kc_ref[...], q)
    t_row = q_base + qi * tq + lax.broadcasted_iota(jnp.int32, (1, rows), 1) % tq
    c_end = lax.broadcasted_iota(jnp.int32, (ncp, 1), 0) * CMP_STRIDE + (CMP_LEN - 1)
    s = jnp.where(c_end <= t_row, s, NEG_INF)
    m = jnp.max(s, axis=0, keepdims=True)
    e = jnp.exp(s - jnp.where(m == NEG_INF, 0.0, m))
    p = e / jnp.maximum(jnp.sum(e, axis=0, keepdims=True), 1e-30)
    oc_ref[...] = _mm(vct_ref[...], p)
    p_sum = p[:, 0:tq]
    for r in range(1, n_rep):
        p_sum = p_sum + p[:, r * tq:(r + 1) * tq]
    imp = _mm(ovt_ref[...], p_sum, exact=True)
    cur = (q_base + qi * tq + lax.broadcasted_iota(jnp.int32, (1, tq), 1)) // SEL_BLOCK
    blk = lax.broadcasted_iota(jnp.int32, (nr, 1), 0)
    forced = jnp.where(blk == cur, 1.0, jnp.where(blk == cur - 1, 1.0, jnp.where(blk == 0, 1.0, 0.0)))
    score = jnp.where(blk <= cur, imp + FORCE_BONUS * forced, NEG_INF)
    rank = _rank_rows(score, blk, n_sel)
    bias = jnp.where(blk <= cur, jnp.where(rank < min(SEL_TOPN, n_sel), 0.0, NEG_BIG), NEG_BIG)
    if nbp > nr:
        bias = jnp.concatenate([bias, jnp.zeros((nbp - nr, tq), F32)], axis=0)
    bias_ref[...] = bias.astype(BF16)


def _cmp_select(qt, kc, vct, ovt, tq, q_base, n_sel, nbp):
    BG, nq, dh, rows = qt.shape
    ncp = kc.shape[1]
    n_rep = rows // tq
    qspec = pl.BlockSpec((None, None, dh, rows), lambda i, j: (i, j, 0, 0))
    return pl.pallas_call(
        functools.partial(_cmp_select_kernel, tq, n_rep, q_base, n_sel), grid=(BG, nq),
        in_specs=[qspec, pl.BlockSpec((None, ncp, dh), lambda i, j: (i, 0, 0)),
                  pl.BlockSpec((None, dh, ncp), lambda i, j: (i, 0, 0)),
                  pl.BlockSpec(ovt.shape, lambda i, j: (0, 0))],
        out_specs=[qspec, pl.BlockSpec((None, None, nbp, tq), lambda i, j: (i, j, 0, 0))],
        out_shape=[jax.ShapeDtypeStruct((BG, nq, dh, rows), F32), jax.ShapeDtypeStruct((BG, nq, nbp, tq), BF16)],
        compiler_params=_params("parallel", "parallel"), name="nsa_cmp_select",
    )(qt, kc, vct, ovt)


def _overlap_matrix_t(nc, nsel, ncp, nr):
    i = jnp.arange(nc, dtype=jnp.int32)[None, :]
    j = jnp.arange(nsel, dtype=jnp.int32)[:, None]
    lo = jnp.maximum(i * CMP_STRIDE, j * SEL_BLOCK)
    hi = jnp.minimum(i * CMP_STRIDE + CMP_LEN, (j + 1) * SEL_BLOCK)
    ov = jnp.clip(hi - lo, 0).astype(F32) / CMP_LEN
    return jnp.pad(ov, ((0, nr - nsel), (0, ncp - nc)))


def _means_kernel(n_blk, k_ref, o_ref):
    o_ref[...] = jnp.zeros(o_ref.shape, F32)
    for n in range(n_blk):
        o_ref[n:n + 1, :] = jnp.sum(k_ref[n * MOBA_BLOCK:(n + 1) * MOBA_BLOCK, :], axis=0,
                                    keepdims=True) / MOBA_BLOCK


def _block_means(k, n_blk, nbp):
    BG, Lp, dh = k.shape
    return pl.pallas_call(
        functools.partial(_means_kernel, n_blk), grid=(BG,),
        in_specs=[pl.BlockSpec((None, Lp, dh), lambda i: (i, 0, 0))],
        out_specs=pl.BlockSpec((None, nbp, dh), lambda i: (i, 0, 0)),
        out_shape=jax.ShapeDtypeStruct((BG, nbp, dh), F32), compiler_params=_params("parallel"), name="moba_means",
    )(k)


def _flash_kernel(mode, tq, n_rep, tk, q_base, k_base, n_blk, *refs):
    if mode == "sel":
        q_ref, ke_ref, vt_ref, bias_ref, o_ref, lhs_scr, m_scr, acc_scr = refs
    elif mode == "win":
        q_ref, ke_ref, vt_ref, o_ref, lhs_scr, m_scr, acc_scr = refs
    else:
        q_ref, ke_ref, vt_ref, mean_ref, o_ref, lhs_scr, m_scr, acc_scr = refs
    qi = pl.program_id(1)
    rows = n_rep * tq
    q0 = q_base + qi * tq
    t_row = q0 + lax.broadcasted_iota(jnp.int32, (1, rows), 1) % tq
    q = q_ref[...] * HEAD_DIM ** -0.5
    lhs_scr[0:HEAD_DIM, :] = q.astype(BF16)
    if mode == "sel":
        lhs_scr[HEAD_DIM:, :] = jnp.concatenate([bias_ref[...]] * n_rep, axis=1)
    elif mode == "moba":
        nr = _round_up(n_blk, SUBLANES)
        nbp = mean_ref.shape[0]
        gate = _mm(mean_ref[0:nr, :], q, exact=True)
        blk = lax.broadcasted_iota(jnp.int32, (nr, 1), 0)
        cur = t_row // MOBA_BLOCK
        gate = jnp.where(blk < cur, gate, NEG_INF)
        rank = _rank_rows(gate, blk, n_blk)
        picked = jnp.where(blk < cur, jnp.where(rank < min(MOBA_TOPK, n_blk), 0.0, NEG_BIG), NEG_BIG)
        bias = jnp.where(blk == cur, 0.0, picked)
        if nbp > nr:
            bias = jnp.concatenate([bias, jnp.zeros((nbp - nr, rows), F32)], axis=0)
        lhs_scr[HEAD_DIM:, :] = bias.astype(BF16)
    m_scr[...] = jnp.full(m_scr.shape, NEG_BIG, F32)
    acc_scr[...] = jnp.zeros(acc_scr.shape, F32)

    def tile(ki, positional):
        k0 = pl.multiple_of(ki * tk, tk)
        s = jnp.dot(ke_ref[pl.ds(k0, tk), :], lhs_scr[...], preferred_element_type=F32)
        if positional:
            kpos = k_base + k0 + lax.broadcasted_iota(jnp.int32, (tk, 1), 0)
            s = jnp.where(kpos <= t_row, s, NEG_BIG)
            if mode == "win":
                s = jnp.where(kpos > t_row - WINDOW, s, NEG_BIG)
        m_prev = m_scr[...]
        m_new = jnp.maximum(m_prev, jnp.max(s, axis=0, keepdims=True))
        p = jnp.exp(s - m_new)
        acc_scr[...] = jnp.exp(m_prev - m_new) * acc_scr[...] + jnp.dot(
            vt_ref[:, pl.ds(k0, tk)], p.astype(BF16), preferred_element_type=F32)
        m_scr[...] = m_new

    def body(positional):
        def f(ki, carry):
            tile(ki, positional)
            return carry
        return f

    last = (q0 + tq - 1 - k_base) // tk
    if mode == "win":
        lax.fori_loop(jnp.maximum(q0 - (WINDOW - 1) - k_base, 0) // tk, last + 1, body(True), 0)
    else:
        lax.fori_loop(0, last, body(False), 0)
        tile(last, True)
    acc = acc_scr[...]
    o_ref[...] = acc[0:HEAD_DIM] / jnp.maximum(acc[HEAD_DIM:HEAD_DIM + 1], 1e-30)


def _flash(mode, qt, ke, vt, extra, tq, q_base, k_base, n_blk):
    BG, nq, dh, rows = qt.shape
    Lp, kw = ke.shape[1:]
    n_rep = rows // tq
    tk = ATT_K_TILE
    assert Lp % tk == 0 and (q_base + nq * tq - 1 - k_base) // tk < Lp // tk
    assert tk % tq == 0 and (q_base - k_base) % tq == 0
    qspec = pl.BlockSpec((None, None, dh, rows), lambda i, j: (i, j, 0, 0))
    ins = [qt, ke, vt]
    in_specs = [qspec, pl.BlockSpec((None, Lp, kw), lambda i, j: (i, 0, 0)),
                pl.BlockSpec((None, V_ROWS, Lp), lambda i, j: (i, 0, 0))]
    if mode == "sel":
        ins.append(extra)
        in_specs.append(pl.BlockSpec((None, None, kw - dh, tq), lambda i, j: (i, j, 0, 0)))
    if mode == "moba":
        ins.append(extra)
        in_specs.append(pl.BlockSpec((None, kw - dh, dh), lambda i, j: (i, 0, 0)))
    scratch = [pltpu.VMEM((kw, rows), BF16), pltpu.VMEM((1, rows), F32), pltpu.VMEM((V_ROWS, rows), F32)]
    return pl.pallas_call(
        functools.partial(_flash_kernel, mode, tq, n_rep, tk, q_base, k_base, n_blk), grid=(BG, nq),
        in_specs=in_specs, out_specs=qspec, out_shape=jax.ShapeDtypeStruct((BG, nq, dh, rows), F32),
        scratch_shapes=scratch, compiler_params=_params("parallel", "parallel"), name="flash_" + mode,
    )(*ins)


def _group_queries_t(q, n_groups, tq):
    B, T, n = q.shape
    R = n // HEAD_DIM // n_groups
    q = q.reshape(B, T // tq, tq, n_groups, R, HEAD_DIM).transpose(0, 3, 1, 5, 4, 2)
    return q.reshape(B * n_groups, T // tq, HEAD_DIM, R * tq)


def _ungroup_t(o, B, n_groups, tq):
    BG, nq, dh, rows = o.shape
    R = rows // tq
    o = o.reshape(B, n_groups, nq, dh, R, tq).transpose(0, 2, 5, 1, 4, 3)
    return o.reshape(B, nq * tq, n_groups * R * dh)


def _head_major(x, Lp):
    B, L, G, dh = x.shape
    x = jnp.pad(x.transpose(0, 2, 1, 3), ((0, 0), (0, 0), (0, Lp - L), (0, 0)))
    return x.reshape(B * G, Lp, dh)


def _key_operand(x, Lp, block=None, nbp=0):
    k = _head_major(x, Lp).astype(BF16)
    if block is None:
        return k
    onehot = (jnp.arange(Lp)[:, None] // block == jnp.arange(nbp)[None, :]).astype(BF16)
    return jnp.concatenate([k, jnp.broadcast_to(onehot[None], (k.shape[0], Lp, nbp))], axis=-1)


def _value_operand(x, Lp):
    B, L, G, dh = x.shape
    v = jnp.pad(x.transpose(0, 2, 3, 1), ((0, 0), (0, 0), (0, 0), (0, Lp - L))).reshape(B * G, dh, Lp)
    return jnp.concatenate([v, jnp.ones((B * G, V_ROWS - dh, Lp), F32)], axis=1).astype(BF16)


def _bias_rows(n):
    return _round_up(HEAD_DIM + n, LANES) - HEAD_DIM


def _nsa_attend(q, kv4, win_kv, cw, tq, q_base, win_base):
    B, Tq, _ = q.shape
    L = kv4.shape[1]
    G = NSA_KV_HEADS
    w1, pe, w2 = cw
    qt = _group_queries_t(q, G, tq)
    cmp_rows = jnp.stack([kv4[:, :, 0], kv4[:, :, 1]]).transpose(0, 1, 3, 2, 4).reshape(2, B * G, L, HEAD_DIM)
    kvc = _nsa_compress(cmp_rows, w1, pe, w2)
    nc = L // CMP_STRIDE - 1
    l_max = max(L, q_base + Tq)
    n_sel = -(-l_max // SEL_BLOCK)
    nbp = _bias_rows(n_sel)
    ovt = _overlap_matrix_t(nc, n_sel, kvc.shape[2], _round_up(n_sel, SUBLANES))
    oc, bias = _cmp_select(qt, kvc[0], kvc[1].transpose(0, 2, 1), ovt, tq, q_base, n_sel, nbp)
    Lp = _round_up(l_max, ATT_K_TILE)
    o_s = _flash("sel", qt, _key_operand(kv4[:, :, 2], Lp, SEL_BLOCK, nbp), _value_operand(kv4[:, :, 3], Lp), bias,
                 tq, q_base, 0, n_sel)
    Lwp = _round_up(max(win_kv.shape[1], q_base + Tq - win_base), ATT_K_TILE)
    o_w = _flash("win", qt, _key_operand(win_kv[:, :, 0], Lwp), _value_operand(win_kv[:, :, 1], Lwp), None, tq,
                 q_base, win_base, 0)
    return tuple(_ungroup_t(o, B, G, tq) for o in (oc, o_s, o_w))


def _moba_attend(q, kv, tq, q_base):
    B, Tq, _ = q.shape
    L = kv.shape[1]
    G = MOBA_KV_HEADS
    l_max = max(L, q_base + Tq)
    n_blk = -(-l_max // MOBA_BLOCK)
    nbp = _bias_rows(n_blk)
    Lp = _round_up(l_max, ATT_K_TILE)
    qt = _group_queries_t(q, G, tq)
    means = _block_means(_head_major(kv[:, :, 0], Lp), n_blk, nbp)
    o = _flash("moba", qt, _key_operand(kv[:, :, 0], Lp, MOBA_BLOCK, nbp), _value_operand(kv[:, :, 1], Lp), means,
               tq, q_base, 0, n_blk)
    return _ungroup_t(o, B, G, tq)


def _gather_kernel(pt_ref, *refs):
    del pt_ref
    o_ref = refs[-1]
    rows = refs[0].shape[0]
    for j, src in enumerate(refs[:-1]):
        o_ref[j * rows:(j + 1) * rows, :] = src[...]


def _gather_pages(cache, page_table):
    _, page, W = cache.shape
    Bs, n_pages = page_table.shape
    pps = math.gcd(PAGES_PER_STEP, n_pages)

    def in_map(j):
        return lambda b, p, pt: (pt[b, p * pps + j], 0, 0)

    grid_spec = pltpu.PrefetchScalarGridSpec(
        num_scalar_prefetch=1, grid=(Bs, n_pages // pps),
        in_specs=[pl.BlockSpec((None, page, W), in_map(j)) for j in range(pps)],
        out_specs=pl.BlockSpec((None, pps * page, W), lambda b, p, pt: (b, p, 0)))
    return pl.pallas_call(
        _gather_kernel, grid_spec=grid_spec, out_shape=jax.ShapeDtypeStruct((Bs, n_pages * page, W), F32),
        compiler_params=_params("parallel", "parallel"), name="gather_pages",
    )(page_table, *([cache] * pps))


def _pad_queries(x, tq):
    return jnp.pad(x, ((0, 0), (0, tq - x.shape[1]), (0, 0)))


def kernel(x_prompt, x_sample, cache_nsa_kv, cache_moba_kv, state_win_kv, state_wkv, state_shift, page_table, norm_mix, norm_ffn, norm_final, even_w_in, even_w_out, rwkv_mu, rwkv_w0, rwkv_w_up, rwkv_a0, rwkv_a_up, rwkv_g_up, rwkv_k_k, rwkv_k_a, rwkv_r_k, rwkv_ln_g, rwkv_ln_b, nsa_gate_b, nsa_cmp_w1, nsa_cmp_pe, nsa_cmp_w2, odd_w_in, odd_w_out, ffn_w_gate, ffn_w_up, ffn_w_down):
    B, T, D = x_prompt.shape
    Bs, Ts, _ = x_sample.shape
    depth = norm_mix.shape[0]
    page = cache_nsa_kv.shape[2]
    past_len = page_table.shape[1] * page
    rwkv_dim = rwkv_w0.shape[1]
    rwkv_cols = rwkv_mu.shape[1]
    nsa_heads = nsa_gate_b.shape[1] // 3
    nsa_dim = nsa_heads * HEAD_DIM
    nsa_kv_cols = 6 * NSA_KV_HEADS * HEAD_DIM
    moba_kv_cols = 2 * MOBA_KV_HEADS * HEAD_DIM
    moba_dim = odd_w_in.shape[2] - moba_kv_cols
    tq_p = min(ATT_Q_TILE, T)
    tq_s = SAMPLE_Q_PAD
    n_win = state_win_kv.shape[2]

    cos_p, sin_p = _rope_tables(jnp.arange(T, dtype=jnp.int32))
    pos_s = past_len + jnp.arange(Ts, dtype=jnp.int32)
    cos_s, sin_s = _rope_tables(jnp.tile(pos_s, Bs))

    even_spec = ((rwkv_cols, (), False), (nsa_dim, tuple(range(nsa_dim // LANES)), False),
                 (nsa_kv_cols, tuple(range(0, nsa_kv_cols // LANES, 2)), False), (LANES, (), True))
    k_chunks = MOBA_KV_HEADS * HEAD_DIM // LANES
    odd_spec = ((moba_dim, tuple(range(moba_dim // LANES)), False), (moba_kv_cols, tuple(range(k_chunks)), False))

    hp = x_prompt.reshape(B * T, D)
    hs = x_sample.reshape(Bs * Ts, D)
    nsa_p, nsa_s, moba_p, moba_s = [], [], [], []
    win_p, win_s, wkv_p, wkv_s, sh_p, sh_s = [], [], [], [], [], []
    for layer in range(depth):
        i = layer // 2
        if layer % 2 == 0:
            w_in = even_w_in[i].astype(BF16)
            o = rwkv_cols
            n_gate = 3 * nsa_heads
            weights = [w_in[:, :o], w_in[:, o:o + nsa_dim], w_in[:, o + nsa_dim:o + nsa_dim + nsa_kv_cols],
                       jnp.pad(w_in[:, o + nsa_dim + nsa_kv_cols:], ((0, 0), (0, LANES - n_gate)))]
            gate_b = jnp.pad(nsa_gate_b[i], (0, LANES - n_gate)).reshape(1, LANES)
            rp = (rwkv_mu[i], rwkv_w0[i], rwkv_w_up[i], rwkv_a0[i], rwkv_a_up[i], rwkv_g_up[i],
                  rwkv_k_k[i], rwkv_k_a[i], rwkv_r_k[i], rwkv_ln_g[i], rwkv_ln_b[i])
            cw = (nsa_cmp_w1[i], nsa_cmp_pe[i], nsa_cmp_w2[i])

            rw, q, kv, gates = _project(hp, norm_mix[layer], cos_p, sin_p, weights, [gate_b], even_spec, "even_proj")
            rw3 = rw.reshape(B, T, rwkv_cols)
            kv6 = kv.reshape(B, T, 6, NSA_KV_HEADS, HEAD_DIM)
            y, g, wkv_new = _rwkv_mix(rw3, jnp.zeros((B, rwkv_cols), F32),
                                      jnp.zeros((B, rwkv_dim // HEAD_DIM, HEAD_DIM, HEAD_DIM), F32), rp, SCAN_CHUNK)
            o_c, o_s, o_w = _nsa_attend(q.reshape(B, T, nsa_dim), kv6[:, :, :4], kv6[:, :, 4:6], cw, tq_p, 0, 0)
            hp = _even_out(hp, y, g, o_c.reshape(B * T, nsa_dim), o_s.reshape(B * T, nsa_dim),
                           o_w.reshape(B * T, nsa_dim), gates, even_w_out[i])
            nsa_p.append(kv6[:, :, :4])
            win_p.append(kv6[:, T - min(WINDOW, T):, 4:6])
            wkv_p.append(wkv_new)
            sh_p.append(rw3[:, -1])

            rw, q, kv, gates = _project(hs, norm_mix[layer], cos_s, sin_s, weights, [gate_b], even_spec, "even_proj_s")
            rw3 = rw.reshape(Bs, Ts, rwkv_cols)
            kv6 = kv.reshape(Bs, Ts, 6, NSA_KV_HEADS, HEAD_DIM)
            y, g, wkv_new = _rwkv_mix(rw3, state_shift[i], state_wkv[i], rp, SUBLANES)
            past = _gather_pages(cache_nsa_kv[i].reshape(-1, page, 4 * NSA_KV_HEADS * HEAD_DIM), page_table)
            kv4 = jnp.concatenate([past.reshape(Bs, past_len, 4, NSA_KV_HEADS, HEAD_DIM), kv6[:, :, :4]], axis=1)
            win = jnp.concatenate([state_win_kv[i], kv6[:, :, 4:6]], axis=1)
            qs = _pad_queries(q.reshape(Bs, Ts, nsa_dim), tq_s)
            outs = _nsa_attend(qs, kv4, win, cw, tq_s, past_len, past_len - n_win)
            o_c, o_s, o_w = [t[:, :Ts].reshape(Bs * Ts, nsa_dim) for t in outs]
            hs = _even_out(hs, y, g, o_c, o_s, o_w, gates, even_w_out[i])
            nsa_s.append(kv6[:, :, :4])
            win_s.append(win[:, Ts:])
            wkv_s.append(wkv_new)
            sh_s.append(rw3[:, -1])
        else:
            w_in = odd_w_in[i].astype(BF16)
            weights = [w_in[:, :moba_dim], w_in[:, moba_dim:]]
            q, kv = _project(hp, norm_mix[layer], cos_p, sin_p, weights, [], odd_spec, "odd_proj")
            kv2 = kv.reshape(B, T, 2, MOBA_KV_HEADS, HEAD_DIM)
            a = _moba_attend(q.reshape(B, T, moba_dim), kv2, tq_p, 0)
            hp = _odd_out(hp, a.reshape(B * T, moba_dim), odd_w_out[i])
            moba_p.append(kv2)

            q, kv = _project(hs, norm_mix[layer], cos_s, sin_s, weights, [], odd_spec, "odd_proj_s")
            kv2 = kv.reshape(Bs, Ts, 2, MOBA_KV_HEADS, HEAD_DIM)
            past = _gather_pages(cache_moba_kv[i].reshape(-1, page, moba_kv_cols), page_table)
            kv_all = jnp.concatenate([past.reshape(Bs, past_len, 2, MOBA_KV_HEADS, HEAD_DIM), kv2], axis=1)
            a = _moba_attend(_pad_queries(q.reshape(Bs, Ts, moba_dim), tq_s), kv_all, tq_s, past_len)
            hs = _odd_out(hs, a[:, :Ts].reshape(Bs * Ts, moba_dim), odd_w_out[i])
            moba_s.append(kv2)
        g_final = norm_final if layer == depth - 1 else None
        hp = _ffn(hp, norm_ffn[layer], ffn_w_gate[layer], ffn_w_up[layer], ffn_w_down[layer], g_final)
        hs = _ffn(hs, norm_ffn[layer], ffn_w_gate[layer], ffn_w_up[layer], ffn_w_down[layer], g_final)
    return (hp.reshape(B, T, D), hs.reshape(Bs, Ts, D), jnp.stack(nsa_p), jnp.stack(nsa_s), jnp.stack(moba_p),
            jnp.stack(moba_s), jnp.stack(win_p), jnp.stack(win_s), jnp.stack(wkv_p), jnp.stack(wkv_s),
            jnp.stack(sh_p), jnp.stack(sh_s))
```

```python
import functools
import math

import jax
import jax.numpy as jnp
from jax import lax
from jax.experimental import pallas as pl
from jax.experimental.pallas import tpu as pltpu

F32 = jnp.float32
BF16 = jnp.bfloat16
HIGHEST = lax.Precision.HIGHEST

HEAD_DIM = 64
NORM_EPS = 1e-6
ROPE_THETA = 10000.0
DECAY_LORA = 64
AAA_LORA = 64
GATE_LORA = 128
RWKV_GN_EPS = 64e-5
NSA_KV_HEADS = 2
CMP_STRIDE = 16
CMP_LEN = 2 * CMP_STRIDE
SEL_BLOCK = 64
SEL_TOPN = 16
WINDOW = 512
FORCE_BONUS = 100.0
MOBA_KV_HEADS = 4
MOBA_BLOCK = 256
MOBA_TOPK = 3

LANES = 128
SUBLANES = 8
VMEM_LIMIT = 56 * 1024 * 1024

ROW_TILE = 512
FFN_ROW_TILE = 1024
FFN_COL_TILE = 256
ATT_Q_TILE = 128
ATT_K_TILE = 512
SCAN_CHUNK = 64
SCAN_CHUNKS_PER_STEP = 4
SAMPLE_Q_PAD = 32
PAGES_PER_STEP = 8

NT = (((1,), (1,)), ((), ()))
TN = (((0,), (0,)), ((), ()))
NEG_INF = float("-inf")
NEG_BIG = -1e30
V_ROWS = HEAD_DIM + 16


def _round_up(x, m):
    return -(-x // m) * m


def _mm(a, b, dims=None, exact=False):
    if dims is None:
        dims = (((a.ndim - 1,), (0,)), ((), ()))
    if exact:
        return lax.dot_general(a.astype(F32), b.astype(F32), dims, precision=HIGHEST,
                               preferred_element_type=F32)
    return lax.dot_general(a.astype(BF16), b.astype(BF16), dims, preferred_element_type=F32)


def _params(*sem):
    return pltpu.CompilerParams(dimension_semantics=sem, vmem_limit_bytes=VMEM_LIMIT)


def _rms_norm(x, g):
    return x * lax.rsqrt(jnp.mean(x * x, axis=-1, keepdims=True) + NORM_EPS) * g


def _rope_chunk(x, cos, sin):
    lane = lax.broadcasted_iota(jnp.int32, x.shape, 1)
    half = HEAD_DIM // 2
    partner = jnp.where((lane % HEAD_DIM) < half, pltpu.roll(x, LANES - half, 1), pltpu.roll(x, half, 1))
    return x * cos + partner * sin


def _rank_rows(score, blk, ncand):
    rank = jnp.zeros(score.shape, F32)
    for j in range(ncand):
        row = score[j:j + 1, :]
        rank = rank + jnp.where(row > score, 1.0, 0.0) + jnp.where(row == score, jnp.where(blk > j, 1.0, 0.0), 0.0)
    return rank


def _proj_kernel(spec, x_ref, g_ref, cos_ref, sin_ref, *refs):
    nseg = len(spec)
    w_refs = refs[:nseg]
    nbias = sum(1 for s in spec if s[2])
    b_refs = list(refs[nseg:nseg + nbias])
    o_refs = refs[nseg + nbias:]
    xn = _rms_norm(x_ref[...], g_ref[...]).astype(BF16)
    for (ncols, rope_chunks, sig), w_ref, o_ref in zip(spec, w_refs, o_refs):
        y = jnp.dot(xn, w_ref[...], preferred_element_type=F32)
        if sig:
            y = jax.nn.sigmoid(y + b_refs.pop(0)[...])
        if rope_chunks:
            cos = cos_ref[...]
            sin = sin_ref[...]
            for c in range(ncols // LANES):
                yc = y[:, c * LANES:(c + 1) * LANES]
                if c in rope_chunks:
                    yc = _rope_chunk(yc, cos, sin)
                o_ref[:, c * LANES:(c + 1) * LANES] = yc
        else:
            o_ref[...] = y


def _project(x, g, cos_tab, sin_tab, weights, biases, spec, name):
    M, D = x.shape
    tm = min(ROW_TILE, M)
    ntab = cos_tab.shape[0] // tm
    in_specs = [pl.BlockSpec((tm, D), lambda i: (i, 0)),
                pl.BlockSpec((1, D), lambda i: (0, 0)),
                pl.BlockSpec((tm, LANES), lambda i: (i % ntab, 0)),
                pl.BlockSpec((tm, LANES), lambda i: (i % ntab, 0))]
    in_specs += [pl.BlockSpec(w.shape, lambda i: (0, 0)) for w in weights]
    in_specs += [pl.BlockSpec(b.shape, lambda i: (0, 0)) for b in biases]
    out_shape = [jax.ShapeDtypeStruct((M, s[0]), F32) for s in spec]
    out_specs = [pl.BlockSpec((tm, s[0]), lambda i: (i, 0)) for s in spec]
    return pl.pallas_call(
        functools.partial(_proj_kernel, spec), grid=(M // tm,), in_specs=in_specs, out_specs=out_specs,
        out_shape=out_shape, compiler_params=_params("parallel"), name=name,
    )(x, g.reshape(1, D), cos_tab, sin_tab, *weights, *biases)


def _rope_tables(pos):
    half = HEAD_DIM // 2
    inv = ROPE_THETA ** (-jnp.arange(half, dtype=F32) / half)
    ang = pos.astype(F32)[:, None] * inv[None, :]
    cos = jnp.cos(ang)
    sin = jnp.sin(ang)
    cos_t = jnp.tile(cos, (1, LANES // half))
    sin_t = jnp.tile(jnp.concatenate([-sin, sin], axis=1), (1, LANES // HEAD_DIM))
    return cos_t, sin_t


def _even_out_kernel(res_ref, y_ref, g_ref, oc_ref, os_ref, ow_ref, gt_ref, ec_ref, es_ref, ew_ref,
                     w1_ref, w2_ref, o_ref):
    a = y_ref[...] * g_ref[...]
    gt = gt_ref[...]
    b = (_mm(gt, ec_ref[...], exact=True) * oc_ref[...] + _mm(gt, es_ref[...], exact=True) * os_ref[...]
         + _mm(gt, ew_ref[...], exact=True) * ow_ref[...])
    o_ref[...] = res_ref[...] + _mm(a, w1_ref[...]) + _mm(b, w2_ref[...])


def _even_out(res, y, g, o_c, o_s, o_w, gates, w_out):
    M, D = res.shape
    n_rw = y.shape[1]
    n_nsa = o_c.shape[1]
    tm = min(ROW_TILE, M)
    heads = n_nsa // HEAD_DIM
    col = jnp.arange(n_nsa)[None, :] // HEAD_DIM
    row = jnp.arange(LANES)[:, None]
    expand = [(row == col * 3 + br).astype(F32) for br in range(3)]
    del heads
    row_spec = lambda n: pl.BlockSpec((tm, n), lambda i: (i, 0))
    full = lambda a: pl.BlockSpec(a.shape, lambda i: (0, 0))
    w1 = w_out[:n_rw].astype(BF16)
    w2 = w_out[n_rw:].astype(BF16)
    return pl.pallas_call(
        _even_out_kernel, grid=(M // tm,),
        in_specs=[row_spec(D), row_spec(n_rw), row_spec(n_rw), row_spec(n_nsa), row_spec(n_nsa), row_spec(n_nsa),
                  row_spec(LANES), full(expand[0]), full(expand[1]), full(expand[2]), full(w1), full(w2)],
        out_specs=row_spec(D), out_shape=jax.ShapeDtypeStruct((M, D), F32),
        compiler_params=_params("parallel"), name="even_out",
    )(res, y, g, o_c, o_s, o_w, gates, *expand, w1, w2)


def _odd_out_kernel(res_ref, a_ref, w_ref, o_ref):
    o_ref[...] = res_ref[...] + _mm(a_ref[...], w_ref[...])


def _odd_out(res, a, w_out):
    M, D = res.shape
    tm = min(ROW_TILE, M)
    w = w_out.astype(BF16)
    return pl.pallas_call(
        _odd_out_kernel, grid=(M // tm,),
        in_specs=[pl.BlockSpec((tm, D), lambda i: (i, 0)), pl.BlockSpec((tm, a.shape[1]), lambda i: (i, 0)),
                  pl.BlockSpec(w.shape, lambda i: (0, 0))],
        out_specs=pl.BlockSpec((tm, D), lambda i: (i, 0)), out_shape=jax.ShapeDtypeStruct((M, D), F32),
        compiler_params=_params("parallel"), name="odd_out",
    )(res, a, w)


def _ffn_kernel(final_norm, x_ref, g_ref, wg_ref, wu_ref, wd_ref, gf_ref, o_ref, xn_scr, acc_scr):
    j = pl.program_id(1)

    @pl.when(j == 0)
    def _():
        xn_scr[...] = _rms_norm(x_ref[...], g_ref[...]).astype(BF16)
        acc_scr[...] = jnp.zeros(acc_scr.shape, F32)

    xn = xn_scr[...]
    h = jax.nn.silu(jnp.dot(xn, wg_ref[...], preferred_element_type=F32)) * jnp.dot(
        xn, wu_ref[...], preferred_element_type=F32)
    acc_scr[...] += _mm(h, wd_ref[...])

    @pl.when(j == pl.num_programs(1) - 1)
    def _():
        y = x_ref[...] + acc_scr[...]
        if final_norm:
            y = _rms_norm(y, gf_ref[...])
        o_ref[...] = y


def _ffn(x, g, wg, wu, wd, g_final=None):
    M, D = x.shape
    F = wg.shape[1]
    tm = min(FFN_ROW_TILE, M)
    tf = FFN_COL_TILE
    final_norm = g_final is not None
    gf = (g_final if final_norm else g).reshape(1, D)
    return pl.pallas_call(
        functools.partial(_ffn_kernel, final_norm), grid=(M // tm, F // tf),
        in_specs=[pl.BlockSpec((tm, D), lambda i, j: (i, 0)), pl.BlockSpec((1, D), lambda i, j: (0, 0)),
                  pl.BlockSpec((D, tf), lambda i, j: (0, j)), pl.BlockSpec((D, tf), lambda i, j: (0, j)),
                  pl.BlockSpec((tf, D), lambda i, j: (j, 0)), pl.BlockSpec((1, D), lambda i, j: (0, 0))],
        out_specs=pl.BlockSpec((tm, D), lambda i, j: (i, 0)), out_shape=jax.ShapeDtypeStruct((M, D), F32),
        scratch_shapes=[pltpu.VMEM((tm, D), BF16), pltpu.VMEM((tm, D), F32)],
        compiler_params=_params("parallel", "arbitrary"), name="ffn",
    )(x, g.reshape(1, D), wg.astype(BF16), wu.astype(BF16), wd.astype(BF16), gf)


def _rwkv_pre_kernel(n_dim, rw_ref, prev_ref, mu_ref, w0_ref, a0_ref, kk_ref, ka_ref, wup_ref, aup_ref, gup_ref,
                     hsum_ref, r_out, lw_out, k_out, v_out, kk_out, b_out, g_out):
    rw = rw_ref[...]
    xm = rw + (prev_ref[...] - rw) * mu_ref[...]
    r = xm[:, :n_dim]
    k = xm[:, n_dim:2 * n_dim]
    v = xm[:, 2 * n_dim:3 * n_dim]
    lora = xm[:, 3 * n_dim:3 * n_dim + DECAY_LORA + AAA_LORA]
    xg = xm[:, 3 * n_dim + DECAY_LORA + AAA_LORA:]
    lw = -math.exp(-0.5) * jax.nn.sigmoid(w0_ref[...] + _mm(jnp.tanh(lora), wup_ref[...], exact=True))
    a = jax.nn.sigmoid(a0_ref[...] + _mm(lora, aup_ref[...], exact=True))
    g = _mm(jax.nn.sigmoid(xg), gup_ref[...])
    kk = k * kk_ref[...]
    norm = jnp.sqrt(_mm(kk * kk, hsum_ref[...], exact=True))
    kk = kk / jnp.maximum(norm, 1e-12)
    r_out[...] = r
    lw_out[...] = lw
    k_out[...] = k * (1.0 + (a - 1.0) * ka_ref[...])
    v_out[...] = v
    kk_out[...] = kk
    b_out[...] = kk * a
    g_out[...] = g


def _rwkv_pre(rw, prev, mu, w0, w_up, a0, a_up, g_up, k_k, k_a):
    M, ncols = rw.shape
    n_dim = w0.shape[0]
    tm = min(ROW_TILE, M)
    zeros = jnp.zeros((AAA_LORA, n_dim), F32)
    wup_pad = jnp.concatenate([w_up, zeros], axis=0)
    aup_pad = jnp.concatenate([jnp.zeros((DECAY_LORA, n_dim), F32), a_up], axis=0)
    head = jnp.arange(n_dim) // HEAD_DIM
    hsum = (head[:, None] == head[None, :]).astype(F32)
    vec = lambda a: a.reshape(1, -1)
    row = lambda n: pl.BlockSpec((tm, n), lambda i: (i, 0))
    full = lambda a: pl.BlockSpec(a.shape, lambda i: (0, 0))
    ins = [rw, prev, vec(mu), vec(w0), vec(a0), vec(k_k), vec(k_a), wup_pad, aup_pad, g_up, hsum]
    return pl.pallas_call(
        functools.partial(_rwkv_pre_kernel, n_dim), grid=(M // tm,),
        in_specs=[row(ncols), row(ncols)] + [full(a) for a in ins[2:]],
        out_specs=[row(n_dim)] * 7, out_shape=[jax.ShapeDtypeStruct((M, n_dim), F32)] * 7,
        compiler_params=_params("parallel"), name="rwkv_pre",
    )(*ins)


def _rwkv_chunk_kernel(n_heads, chunk, r_ref, lw_ref, k_ref, v_ref, kk_ref, b_ref, rk_ref,
                       rw_ref, y0_ref, bonus_ref, a_ref, s1_ref):
    C = chunk
    dh = HEAD_DIM
    ti = lax.broadcasted_iota(jnp.int32, (C, C), 0)
    si = lax.broadcasted_iota(jnp.int32, (C, C), 1)
    incl = jnp.where(si <= ti, 1.0, 0.0)
    eye = jnp.where(lax.broadcasted_iota(jnp.int32, (dh, dh), 0) == lax.broadcasted_iota(jnp.int32, (dh, dh), 1),
                    1.0, 0.0)
    n_double = max(1, math.ceil(math.log2(C)))
    heads = range(n_heads)
    r = [r_ref[h] for h in heads]
    lw = [lw_ref[h] for h in heads]
    k = [k_ref[h] for h in heads]
    v = [v_ref[h] for h in heads]
    b = [b_ref[h] for h in heads]
    cum = [_mm(incl, lw[h], exact=True) for h in heads]
    cum_end = [cum[h][C - 1:C, :] for h in heads]
    kkw = [kk_ref[h] * jnp.exp(cum[h] - lw[h]) for h in heads]
    rwc = [r[h] * jnp.exp(cum[h]) for h in heads]
    w_inv = [jnp.exp(-cum[h]) for h in heads]
    kd = [k[h] * w_inv[h] for h in heads]
    bd = [b[h] * w_inv[h] for h in heads]
    w_end = [jnp.exp(cum_end[h] - cum[h]) for h in heads]
    a_ub = [jnp.where(si < ti, _mm(kkw[h], bd[h], NT), 0.0) for h in heads]
    a_vk = [jnp.where(si < ti, _mm(kkw[h], kd[h], NT), 0.0) for h in heads]
    b_rb = [jnp.where(si <= ti, _mm(rwc[h], bd[h], NT), 0.0) for h in heads]
    b_rk = [jnp.where(si <= ti, _mm(rwc[h], kd[h], NT), 0.0) for h in heads]
    x = [jnp.concatenate([kkw[h], _mm(a_vk[h], v[h])], axis=1) for h in heads]
    p = [-a_ub[h] for h in heads]
    x = [x[h] + _mm(p[h], x[h]) for h in heads]
    for _ in range(n_double - 1):
        p = [_mm(p[h], p[h]) for h in heads]
        x = [x[h] + _mm(p[h], x[h]) for h in heads]
    for h in heads:
        bx = _mm(b_rb[h], x[h])
        rw_ref[h] = rwc[h] - bx[:, :dh]
        y0_ref[h] = _mm(b_rk[h], v[h]) - bx[:, dh:]
        kw = x[h][:, :dh]
        uv = x[h][:, dh:]
        a_ref[h] = eye * jnp.exp(cum_end[h]) - _mm(kw, b[h] * w_end[h], TN)
        s1_ref[h] = _mm(v[h], k[h] * w_end[h], TN) - _mm(uv, b[h] * w_end[h], TN)
        bonus_ref[h] = jnp.sum(r[h] * k[h] * rk_ref[h:h + 1, :], axis=-1, keepdims=True) * v[h]


def _rwkv_state_kernel(n_heads, n_sub, rw_ref, y0_ref, bonus_ref, a_ref, s1_ref, s0_ref, lng_ref, lnb_ref,
                       y_ref, s_out_ref, s_scr):
    c = pl.program_id(1)

    @pl.when(c == 0)
    def _():
        s_scr[...] = s0_ref[...]

    for h in range(n_heads):
        s = s_scr[h]
        for j in range(n_sub):
            y = _mm(rw_ref[h, j], s, NT, exact=True) + y0_ref[h, j]
            s = _mm(s, a_ref[h, j], exact=True) + s1_ref[h, j]
            mean = jnp.mean(y, axis=-1, keepdims=True)
            var = jnp.mean(jnp.square(y - mean), axis=-1, keepdims=True)
            yn = (y - mean) * lax.rsqrt(var + RWKV_GN_EPS) * lng_ref[h:h + 1, :] + lnb_ref[h:h + 1, :]
            y_ref[h, j] = yn + bonus_ref[h, j]
        s_scr[h] = s

    @pl.when(c == pl.num_programs(1) - 1)
    def _():
        s_out_ref[...] = s_scr[...]


def _rwkv_scan(r, lw, k, v, kk, b, s0, r_k, ln_g, ln_b, chunk):
    n_bh, T, dh = r.shape
    H = r_k.shape[0]
    n_chunks = T // chunk
    seq = pl.BlockSpec((H, chunk, dh), lambda i, c: (i, c, 0))
    mat = pl.BlockSpec((H, None, dh, dh), lambda i, c: (i, c, 0, 0))
    par = pl.BlockSpec((H, dh), lambda i, c: (0, 0))
    seq_shape = jax.ShapeDtypeStruct((n_bh, T, dh), F32)
    mat_shape = jax.ShapeDtypeStruct((n_bh, n_chunks, dh, dh), F32)
    rw, y0, bonus, a, s1 = pl.pallas_call(
        functools.partial(_rwkv_chunk_kernel, H, chunk), grid=(n_bh // H, n_chunks),
        in_specs=[seq] * 6 + [par], out_specs=[seq, seq, seq, mat, mat],
        out_shape=[seq_shape, seq_shape, seq_shape, mat_shape, mat_shape],
        compiler_params=_params("parallel", "parallel"), name="rwkv_chunk",
    )(r, lw, k, v, kk, b, r_k)
    n_sub = math.gcd(SCAN_CHUNKS_PER_STEP, n_chunks)
    split = lambda t: t.reshape(n_bh, n_chunks, chunk, dh)
    seq4 = pl.BlockSpec((H, n_sub, chunk, dh), lambda i, c: (i, c, 0, 0))
    mat4 = pl.BlockSpec((H, n_sub, dh, dh), lambda i, c: (i, c, 0, 0))
    state = pl.BlockSpec((H, dh, dh), lambda i, c: (i, 0, 0))
    y, s_new = pl.pallas_call(
        functools.partial(_rwkv_state_kernel, H, n_sub), grid=(n_bh // H, n_chunks // n_sub),
        in_specs=[seq4, seq4, seq4, mat4, mat4, state, par, par], out_specs=[seq4, state],
        out_shape=[jax.ShapeDtypeStruct((n_bh, n_chunks, chunk, dh), F32),
                   jax.ShapeDtypeStruct((n_bh, dh, dh), F32)],
        scratch_shapes=[pltpu.VMEM((H, dh, dh), F32)],
        compiler_params=_params("parallel", "arbitrary"), name="rwkv_state",
    )(split(rw), split(y0), split(bonus), a, s1, s0, ln_g.reshape(H, dh), ln_b.reshape(H, dh))
    return y.reshape(n_bh, T, dh), s_new


def _rwkv_mix(rw, shift_prev, wkv0, p, chunk):
    mu, w0, w_up, a0, a_up, g_up, k_k, k_a, r_k, ln_g, ln_b = p
    B, T, ncols = rw.shape
    H = r_k.shape[0]
    n_dim = H * HEAD_DIM
    prev = jnp.concatenate([shift_prev[:, None], rw[:, :-1]], axis=1)
    outs = _rwkv_pre(rw.reshape(B * T, ncols), prev.reshape(B * T, ncols), mu, w0, w_up, a0, a_up, g_up, k_k, k_a)
    g = outs[6]
    Tp = _round_up(T, chunk)

    def heads(t):
        t = t.reshape(B, T, H, HEAD_DIM).transpose(0, 2, 1, 3)
        t = jnp.pad(t, ((0, 0), (0, 0), (0, Tp - T), (0, 0)))
        return t.reshape(B * H, Tp, HEAD_DIM)

    y, s_new = _rwkv_scan(*[heads(t) for t in outs[:6]], wkv0.reshape(B * H, HEAD_DIM, HEAD_DIM), r_k, ln_g, ln_b,
                          chunk)
    y = y.reshape(B, H, Tp, HEAD_DIM)[:, :, :T].transpose(0, 2, 1, 3).reshape(B * T, n_dim)
    return y, g, s_new.reshape(B, H, HEAD_DIM, HEAD_DIM)


def _compress_kernel(x_ref, w1_ref, pe_ref, w2_ref, o_ref):
    x = x_ref[...]
    w1 = w1_ref[...]
    nch = x.shape[0]
    half = w1.shape[0] // 2
    bias = _mm(pe_ref[...], w1)[0:1, :]
    first = _mm(x, w1[:half])
    second = _mm(x, w1[half:])
    h = jax.nn.gelu(first + pltpu.roll(second, nch - 1, 0) + bias)
    o_ref[...] = _mm(h, w2_ref[...])


def _nsa_compress(kv_cmp, w1, pe, w2):
    _, BG, L, dh = kv_cmp.shape
    nch = L // CMP_STRIDE
    x = kv_cmp[:, :, :nch * CMP_STRIDE].reshape(2, BG, nch, CMP_STRIDE * dh)
    kdim = CMP_LEN * dh
    pe8 = jnp.pad(pe.reshape(2, 1, kdim), ((0, 0), (0, SUBLANES - 1), (0, 0)))
    return pl.pallas_call(
        _compress_kernel, grid=(2, BG),
        in_specs=[pl.BlockSpec((None, None, nch, kdim // 2), lambda s, i: (s, i, 0, 0)),
                  pl.BlockSpec((None, kdim, dh), lambda s, i: (s, 0, 0)),
                  pl.BlockSpec((None, SUBLANES, kdim), lambda s, i: (s, 0, 0)),
                  pl.BlockSpec((None, dh, dh), lambda s, i: (s, 0, 0))],
        out_specs=pl.BlockSpec((None, None, nch, dh), lambda s, i: (s, i, 0, 0)),
        out_shape=jax.ShapeDtypeStruct((2, BG, nch, dh), F32),
        compiler_params=_params("parallel", "parallel"), name="nsa_compress",
    )(x, w1.reshape(2, kdim, dh), pe8, w2)


def _cmp_select_kernel(tq, n_rep, q_base, n_sel, q_ref, kc_ref, vct_ref, ovt_ref, oc_ref, bias_ref):
    qi = pl.program_id(1)
    rows = n_rep * tq
    ncp = kc_ref.shape[0]
    nr = ovt_ref.shape[0]
    nbp = bias_ref.shape[0]
    q = q_ref[...] * HEAD_DIM ** -0.5
    s = _mm(kc_ref[...], q)
    t_row = q_base + qi * tq + lax.broadcasted_iota(jnp.int32, (1, rows), 1) % tq
    c_end = lax.broadcasted_iota(jnp.int32, (ncp, 1), 0) * CMP_STRIDE + (CMP_LEN - 1)
    s = jnp.where(c_end <= t_row, s, NEG_INF)
    m = jnp.max(s, axis=0, keepdims=True)
    e = jnp.exp(s - jnp.where(m == NEG_INF, 0.0, m))
    p = e / jnp.maximum(jnp.sum(e, axis=0, keepdims=True), 1e-30)
    oc_ref[...] = _mm(vct_ref[...], p)
    p_sum = p[:, 0:tq]
    for r in range(1, n_rep):
        p_sum = p_sum + p[:, r * tq:(r + 1) * tq]
    imp = _mm(ovt_ref[...], p_sum, exact=True)
    cur = (q_base + qi * tq + lax.broadcasted_iota(jnp.int32, (1, tq), 1)) // SEL_BLOCK
    blk = lax.broadcasted_iota(jnp.int32, (nr, 1), 0)
    forced = jnp.where(blk == cur, 1.0, jnp.where(blk == cur - 1, 1.0, jnp.where(blk == 0, 1.0, 0.0)))
    score = jnp.where(blk <= cur, imp + FORCE_BONUS * forced, NEG_INF)
    rank = _rank_rows(score, blk, n_sel)
    bias = jnp.where(blk <= cur, jnp.where(rank < min(SEL_TOPN, n_sel), 0.0, NEG_BIG), NEG_BIG)
    if nbp > nr:
        bias = jnp.concatenate([bias, jnp.zeros((nbp - nr, tq), F32)], axis=0)
    bias_ref[...] = bias.astype(BF16)


def _cmp_select(qt, kc, vct, ovt, tq, q_base, n_sel, nbp):
    BG, nq, dh, rows = qt.shape
    ncp = kc.shape[1]
    n_rep = rows // tq
    qspec = pl.BlockSpec((None, None, dh, rows), lambda i, j: (i, j, 0, 0))
    return pl.pallas_call(
        functools.partial(_cmp_select_kernel, tq, n_rep, q_base, n_sel), grid=(BG, nq),
        in_specs=[qspec, pl.BlockSpec((None, ncp, dh), lambda i, j: (i, 0, 0)),
                  pl.BlockSpec((None, dh, ncp), lambda i, j: (i, 0, 0)),
                  pl.BlockSpec(ovt.shape, lambda i, j: (0, 0))],
        out_specs=[qspec, pl.BlockSpec((None, None, nbp, tq), lambda i, j: (i, j, 0, 0))],
        out_shape=[jax.ShapeDtypeStruct((BG, nq, dh, rows), F32), jax.ShapeDtypeStruct((BG, nq, nbp, tq), BF16)],
        compiler_params=_params("parallel", "parallel"), name="nsa_cmp_select",
    )(qt, kc, vct, ovt)


def _overlap_matrix_t(nc, nsel, ncp, nr):
    i = jnp.arange(nc, dtype=jnp.int32)[None, :]
    j = jnp.arange(nsel, dtype=jnp.int32)[:, None]
    lo = jnp.maximum(i * CMP_STRIDE, j * SEL_BLOCK)
    hi = jnp.minimum(i * CMP_STRIDE + CMP_LEN, (j + 1) * SEL_BLOCK)
    ov = jnp.clip(hi - lo, 0).astype(F32) / CMP_LEN
    return jnp.pad(ov, ((0, nr - nsel), (0, ncp - nc)))


def _means_kernel(n_blk, k_ref, o_ref):
    o_ref[...] = jnp.zeros(o_ref.shape, F32)
    for n in range(n_blk):
        o_ref[n:n + 1, :] = jnp.sum(k_ref[n * MOBA_BLOCK:(n + 1) * MOBA_BLOCK, :], axis=0,
                                    keepdims=True) / MOBA_BLOCK


def _block_means(k, n_blk, nbp):
    BG, Lp, dh = k.shape
    return pl.pallas_call(
        functools.partial(_means_kernel, n_blk), grid=(BG,),
        in_specs=[pl.BlockSpec((None, Lp, dh), lambda i: (i, 0, 0))],
        out_specs=pl.BlockSpec((None, nbp, dh), lambda i: (i, 0, 0)),
        out_shape=jax.ShapeDtypeStruct((BG, nbp, dh), F32), compiler_params=_params("parallel"), name="moba_means",
    )(k)


def _flash_kernel(mode, tq, n_rep, tk, q_base, k_base, n_blk, *refs):
    if mode == "sel":
        q_ref, ke_ref, vt_ref, bias_ref, o_ref, lhs_scr, m_scr, acc_scr = refs
    elif mode == "win":
        q_ref, ke_ref, vt_ref, o_ref, lhs_scr, m_scr, acc_scr = refs
    else:
        q_ref, ke_ref, vt_ref, mean_ref, o_ref, lhs_scr, m_scr, acc_scr = refs
    qi = pl.program_id(1)
    rows = n_rep * tq
    q0 = q_base + qi * tq
    t_row = q0 + lax.broadcasted_iota(jnp.int32, (1, rows), 1) % tq
    q = q_ref[...] * HEAD_DIM ** -0.5
    lhs_scr[0:HEAD_DIM, :] = q.astype(BF16)
    if mode == "sel":
        lhs_scr[HEAD_DIM:, :] = jnp.concatenate([bias_ref[...]] * n_rep, axis=1)
    elif mode == "moba":
        nr = _round_up(n_blk, SUBLANES)
        nbp = mean_ref.shape[0]
        gate = _mm(mean_ref[0:nr, :], q, exact=True)
        blk = lax.broadcasted_iota(jnp.int32, (nr, 1), 0)
        cur = t_row // MOBA_BLOCK
        gate = jnp.where(blk < cur, gate, NEG_INF)
        rank = _rank_rows(gate, blk, n_blk)
        picked = jnp.where(blk < cur, jnp.where(rank < min(MOBA_TOPK, n_blk), 0.0, NEG_BIG), NEG_BIG)
        bias = jnp.where(blk == cur, 0.0, picked)
        if nbp > nr:
            bias = jnp.concatenate([bias, jnp.zeros((nbp - nr, rows), F32)], axis=0)
        lhs_scr[HEAD_DIM:, :] = bias.astype(BF16)
    m_scr[...] = jnp.full(m_scr.shape, NEG_BIG, F32)
    acc_scr[...] = jnp.zeros(acc_scr.shape, F32)

    def tile(ki, positional):
        k0 = pl.multiple_of(ki * tk, tk)
        s = jnp.dot(ke_ref[pl.ds(k0, tk), :], lhs_scr[...], preferred_element_type=F32)
        if positional:
            kpos = k_base + k0 + lax.broadcasted_iota(jnp.int32, (tk, 1), 0)
            s = jnp.where(kpos <= t_row, s, NEG_BIG)
            if mode == "win":
                s = jnp.where(kpos > t_row - WINDOW, s, NEG_BIG)
        m_prev = m_scr[...]
        m_new = jnp.maximum(m_prev, jnp.max(s, axis=0, keepdims=True))
        p = jnp.exp(s - m_new)
        acc_scr[...] = jnp.exp(m_prev - m_new) * acc_scr[...] + jnp.dot(
            vt_ref[:, pl.ds(k0, tk)], p.astype(BF16), preferred_element_type=F32)
        m_scr[...] = m_new

    def body(positional):
        def f(ki, carry):
            tile(ki, positional)
            return carry
        return f

    last = (q0 + tq - 1 - k_base) // tk
    if mode == "win":
        lax.fori_loop(jnp.maximum(q0 - (WINDOW - 1) - k_base, 0) // tk, last + 1, body(True), 0)
    else:
        lax.fori_loop(0, last, body(False), 0)
        tile(last, True)
    acc = acc_scr[...]
    o_ref[...] = acc[0:HEAD_DIM] / jnp.maximum(acc[HEAD_DIM:HEAD_DIM + 1], 1e-30)


def _flash(mode, qt, ke, vt, extra, tq, q_base, k_base, n_blk, tk=ATT_K_TILE):
    BG, nq, dh, rows = qt.shape
    Lp, kw = ke.shape[1:]
    n_rep = rows // tq
    assert Lp % tk == 0 and (q_base + nq * tq - 1 - k_base) // tk < Lp // tk
    assert tk % tq == 0 and (q_base - k_base) % tq == 0
    qspec = pl.BlockSpec((None, None, dh, rows), lambda i, j: (i, j, 0, 0))
    ins = [qt, ke, vt]
    in_specs = [qspec, pl.BlockSpec((None, Lp, kw), lambda i, j: (i, 0, 0)),
                pl.BlockSpec((None, V_ROWS, Lp), lambda i, j: (i, 0, 0))]
    if mode == "sel":
        ins.append(extra)
        in_specs.append(pl.BlockSpec((None, None, kw - dh, tq), lambda i, j: (i, j, 0, 0)))
    if mode == "moba":
        ins.append(extra)
        in_specs.append(pl.BlockSpec((None, kw - dh, dh), lambda i, j: (i, 0, 0)))
    scratch = [pltpu.VMEM((kw, rows), BF16), pltpu.VMEM((1, rows), F32), pltpu.VMEM((V_ROWS, rows), F32)]
    return pl.pallas_call(
        functools.partial(_flash_kernel, mode, tq, n_rep, tk, q_base, k_base, n_blk), grid=(BG, nq),
        in_specs=in_specs, out_specs=qspec, out_shape=jax.ShapeDtypeStruct((BG, nq, dh, rows), F32),
        scratch_shapes=scratch, compiler_params=_params("parallel", "parallel"), name="flash_" + mode,
    )(*ins)


def _group_queries_t(q, n_groups, tq):
    B, T, n = q.shape
    R = n // HEAD_DIM // n_groups
    q = q.reshape(B, T // tq, tq, n_groups, R, HEAD_DIM).transpose(0, 3, 1, 5, 4, 2)
    return q.reshape(B * n_groups, T // tq, HEAD_DIM, R * tq)


def _ungroup_t(o, B, n_groups, tq):
    BG, nq, dh, rows = o.shape
    R = rows // tq
    o = o.reshape(B, n_groups, nq, dh, R, tq).transpose(0, 2, 5, 1, 4, 3)
    return o.reshape(B, nq * tq, n_groups * R * dh)


def _head_major(x, Lp):
    B, L, G, dh = x.shape
    x = jnp.pad(x.transpose(0, 2, 1, 3), ((0, 0), (0, 0), (0, Lp - L), (0, 0)))
    return x.reshape(B * G, Lp, dh)


def _key_operand(x, Lp, block=None, nbp=0, first_pos=0):
    k = _head_major(x, Lp).astype(BF16)
    if block is None:
        return k
    onehot = ((first_pos + jnp.arange(Lp))[:, None] // block == jnp.arange(nbp)[None, :]).astype(BF16)
    return jnp.concatenate([k, jnp.broadcast_to(onehot[None], (k.shape[0], Lp, nbp))], axis=-1)


def _value_operand(x, Lp):
    B, L, G, dh = x.shape
    v = jnp.pad(x.transpose(0, 2, 3, 1), ((0, 0), (0, 0), (0, 0), (0, Lp - L))).reshape(B * G, dh, Lp)
    return jnp.concatenate([v, jnp.ones((B * G, V_ROWS - dh, Lp), F32)], axis=1).astype(BF16)


def _bias_rows(n):
    return _round_up(HEAD_DIM + n, LANES) - HEAD_DIM


def _sel_blocks(n_pos):
    n_sel = -(-n_pos // SEL_BLOCK)
    return n_sel, _bias_rows(n_sel)


def _nsa_branches(q, cmp_rows, n_rows, ke_sel, vt_sel, win_kv, cw, tq, q_base, win_base, tk):
    B, Tq, _ = q.shape
    G = NSA_KV_HEADS
    assert cmp_rows.shape[2] // CMP_STRIDE == n_rows // CMP_STRIDE
    qt = _group_queries_t(q, G, tq)
    kvc = _nsa_compress(cmp_rows, *cw)
    n_sel, nbp = _sel_blocks(max(n_rows, q_base + Tq))
    ovt = _overlap_matrix_t(n_rows // CMP_STRIDE - 1, n_sel, kvc.shape[2], _round_up(n_sel, SUBLANES))
    oc, bias = _cmp_select(qt, kvc[0], kvc[1].transpose(0, 2, 1), ovt, tq, q_base, n_sel, nbp)
    o_s = _flash("sel", qt, ke_sel, vt_sel, bias, tq, q_base, 0, n_sel, tk)
    Lwp = _round_up(max(win_kv.shape[1], q_base + Tq - win_base), ATT_K_TILE)
    o_w = _flash("win", qt, _key_operand(win_kv[:, :, 0], Lwp), _value_operand(win_kv[:, :, 1], Lwp), None, tq,
                 q_base, win_base, 0)
    return tuple(_ungroup_t(o, B, G, tq) for o in (oc, o_s, o_w))


def _nsa_prompt(q, kv6, cw, tq):
    B, T = q.shape[:2]
    cmp_rows = jnp.stack([kv6[:, :, 0], kv6[:, :, 1]]).transpose(0, 1, 3, 2, 4).reshape(2, -1, T, HEAD_DIM)
    _, nbp = _sel_blocks(T)
    Lp = _round_up(T, ATT_K_TILE)
    return _nsa_branches(q, cmp_rows, T, _key_operand(kv6[:, :, 2], Lp, SEL_BLOCK, nbp),
                         _value_operand(kv6[:, :, 3], Lp), kv6[:, :, 4:6], cw, tq, 0, 0, ATT_K_TILE)


def _nsa_sample(q, kv6, cache, page_table, win_buf, cw, tq):
    Bs, Ts = q.shape[:2]
    page = cache.shape[1]
    past_len = page_table.shape[1] * page
    step = PAGES_PER_STEP * page
    assert past_len % CMP_STRIDE == 0 and Ts < CMP_STRIDE and tq <= step
    _, nbp = _sel_blocks(past_len + tq)
    tail_ke = _key_operand(kv6[:, :, 2], step, SEL_BLOCK, nbp, past_len)
    tail_vt = _value_operand(kv6[:, :, 3], step)
    rows, ke, vt, _ = _gather_kv(cache, page_table, NSA_KV_HEADS, (0, 1), 2, 3, SEL_BLOCK, tail_ke, tail_vt, False)
    win = jnp.concatenate([win_buf, kv6[:, :, 4:6]], axis=1)
    outs = _nsa_branches(_pad_queries(q, tq), rows, past_len + Ts, ke, vt, win, cw, tq, past_len,
                         past_len - win_buf.shape[1], step)
    return [o[:, :Ts] for o in outs], win[:, Ts:]


def _moba_prompt(q, kv, tq):
    B, T, _ = q.shape
    n_blk = -(-T // MOBA_BLOCK)
    nbp = _bias_rows(n_blk)
    Lp = _round_up(T, ATT_K_TILE)
    qt = _group_queries_t(q, MOBA_KV_HEADS, tq)
    means = _block_means(_head_major(kv[:, :, 0], Lp), n_blk, nbp)
    o = _flash("moba", qt, _key_operand(kv[:, :, 0], Lp, MOBA_BLOCK, nbp), _value_operand(kv[:, :, 1], Lp), means,
               tq, 0, 0, n_blk)
    return _ungroup_t(o, B, MOBA_KV_HEADS, tq)


def _moba_sample(q, kv, cache, page_table, tq):
    Bs, Ts = q.shape[:2]
    page = cache.shape[1]
    past_len = page_table.shape[1] * page
    step = PAGES_PER_STEP * page
    assert step % MOBA_BLOCK == 0 and tq <= MOBA_BLOCK
    n_blk = -(-(past_len + tq) // MOBA_BLOCK)
    nbp = _bias_rows(n_blk)
    tail_ke = _key_operand(kv[:, :, 0], step, MOBA_BLOCK, nbp, past_len)
    tail_vt = _value_operand(kv[:, :, 1], step)
    _, ke, vt, means = _gather_kv(cache, page_table, MOBA_KV_HEADS, (), 0, 1, MOBA_BLOCK, tail_ke, tail_vt, True)
    means = jnp.pad(means, ((0, 0), (0, max(nbp - means.shape[1], 0)), (0, 0)))[:, :nbp]
    qt = _group_queries_t(_pad_queries(q, tq), MOBA_KV_HEADS, tq)
    o = _flash("moba", qt, ke, vt, means, tq, past_len, 0, n_blk, step)
    return _ungroup_t(o, Bs, MOBA_KV_HEADS, tq)[:, :Ts]


def _gather_kv_kernel(n_groups, row_slots, k_slot, v_slot, block, with_means, pt_ref, *refs):
    del pt_ref
    G = n_groups
    dh = HEAD_DIM
    pps = PAGES_PER_STEP
    pages = refs[:pps]
    tail_ke_ref, tail_vt_ref = refs[pps:pps + 2]
    outs = list(refs[pps + 2:])
    rows_ref = outs.pop(0) if row_slots else None
    ke_ref, vt_ref = outs[0], outs[1]
    means_ref = outs[2] if with_means else None
    p = pl.program_id(1)
    last = pl.num_programs(1) - 1
    page = pages[0].shape[0]
    step, kw = ke_ref.shape[1:]

    @pl.when(p < last)
    def _():
        lane = lax.broadcasted_iota(jnp.int32, (step, kw), 1)
        row = p * step + lax.broadcasted_iota(jnp.int32, (step, kw), 0)
        onehot = jnp.where(lane - dh == row // block, 1.0, 0.0).astype(BF16)
        for g in range(G):
            ke_ref[g] = onehot
        vt_ref[:, dh:, :] = jnp.ones((G, V_ROWS - dh, step), BF16)
        for j, pg in enumerate(pages):
            r0 = j * page
            for si, s in enumerate(row_slots):
                for g in range(G):
                    c0 = (s * G + g) * dh
                    rows_ref[si, g, r0:r0 + page, :] = pg[:, c0:c0 + dh]
            for g in range(G):
                c0 = (k_slot * G + g) * dh
                ke_ref[g, r0:r0 + page, 0:dh] = pg[:, c0:c0 + dh].astype(BF16)
            for pair in range(G // 2):
                c0 = (v_slot * G + 2 * pair) * dh
                vt = pg[:, c0:c0 + 2 * dh].T.astype(BF16)
                for gg in range(2):
                    vt_ref[2 * pair + gg, 0:dh, r0:r0 + page] = vt[gg * dh:(gg + 1) * dh]
        if with_means:
            ppb = block // page
            c0 = k_slot * G * dh
            for n in range(step // block):
                tot = jnp.sum(pages[n * ppb][:, c0:c0 + G * dh], axis=0, keepdims=True)
                for j in range(n * ppb + 1, (n + 1) * ppb):
                    tot = tot + jnp.sum(pages[j][:, c0:c0 + G * dh], axis=0, keepdims=True)
                for g in range(G):
                    means_ref[g, n:n + 1, :] = tot[:, g * dh:(g + 1) * dh] / block

    @pl.when(p == last)
    def _():
        ke_ref[...] = tail_ke_ref[...]
        vt_ref[...] = tail_vt_ref[...]
        if with_means:
            means_ref[...] = jnp.zeros(means_ref.shape, F32)


def _gather_kv(cache, page_table, n_groups, row_slots, k_slot, v_slot, block, tail_ke, tail_vt, with_means):
    _, page, W = cache.shape
    Bs, n_pages = page_table.shape
    G = n_groups
    dh = HEAD_DIM
    pps = PAGES_PER_STEP
    step = pps * page
    assert n_pages % pps == 0
    if with_means:
        assert step % block == 0 and block % page == 0
    n_steps = n_pages // pps
    kw = tail_ke.shape[-1]
    Lp = (n_steps + 1) * step
    bps = step // block

    def page_map(j):
        return lambda b, p, pt: (pt[b, jnp.minimum(p, n_steps - 1) * pps + j], 0, 0)

    in_specs = [pl.BlockSpec((None, page, W), page_map(j)) for j in range(pps)]
    in_specs += [pl.BlockSpec((None, G, step, kw), lambda b, p, pt: (b, 0, 0, 0)),
                 pl.BlockSpec((None, G, V_ROWS, step), lambda b, p, pt: (b, 0, 0, 0))]
    out_shape, out_specs = [], []
    if row_slots:
        n_rs = len(row_slots)
        out_shape.append(jax.ShapeDtypeStruct((n_rs, Bs, G, n_steps * step, dh), F32))
        out_specs.append(pl.BlockSpec((n_rs, None, G, step, dh),
                                      lambda b, p, pt: (0, b, 0, jnp.minimum(p, n_steps - 1), 0)))
    out_shape += [jax.ShapeDtypeStruct((Bs, G, Lp, kw), BF16), jax.ShapeDtypeStruct((Bs, G, V_ROWS, Lp), BF16)]
    out_specs += [pl.BlockSpec((None, G, step, kw), lambda b, p, pt: (b, 0, p, 0)),
                  pl.BlockSpec((None, G, V_ROWS, step), lambda b, p, pt: (b, 0, 0, p))]
    if with_means:
        out_shape.append(jax.ShapeDtypeStruct((Bs, G, n_steps + 1, bps, dh), F32))
        out_specs.append(pl.BlockSpec((None, G, None, bps, dh), lambda b, p, pt: (b, 0, p, 0, 0)))
    grid_spec = pltpu.PrefetchScalarGridSpec(num_scalar_prefetch=1, grid=(Bs, n_steps + 1), in_specs=in_specs,
                                             out_specs=out_specs)
    outs = list(pl.pallas_call(
        functools.partial(_gather_kv_kernel, G, tuple(row_slots), k_slot, v_slot, block, with_means),
        grid_spec=grid_spec, out_shape=out_shape, compiler_params=_params("parallel", "arbitrary"),
        name="gather_kv",
    )(page_table, *([cache] * pps), tail_ke.reshape(Bs, G, step, kw), tail_vt.reshape(Bs, G, V_ROWS, step)))
    rows = outs.pop(0).reshape(len(row_slots), Bs * G, n_steps * step, dh) if row_slots else None
    ke = outs[0].reshape(Bs * G, Lp, kw)
    vt = outs[1].reshape(Bs * G, V_ROWS, Lp)
    means = outs[2].reshape(Bs * G, (n_steps + 1) * bps, dh) if with_means else None
    return rows, ke, vt, means


def _pad_queries(x, tq):
    return jnp.pad(x, ((0, 0), (0, tq - x.shape[1]), (0, 0)))


def kernel(x_prompt, x_sample, cache_nsa_kv, cache_moba_kv, state_win_kv, state_wkv, state_shift, page_table, norm_mix, norm_ffn, norm_final, even_w_in, even_w_out, rwkv_mu, rwkv_w0, rwkv_w_up, rwkv_a0, rwkv_a_up, rwkv_g_up, rwkv_k_k, rwkv_k_a, rwkv_r_k, rwkv_ln_g, rwkv_ln_b, nsa_gate_b, nsa_cmp_w1, nsa_cmp_pe, nsa_cmp_w2, odd_w_in, odd_w_out, ffn_w_gate, ffn_w_up, ffn_w_down):
    B, T, D = x_prompt.shape
    Bs, Ts, _ = x_sample.shape
    depth = norm_mix.shape[0]
    page = cache_nsa_kv.shape[2]
    past_len = page_table.shape[1] * page
    rwkv_dim = rwkv_w0.shape[1]
    rwkv_cols = rwkv_mu.shape[1]
    nsa_heads = nsa_gate_b.shape[1] // 3
    nsa_dim = nsa_heads * HEAD_DIM
    nsa_kv_cols = 6 * NSA_KV_HEADS * HEAD_DIM
    moba_kv_cols = 2 * MOBA_KV_HEADS * HEAD_DIM
    moba_dim = odd_w_in.shape[2] - moba_kv_cols
    tq_p = min(ATT_Q_TILE, T)
    tq_s = SAMPLE_Q_PAD

    cos_p, sin_p = _rope_tables(jnp.arange(T, dtype=jnp.int32))
    pos_s = past_len + jnp.arange(Ts, dtype=jnp.int32)
    cos_s, sin_s = _rope_tables(jnp.tile(pos_s, Bs))

    even_spec = ((rwkv_cols, (), False), (nsa_dim, tuple(range(nsa_dim // LANES)), False),
                 (nsa_kv_cols, tuple(range(0, nsa_kv_cols // LANES, 2)), False), (LANES, (), True))
    k_chunks = MOBA_KV_HEADS * HEAD_DIM // LANES
    odd_spec = ((moba_dim, tuple(range(moba_dim // LANES)), False), (moba_kv_cols, tuple(range(k_chunks)), False))

    hp = x_prompt.reshape(B * T, D)
    hs = x_sample.reshape(Bs * Ts, D)
    nsa_p, nsa_s, moba_p, moba_s = [], [], [], []
    win_p, win_s, wkv_p, wkv_s, sh_p, sh_s = [], [], [], [], [], []
    for layer in range(depth):
        i = layer // 2
        if layer % 2 == 0:
            w_in = even_w_in[i].astype(BF16)
            o = rwkv_cols
            n_gate = 3 * nsa_heads
            weights = [w_in[:, :o], w_in[:, o:o + nsa_dim], w_in[:, o + nsa_dim:o + nsa_dim + nsa_kv_cols],
                       jnp.pad(w_in[:, o + nsa_dim + nsa_kv_cols:], ((0, 0), (0, LANES - n_gate)))]
            gate_b = jnp.pad(nsa_gate_b[i], (0, LANES - n_gate)).reshape(1, LANES)
            rp = (rwkv_mu[i], rwkv_w0[i], rwkv_w_up[i], rwkv_a0[i], rwkv_a_up[i], rwkv_g_up[i],
                  rwkv_k_k[i], rwkv_k_a[i], rwkv_r_k[i], rwkv_ln_g[i], rwkv_ln_b[i])
            cw = (nsa_cmp_w1[i], nsa_cmp_pe[i], nsa_cmp_w2[i])

            rw, q, kv, gates = _project(hp, norm_mix[layer], cos_p, sin_p, weights, [gate_b], even_spec, "even_proj")
            rw3 = rw.reshape(B, T, rwkv_cols)
            kv6 = kv.reshape(B, T, 6, NSA_KV_HEADS, HEAD_DIM)
            y, g, wkv_new = _rwkv_mix(rw3, jnp.zeros((B, rwkv_cols), F32),
                                      jnp.zeros((B, rwkv_dim // HEAD_DIM, HEAD_DIM, HEAD_DIM), F32), rp, SCAN_CHUNK)
            o_c, o_s, o_w = _nsa_prompt(q.reshape(B, T, nsa_dim), kv6, cw, tq_p)
            hp = _even_out(hp, y, g, o_c.reshape(B * T, nsa_dim), o_s.reshape(B * T, nsa_dim),
                           o_w.reshape(B * T, nsa_dim), gates, even_w_out[i])
            nsa_p.append(kv6[:, :, :4])
            win_p.append(kv6[:, T - min(WINDOW, T):, 4:6])
            wkv_p.append(wkv_new)
            sh_p.append(rw3[:, -1])

            rw, q, kv, gates = _project(hs, norm_mix[layer], cos_s, sin_s, weights, [gate_b], even_spec, "even_proj_s")
            rw3 = rw.reshape(Bs, Ts, rwkv_cols)
            kv6 = kv.reshape(Bs, Ts, 6, NSA_KV_HEADS, HEAD_DIM)
            y, g, wkv_new = _rwkv_mix(rw3, state_shift[i], state_wkv[i], rp, SUBLANES)
            outs, win_new = _nsa_sample(q.reshape(Bs, Ts, nsa_dim), kv6,
                                        cache_nsa_kv[i].reshape(-1, page, 4 * NSA_KV_HEADS * HEAD_DIM), page_table,
                                        state_win_kv[i], cw, tq_s)
            o_c, o_s, o_w = [t.reshape(Bs * Ts, nsa_dim) for t in outs]
            hs = _even_out(hs, y, g, o_c, o_s, o_w, gates, even_w_out[i])
            nsa_s.append(kv6[:, :, :4])
            win_s.append(win_new)
            wkv_s.append(wkv_new)
            sh_s.append(rw3[:, -1])
        else:
            w_in = odd_w_in[i].astype(BF16)
            weights = [w_in[:, :moba_dim], w_in[:, moba_dim:]]
            q, kv = _project(hp, norm_mix[layer], cos_p, sin_p, weights, [], odd_spec, "odd_proj")
            kv2 = kv.reshape(B, T, 2, MOBA_KV_HEADS, HEAD_DIM)
            a = _moba_prompt(q.reshape(B, T, moba_dim), kv2, tq_p)
            hp = _odd_out(hp, a.reshape(B * T, moba_dim), odd_w_out[i])
            moba_p.append(kv2)

            q, kv = _project(hs, norm_mix[layer], cos_s, sin_s, weights, [], odd_spec, "odd_proj_s")
            kv2 = kv.reshape(Bs, Ts, 2, MOBA_KV_HEADS, HEAD_DIM)
            a = _moba_sample(q.reshape(Bs, Ts, moba_dim), kv2, cache_moba_kv[i].reshape(-1, page, moba_kv_cols),
                             page_table, tq_s)
            hs = _odd_out(hs, a.reshape(Bs * Ts, moba_dim), odd_w_out[i])
            moba_s.append(kv2)
        g_final = norm_final if layer == depth - 1 else None
        hp = _ffn(hp, norm_ffn[layer], ffn_w_gate[layer], ffn_w_up[layer], ffn_w_down[layer], g_final)
        hs = _ffn(hs, norm_ffn[layer], ffn_w_gate[layer], ffn_w_up[layer], ffn_w_down[layer], g_final)
    return (hp.reshape(B, T, D), hs.reshape(Bs, Ts, D), jnp.stack(nsa_p), jnp.stack(nsa_s), jnp.stack(moba_p),
            jnp.stack(moba_s), jnp.stack(win_p), jnp.stack(win_s), jnp.stack(wkv_p), jnp.stack(wkv_s),
            jnp.stack(sh_p), jnp.stack(sh_s))
```

```python
import functools
import math

import jax
import jax.numpy as jnp
from jax import lax
from jax.experimental import pallas as pl
from jax.experimental.pallas import tpu as pltpu

F32 = jnp.float32
BF16 = jnp.bfloat16
HIGHEST = lax.Precision.HIGHEST

HEAD_DIM = 64
NORM_EPS = 1e-6
ROPE_THETA = 10000.0
DECAY_LORA = 64
AAA_LORA = 64
GATE_LORA = 128
RWKV_GN_EPS = 64e-5
NSA_KV_HEADS = 2
CMP_STRIDE = 16
CMP_LEN = 2 * CMP_STRIDE
SEL_BLOCK = 64
SEL_TOPN = 16
WINDOW = 512
FORCE_BONUS = 100.0
MOBA_KV_HEADS = 4
MOBA_BLOCK = 256
MOBA_TOPK = 3

LANES = 128
SUBLANES = 8
VMEM_LIMIT = 56 * 1024 * 1024

ROW_TILE = 512
FFN_ROW_TILE = 1024
FFN_COL_TILE = 256
ATT_Q_TILE = 128
ATT_K_TILE = 512
SCAN_CHUNK = 64
SCAN_CHUNKS_PER_STEP = 4
SAMPLE_Q_PAD = 32
PAGES_PER_STEP = 8

NT = (((1,), (1,)), ((), ()))
TN = (((0,), (0,)), ((), ()))
NEG_INF = float("-inf")
NEG_BIG = -1e30
V_ROWS = HEAD_DIM + 16


def _round_up(x, m):
    return -(-x // m) * m


def _mm(a, b, dims=None, exact=False):
    if dims is None:
        dims = (((a.ndim - 1,), (0,)), ((), ()))
    if exact:
        return lax.dot_general(a.astype(F32), b.astype(F32), dims, precision=HIGHEST,
                               preferred_element_type=F32)
    return lax.dot_general(a.astype(BF16), b.astype(BF16), dims, preferred_element_type=F32)


def _params(*sem):
    return pltpu.CompilerParams(dimension_semantics=sem, vmem_limit_bytes=VMEM_LIMIT)


def _rms_norm(x, g):
    return x * lax.rsqrt(jnp.mean(x * x, axis=-1, keepdims=True) + NORM_EPS) * g


def _rope_chunk(x, cos, sin):
    lane = lax.broadcasted_iota(jnp.int32, x.shape, 1)
    half = HEAD_DIM // 2
    partner = jnp.where((lane % HEAD_DIM) < half, pltpu.roll(x, LANES - half, 1), pltpu.roll(x, half, 1))
    return x * cos + partner * sin


def _rank_rows(score, blk, ncand):
    rank = jnp.zeros(score.shape, F32)
    for j in range(ncand):
        row = score[j:j + 1, :]
        rank = rank + jnp.where(row > score, 1.0, 0.0) + jnp.where(row == score, jnp.where(blk > j, 1.0, 0.0), 0.0)
    return rank


def _proj_kernel(spec, x_ref, g_ref, cos_ref, sin_ref, *refs):
    nseg = len(spec)
    w_refs = refs[:nseg]
    nbias = sum(1 for s in spec if s[2])
    b_refs = list(refs[nseg:nseg + nbias])
    o_refs = refs[nseg + nbias:]
    xn = _rms_norm(x_ref[...], g_ref[...]).astype(BF16)
    for (ncols, rope_chunks, sig), w_ref, o_ref in zip(spec, w_refs, o_refs):
        y = jnp.dot(xn, w_ref[...], preferred_element_type=F32)
        if sig:
            y = jax.nn.sigmoid(y + b_refs.pop(0)[...])
        if rope_chunks:
            cos = cos_ref[...]
            sin = sin_ref[...]
            for c in range(ncols // LANES):
                yc = y[:, c * LANES:(c + 1) * LANES]
                if c in rope_chunks:
                    yc = _rope_chunk(yc, cos, sin)
                o_ref[:, c * LANES:(c + 1) * LANES] = yc
        else:
            o_ref[...] = y


def _project(x, g, cos_tab, sin_tab, weights, biases, spec, name):
    M, D = x.shape
    tm = min(ROW_TILE, M)
    ntab = cos_tab.shape[0] // tm
    in_specs = [pl.BlockSpec((tm, D), lambda i: (i, 0)),
                pl.BlockSpec((1, D), lambda i: (0, 0)),
                pl.BlockSpec((tm, LANES), lambda i: (i % ntab, 0)),
                pl.BlockSpec((tm, LANES), lambda i: (i % ntab, 0))]
    in_specs += [pl.BlockSpec(w.shape, lambda i: (0, 0)) for w in weights]
    in_specs += [pl.BlockSpec(b.shape, lambda i: (0, 0)) for b in biases]
    out_shape = [jax.ShapeDtypeStruct((M, s[0]), F32) for s in spec]
    out_specs = [pl.BlockSpec((tm, s[0]), lambda i: (i, 0)) for s in spec]
    return pl.pallas_call(
        functools.partial(_proj_kernel, spec), grid=(M // tm,), in_specs=in_specs, out_specs=out_specs,
        out_shape=out_shape, compiler_params=_params("parallel"), name=name,
    )(x, g.reshape(1, D), cos_tab, sin_tab, *weights, *biases)


def _rope_tables(pos):
    half = HEAD_DIM // 2
    inv = ROPE_THETA ** (-jnp.arange(half, dtype=F32) / half)
    ang = pos.astype(F32)[:, None] * inv[None, :]
    cos = jnp.cos(ang)
    sin = jnp.sin(ang)
    cos_t = jnp.tile(cos, (1, LANES // half))
    sin_t = jnp.tile(jnp.concatenate([-sin, sin], axis=1), (1, LANES // HEAD_DIM))
    return cos_t, sin_t


def _even_out_kernel(res_ref, y_ref, g_ref, oc_ref, os_ref, ow_ref, gt_ref, ec_ref, es_ref, ew_ref,
                     w1_ref, w2_ref, o_ref):
    a = y_ref[...] * g_ref[...]
    gt = gt_ref[...]
    b = (_mm(gt, ec_ref[...], exact=True) * oc_ref[...] + _mm(gt, es_ref[...], exact=True) * os_ref[...]
         + _mm(gt, ew_ref[...], exact=True) * ow_ref[...])
    o_ref[...] = res_ref[...] + _mm(a, w1_ref[...]) + _mm(b, w2_ref[...])


def _even_out(res, y, g, o_c, o_s, o_w, gates, w_out):
    M, D = res.shape
    n_rw = y.shape[1]
    n_nsa = o_c.shape[1]
    tm = min(ROW_TILE, M)
    heads = n_nsa // HEAD_DIM
    col = jnp.arange(n_nsa)[None, :] // HEAD_DIM
    row = jnp.arange(LANES)[:, None]
    expand = [(row == col * 3 + br).astype(F32) for br in range(3)]
    del heads
    row_spec = lambda n: pl.BlockSpec((tm, n), lambda i: (i, 0))
    full = lambda a: pl.BlockSpec(a.shape, lambda i: (0, 0))
    w1 = w_out[:n_rw].astype(BF16)
    w2 = w_out[n_rw:].astype(BF16)
    return pl.pallas_call(
        _even_out_kernel, grid=(M // tm,),
        in_specs=[row_spec(D), row_spec(n_rw), row_spec(n_rw), row_spec(n_nsa), row_spec(n_nsa), row_spec(n_nsa),
                  row_spec(LANES), full(expand[0]), full(expand[1]), full(expand[2]), full(w1), full(w2)],
        out_specs=row_spec(D), out_shape=jax.ShapeDtypeStruct((M, D), F32),
        compiler_params=_params("parallel"), name="even_out",
    )(res, y, g, o_c, o_s, o_w, gates, *expand, w1, w2)


def _odd_out_kernel(res_ref, a_ref, w_ref, o_ref):
    o_ref[...] = res_ref[...] + _mm(a_ref[...], w_ref[...])


def _odd_out(res, a, w_out):
    M, D = res.shape
    tm = min(ROW_TILE, M)
    w = w_out.astype(BF16)
    return pl.pallas_call(
        _odd_out_kernel, grid=(M // tm,),
        in_specs=[pl.BlockSpec((tm, D), lambda i: (i, 0)), pl.BlockSpec((tm, a.shape[1]), lambda i: (i, 0)),
                  pl.BlockSpec(w.shape, lambda i: (0, 0))],
        out_specs=pl.BlockSpec((tm, D), lambda i: (i, 0)), out_shape=jax.ShapeDtypeStruct((M, D), F32),
        compiler_params=_params("parallel"), name="odd_out",
    )(res, a, w)


def _ffn_kernel(final_norm, x_ref, g_ref, wg_ref, wu_ref, wd_ref, gf_ref, o_ref, xn_scr, acc_scr):
    j = pl.program_id(1)

    @pl.when(j == 0)
    def _():
        xn_scr[...] = _rms_norm(x_ref[...], g_ref[...]).astype(BF16)
        acc_scr[...] = jnp.zeros(acc_scr.shape, F32)

    xn = xn_scr[...]
    h = jax.nn.silu(jnp.dot(xn, wg_ref[...], preferred_element_type=F32)) * jnp.dot(
        xn, wu_ref[...], preferred_element_type=F32)
    acc_scr[...] += _mm(h, wd_ref[...])

    @pl.when(j == pl.num_programs(1) - 1)
    def _():
        y = x_ref[...] + acc_scr[...]
        if final_norm:
            y = _rms_norm(y, gf_ref[...])
        o_ref[...] = y


def _ffn(x, g, wg, wu, wd, g_final=None):
    M, D = x.shape
    F = wg.shape[1]
    tm = min(FFN_ROW_TILE, M)
    tf = FFN_COL_TILE
    final_norm = g_final is not None
    gf = (g_final if final_norm else g).reshape(1, D)
    return pl.pallas_call(
        functools.partial(_ffn_kernel, final_norm), grid=(M // tm, F // tf),
        in_specs=[pl.BlockSpec((tm, D), lambda i, j: (i, 0)), pl.BlockSpec((1, D), lambda i, j: (0, 0)),
                  pl.BlockSpec((D, tf), lambda i, j: (0, j)), pl.BlockSpec((D, tf), lambda i, j: (0, j)),
                  pl.BlockSpec((tf, D), lambda i, j: (j, 0)), pl.BlockSpec((1, D), lambda i, j: (0, 0))],
        out_specs=pl.BlockSpec((tm, D), lambda i, j: (i, 0)), out_shape=jax.ShapeDtypeStruct((M, D), F32),
        scratch_shapes=[pltpu.VMEM((tm, D), BF16), pltpu.VMEM((tm, D), F32)],
        compiler_params=_params("parallel", "arbitrary"), name="ffn",
    )(x, g.reshape(1, D), wg.astype(BF16), wu.astype(BF16), wd.astype(BF16), gf)


def _rwkv_pre_kernel(n_dim, rw_ref, prev_ref, mu_ref, w0_ref, a0_ref, kk_ref, ka_ref, wup_ref, aup_ref, gup_ref,
                     hsum_ref, r_out, lw_out, k_out, v_out, kk_out, b_out, g_out):
    rw = rw_ref[...]
    xm = rw + (prev_ref[...] - rw) * mu_ref[...]
    r = xm[:, :n_dim]
    k = xm[:, n_dim:2 * n_dim]
    v = xm[:, 2 * n_dim:3 * n_dim]
    lora = xm[:, 3 * n_dim:3 * n_dim + DECAY_LORA + AAA_LORA]
    xg = xm[:, 3 * n_dim + DECAY_LORA + AAA_LORA:]
    lw = -math.exp(-0.5) * jax.nn.sigmoid(w0_ref[...] + _mm(jnp.tanh(lora), wup_ref[...], exact=True))
    a = jax.nn.sigmoid(a0_ref[...] + _mm(lora, aup_ref[...], exact=True))
    g = _mm(jax.nn.sigmoid(xg), gup_ref[...])
    kk = k * kk_ref[...]
    norm = jnp.sqrt(_mm(kk * kk, hsum_ref[...], exact=True))
    kk = kk / jnp.maximum(norm, 1e-12)
    r_out[...] = r
    lw_out[...] = lw
    k_out[...] = k * (1.0 + (a - 1.0) * ka_ref[...])
    v_out[...] = v
    kk_out[...] = kk
    b_out[...] = kk * a
    g_out[...] = g


def _rwkv_pre(rw, prev, mu, w0, w_up, a0, a_up, g_up, k_k, k_a):
    M, ncols = rw.shape
    n_dim = w0.shape[0]
    tm = min(ROW_TILE, M)
    zeros = jnp.zeros((AAA_LORA, n_dim), F32)
    wup_pad = jnp.concatenate([w_up, zeros], axis=0)
    aup_pad = jnp.concatenate([jnp.zeros((DECAY_LORA, n_dim), F32), a_up], axis=0)
    head = jnp.arange(n_dim) // HEAD_DIM
    hsum = (head[:, None] == head[None, :]).astype(F32)
    vec = lambda a: a.reshape(1, -1)
    row = lambda n: pl.BlockSpec((tm, n), lambda i: (i, 0))
    full = lambda a: pl.BlockSpec(a.shape, lambda i: (0, 0))
    ins = [rw, prev, vec(mu), vec(w0), vec(a0), vec(k_k), vec(k_a), wup_pad, aup_pad, g_up, hsum]
    return pl.pallas_call(
        functools.partial(_rwkv_pre_kernel, n_dim), grid=(M // tm,),
        in_specs=[row(ncols), row(ncols)] + [full(a) for a in ins[2:]],
        out_specs=[row(n_dim)] * 7, out_shape=[jax.ShapeDtypeStruct((M, n_dim), F32)] * 7,
        compiler_params=_params("parallel"), name="rwkv_pre",
    )(*ins)


def _rwkv_chunk_kernel(n_heads, chunk, r_ref, lw_ref, k_ref, v_ref, kk_ref, b_ref, rk_ref,
                       rw_ref, y0_ref, bonus_ref, a_ref, s1_ref):
    C = chunk
    dh = HEAD_DIM
    ti = lax.broadcasted_iota(jnp.int32, (C, C), 0)
    si = lax.broadcasted_iota(jnp.int32, (C, C), 1)
    incl = jnp.where(si <= ti, 1.0, 0.0)
    eye = jnp.where(lax.broadcasted_iota(jnp.int32, (dh, dh), 0) == lax.broadcasted_iota(jnp.int32, (dh, dh), 1),
                    1.0, 0.0)
    n_double = max(1, math.ceil(math.log2(C)))
    heads = range(n_heads)
    r = [r_ref[h] for h in heads]
    lw = [lw_ref[h] for h in heads]
    k = [k_ref[h] for h in heads]
    v = [v_ref[h] for h in heads]
    b = [b_ref[h] for h in heads]
    cum = [_mm(incl, lw[h], exact=True) for h in heads]
    cum_end = [cum[h][C - 1:C, :] for h in heads]
    kkw = [kk_ref[h] * jnp.exp(cum[h] - lw[h]) for h in heads]
    rwc = [r[h] * jnp.exp(cum[h]) for h in heads]
    w_inv = [jnp.exp(-cum[h]) for h in heads]
    kd = [k[h] * w_inv[h] for h in heads]
    bd = [b[h] * w_inv[h] for h in heads]
    w_end = [jnp.exp(cum_end[h] - cum[h]) for h in heads]
    a_ub = [jnp.where(si < ti, _mm(kkw[h], bd[h], NT), 0.0) for h in heads]
    a_vk = [jnp.where(si < ti, _mm(kkw[h], kd[h], NT), 0.0) for h in heads]
    b_rb = [jnp.where(si <= ti, _mm(rwc[h], bd[h], NT), 0.0) for h in heads]
    b_rk = [jnp.where(si <= ti, _mm(rwc[h], kd[h], NT), 0.0) for h in heads]
    x = [jnp.concatenate([kkw[h], _mm(a_vk[h], v[h])], axis=1) for h in heads]
    p = [-a_ub[h] for h in heads]
    x = [x[h] + _mm(p[h], x[h]) for h in heads]
    for _ in range(n_double - 1):
        p = [_mm(p[h], p[h]) for h in heads]
        x = [x[h] + _mm(p[h], x[h]) for h in heads]
    for h in heads:
        bx = _mm(b_rb[h], x[h])
        rw_ref[h] = rwc[h] - bx[:, :dh]
        y0_ref[h] = _mm(b_rk[h], v[h]) - bx[:, dh:]
        kw = x[h][:, :dh]
        uv = x[h][:, dh:]
        a_ref[h] = eye * jnp.exp(cum_end[h]) - _mm(kw, b[h] * w_end[h], TN)
        s1_ref[h] = _mm(v[h], k[h] * w_end[h], TN) - _mm(uv, b[h] * w_end[h], TN)
        bonus_ref[h] = jnp.sum(r[h] * k[h] * rk_ref[h:h + 1, :], axis=-1, keepdims=True) * v[h]


def _rwkv_state_kernel(n_heads, n_sub, rw_ref, y0_ref, bonus_ref, a_ref, s1_ref, s0_ref, lng_ref, lnb_ref,
                       y_ref, s_out_ref, s_scr):
    c = pl.program_id(1)

    @pl.when(c == 0)
    def _():
        s_scr[...] = s0_ref[...]

    for h in range(n_heads):
        s = s_scr[h]
        for j in range(n_sub):
            y = _mm(rw_ref[h, j], s, NT, exact=True) + y0_ref[h, j]
            s = _mm(s, a_ref[h, j], exact=True) + s1_ref[h, j]
            mean = jnp.mean(y, axis=-1, keepdims=True)
            var = jnp.mean(jnp.square(y - mean), axis=-1, keepdims=True)
            yn = (y - mean) * lax.rsqrt(var + RWKV_GN_EPS) * lng_ref[h:h + 1, :] + lnb_ref[h:h + 1, :]
            y_ref[h, j] = yn + bonus_ref[h, j]
        s_scr[h] = s

    @pl.when(c == pl.num_programs(1) - 1)
    def _():
        s_out_ref[...] = s_scr[...]


def _rwkv_scan(r, lw, k, v, kk, b, s0, r_k, ln_g, ln_b, chunk):
    n_bh, T, dh = r.shape
    H = r_k.shape[0]
    n_chunks = T // chunk
    seq = pl.BlockSpec((H, chunk, dh), lambda i, c: (i, c, 0))
    mat = pl.BlockSpec((H, None, dh, dh), lambda i, c: (i, c, 0, 0))
    par = pl.BlockSpec((H, dh), lambda i, c: (0, 0))
    seq_shape = jax.ShapeDtypeStruct((n_bh, T, dh), F32)
    mat_shape = jax.ShapeDtypeStruct((n_bh, n_chunks, dh, dh), F32)
    rw, y0, bonus, a, s1 = pl.pallas_call(
        functools.partial(_rwkv_chunk_kernel, H, chunk), grid=(n_bh // H, n_chunks),
        in_specs=[seq] * 6 + [par], out_specs=[seq, seq, seq, mat, mat],
        out_shape=[seq_shape, seq_shape, seq_shape, mat_shape, mat_shape],
        compiler_params=_params("parallel", "parallel"), name="rwkv_chunk",
    )(r, lw, k, v, kk, b, r_k)
    n_sub = math.gcd(SCAN_CHUNKS_PER_STEP, n_chunks)
    split = lambda t: t.reshape(n_bh, n_chunks, chunk, dh)
    seq4 = pl.BlockSpec((H, n_sub, chunk, dh), lambda i, c: (i, c, 0, 0))
    mat4 = pl.BlockSpec((H, n_sub, dh, dh), lambda i, c: (i, c, 0, 0))
    state = pl.BlockSpec((H, dh, dh), lambda i, c: (i, 0, 0))
    y, s_new = pl.pallas_call(
        functools.partial(_rwkv_state_kernel, H, n_sub), grid=(n_bh // H, n_chunks // n_sub),
        in_specs=[seq4, seq4, seq4, mat4, mat4, state, par, par], out_specs=[seq4, state],
        out_shape=[jax.ShapeDtypeStruct((n_bh, n_chunks, chunk, dh), F32),
                   jax.ShapeDtypeStruct((n_bh, dh, dh), F32)],
        scratch_shapes=[pltpu.VMEM((H, dh, dh), F32)],
        compiler_params=_params("parallel", "arbitrary"), name="rwkv_state",
    )(split(rw), split(y0), split(bonus), a, s1, s0, ln_g.reshape(H, dh), ln_b.reshape(H, dh))
    return y.reshape(n_bh, T, dh), s_new


def _rwkv_mix(rw, shift_prev, wkv0, p, chunk):
    mu, w0, w_up, a0, a_up, g_up, k_k, k_a, r_k, ln_g, ln_b = p
    B, T, ncols = rw.shape
    H = r_k.shape[0]
    n_dim = H * HEAD_DIM
    prev = jnp.concatenate([shift_prev[:, None], rw[:, :-1]], axis=1)
    outs = _rwkv_pre(rw.reshape(B * T, ncols), prev.reshape(B * T, ncols), mu, w0, w_up, a0, a_up, g_up, k_k, k_a)
    g = outs[6]
    Tp = _round_up(T, chunk)

    def heads(t):
        t = t.reshape(B, T, H, HEAD_DIM).transpose(0, 2, 1, 3)
        t = jnp.pad(t, ((0, 0), (0, 0), (0, Tp - T), (0, 0)))
        return t.reshape(B * H, Tp, HEAD_DIM)

    y, s_new = _rwkv_scan(*[heads(t) for t in outs[:6]], wkv0.reshape(B * H, HEAD_DIM, HEAD_DIM), r_k, ln_g, ln_b,
                          chunk)
    y = y.reshape(B, H, Tp, HEAD_DIM)[:, :, :T].transpose(0, 2, 1, 3).reshape(B * T, n_dim)
    return y, g, s_new.reshape(B, H, HEAD_DIM, HEAD_DIM)


def _compress_kernel(x_ref, w1_ref, pe_ref, w2_ref, o_ref):
    nch, dh = o_ref.shape
    w1 = w1_ref[...]
    bias = _mm(pe_ref[...], w1)[0:1, :]
    first = jnp.zeros((nch, dh), F32)
    second = jnp.zeros((nch, dh), F32)
    for c in range(CMP_STRIDE):
        xc = x_ref[pl.ds(c, nch, stride=CMP_STRIDE), :]
        first = first + _mm(xc, w1[c * dh:(c + 1) * dh])
        second = second + _mm(xc, w1[(CMP_STRIDE + c) * dh:(CMP_STRIDE + c + 1) * dh])
    h = jax.nn.gelu(first + pltpu.roll(second, nch - 1, 0) + bias)
    o_ref[...] = _mm(h, w2_ref[...])


def _nsa_compress(kv_cmp, w1, pe, w2):
    _, BG, L, dh = kv_cmp.shape
    nch = L // CMP_STRIDE
    x = kv_cmp[:, :, :nch * CMP_STRIDE]
    kdim = CMP_LEN * dh
    pe8 = jnp.pad(pe.reshape(2, 1, kdim), ((0, 0), (0, SUBLANES - 1), (0, 0)))
    return pl.pallas_call(
        _compress_kernel, grid=(2, BG),
        in_specs=[pl.BlockSpec((None, None, nch * CMP_STRIDE, dh), lambda s, i: (s, i, 0, 0)),
                  pl.BlockSpec((None, kdim, dh), lambda s, i: (s, 0, 0)),
                  pl.BlockSpec((None, SUBLANES, kdim), lambda s, i: (s, 0, 0)),
                  pl.BlockSpec((None, dh, dh), lambda s, i: (s, 0, 0))],
        out_specs=pl.BlockSpec((None, None, nch, dh), lambda s, i: (s, i, 0, 0)),
        out_shape=jax.ShapeDtypeStruct((2, BG, nch, dh), F32),
        compiler_params=_params("parallel", "parallel"), name="nsa_compress",
    )(x, w1.reshape(2, kdim, dh), pe8, w2)


def _load_queries(q_ref, n_rep, tok_major):
    if not tok_major:
        return q_ref[...]
    qt = q_ref[...].T
    return jnp.concatenate([qt[r * HEAD_DIM:(r + 1) * HEAD_DIM] for r in range(n_rep)], axis=1)


def _store_outputs(o_ref, o, n_rep, tok_major):
    if not tok_major:
        o_ref[...] = o
        return
    tq = o.shape[1] // n_rep
    o_ref[...] = jnp.concatenate([o[:, r * tq:(r + 1) * tq] for r in range(n_rep)], axis=0).T


def _query_layout(q, tq, n_groups):
    if q.ndim == 4:
        BG, nq, dh, rows = q.shape
        return q, False, BG, nq, rows // tq, pl.BlockSpec((None, None, dh, rows), lambda i, j: (i, j, 0, 0))
    B, T, n = q.shape
    G = n_groups
    n_rep = n // HEAD_DIM // G
    nq = T // tq
    spec = pl.BlockSpec((tq, n_rep * HEAD_DIM), lambda i, j: ((i // G) * nq + j, i % G))
    return q.reshape(B * T, n), True, B * G, nq, n_rep, spec


def _cmp_select_kernel(tq, n_rep, q_base, n_sel, tok_major, q_ref, kc_ref, vct_ref, ovt_ref, oc_ref, bias_ref):
    qi = pl.program_id(1)
    rows = n_rep * tq
    ncp = kc_ref.shape[0]
    nr = ovt_ref.shape[0]
    nbp = bias_ref.shape[0]
    q = _load_queries(q_ref, n_rep, tok_major) * HEAD_DIM ** -0.5
    s = _mm(kc_ref[...], q)
    t_row = q_base + qi * tq + lax.broadcasted_iota(jnp.int32, (1, rows), 1) % tq
    c_end = lax.broadcasted_iota(jnp.int32, (ncp, 1), 0) * CMP_STRIDE + (CMP_LEN - 1)
    s = jnp.where(c_end <= t_row, s, NEG_INF)
    m = jnp.max(s, axis=0, keepdims=True)
    e = jnp.exp(s - jnp.where(m == NEG_INF, 0.0, m))
    p = e / jnp.maximum(jnp.sum(e, axis=0, keepdims=True), 1e-30)
    _store_outputs(oc_ref, _mm(vct_ref[...], p), n_rep, tok_major)
    p_sum = p[:, 0:tq]
    for r in range(1, n_rep):
        p_sum = p_sum + p[:, r * tq:(r + 1) * tq]
    imp = _mm(ovt_ref[...], p_sum, exact=True)
    cur = (q_base + qi * tq + lax.broadcasted_iota(jnp.int32, (1, tq), 1)) // SEL_BLOCK
    blk = lax.broadcasted_iota(jnp.int32, (nr, 1), 0)
    forced = jnp.where(blk == cur, 1.0, jnp.where(blk == cur - 1, 1.0, jnp.where(blk == 0, 1.0, 0.0)))
    score = jnp.where(blk <= cur, imp + FORCE_BONUS * forced, NEG_INF)
    rank = _rank_rows(score, blk, n_sel)
    bias = jnp.where(blk <= cur, jnp.where(rank < min(SEL_TOPN, n_sel), 0.0, NEG_BIG), NEG_BIG)
    if nbp > nr:
        bias = jnp.concatenate([bias, jnp.zeros((nbp - nr, tq), F32)], axis=0)
    bias_ref[...] = bias.astype(BF16)


def _cmp_select(q, kc, vct, ovt, tq, q_base, n_sel, nbp):
    qt, tok_major, BG, nq, n_rep, qspec = _query_layout(q, tq, NSA_KV_HEADS)
    ncp, dh = kc.shape[1:]
    return pl.pallas_call(
        functools.partial(_cmp_select_kernel, tq, n_rep, q_base, n_sel, tok_major), grid=(BG, nq),
        in_specs=[qspec, pl.BlockSpec((None, ncp, dh), lambda i, j: (i, 0, 0)),
                  pl.BlockSpec((None, dh, ncp), lambda i, j: (i, 0, 0)),
                  pl.BlockSpec(ovt.shape, lambda i, j: (0, 0))],
        out_specs=[qspec, pl.BlockSpec((None, None, nbp, tq), lambda i, j: (i, j, 0, 0))],
        out_shape=[jax.ShapeDtypeStruct(qt.shape, F32), jax.ShapeDtypeStruct((BG, nq, nbp, tq), BF16)],
        compiler_params=_params("parallel", "parallel"), name="nsa_cmp_select",
    )(qt, kc, vct, ovt)


def _overlap_matrix_t(nc, nsel, ncp, nr):
    i = jnp.arange(nc, dtype=jnp.int32)[None, :]
    j = jnp.arange(nsel, dtype=jnp.int32)[:, None]
    lo = jnp.maximum(i * CMP_STRIDE, j * SEL_BLOCK)
    hi = jnp.minimum(i * CMP_STRIDE + CMP_LEN, (j + 1) * SEL_BLOCK)
    ov = jnp.clip(hi - lo, 0).astype(F32) / CMP_LEN
    return jnp.pad(ov, ((0, nr - nsel), (0, ncp - nc)))


def _means_kernel(n_blk, k_ref, o_ref):
    o_ref[...] = jnp.zeros(o_ref.shape, F32)
    for n in range(n_blk):
        o_ref[n:n + 1, :] = jnp.sum(k_ref[n * MOBA_BLOCK:(n + 1) * MOBA_BLOCK, :], axis=0,
                                    keepdims=True) / MOBA_BLOCK


def _block_means(k, n_blk, nbp):
    BG, Lp, dh = k.shape
    return pl.pallas_call(
        functools.partial(_means_kernel, n_blk), grid=(BG,),
        in_specs=[pl.BlockSpec((None, Lp, dh), lambda i: (i, 0, 0))],
        out_specs=pl.BlockSpec((None, nbp, dh), lambda i: (i, 0, 0)),
        out_shape=jax.ShapeDtypeStruct((BG, nbp, dh), F32), compiler_params=_params("parallel"), name="moba_means",
    )(k)


def _flash_kernel(mode, tq, n_rep, tk, q_base, k_base, n_blk, tok_major, *refs):
    if mode == "sel":
        q_ref, ke_ref, vt_ref, bias_ref, o_ref, lhs_scr, m_scr, acc_scr = refs
    elif mode == "win":
        q_ref, ke_ref, vt_ref, o_ref, lhs_scr, m_scr, acc_scr = refs
    else:
        q_ref, ke_ref, vt_ref, mean_ref, o_ref, lhs_scr, m_scr, acc_scr = refs
    qi = pl.program_id(1)
    rows = n_rep * tq
    q0 = q_base + qi * tq
    t_row = q0 + lax.broadcasted_iota(jnp.int32, (1, rows), 1) % tq
    q = _load_queries(q_ref, n_rep, tok_major) * HEAD_DIM ** -0.5
    lhs_scr[0:HEAD_DIM, :] = q.astype(BF16)
    if mode == "sel":
        lhs_scr[HEAD_DIM:, :] = jnp.concatenate([bias_ref[...]] * n_rep, axis=1)
    elif mode == "moba":
        nr = _round_up(n_blk, SUBLANES)
        nbp = mean_ref.shape[0]
        gate = _mm(mean_ref[0:nr, :], q, exact=True)
        blk = lax.broadcasted_iota(jnp.int32, (nr, 1), 0)
        cur = t_row // MOBA_BLOCK
        gate = jnp.where(blk < cur, gate, NEG_INF)
        rank = _rank_rows(gate, blk, n_blk)
        picked = jnp.where(blk < cur, jnp.where(rank < min(MOBA_TOPK, n_blk), 0.0, NEG_BIG), NEG_BIG)
        bias = jnp.where(blk == cur, 0.0, picked)
        if nbp > nr:
            bias = jnp.concatenate([bias, jnp.zeros((nbp - nr, rows), F32)], axis=0)
        lhs_scr[HEAD_DIM:, :] = bias.astype(BF16)
    m_scr[...] = jnp.full(m_scr.shape, NEG_BIG, F32)
    acc_scr[...] = jnp.zeros(acc_scr.shape, F32)

    def tile(ki, positional):
        k0 = pl.multiple_of(ki * tk, tk)
        s = jnp.dot(ke_ref[pl.ds(k0, tk), :], lhs_scr[...], preferred_element_type=F32)
        if positional:
            kpos = k_base + k0 + lax.broadcasted_iota(jnp.int32, (tk, 1), 0)
            s = jnp.where(kpos <= t_row, s, NEG_BIG)
            if mode == "win":
                s = jnp.where(kpos > t_row - WINDOW, s, NEG_BIG)
        m_prev = m_scr[...]
        m_new = jnp.maximum(m_prev, jnp.max(s, axis=0, keepdims=True))
        p = jnp.exp(s - m_new)
        acc_scr[...] = jnp.exp(m_prev - m_new) * acc_scr[...] + jnp.dot(
            vt_ref[:, pl.ds(k0, tk)], p.astype(BF16), preferred_element_type=F32)
        m_scr[...] = m_new

    def body(positional):
        def f(ki, carry):
            tile(ki, positional)
            return carry
        return f

    last = (q0 + tq - 1 - k_base) // tk
    if mode == "win":
        lax.fori_loop(jnp.maximum(q0 - (WINDOW - 1) - k_base, 0) // tk, last + 1, body(True), 0)
    else:
        lax.fori_loop(0, last, body(False), 0)
        tile(last, True)
    acc = acc_scr[...]
    _store_outputs(o_ref, acc[0:HEAD_DIM] / jnp.maximum(acc[HEAD_DIM:HEAD_DIM + 1], 1e-30), n_rep, tok_major)


def _flash(mode, q, ke, vt, extra, tq, q_base, k_base, n_blk, n_groups, tk=ATT_K_TILE):
    qt, tok_major, BG, nq, n_rep, qspec = _query_layout(q, tq, n_groups)
    dh = HEAD_DIM
    rows = n_rep * tq
    Lp, kw = ke.shape[1:]
    assert Lp % tk == 0 and (q_base + nq * tq - 1 - k_base) // tk < Lp // tk
    assert tk % tq == 0 and (q_base - k_base) % tq == 0
    ins = [qt, ke, vt]
    in_specs = [qspec, pl.BlockSpec((None, Lp, kw), lambda i, j: (i, 0, 0)),
                pl.BlockSpec((None, V_ROWS, Lp), lambda i, j: (i, 0, 0))]
    if mode == "sel":
        ins.append(extra)
        in_specs.append(pl.BlockSpec((None, None, kw - dh, tq), lambda i, j: (i, j, 0, 0)))
    if mode == "moba":
        ins.append(extra)
        in_specs.append(pl.BlockSpec((None, kw - dh, dh), lambda i, j: (i, 0, 0)))
    scratch = [pltpu.VMEM((kw, rows), BF16), pltpu.VMEM((1, rows), F32), pltpu.VMEM((V_ROWS, rows), F32)]
    return pl.pallas_call(
        functools.partial(_flash_kernel, mode, tq, n_rep, tk, q_base, k_base, n_blk, tok_major), grid=(BG, nq),
        in_specs=in_specs, out_specs=qspec, out_shape=jax.ShapeDtypeStruct(qt.shape, F32),
        scratch_shapes=scratch, compiler_params=_params("parallel", "parallel"), name="flash_" + mode,
    )(*ins)


def _group_queries_t(q, n_groups, tq):
    B, T, n = q.shape
    R = n // HEAD_DIM // n_groups
    q = q.reshape(B, T // tq, tq, n_groups, R, HEAD_DIM).transpose(0, 3, 1, 5, 4, 2)
    return q.reshape(B * n_groups, T // tq, HEAD_DIM, R * tq)


def _ungroup_t(o, B, n_groups, tq):
    BG, nq, dh, rows = o.shape
    R = rows // tq
    o = o.reshape(B, n_groups, nq, dh, R, tq).transpose(0, 2, 5, 1, 4, 3)
    return o.reshape(B, nq * tq, n_groups * R * dh)


def _head_major(x, Lp):
    B, L, G, dh = x.shape
    x = jnp.pad(x.transpose(0, 2, 1, 3), ((0, 0), (0, 0), (0, Lp - L), (0, 0)))
    return x.reshape(B * G, Lp, dh)


def _key_operand(x, Lp, block=None, nbp=0, first_pos=0):
    k = _head_major(x, Lp).astype(BF16)
    if block is None:
        return k
    onehot = ((first_pos + jnp.arange(Lp))[:, None] // block == jnp.arange(nbp)[None, :]).astype(BF16)
    return jnp.concatenate([k, jnp.broadcast_to(onehot[None], (k.shape[0], Lp, nbp))], axis=-1)


def _value_operand(x, Lp):
    B, L, G, dh = x.shape
    v = jnp.pad(x.transpose(0, 2, 3, 1), ((0, 0), (0, 0), (0, 0), (0, Lp - L))).reshape(B * G, dh, Lp)
    return jnp.concatenate([v, jnp.ones((B * G, V_ROWS - dh, Lp), F32)], axis=1).astype(BF16)


def _bias_rows(n):
    return _round_up(HEAD_DIM + n, LANES) - HEAD_DIM


def _sel_blocks(n_pos):
    n_sel = -(-n_pos // SEL_BLOCK)
    return n_sel, _bias_rows(n_sel)


def _nsa_branches(q, n_q, cmp_rows, n_rows, ke_sel, vt_sel, win_kv, cw, tq, q_base, win_base, tk):
    G = NSA_KV_HEADS
    assert cmp_rows.shape[2] // CMP_STRIDE == n_rows // CMP_STRIDE
    kvc = _nsa_compress(cmp_rows, *cw)
    n_sel, nbp = _sel_blocks(max(n_rows, q_base + n_q))
    ovt = _overlap_matrix_t(n_rows // CMP_STRIDE - 1, n_sel, kvc.shape[2], _round_up(n_sel, SUBLANES))
    oc, bias = _cmp_select(q, kvc[0], kvc[1].transpose(0, 2, 1), ovt, tq, q_base, n_sel, nbp)
    o_s = _flash("sel", q, ke_sel, vt_sel, bias, tq, q_base, 0, n_sel, G, tk)
    Lwp = _round_up(max(win_kv.shape[1], q_base + n_q - win_base), ATT_K_TILE)
    o_w = _flash("win", q, _key_operand(win_kv[:, :, 0], Lwp), _value_operand(win_kv[:, :, 1], Lwp), None, tq,
                 q_base, win_base, 0, G)
    return oc, o_s, o_w


def _nsa_prompt(q, kv6, cw, tq):
    B, T = q.shape[:2]
    cmp_rows = jnp.stack([kv6[:, :, 0], kv6[:, :, 1]]).transpose(0, 1, 3, 2, 4).reshape(2, -1, T, HEAD_DIM)
    _, nbp = _sel_blocks(T)
    Lp = _round_up(T, ATT_K_TILE)
    return _nsa_branches(q, T, cmp_rows, T, _key_operand(kv6[:, :, 2], Lp, SEL_BLOCK, nbp),
                         _value_operand(kv6[:, :, 3], Lp), kv6[:, :, 4:6], cw, tq, 0, 0, ATT_K_TILE)


def _nsa_sample(q, kv6, cache, page_table, win_buf, cw, tq):
    Bs, Ts = q.shape[:2]
    page = cache.shape[-1]
    past_len = page_table.shape[1] * page
    step = PAGES_PER_STEP * page
    assert past_len % CMP_STRIDE == 0 and Ts < CMP_STRIDE and tq <= step
    _, nbp = _sel_blocks(past_len + tq)
    tail_ke = _key_operand(kv6[:, :, 2], step, SEL_BLOCK, nbp, past_len)
    tail_vt = _value_operand(kv6[:, :, 3], step)
    rows, ke, vt, _ = _gather_kv(cache, page_table, NSA_KV_HEADS, (0, 1), 2, 3, SEL_BLOCK, tail_ke, tail_vt, False)
    win = jnp.concatenate([win_buf, kv6[:, :, 4:6]], axis=1)
    qt = _group_queries_t(_pad_queries(q, tq), NSA_KV_HEADS, tq)
    outs = _nsa_branches(qt, tq, rows, past_len + Ts, ke, vt, win, cw, tq, past_len, past_len - win_buf.shape[1],
                         step)
    return [_ungroup_t(o, Bs, NSA_KV_HEADS, tq)[:, :Ts] for o in outs], win[:, Ts:]


def _moba_prompt(q, kv, tq):
    B, T, _ = q.shape
    n_blk = -(-T // MOBA_BLOCK)
    nbp = _bias_rows(n_blk)
    Lp = _round_up(T, ATT_K_TILE)
    means = _block_means(_head_major(kv[:, :, 0], Lp), n_blk, nbp)
    return _flash("moba", q, _key_operand(kv[:, :, 0], Lp, MOBA_BLOCK, nbp), _value_operand(kv[:, :, 1], Lp), means,
                  tq, 0, 0, n_blk, MOBA_KV_HEADS)


def _moba_sample(q, kv, cache, page_table, tq):
    Bs, Ts = q.shape[:2]
    page = cache.shape[-1]
    past_len = page_table.shape[1] * page
    step = PAGES_PER_STEP * page
    assert step % MOBA_BLOCK == 0 and tq <= MOBA_BLOCK
    n_blk = -(-(past_len + tq) // MOBA_BLOCK)
    nbp = _bias_rows(n_blk)
    tail_ke = _key_operand(kv[:, :, 0], step, MOBA_BLOCK, nbp, past_len)
    tail_vt = _value_operand(kv[:, :, 1], step)
    _, ke, vt, means = _gather_kv(cache, page_table, MOBA_KV_HEADS, (), 0, 1, MOBA_BLOCK, tail_ke, tail_vt, True)
    means = jnp.pad(means, ((0, 0), (0, max(nbp - means.shape[1], 0)), (0, 0)))[:, :nbp]
    qt = _group_queries_t(_pad_queries(q, tq), MOBA_KV_HEADS, tq)
    o = _flash("moba", qt, ke, vt, means, tq, past_len, 0, n_blk, MOBA_KV_HEADS, step)
    return _ungroup_t(o, Bs, MOBA_KV_HEADS, tq)[:, :Ts]


def _gather_kv_kernel(n_groups, row_slots, k_slot, v_slot, block, with_means, pt_ref, *refs):
    del pt_ref
    G = n_groups
    dh = HEAD_DIM
    pps = PAGES_PER_STEP
    pages = refs[:pps]
    tail_ke_ref, tail_vt_ref = refs[pps:pps + 2]
    outs = list(refs[pps + 2:])
    rows_ref = outs.pop(0) if row_slots else None
    ke_ref, vt_ref = outs[0], outs[1]
    means_ref = outs[2] if with_means else None
    p = pl.program_id(1)
    last = pl.num_programs(1) - 1
    page = pages[0].shape[-1]
    step, kw = ke_ref.shape[1:]

    @pl.when(p < last)
    def _():
        lane = lax.broadcasted_iota(jnp.int32, (step, kw), 1)
        row = p * step + lax.broadcasted_iota(jnp.int32, (step, kw), 0)
        onehot = jnp.where(lane - dh == row // block, 1.0, 0.0).astype(BF16)
        for g in range(G):
            ke_ref[g] = onehot
        vt_ref[:, dh:, :] = jnp.ones((G, V_ROWS - dh, step), BF16)
        ppb = block // page
        for g in range(G):
            tot = None
            for j, pg in enumerate(pages):
                r0 = j * page
                for si, s in enumerate(row_slots):
                    rows_ref[si, g, r0:r0 + page, :] = pg[s, g].T
                k = pg[k_slot, g].T
                ke_ref[g, r0:r0 + page, 0:dh] = k.astype(BF16)
                vt_ref[g, 0:dh, r0:r0 + page] = pg[v_slot, g].astype(BF16)
                if with_means:
                    ksum = jnp.sum(k, axis=0, keepdims=True)
                    tot = ksum if j % ppb == 0 else tot + ksum
                    if j % ppb == ppb - 1:
                        n = j // ppb
                        means_ref[g, n:n + 1, :] = tot / block

    @pl.when(p == last)
    def _():
        ke_ref[...] = tail_ke_ref[...]
        vt_ref[...] = tail_vt_ref[...]
        if with_means:
            means_ref[...] = jnp.zeros(means_ref.shape, F32)


def _gather_kv(cache, page_table, n_groups, row_slots, k_slot, v_slot, block, tail_ke, tail_vt, with_means):
    _, n_slots, G, dh, page = cache.shape
    Bs, n_pages = page_table.shape
    assert G == n_groups and dh == HEAD_DIM
    pps = PAGES_PER_STEP
    step = pps * page
    assert n_pages % pps == 0
    if with_means:
        assert step % block == 0 and block % page == 0
    n_steps = n_pages // pps
    kw = tail_ke.shape[-1]
    Lp = (n_steps + 1) * step
    bps = step // block

    def page_map(j):
        return lambda b, p, pt: (pt[b, jnp.minimum(p, n_steps - 1) * pps + j], 0, 0, 0, 0)

    in_specs = [pl.BlockSpec((None, n_slots, G, dh, page), page_map(j)) for j in range(pps)]
    in_specs += [pl.BlockSpec((None, G, step, kw), lambda b, p, pt: (b, 0, 0, 0)),
                 pl.BlockSpec((None, G, V_ROWS, step), lambda b, p, pt: (b, 0, 0, 0))]
    out_shape, out_specs = [], []
    if row_slots:
        n_rs = len(row_slots)
        out_shape.append(jax.ShapeDtypeStruct((n_rs, Bs, G, n_steps * step, dh), F32))
        out_specs.append(pl.BlockSpec((n_rs, None, G, step, dh),
                                      lambda b, p, pt: (0, b, 0, jnp.minimum(p, n_steps - 1), 0)))
    out_shape += [jax.ShapeDtypeStruct((Bs, G, Lp, kw), BF16), jax.ShapeDtypeStruct((Bs, G, V_ROWS, Lp), BF16)]
    out_specs += [pl.BlockSpec((None, G, step, kw), lambda b, p, pt: (b, 0, p, 0)),
                  pl.BlockSpec((None, G, V_ROWS, step), lambda b, p, pt: (b, 0, 0, p))]
    if with_means:
        out_shape.append(jax.ShapeDtypeStruct((Bs, G, n_steps + 1, bps, dh), F32))
        out_specs.append(pl.BlockSpec((None, G, None, bps, dh), lambda b, p, pt: (b, 0, p, 0, 0)))
    grid_spec = pltpu.PrefetchScalarGridSpec(num_scalar_prefetch=1, grid=(Bs, n_steps + 1), in_specs=in_specs,
                                             out_specs=out_specs)
    outs = list(pl.pallas_call(
        functools.partial(_gather_kv_kernel, G, tuple(row_slots), k_slot, v_slot, block, with_means),
        grid_spec=grid_spec, out_shape=out_shape, compiler_params=_params("parallel", "arbitrary"),
        name="gather_kv",
    )(page_table, *([cache] * pps), tail_ke.reshape(Bs, G, step, kw), tail_vt.reshape(Bs, G, V_ROWS, step)))
    rows = outs.pop(0).reshape(len(row_slots), Bs * G, n_steps * step, dh) if row_slots else None
    ke = outs[0].reshape(Bs * G, Lp, kw)
    vt = outs[1].reshape(Bs * G, V_ROWS, Lp)
    means = outs[2].reshape(Bs * G, (n_steps + 1) * bps, dh) if with_means else None
    return rows, ke, vt, means


def _pad_queries(x, tq):
    return jnp.pad(x, ((0, 0), (0, tq - x.shape[1]), (0, 0)))


def kernel(x_prompt, x_sample, cache_nsa_kv, cache_moba_kv, state_win_kv, state_wkv, state_shift, page_table, norm_mix, norm_ffn, norm_final, even_w_in, even_w_out, rwkv_mu, rwkv_w0, rwkv_w_up, rwkv_a0, rwkv_a_up, rwkv_g_up, rwkv_k_k, rwkv_k_a, rwkv_r_k, rwkv_ln_g, rwkv_ln_b, nsa_gate_b, nsa_cmp_w1, nsa_cmp_pe, nsa_cmp_w2, odd_w_in, odd_w_out, ffn_w_gate, ffn_w_up, ffn_w_down):
    B, T, D = x_prompt.shape
    Bs, Ts, _ = x_sample.shape
    depth = norm_mix.shape[0]
    page = cache_nsa_kv.shape[2]
    past_len = page_table.shape[1] * page
    rwkv_dim = rwkv_w0.shape[1]
    rwkv_cols = rwkv_mu.shape[1]
    nsa_heads = nsa_gate_b.shape[1] // 3
    nsa_dim = nsa_heads * HEAD_DIM
    nsa_kv_cols = 6 * NSA_KV_HEADS * HEAD_DIM
    moba_kv_cols = 2 * MOBA_KV_HEADS * HEAD_DIM
    moba_dim = odd_w_in.shape[2] - moba_kv_cols
    tq_p = min(ATT_Q_TILE, T)
    tq_s = SAMPLE_Q_PAD

    cos_p, sin_p = _rope_tables(jnp.arange(T, dtype=jnp.int32))
    pos_s = past_len + jnp.arange(Ts, dtype=jnp.int32)
    cos_s, sin_s = _rope_tables(jnp.tile(pos_s, Bs))

    even_spec = ((rwkv_cols, (), False), (nsa_dim, tuple(range(nsa_dim // LANES)), False),
                 (nsa_kv_cols, tuple(range(0, nsa_kv_cols // LANES, 2)), False), (LANES, (), True))
    k_chunks = MOBA_KV_HEADS * HEAD_DIM // LANES
    odd_spec = ((moba_dim, tuple(range(moba_dim // LANES)), False), (moba_kv_cols, tuple(range(k_chunks)), False))

    hp = x_prompt.reshape(B * T, D)
    hs = x_sample.reshape(Bs * Ts, D)
    nsa_p, nsa_s, moba_p, moba_s = [], [], [], []
    win_p, win_s, wkv_p, wkv_s, sh_p, sh_s = [], [], [], [], [], []
    for layer in range(depth):
        i = layer // 2
        if layer % 2 == 0:
            w_in = even_w_in[i].astype(BF16)
            o = rwkv_cols
            n_gate = 3 * nsa_heads
            weights = [w_in[:, :o], w_in[:, o:o + nsa_dim], w_in[:, o + nsa_dim:o + nsa_dim + nsa_kv_cols],
                       jnp.pad(w_in[:, o + nsa_dim + nsa_kv_cols:], ((0, 0), (0, LANES - n_gate)))]
            gate_b = jnp.pad(nsa_gate_b[i], (0, LANES - n_gate)).reshape(1, LANES)
            rp = (rwkv_mu[i], rwkv_w0[i], rwkv_w_up[i], rwkv_a0[i], rwkv_a_up[i], rwkv_g_up[i],
                  rwkv_k_k[i], rwkv_k_a[i], rwkv_r_k[i], rwkv_ln_g[i], rwkv_ln_b[i])
            cw = (nsa_cmp_w1[i], nsa_cmp_pe[i], nsa_cmp_w2[i])

            rw, q, kv, gates = _project(hp, norm_mix[layer], cos_p, sin_p, weights, [gate_b], even_spec, "even_proj")
            rw3 = rw.reshape(B, T, rwkv_cols)
            kv6 = kv.reshape(B, T, 6, NSA_KV_HEADS, HEAD_DIM)
            y, g, wkv_new = _rwkv_mix(rw3, jnp.zeros((B, rwkv_cols), F32),
                                      jnp.zeros((B, rwkv_dim // HEAD_DIM, HEAD_DIM, HEAD_DIM), F32), rp, SCAN_CHUNK)
            o_c, o_s, o_w = _nsa_prompt(q.reshape(B, T, nsa_dim), kv6, cw, tq_p)
            hp = _even_out(hp, y, g, o_c.reshape(B * T, nsa_dim), o_s.reshape(B * T, nsa_dim),
                           o_w.reshape(B * T, nsa_dim), gates, even_w_out[i])
            nsa_p.append(kv6[:, :, :4])
            win_p.append(kv6[:, T - min(WINDOW, T):, 4:6])
            wkv_p.append(wkv_new)
            sh_p.append(rw3[:, -1])

            rw, q, kv, gates = _project(hs, norm_mix[layer], cos_s, sin_s, weights, [gate_b], even_spec, "even_proj_s")
            rw3 = rw.reshape(Bs, Ts, rwkv_cols)
            kv6 = kv.reshape(Bs, Ts, 6, NSA_KV_HEADS, HEAD_DIM)
            y, g, wkv_new = _rwkv_mix(rw3, state_shift[i], state_wkv[i], rp, SUBLANES)
            outs, win_new = _nsa_sample(q.reshape(Bs, Ts, nsa_dim), kv6,
                                        cache_nsa_kv[i].transpose(0, 2, 3, 4, 1), page_table,
                                        state_win_kv[i], cw, tq_s)
            o_c, o_s, o_w = [t.reshape(Bs * Ts, nsa_dim) for t in outs]
            hs = _even_out(hs, y, g, o_c, o_s, o_w, gates, even_w_out[i])
            nsa_s.append(kv6[:, :, :4])
            win_s.append(win_new)
            wkv_s.append(wkv_new)
            sh_s.append(rw3[:, -1])
        else:
            w_in = odd_w_in[i].astype(BF16)
            weights = [w_in[:, :moba_dim], w_in[:, moba_dim:]]
            q, kv = _project(hp, norm_mix[layer], cos_p, sin_p, weights, [], odd_spec, "odd_proj")
            kv2 = kv.reshape(B, T, 2, MOBA_KV_HEADS, HEAD_DIM)
            a = _moba_prompt(q.reshape(B, T, moba_dim), kv2, tq_p)
            hp = _odd_out(hp, a.reshape(B * T, moba_dim), odd_w_out[i])
            moba_p.append(kv2)

            q, kv = _project(hs, norm_mix[layer], cos_s, sin_s, weights, [], odd_spec, "odd_proj_s")
            kv2 = kv.reshape(Bs, Ts, 2, MOBA_KV_HEADS, HEAD_DIM)
            a = _moba_sample(q.reshape(Bs, Ts, moba_dim), kv2, cache_moba_kv[i].transpose(0, 2, 3, 4, 1),
                             page_table, tq_s)
            hs = _odd_out(hs, a.reshape(Bs * Ts, moba_dim), odd_w_out[i])
            moba_s.append(kv2)
        g_final = norm_final if layer == depth - 1 else None
        hp = _ffn(hp, norm_ffn[layer], ffn_w_gate[layer], ffn_w_up[layer], ffn_w_down[layer], g_final)
        hs = _ffn(hs, norm_ffn[layer], ffn_w_gate[layer], ffn_w_up[layer], ffn_w_down[layer], g_final)
    return (hp.reshape(B, T, D), hs.reshape(Bs, Ts, D), jnp.stack(nsa_p), jnp.stack(nsa_s), jnp.stack(moba_p),
            jnp.stack(moba_s), jnp.stack(win_p), jnp.stack(win_s), jnp.stack(wkv_p), jnp.stack(wkv_s),
            jnp.stack(sh_p), jnp.stack(sh_s))
```

```python
import functools
import math

import jax
import jax.numpy as jnp
from jax import lax
from jax.experimental import pallas as pl
from jax.experimental.pallas import tpu as pltpu

F32 = jnp.float32
BF16 = jnp.bfloat16
HIGHEST = lax.Precision.HIGHEST

HEAD_DIM = 64
NORM_EPS = 1e-6
ROPE_THETA = 10000.0
DECAY_LORA = 64
AAA_LORA = 64
GATE_LORA = 128
RWKV_GN_EPS = 64e-5
NSA_KV_HEADS = 2
CMP_STRIDE = 16
CMP_LEN = 2 * CMP_STRIDE
SEL_BLOCK = 64
SEL_TOPN = 16
WINDOW = 512
FORCE_BONUS = 100.0
MOBA_KV_HEADS = 4
MOBA_BLOCK = 256
MOBA_TOPK = 3

LANES = 128
SUBLANES = 8
VMEM_LIMIT = 56 * 1024 * 1024

ROW_TILE = 512
FFN_ROW_TILE = 1024
FFN_COL_TILE = 256
ATT_Q_TILE = 256
ATT_K_TILE = 512
SCAN_CHUNK = 64
SCAN_CHUNKS_PER_STEP = 4
SAMPLE_Q_PAD = 32
PAGES_PER_STEP = 8

NT = (((1,), (1,)), ((), ()))
TN = (((0,), (0,)), ((), ()))
NEG_INF = float("-inf")
NEG_BIG = -1e30
V_ROWS = HEAD_DIM + 16


def _round_up(x, m):
    return -(-x // m) * m


def _mm(a, b, dims=None, exact=False):
    if dims is None:
        dims = (((a.ndim - 1,), (0,)), ((), ()))
    if exact:
        return lax.dot_general(a.astype(F32), b.astype(F32), dims, precision=HIGHEST,
                               preferred_element_type=F32)
    return lax.dot_general(a.astype(BF16), b.astype(BF16), dims, preferred_element_type=F32)


def _params(*sem):
    return pltpu.CompilerParams(dimension_semantics=sem, vmem_limit_bytes=VMEM_LIMIT)


def _rms_norm(x, g):
    return x * lax.rsqrt(jnp.mean(x * x, axis=-1, keepdims=True) + NORM_EPS) * g


def _rope_chunk(x, cos, sin):
    lane = lax.broadcasted_iota(jnp.int32, x.shape, 1)
    half = HEAD_DIM // 2
    partner = jnp.where((lane % HEAD_DIM) < half, pltpu.roll(x, LANES - half, 1), pltpu.roll(x, half, 1))
    return x * cos + partner * sin


def _rank_rows(score, blk, ncand):
    rank = jnp.zeros(score.shape, F32)
    for j in range(ncand):
        row = score[j:j + 1, :]
        rank = rank + jnp.where(row > score, 1.0, 0.0) + jnp.where(row == score, jnp.where(blk > j, 1.0, 0.0), 0.0)
    return rank


def _proj_kernel(spec, x_ref, g_ref, cos_ref, sin_ref, *refs):
    nseg = len(spec)
    w_refs = refs[:nseg]
    nbias = sum(1 for s in spec if s[2])
    b_refs = list(refs[nseg:nseg + nbias])
    o_refs = refs[nseg + nbias:]
    xn = _rms_norm(x_ref[...], g_ref[...]).astype(BF16)
    for (ncols, rope_chunks, sig), w_ref, o_ref in zip(spec, w_refs, o_refs):
        y = jnp.dot(xn, w_ref[...], preferred_element_type=F32)
        if sig:
            y = jax.nn.sigmoid(y + b_refs.pop(0)[...])
        if rope_chunks:
            cos = cos_ref[...]
            sin = sin_ref[...]
            for c in range(ncols // LANES):
                yc = y[:, c * LANES:(c + 1) * LANES]
                if c in rope_chunks:
                    yc = _rope_chunk(yc, cos, sin)
                o_ref[:, c * LANES:(c + 1) * LANES] = yc
        else:
            o_ref[...] = y


def _project(x, g, cos_tab, sin_tab, weights, biases, spec, name):
    M, D = x.shape
    tm = min(ROW_TILE, M)
    ntab = cos_tab.shape[0] // tm
    in_specs = [pl.BlockSpec((tm, D), lambda i: (i, 0)),
                pl.BlockSpec((1, D), lambda i: (0, 0)),
                pl.BlockSpec((tm, LANES), lambda i: (i % ntab, 0)),
                pl.BlockSpec((tm, LANES), lambda i: (i % ntab, 0))]
    in_specs += [pl.BlockSpec(w.shape, lambda i: (0, 0)) for w in weights]
    in_specs += [pl.BlockSpec(b.shape, lambda i: (0, 0)) for b in biases]
    out_shape = [jax.ShapeDtypeStruct((M, s[0]), F32) for s in spec]
    out_specs = [pl.BlockSpec((tm, s[0]), lambda i: (i, 0)) for s in spec]
    return pl.pallas_call(
        functools.partial(_proj_kernel, spec), grid=(M // tm,), in_specs=in_specs, out_specs=out_specs,
        out_shape=out_shape, compiler_params=_params("parallel"), name=name,
    )(x, g.reshape(1, D), cos_tab, sin_tab, *weights, *biases)


def _rope_tables(pos):
    half = HEAD_DIM // 2
    inv = ROPE_THETA ** (-jnp.arange(half, dtype=F32) / half)
    ang = pos.astype(F32)[:, None] * inv[None, :]
    cos = jnp.cos(ang)
    sin = jnp.sin(ang)
    cos_t = jnp.tile(cos, (1, LANES // half))
    sin_t = jnp.tile(jnp.concatenate([-sin, sin], axis=1), (1, LANES // HEAD_DIM))
    return cos_t, sin_t


def _even_out_kernel(res_ref, y_ref, g_ref, oc_ref, os_ref, ow_ref, gt_ref, ec_ref, es_ref, ew_ref,
                     w1_ref, w2_ref, o_ref):
    a = y_ref[...] * g_ref[...]
    gt = gt_ref[...]
    b = (_mm(gt, ec_ref[...], exact=True) * oc_ref[...] + _mm(gt, es_ref[...], exact=True) * os_ref[...]
         + _mm(gt, ew_ref[...], exact=True) * ow_ref[...])
    o_ref[...] = res_ref[...] + _mm(a, w1_ref[...]) + _mm(b, w2_ref[...])


def _even_out(res, y, g, o_c, o_s, o_w, gates, w_out):
    M, D = res.shape
    n_rw = y.shape[1]
    n_nsa = o_c.shape[1]
    tm = min(ROW_TILE, M)
    heads = n_nsa // HEAD_DIM
    col = jnp.arange(n_nsa)[None, :] // HEAD_DIM
    row = jnp.arange(LANES)[:, None]
    expand = [(row == col * 3 + br).astype(F32) for br in range(3)]
    del heads
    row_spec = lambda n: pl.BlockSpec((tm, n), lambda i: (i, 0))
    full = lambda a: pl.BlockSpec(a.shape, lambda i: (0, 0))
    w1 = w_out[:n_rw].astype(BF16)
    w2 = w_out[n_rw:].astype(BF16)
    return pl.pallas_call(
        _even_out_kernel, grid=(M // tm,),
        in_specs=[row_spec(D), row_spec(n_rw), row_spec(n_rw), row_spec(n_nsa), row_spec(n_nsa), row_spec(n_nsa),
                  row_spec(LANES), full(expand[0]), full(expand[1]), full(expand[2]), full(w1), full(w2)],
        out_specs=row_spec(D), out_shape=jax.ShapeDtypeStruct((M, D), F32),
        compiler_params=_params("parallel"), name="even_out",
    )(res, y, g, o_c, o_s, o_w, gates, *expand, w1, w2)


def _odd_out_kernel(res_ref, a_ref, w_ref, o_ref):
    o_ref[...] = res_ref[...] + _mm(a_ref[...], w_ref[...])


def _odd_out(res, a, w_out):
    M, D = res.shape
    tm = min(ROW_TILE, M)
    w = w_out.astype(BF16)
    return pl.pallas_call(
        _odd_out_kernel, grid=(M // tm,),
        in_specs=[pl.BlockSpec((tm, D), lambda i: (i, 0)), pl.BlockSpec((tm, a.shape[1]), lambda i: (i, 0)),
                  pl.BlockSpec(w.shape, lambda i: (0, 0))],
        out_specs=pl.BlockSpec((tm, D), lambda i: (i, 0)), out_shape=jax.ShapeDtypeStruct((M, D), F32),
        compiler_params=_params("parallel"), name="odd_out",
    )(res, a, w)


def _ffn_kernel(final_norm, x_ref, g_ref, wg_ref, wu_ref, wd_ref, gf_ref, o_ref, xn_scr, acc_scr):
    j = pl.program_id(1)

    @pl.when(j == 0)
    def _():
        xn_scr[...] = _rms_norm(x_ref[...], g_ref[...]).astype(BF16)
        acc_scr[...] = jnp.zeros(acc_scr.shape, F32)

    xn = xn_scr[...]
    h = jax.nn.silu(jnp.dot(xn, wg_ref[...], preferred_element_type=F32)) * jnp.dot(
        xn, wu_ref[...], preferred_element_type=F32)
    acc_scr[...] += _mm(h, wd_ref[...])

    @pl.when(j == pl.num_programs(1) - 1)
    def _():
        y = x_ref[...] + acc_scr[...]
        if final_norm:
            y = _rms_norm(y, gf_ref[...])
        o_ref[...] = y


def _ffn(x, g, wg, wu, wd, g_final=None):
    M, D = x.shape
    F = wg.shape[1]
    tm = min(FFN_ROW_TILE, M)
    tf = FFN_COL_TILE
    final_norm = g_final is not None
    gf = (g_final if final_norm else g).reshape(1, D)
    return pl.pallas_call(
        functools.partial(_ffn_kernel, final_norm), grid=(M // tm, F // tf),
        in_specs=[pl.BlockSpec((tm, D), lambda i, j: (i, 0)), pl.BlockSpec((1, D), lambda i, j: (0, 0)),
                  pl.BlockSpec((D, tf), lambda i, j: (0, j)), pl.BlockSpec((D, tf), lambda i, j: (0, j)),
                  pl.BlockSpec((tf, D), lambda i, j: (j, 0)), pl.BlockSpec((1, D), lambda i, j: (0, 0))],
        out_specs=pl.BlockSpec((tm, D), lambda i, j: (i, 0)), out_shape=jax.ShapeDtypeStruct((M, D), F32),
        scratch_shapes=[pltpu.VMEM((tm, D), BF16), pltpu.VMEM((tm, D), F32)],
        compiler_params=_params("parallel", "arbitrary"), name="ffn",
    )(x, g.reshape(1, D), wg.astype(BF16), wu.astype(BF16), wd.astype(BF16), gf)


def _rwkv_pre_kernel(n_dim, rw_ref, prev_ref, mu_ref, w0_ref, a0_ref, kk_ref, ka_ref, wup_ref, aup_ref, gup_ref,
                     hsum_ref, r_out, lw_out, k_out, v_out, kk_out, b_out, g_out):
    rw = rw_ref[...]
    xm = rw + (prev_ref[...] - rw) * mu_ref[...]
    r = xm[:, :n_dim]
    k = xm[:, n_dim:2 * n_dim]
    v = xm[:, 2 * n_dim:3 * n_dim]
    lora = xm[:, 3 * n_dim:3 * n_dim + DECAY_LORA + AAA_LORA]
    xg = xm[:, 3 * n_dim + DECAY_LORA + AAA_LORA:]
    lw = -math.exp(-0.5) * jax.nn.sigmoid(w0_ref[...] + _mm(jnp.tanh(lora), wup_ref[...], exact=True))
    a = jax.nn.sigmoid(a0_ref[...] + _mm(lora, aup_ref[...], exact=True))
    g = _mm(jax.nn.sigmoid(xg), gup_ref[...])
    kk = k * kk_ref[...]
    norm = jnp.sqrt(_mm(kk * kk, hsum_ref[...], exact=True))
    kk = kk / jnp.maximum(norm, 1e-12)
    g_out[...] = g
    outs = ((r_out, r), (lw_out, lw), (k_out, k * (1.0 + (a - 1.0) * ka_ref[...])), (v_out, v), (kk_out, kk),
            (b_out, kk * a))
    for o_ref, val in outs:
        if len(o_ref.shape) == 2:
            o_ref[...] = val
        else:
            for h in range(o_ref.shape[0]):
                o_ref[h] = val[:, h * HEAD_DIM:(h + 1) * HEAD_DIM]


def _rwkv_pre(rw, prev, mu, w0, w_up, a0, a_up, g_up, k_k, k_a, seq_len):
    M, ncols = rw.shape
    n_dim = w0.shape[0]
    tm = min(ROW_TILE, M)
    H = n_dim // HEAD_DIM
    head_major = seq_len % tm == 0
    nt = seq_len // tm if head_major else 1
    zeros = jnp.zeros((AAA_LORA, n_dim), F32)
    wup_pad = jnp.concatenate([w_up, zeros], axis=0)
    aup_pad = jnp.concatenate([jnp.zeros((DECAY_LORA, n_dim), F32), a_up], axis=0)
    head = jnp.arange(n_dim) // HEAD_DIM
    hsum = (head[:, None] == head[None, :]).astype(F32)
    vec = lambda a: a.reshape(1, -1)
    row = lambda n: pl.BlockSpec((tm, n), lambda i: (i, 0))
    full = lambda a: pl.BlockSpec(a.shape, lambda i: (0, 0))
    ins = [rw, prev, vec(mu), vec(w0), vec(a0), vec(k_k), vec(k_a), wup_pad, aup_pad, g_up, hsum]
    if head_major:
        seq_spec = pl.BlockSpec((None, H, tm, HEAD_DIM), lambda i: (i // nt, 0, i % nt, 0))
        seq_shape = jax.ShapeDtypeStruct((M // seq_len, H, seq_len, HEAD_DIM), F32)
    else:
        seq_spec, seq_shape = row(n_dim), jax.ShapeDtypeStruct((M, n_dim), F32)
    outs = pl.pallas_call(
        functools.partial(_rwkv_pre_kernel, n_dim), grid=(M // tm,),
        in_specs=[row(ncols), row(ncols)] + [full(a) for a in ins[2:]],
        out_specs=[seq_spec] * 6 + [row(n_dim)],
        out_shape=[seq_shape] * 6 + [jax.ShapeDtypeStruct((M, n_dim), F32)],
        compiler_params=_params("parallel"), name="rwkv_pre",
    )(*ins)
    if head_major:
        outs = [t.reshape(-1, seq_len, HEAD_DIM) for t in outs[:6]] + [outs[6]]
    return outs, head_major


def _rwkv_chunk_kernel(n_heads, chunk, r_ref, lw_ref, k_ref, v_ref, kk_ref, b_ref, rk_ref,
                       rw_ref, y0_ref, bonus_ref, a_ref, s1_ref):
    C = chunk
    dh = HEAD_DIM
    ti = lax.broadcasted_iota(jnp.int32, (C, C), 0)
    si = lax.broadcasted_iota(jnp.int32, (C, C), 1)
    incl = jnp.where(si <= ti, 1.0, 0.0)
    eye = jnp.where(lax.broadcasted_iota(jnp.int32, (dh, dh), 0) == lax.broadcasted_iota(jnp.int32, (dh, dh), 1),
                    1.0, 0.0)
    n_double = max(1, math.ceil(math.log2(C)))
    heads = range(n_heads)
    r = [r_ref[h] for h in heads]
    lw = [lw_ref[h] for h in heads]
    k = [k_ref[h] for h in heads]
    v = [v_ref[h] for h in heads]
    b = [b_ref[h] for h in heads]
    cum = [_mm(incl, lw[h], exact=True) for h in heads]
    cum_end = [cum[h][C - 1:C, :] for h in heads]
    kkw = [kk_ref[h] * jnp.exp(cum[h] - lw[h]) for h in heads]
    rwc = [r[h] * jnp.exp(cum[h]) for h in heads]
    w_inv = [jnp.exp(-cum[h]) for h in heads]
    kd = [k[h] * w_inv[h] for h in heads]
    bd = [b[h] * w_inv[h] for h in heads]
    w_end = [jnp.exp(cum_end[h] - cum[h]) for h in heads]
    a_ub = [jnp.where(si < ti, _mm(kkw[h], bd[h], NT), 0.0) for h in heads]
    a_vk = [jnp.where(si < ti, _mm(kkw[h], kd[h], NT), 0.0) for h in heads]
    b_rb = [jnp.where(si <= ti, _mm(rwc[h], bd[h], NT), 0.0) for h in heads]
    b_rk = [jnp.where(si <= ti, _mm(rwc[h], kd[h], NT), 0.0) for h in heads]
    x = [jnp.concatenate([kkw[h], _mm(a_vk[h], v[h])], axis=1) for h in heads]
    p = [-a_ub[h] for h in heads]
    x = [x[h] + _mm(p[h], x[h]) for h in heads]
    for _ in range(n_double - 1):
        p = [_mm(p[h], p[h]) for h in heads]
        x = [x[h] + _mm(p[h], x[h]) for h in heads]
    for h in heads:
        bx = _mm(b_rb[h], x[h])
        rw_ref[h] = rwc[h] - bx[:, :dh]
        y0_ref[h] = _mm(b_rk[h], v[h]) - bx[:, dh:]
        kw = x[h][:, :dh]
        uv = x[h][:, dh:]
        a_ref[h] = eye * jnp.exp(cum_end[h]) - _mm(kw, b[h] * w_end[h], TN)
        s1_ref[h] = _mm(v[h], k[h] * w_end[h], TN) - _mm(uv, b[h] * w_end[h], TN)
        bonus_ref[h] = jnp.sum(r[h] * k[h] * rk_ref[h:h + 1, :], axis=-1, keepdims=True) * v[h]


def _rwkv_state_kernel(n_heads, n_sub, rw_ref, y0_ref, bonus_ref, a_ref, s1_ref, s0_ref, lng_ref, lnb_ref,
                       y_ref, s_out_ref, s_scr):
    c = pl.program_id(1)

    @pl.when(c == 0)
    def _():
        s_scr[...] = s0_ref[...]

    chunk = rw_ref.shape[2]
    for h in range(n_heads):
        s = s_scr[h]
        for j in range(n_sub):
            y = _mm(rw_ref[h, j], s, NT) + y0_ref[h, j]
            s = _mm(s, a_ref[h, j]) + s1_ref[h, j]
            mean = jnp.mean(y, axis=-1, keepdims=True)
            var = jnp.mean(jnp.square(y - mean), axis=-1, keepdims=True)
            yn = (y - mean) * lax.rsqrt(var + RWKV_GN_EPS) * lng_ref[h:h + 1, :] + lnb_ref[h:h + 1, :]
            y_ref[j * chunk:(j + 1) * chunk, h * HEAD_DIM:(h + 1) * HEAD_DIM] = yn + bonus_ref[h, j]
        s_scr[h] = s

    @pl.when(c == pl.num_programs(1) - 1)
    def _():
        s_out_ref[...] = s_scr[...]


def _rwkv_scan(r, lw, k, v, kk, b, s0, r_k, ln_g, ln_b, chunk):
    n_bh, T, dh = r.shape
    H = r_k.shape[0]
    n_chunks = T // chunk
    seq = pl.BlockSpec((H, chunk, dh), lambda i, c: (i, c, 0))
    mat = pl.BlockSpec((H, None, dh, dh), lambda i, c: (i, c, 0, 0))
    par = pl.BlockSpec((H, dh), lambda i, c: (0, 0))
    seq_shape = jax.ShapeDtypeStruct((n_bh, T, dh), F32)
    mat_shape = jax.ShapeDtypeStruct((n_bh, n_chunks, dh, dh), F32)
    rw, y0, bonus, a, s1 = pl.pallas_call(
        functools.partial(_rwkv_chunk_kernel, H, chunk), grid=(n_bh // H, n_chunks),
        in_specs=[seq] * 6 + [par], out_specs=[seq, seq, seq, mat, mat],
        out_shape=[seq_shape, seq_shape, seq_shape, mat_shape, mat_shape],
        compiler_params=_params("parallel", "parallel"), name="rwkv_chunk",
    )(r, lw, k, v, kk, b, r_k)
    n_sub = math.gcd(SCAN_CHUNKS_PER_STEP, n_chunks)
    split = lambda t: t.reshape(n_bh, n_chunks, chunk, dh)
    seq4 = pl.BlockSpec((H, n_sub, chunk, dh), lambda i, c: (i, c, 0, 0))
    mat4 = pl.BlockSpec((H, n_sub, dh, dh), lambda i, c: (i, c, 0, 0))
    state = pl.BlockSpec((H, dh, dh), lambda i, c: (i, 0, 0))
    y, s_new = pl.pallas_call(
        functools.partial(_rwkv_state_kernel, H, n_sub), grid=(n_bh // H, n_chunks // n_sub),
        in_specs=[seq4, seq4, seq4, mat4, mat4, state, par, par],
        out_specs=[pl.BlockSpec((n_sub * chunk, H * dh), lambda i, c: (i * (n_chunks // n_sub) + c, 0)), state],
        out_shape=[jax.ShapeDtypeStruct((n_bh // H * T, H * dh), F32), jax.ShapeDtypeStruct((n_bh, dh, dh), F32)],
        scratch_shapes=[pltpu.VMEM((H, dh, dh), F32)],
        compiler_params=_params("parallel", "arbitrary"), name="rwkv_state",
    )(split(rw), split(y0), split(bonus), a, s1, s0, ln_g.reshape(H, dh), ln_b.reshape(H, dh))
    return y, s_new


def _rwkv_mix(rw, shift_prev, wkv0, p, chunk):
    mu, w0, w_up, a0, a_up, g_up, k_k, k_a, r_k, ln_g, ln_b = p
    B, T, ncols = rw.shape
    H = r_k.shape[0]
    n_dim = H * HEAD_DIM
    prev = jnp.concatenate([shift_prev[:, None], rw[:, :-1]], axis=1)
    outs, head_major = _rwkv_pre(rw.reshape(B * T, ncols), prev.reshape(B * T, ncols), mu, w0, w_up, a0, a_up, g_up,
                                 k_k, k_a, T)
    g = outs[6]
    Tp = _round_up(T, chunk)

    def heads(t):
        if not head_major:
            t = t.reshape(B, T, H, HEAD_DIM).transpose(0, 2, 1, 3).reshape(B * H, T, HEAD_DIM)
        return jnp.pad(t, ((0, 0), (0, Tp - T), (0, 0)))

    y, s_new = _rwkv_scan(*[heads(t) for t in outs[:6]], wkv0.reshape(B * H, HEAD_DIM, HEAD_DIM), r_k, ln_g, ln_b,
                          chunk)
    y = y.reshape(B, Tp, n_dim)[:, :T].reshape(B * T, n_dim)
    return y, g, s_new.reshape(B, H, HEAD_DIM, HEAD_DIM)


def _compress_kernel(x_ref, w1_ref, pe_ref, w2_ref, o_ref):
    nch, dh = o_ref.shape
    w1 = w1_ref[...]
    bias = _mm(pe_ref[...], w1)[0:1, :]
    first = jnp.zeros((nch, dh), F32)
    second = jnp.zeros((nch, dh), F32)
    for c in range(CMP_STRIDE):
        xc = x_ref[pl.ds(c, nch, stride=CMP_STRIDE), :]
        first = first + _mm(xc, w1[c * dh:(c + 1) * dh])
        second = second + _mm(xc, w1[(CMP_STRIDE + c) * dh:(CMP_STRIDE + c + 1) * dh])
    h = jax.nn.gelu(first + pltpu.roll(second, nch - 1, 0) + bias)
    o_ref[...] = _mm(h, w2_ref[...])


def _nsa_compress(kv_cmp, w1, pe, w2):
    _, BG, L, dh = kv_cmp.shape
    nch = L // CMP_STRIDE
    x = kv_cmp[:, :, :nch * CMP_STRIDE]
    kdim = CMP_LEN * dh
    pe8 = jnp.pad(pe.reshape(2, 1, kdim), ((0, 0), (0, SUBLANES - 1), (0, 0)))
    return pl.pallas_call(
        _compress_kernel, grid=(2, BG),
        in_specs=[pl.BlockSpec((None, None, nch * CMP_STRIDE, dh), lambda s, i: (s, i, 0, 0)),
                  pl.BlockSpec((None, kdim, dh), lambda s, i: (s, 0, 0)),
                  pl.BlockSpec((None, SUBLANES, kdim), lambda s, i: (s, 0, 0)),
                  pl.BlockSpec((None, dh, dh), lambda s, i: (s, 0, 0))],
        out_specs=pl.BlockSpec((None, None, nch, dh), lambda s, i: (s, i, 0, 0)),
        out_shape=jax.ShapeDtypeStruct((2, BG, nch, dh), F32),
        compiler_params=_params("parallel", "parallel"), name="nsa_compress",
    )(x, w1.reshape(2, kdim, dh), pe8, w2)


def _load_queries(q_ref, n_rep, tok_major):
    if not tok_major:
        return q_ref[...]
    qt = q_ref[...].T
    return jnp.concatenate([qt[r * HEAD_DIM:(r + 1) * HEAD_DIM] for r in range(n_rep)], axis=1)


def _store_outputs(o_ref, o, n_rep, tok_major):
    if not tok_major:
        o_ref[...] = o
        return
    tq = o.shape[1] // n_rep
    o_ref[...] = jnp.concatenate([o[:, r * tq:(r + 1) * tq] for r in range(n_rep)], axis=0).T


def _query_layout(q, tq, n_groups):
    if q.ndim == 4:
        BG, nq, dh, rows = q.shape
        return q, False, BG, nq, rows // tq, pl.BlockSpec((None, None, dh, rows), lambda i, j: (i, j, 0, 0))
    B, T, n = q.shape
    G = n_groups
    n_rep = n // HEAD_DIM // G
    nq = T // tq
    spec = pl.BlockSpec((tq, n_rep * HEAD_DIM), lambda i, j: ((i // G) * nq + j, i % G))
    return q.reshape(B * T, n), True, B * G, nq, n_rep, spec


def _cmp_select_kernel(tq, n_rep, q_base, n_sel, tok_major, q_ref, kc_ref, vct_ref, ovt_ref, oc_ref, bias_ref):
    qi = pl.program_id(1)
    rows = n_rep * tq
    ncp = kc_ref.shape[0]
    nr = ovt_ref.shape[0]
    nbp = bias_ref.shape[0]
    q = _load_queries(q_ref, n_rep, tok_major) * HEAD_DIM ** -0.5
    s = _mm(kc_ref[...], q)
    t_row = q_base + qi * tq + lax.broadcasted_iota(jnp.int32, (1, rows), 1) % tq
    c_end = lax.broadcasted_iota(jnp.int32, (ncp, 1), 0) * CMP_STRIDE + (CMP_LEN - 1)
    s = jnp.where(c_end <= t_row, s, NEG_INF)
    m = jnp.max(s, axis=0, keepdims=True)
    e = jnp.exp(s - jnp.where(m == NEG_INF, 0.0, m))
    p = e / jnp.maximum(jnp.sum(e, axis=0, keepdims=True), 1e-30)
    _store_outputs(oc_ref, _mm(vct_ref[...], p), n_rep, tok_major)
    p_sum = p[:, 0:tq]
    for r in range(1, n_rep):
        p_sum = p_sum + p[:, r * tq:(r + 1) * tq]
    imp = _mm(ovt_ref[...], p_sum, exact=True)
    cur = (q_base + qi * tq + lax.broadcasted_iota(jnp.int32, (1, tq), 1)) // SEL_BLOCK
    blk = lax.broadcasted_iota(jnp.int32, (nr, 1), 0)
    forced = jnp.where(blk == cur, 1.0, jnp.where(blk == cur - 1, 1.0, jnp.where(blk == 0, 1.0, 0.0)))
    score = jnp.where(blk <= cur, imp + FORCE_BONUS * forced, NEG_INF)
    rank = _rank_rows(score, blk, n_sel)
    bias = jnp.where(blk <= cur, jnp.where(rank < min(SEL_TOPN, n_sel), 0.0, NEG_BIG), NEG_BIG)
    if nbp > nr:
        bias = jnp.concatenate([bias, jnp.zeros((nbp - nr, tq), F32)], axis=0)
    bias_ref[...] = bias.astype(BF16)


def _cmp_select(q, kc, vct, ovt, tq, q_base, n_sel, nbp):
    qt, tok_major, BG, nq, n_rep, qspec = _query_layout(q, tq, NSA_KV_HEADS)
    ncp, dh = kc.shape[1:]
    return pl.pallas_call(
        functools.partial(_cmp_select_kernel, tq, n_rep, q_base, n_sel, tok_major), grid=(BG, nq),
        in_specs=[qspec, pl.BlockSpec((None, ncp, dh), lambda i, j: (i, 0, 0)),
                  pl.BlockSpec((None, dh, ncp), lambda i, j: (i, 0, 0)),
                  pl.BlockSpec(ovt.shape, lambda i, j: (0, 0))],
        out_specs=[qspec, pl.BlockSpec((None, None, nbp, tq), lambda i, j: (i, j, 0, 0))],
        out_shape=[jax.ShapeDtypeStruct(qt.shape, F32), jax.ShapeDtypeStruct((BG, nq, nbp, tq), BF16)],
        compiler_params=_params("parallel", "parallel"), name="nsa_cmp_select",
    )(qt, kc, vct, ovt)


def _overlap_matrix_t(nc, nsel, ncp, nr):
    i = jnp.arange(nc, dtype=jnp.int32)[None, :]
    j = jnp.arange(nsel, dtype=jnp.int32)[:, None]
    lo = jnp.maximum(i * CMP_STRIDE, j * SEL_BLOCK)
    hi = jnp.minimum(i * CMP_STRIDE + CMP_LEN, (j + 1) * SEL_BLOCK)
    ov = jnp.clip(hi - lo, 0).astype(F32) / CMP_LEN
    return jnp.pad(ov, ((0, nr - nsel), (0, ncp - nc)))


def _means_kernel(n_blk, k_ref, o_ref):
    o_ref[...] = jnp.zeros(o_ref.shape, F32)
    for n in range(n_blk):
        o_ref[n:n + 1, :] = jnp.sum(k_ref[n * MOBA_BLOCK:(n + 1) * MOBA_BLOCK, :], axis=0,
                                    keepdims=True) / MOBA_BLOCK


def _block_means(k, n_blk, nbp):
    BG, Lp, dh = k.shape
    return pl.pallas_call(
        functools.partial(_means_kernel, n_blk), grid=(BG,),
        in_specs=[pl.BlockSpec((None, Lp, dh), lambda i: (i, 0, 0))],
        out_specs=pl.BlockSpec((None, nbp, dh), lambda i: (i, 0, 0)),
        out_shape=jax.ShapeDtypeStruct((BG, nbp, dh), F32), compiler_params=_params("parallel"), name="moba_means",
    )(k)


def _flash_kernel(mode, tq, n_rep, tk, q_base, k_base, n_blk, tok_major, *refs):
    if mode == "sel":
        q_ref, ke_ref, vt_ref, bias_ref, o_ref, lhs_scr, m_scr, acc_scr = refs
    elif mode == "win":
        q_ref, ke_ref, vt_ref, o_ref, lhs_scr, m_scr, acc_scr = refs
    else:
        q_ref, ke_ref, vt_ref, mean_ref, o_ref, lhs_scr, m_scr, acc_scr = refs
    qi = pl.program_id(1)
    rows = n_rep * tq
    q0 = q_base + qi * tq
    t_row = q0 + lax.broadcasted_iota(jnp.int32, (1, rows), 1) % tq
    q = _load_queries(q_ref, n_rep, tok_major) * HEAD_DIM ** -0.5
    lhs_scr[0:HEAD_DIM, :] = q.astype(BF16)
    if mode == "sel":
        lhs_scr[HEAD_DIM:, :] = jnp.concatenate([bias_ref[...]] * n_rep, axis=1)
    elif mode == "moba":
        nr = _round_up(n_blk, SUBLANES)
        nbp = mean_ref.shape[0]
        gate = _mm(mean_ref[0:nr, :], q, exact=True)
        blk = lax.broadcasted_iota(jnp.int32, (nr, 1), 0)
        cur = t_row // MOBA_BLOCK
        gate = jnp.where(blk < cur, gate, NEG_INF)
        rank = _rank_rows(gate, blk, n_blk)
        picked = jnp.where(blk < cur, jnp.where(rank < min(MOBA_TOPK, n_blk), 0.0, NEG_BIG), NEG_BIG)
        bias = jnp.where(blk == cur, 0.0, picked)
        if nbp > nr:
            bias = jnp.concatenate([bias, jnp.zeros((nbp - nr, rows), F32)], axis=0)
        lhs_scr[HEAD_DIM:, :] = bias.astype(BF16)
    m_scr[...] = jnp.full(m_scr.shape, NEG_BIG, F32)
    acc_scr[...] = jnp.zeros(acc_scr.shape, F32)

    def tile(ki, positional):
        k0 = pl.multiple_of(ki * tk, tk)
        s = jnp.dot(ke_ref[pl.ds(k0, tk), :], lhs_scr[...], preferred_element_type=F32)
        if positional:
            kpos = k_base + k0 + lax.broadcasted_iota(jnp.int32, (tk, 1), 0)
            s = jnp.where(kpos <= t_row, s, NEG_BIG)
            if mode == "win":
                s = jnp.where(kpos > t_row - WINDOW, s, NEG_BIG)
        m_prev = m_scr[...]
        m_new = jnp.maximum(m_prev, jnp.max(s, axis=0, keepdims=True))
        p = jnp.exp(s - m_new)
        acc_scr[...] = jnp.exp(m_prev - m_new) * acc_scr[...] + jnp.dot(
            vt_ref[:, pl.ds(k0, tk)], p.astype(BF16), preferred_element_type=F32)
        m_scr[...] = m_new

    def body(positional):
        def f(ki, carry):
            tile(ki, positional)
            return carry
        return f

    last = (q0 + tq - 1 - k_base) // tk
    if mode == "win":
        lax.fori_loop(jnp.maximum(q0 - (WINDOW - 1) - k_base, 0) // tk, last + 1, body(True), 0)
    else:
        lax.fori_loop(0, last, body(False), 0)
        tile(last, True)
    acc = acc_scr[...]
    _store_outputs(o_ref, acc[0:HEAD_DIM] / jnp.maximum(acc[HEAD_DIM:HEAD_DIM + 1], 1e-30), n_rep, tok_major)


def _flash(mode, q, ke, vt, extra, tq, q_base, k_base, n_blk, n_groups, tk=ATT_K_TILE):
    qt, tok_major, BG, nq, n_rep, qspec = _query_layout(q, tq, n_groups)
    dh = HEAD_DIM
    rows = n_rep * tq
    Lp, kw = ke.shape[1:]
    assert Lp % tk == 0 and (q_base + nq * tq - 1 - k_base) // tk < Lp // tk
    assert tk % tq == 0 and (q_base - k_base) % tq == 0
    ins = [qt, ke, vt]
    in_specs = [qspec, pl.BlockSpec((None, Lp, kw), lambda i, j: (i, 0, 0)),
                pl.BlockSpec((None, V_ROWS, Lp), lambda i, j: (i, 0, 0))]
    if mode == "sel":
        ins.append(extra)
        in_specs.append(pl.BlockSpec((None, None, kw - dh, tq), lambda i, j: (i, j, 0, 0)))
    if mode == "moba":
        ins.append(extra)
        in_specs.append(pl.BlockSpec((None, kw - dh, dh), lambda i, j: (i, 0, 0)))
    scratch = [pltpu.VMEM((kw, rows), BF16), pltpu.VMEM((1, rows), F32), pltpu.VMEM((V_ROWS, rows), F32)]
    return pl.pallas_call(
        functools.partial(_flash_kernel, mode, tq, n_rep, tk, q_base, k_base, n_blk, tok_major), grid=(BG, nq),
        in_specs=in_specs, out_specs=qspec, out_shape=jax.ShapeDtypeStruct(qt.shape, F32),
        scratch_shapes=scratch, compiler_params=_params("parallel", "parallel"), name="flash_" + mode,
    )(*ins)


def _group_queries_t(q, n_groups, tq):
    B, T, n = q.shape
    R = n // HEAD_DIM // n_groups
    q = q.reshape(B, T // tq, tq, n_groups, R, HEAD_DIM).transpose(0, 3, 1, 5, 4, 2)
    return q.reshape(B * n_groups, T // tq, HEAD_DIM, R * tq)


def _ungroup_t(o, B, n_groups, tq):
    BG, nq, dh, rows = o.shape
    R = rows // tq
    o = o.reshape(B, n_groups, nq, dh, R, tq).transpose(0, 2, 5, 1, 4, 3)
    return o.reshape(B, nq * tq, n_groups * R * dh)


def _head_major(x, Lp):
    B, L, G, dh = x.shape
    x = jnp.pad(x.transpose(0, 2, 1, 3), ((0, 0), (0, 0), (0, Lp - L), (0, 0)))
    return x.reshape(B * G, Lp, dh)


def _key_operand(x, Lp, block=None, nbp=0, first_pos=0):
    k = _head_major(x, Lp).astype(BF16)
    if block is None:
        return k
    onehot = ((first_pos + jnp.arange(Lp))[:, None] // block == jnp.arange(nbp)[None, :]).astype(BF16)
    return jnp.concatenate([k, jnp.broadcast_to(onehot[None], (k.shape[0], Lp, nbp))], axis=-1)


def _value_operand(x, Lp):
    B, L, G, dh = x.shape
    v = jnp.pad(x.transpose(0, 2, 3, 1), ((0, 0), (0, 0), (0, 0), (0, Lp - L))).reshape(B * G, dh, Lp)
    return jnp.concatenate([v, jnp.ones((B * G, V_ROWS - dh, Lp), F32)], axis=1).astype(BF16)


def _bias_rows(n):
    return _round_up(HEAD_DIM + n, LANES) - HEAD_DIM


def _sel_blocks(n_pos):
    n_sel = -(-n_pos // SEL_BLOCK)
    return n_sel, _bias_rows(n_sel)


def _nsa_branches(q, n_q, cmp_rows, n_rows, ke_sel, vt_sel, win_kv, cw, tq, q_base, win_base, tk):
    G = NSA_KV_HEADS
    assert cmp_rows.shape[2] // CMP_STRIDE == n_rows // CMP_STRIDE
    kvc = _nsa_compress(cmp_rows, *cw)
    n_sel, nbp = _sel_blocks(max(n_rows, q_base + n_q))
    ovt = _overlap_matrix_t(n_rows // CMP_STRIDE - 1, n_sel, kvc.shape[2], _round_up(n_sel, SUBLANES))
    oc, bias = _cmp_select(q, kvc[0], kvc[1].transpose(0, 2, 1), ovt, tq, q_base, n_sel, nbp)
    o_s = _flash("sel", q, ke_sel, vt_sel, bias, tq, q_base, 0, n_sel, G, tk)
    Lwp = _round_up(max(win_kv.shape[1], q_base + n_q - win_base), ATT_K_TILE)
    o_w = _flash("win", q, _key_operand(win_kv[:, :, 0], Lwp), _value_operand(win_kv[:, :, 1], Lwp), None, tq,
                 q_base, win_base, 0, G)
    return oc, o_s, o_w


def _nsa_prompt(q, kv6, cw, tq):
    B, T = q.shape[:2]
    cmp_rows = jnp.stack([kv6[:, :, 0], kv6[:, :, 1]]).transpose(0, 1, 3, 2, 4).reshape(2, -1, T, HEAD_DIM)
    _, nbp = _sel_blocks(T)
    Lp = _round_up(T, ATT_K_TILE)
    return _nsa_branches(q, T, cmp_rows, T, _key_operand(kv6[:, :, 2], Lp, SEL_BLOCK, nbp),
                         _value_operand(kv6[:, :, 3], Lp), kv6[:, :, 4:6], cw, tq, 0, 0, ATT_K_TILE)


def _nsa_sample(q, kv6, cache, page_table, win_buf, cw, tq):
    Bs, Ts = q.shape[:2]
    page = cache.shape[-1]
    past_len = page_table.shape[1] * page
    step = PAGES_PER_STEP * page
    assert past_len % CMP_STRIDE == 0 and Ts < CMP_STRIDE and tq <= step
    _, nbp = _sel_blocks(past_len + tq)
    tail_ke = _key_operand(kv6[:, :, 2], step, SEL_BLOCK, nbp, past_len)
    tail_vt = _value_operand(kv6[:, :, 3], step)
    rows, ke, vt, _ = _gather_kv(cache, page_table, NSA_KV_HEADS, (0, 1), 2, 3, SEL_BLOCK, tail_ke, tail_vt, False)
    win = jnp.concatenate([win_buf, kv6[:, :, 4:6]], axis=1)
    qt = _group_queries_t(_pad_queries(q, tq), NSA_KV_HEADS, tq)
    outs = _nsa_branches(qt, tq, rows, past_len + Ts, ke, vt, win, cw, tq, past_len, past_len - win_buf.shape[1],
                         step)
    return [_ungroup_t(o, Bs, NSA_KV_HEADS, tq)[:, :Ts] for o in outs], win[:, Ts:]


def _moba_prompt(q, kv, tq):
    B, T, _ = q.shape
    n_blk = -(-T // MOBA_BLOCK)
    nbp = _bias_rows(n_blk)
    Lp = _round_up(T, ATT_K_TILE)
    means = _block_means(_head_major(kv[:, :, 0], Lp), n_blk, nbp)
    return _flash("moba", q, _key_operand(kv[:, :, 0], Lp, MOBA_BLOCK, nbp), _value_operand(kv[:, :, 1], Lp), means,
                  tq, 0, 0, n_blk, MOBA_KV_HEADS)


def _moba_sample(q, kv, cache, page_table, tq):
    Bs, Ts = q.shape[:2]
    page = cache.shape[-1]
    past_len = page_table.shape[1] * page
    step = PAGES_PER_STEP * page
    assert step % MOBA_BLOCK == 0 and tq <= MOBA_BLOCK
    n_blk = -(-(past_len + tq) // MOBA_BLOCK)
    nbp = _bias_rows(n_blk)
    tail_ke = _key_operand(kv[:, :, 0], step, MOBA_BLOCK, nbp, past_len)
    tail_vt = _value_operand(kv[:, :, 1], step)
    _, ke, vt, means = _gather_kv(cache, page_table, MOBA_KV_HEADS, (), 0, 1, MOBA_BLOCK, tail_ke, tail_vt, True)
    means = jnp.pad(means, ((0, 0), (0, max(nbp - means.shape[1], 0)), (0, 0)))[:, :nbp]
    qt = _group_queries_t(_pad_queries(q, tq), MOBA_KV_HEADS, tq)
    o = _flash("moba", qt, ke, vt, means, tq, past_len, 0, n_blk, MOBA_KV_HEADS, step)
    return _ungroup_t(o, Bs, MOBA_KV_HEADS, tq)[:, :Ts]


def _gather_kv_kernel(n_groups, row_slots, k_slot, v_slot, block, with_means, pt_ref, *refs):
    del pt_ref
    G = n_groups
    dh = HEAD_DIM
    pps = PAGES_PER_STEP
    pages = refs[:pps]
    tail_ke_ref, tail_vt_ref = refs[pps:pps + 2]
    outs = list(refs[pps + 2:])
    rows_ref = outs.pop(0) if row_slots else None
    ke_ref, vt_ref = outs[0], outs[1]
    means_ref = outs[2] if with_means else None
    p = pl.program_id(1)
    last = pl.num_programs(1) - 1
    page = pages[0].shape[-1]
    step, kw = ke_ref.shape[1:]

    @pl.when(p < last)
    def _():
        lane = lax.broadcasted_iota(jnp.int32, (step, kw), 1)
        row = p * step + lax.broadcasted_iota(jnp.int32, (step, kw), 0)
        onehot = jnp.where(lane - dh == row // block, 1.0, 0.0).astype(BF16)
        for g in range(G):
            ke_ref[g] = onehot
        vt_ref[:, dh:, :] = jnp.ones((G, V_ROWS - dh, step), BF16)
        ppb = block // page
        for g in range(G):
            tot = None
            for j, pg in enumerate(pages):
                r0 = j * page
                for si, s in enumerate(row_slots):
                    rows_ref[si, g, r0:r0 + page, :] = pg[s, g].T
                k = pg[k_slot, g].T
                ke_ref[g, r0:r0 + page, 0:dh] = k.astype(BF16)
                vt_ref[g, 0:dh, r0:r0 + page] = pg[v_slot, g].astype(BF16)
                if with_means:
                    ksum = jnp.sum(k, axis=0, keepdims=True)
                    tot = ksum if j % ppb == 0 else tot + ksum
                    if j % ppb == ppb - 1:
                        n = j // ppb
                        means_ref[g, n:n + 1, :] = tot / block

    @pl.when(p == last)
    def _():
        ke_ref[...] = tail_ke_ref[...]
        vt_ref[...] = tail_vt_ref[...]
        if with_means:
            means_ref[...] = jnp.zeros(means_ref.shape, F32)


def _gather_kv(cache, page_table, n_groups, row_slots, k_slot, v_slot, block, tail_ke, tail_vt, with_means):
    _, n_slots, G, dh, page = cache.shape
    Bs, n_pages = page_table.shape
    assert G == n_groups and dh == HEAD_DIM
    pps = PAGES_PER_STEP
    step = pps * page
    assert n_pages % pps == 0
    if with_means:
        assert step % block == 0 and block % page == 0
    n_steps = n_pages // pps
    kw = tail_ke.shape[-1]
    Lp = (n_steps + 1) * step
    bps = step // block

    def page_map(j):
        return lambda b, p, pt: (pt[b, jnp.minimum(p, n_steps - 1) * pps + j], 0, 0, 0, 0)

    in_specs = [pl.BlockSpec((None, n_slots, G, dh, page), page_map(j)) for j in range(pps)]
    in_specs += [pl.BlockSpec((None, G, step, kw), lambda b, p, pt: (b, 0, 0, 0)),
                 pl.BlockSpec((None, G, V_ROWS, step), lambda b, p, pt: (b, 0, 0, 0))]
    out_shape, out_specs = [], []
    if row_slots:
        n_rs = len(row_slots)
        out_shape.append(jax.ShapeDtypeStruct((n_rs, Bs, G, n_steps * step, dh), F32))
        out_specs.append(pl.BlockSpec((n_rs, None, G, step, dh),
                                      lambda b, p, pt: (0, b, 0, jnp.minimum(p, n_steps - 1), 0)))
    out_shape += [jax.ShapeDtypeStruct((Bs, G, Lp, kw), BF16), jax.ShapeDtypeStruct((Bs, G, V_ROWS, Lp), BF16)]
    out_specs += [pl.BlockSpec((None, G, step, kw), lambda b, p, pt: (b, 0, p, 0)),
                  pl.BlockSpec((None, G, V_ROWS, step), lambda b, p, pt: (b, 0, 0, p))]
    if with_means:
        out_shape.append(jax.ShapeDtypeStruct((Bs, G, n_steps + 1, bps, dh), F32))
        out_specs.append(pl.BlockSpec((None, G, None, bps, dh), lambda b, p, pt: (b, 0, p, 0, 0)))
    grid_spec = pltpu.PrefetchScalarGridSpec(num_scalar_prefetch=1, grid=(Bs, n_steps + 1), in_specs=in_specs,
                                             out_specs=out_specs)
    outs = list(pl.pallas_call(
        functools.partial(_gather_kv_kernel, G, tuple(row_slots), k_slot, v_slot, block, with_means),
        grid_spec=grid_spec, out_shape=out_shape, compiler_params=_params("parallel", "arbitrary"),
        name="gather_kv",
    )(page_table, *([cache] * pps), tail_ke.reshape(Bs, G, step, kw), tail_vt.reshape(Bs, G, V_ROWS, step)))
    rows = outs.pop(0).reshape(len(row_slots), Bs * G, n_steps * step, dh) if row_slots else None
    ke = outs[0].reshape(Bs * G, Lp, kw)
    vt = outs[1].reshape(Bs * G, V_ROWS, Lp)
    means = outs[2].reshape(Bs * G, (n_steps + 1) * bps, dh) if with_means else None
    return rows, ke, vt, means


def _pad_queries(x, tq):
    return jnp.pad(x, ((0, 0), (0, tq - x.shape[1]), (0, 0)))


def kernel(x_prompt, x_sample, cache_nsa_kv, cache_moba_kv, state_win_kv, state_wkv, state_shift, page_table, norm_mix, norm_ffn, norm_final, even_w_in, even_w_out, rwkv_mu, rwkv_w0, rwkv_w_up, rwkv_a0, rwkv_a_up, rwkv_g_up, rwkv_k_k, rwkv_k_a, rwkv_r_k, rwkv_ln_g, rwkv_ln_b, nsa_gate_b, nsa_cmp_w1, nsa_cmp_pe, nsa_cmp_w2, odd_w_in, odd_w_out, ffn_w_gate, ffn_w_up, ffn_w_down):
    B, T, D = x_prompt.shape
    Bs, Ts, _ = x_sample.shape
    depth = norm_mix.shape[0]
    page = cache_nsa_kv.shape[2]
    past_len = page_table.shape[1] * page
    rwkv_dim = rwkv_w0.shape[1]
    rwkv_cols = rwkv_mu.shape[1]
    nsa_heads = nsa_gate_b.shape[1] // 3
    nsa_dim = nsa_heads * HEAD_DIM
    nsa_kv_cols = 6 * NSA_KV_HEADS * HEAD_DIM
    moba_kv_cols = 2 * MOBA_KV_HEADS * HEAD_DIM
    moba_dim = odd_w_in.shape[2] - moba_kv_cols
    tq_p = min(ATT_Q_TILE, T)
    tq_s = SAMPLE_Q_PAD

    cos_p, sin_p = _rope_tables(jnp.arange(T, dtype=jnp.int32))
    pos_s = past_len + jnp.arange(Ts, dtype=jnp.int32)
    cos_s, sin_s = _rope_tables(jnp.tile(pos_s, Bs))

    even_spec = ((rwkv_cols, (), False), (nsa_dim, tuple(range(nsa_dim // LANES)), False),
                 (nsa_kv_cols, tuple(range(0, nsa_kv_cols // LANES, 2)), False), (LANES, (), True))
    k_chunks = MOBA_KV_HEADS * HEAD_DIM // LANES
    odd_spec = ((moba_dim, tuple(range(moba_dim // LANES)), False), (moba_kv_cols, tuple(range(k_chunks)), False))

    hp = x_prompt.reshape(B * T, D)
    hs = x_sample.reshape(Bs * Ts, D)
    nsa_p, nsa_s, moba_p, moba_s = [], [], [], []
    win_p, win_s, wkv_p, wkv_s, sh_p, sh_s = [], [], [], [], [], []
    for layer in range(depth):
        i = layer // 2
        if layer % 2 == 0:
            w_in = even_w_in[i].astype(BF16)
            o = rwkv_cols
            n_gate = 3 * nsa_heads
            weights = [w_in[:, :o], w_in[:, o:o + nsa_dim], w_in[:, o + nsa_dim:o + nsa_dim + nsa_kv_cols],
                       jnp.pad(w_in[:, o + nsa_dim + nsa_kv_cols:], ((0, 0), (0, LANES - n_gate)))]
            gate_b = jnp.pad(nsa_gate_b[i], (0, LANES - n_gate)).reshape(1, LANES)
            rp = (rwkv_mu[i], rwkv_w0[i], rwkv_w_up[i], rwkv_a0[i], rwkv_a_up[i], rwkv_g_up[i],
                  rwkv_k_k[i], rwkv_k_a[i], rwkv_r_k[i], rwkv_ln_g[i], rwkv_ln_b[i])
            cw = (nsa_cmp_w1[i], nsa_cmp_pe[i], nsa_cmp_w2[i])

            rw, q, kv, gates = _project(hp, norm_mix[layer], cos_p, sin_p, weights, [gate_b], even_spec, "even_proj")
            rw3 = rw.reshape(B, T, rwkv_cols)
            kv6 = kv.reshape(B, T, 6, NSA_KV_HEADS, HEAD_DIM)
            y, g, wkv_new = _rwkv_mix(rw3, jnp.zeros((B, rwkv_cols), F32),
                                      jnp.zeros((B, rwkv_dim // HEAD_DIM, HEAD_DIM, HEAD_DIM), F32), rp, SCAN_CHUNK)
            o_c, o_s, o_w = _nsa_prompt(q.reshape(B, T, nsa_dim), kv6, cw, tq_p)
            hp = _even_out(hp, y, g, o_c.reshape(B * T, nsa_dim), o_s.reshape(B * T, nsa_dim),
                           o_w.reshape(B * T, nsa_dim), gates, even_w_out[i])
            nsa_p.append(kv6[:, :, :4])
            win_p.append(kv6[:, T - min(WINDOW, T):, 4:6])
            wkv_p.append(wkv_new)
            sh_p.append(rw3[:, -1])

            rw, q, kv, gates = _project(hs, norm_mix[layer], cos_s, sin_s, weights, [gate_b], even_spec, "even_proj_s")
            rw3 = rw.reshape(Bs, Ts, rwkv_cols)
            kv6 = kv.reshape(Bs, Ts, 6, NSA_KV_HEADS, HEAD_DIM)
            y, g, wkv_new = _rwkv_mix(rw3, state_shift[i], state_wkv[i], rp, SUBLANES)
            outs, win_new = _nsa_sample(q.reshape(Bs, Ts, nsa_dim), kv6,
                                        cache_nsa_kv[i].transpose(0, 2, 3, 4, 1), page_table,
                                        state_win_kv[i], cw, tq_s)
            o_c, o_s, o_w = [t.reshape(Bs * Ts, nsa_dim) for t in outs]
            hs = _even_out(hs, y, g, o_c, o_s, o_w, gates, even_w_out[i])
            nsa_s.append(kv6[:, :, :4])
            win_s.append(win_new)
            wkv_s.append(wkv_new)
            sh_s.append(rw3[:, -1])
        else:
            w_in = odd_w_in[i].astype(BF16)
            weights = [w_in[:, :moba_dim], w_in[:, moba_dim:]]
            q, kv = _project(hp, norm_mix[layer], cos_p, sin_p, weights, [], odd_spec, "odd_proj")
            kv2 = kv.reshape(B, T, 2, MOBA_KV_HEADS, HEAD_DIM)
            a = _moba_prompt(q.reshape(B, T, moba_dim), kv2, tq_p)
            hp = _odd_out(hp, a.reshape(B * T, moba_dim), odd_w_out[i])
            moba_p.append(kv2)

            q, kv = _project(hs, norm_mix[layer], cos_s, sin_s, weights, [], odd_spec, "odd_proj_s")
            kv2 = kv.reshape(Bs, Ts, 2, MOBA_KV_HEADS, HEAD_DIM)
            a = _moba_sample(q.reshape(Bs, Ts, moba_dim), kv2, cache_moba_kv[i].transpose(0, 2, 3, 4, 1),
                             page_table, tq_s)
            hs = _odd_out(hs, a.reshape(Bs * Ts, moba_dim), odd_w_out[i])
            moba_s.append(kv2)
        g_final = norm_final if layer == depth - 1 else None
        hp = _ffn(hp, norm_ffn[layer], ffn_w_gate[layer], ffn_w_up[layer], ffn_w_down[layer], g_final)
        hs = _ffn(hs, norm_ffn[layer], ffn_w_gate[layer], ffn_w_up[layer], ffn_w_down[layer], g_final)
    return (hp.reshape(B, T, D), hs.reshape(Bs, Ts, D), jnp.stack(nsa_p), jnp.stack(nsa_s), jnp.stack(moba_p),
            jnp.stack(moba_s), jnp.stack(win_p), jnp.stack(win_s), jnp.stack(wkv_p), jnp.stack(wkv_s),
            jnp.stack(sh_p), jnp.stack(sh_s))
```

```python
import functools
import math

import jax
import jax.numpy as jnp
from jax import lax
from jax.experimental import pallas as pl
from jax.experimental.pallas import tpu as pltpu

F32 = jnp.float32
BF16 = jnp.bfloat16
HIGHEST = lax.Precision.HIGHEST

HEAD_DIM = 64
NORM_EPS = 1e-6
ROPE_THETA = 10000.0
DECAY_LORA = 64
AAA_LORA = 64
GATE_LORA = 128
RWKV_GN_EPS = 64e-5
NSA_KV_HEADS = 2
CMP_STRIDE = 16
CMP_LEN = 2 * CMP_STRIDE
SEL_BLOCK = 64
SEL_TOPN = 16
WINDOW = 512
FORCE_BONUS = 100.0
MOBA_KV_HEADS = 4
MOBA_BLOCK = 256
MOBA_TOPK = 3

LANES = 128
SUBLANES = 8
VMEM_LIMIT = 56 * 1024 * 1024

ROW_TILE = 512
FFN_ROW_TILE = 1024
FFN_COL_TILE = 256
ATT_Q_TILE = 256
ATT_K_TILE = 512
SCAN_CHUNK = 64
SCAN_CHUNKS_PER_STEP = 4
SAMPLE_Q_PAD = 32
PAGES_PER_STEP = 8

NT = (((1,), (1,)), ((), ()))
TN = (((0,), (0,)), ((), ()))
NEG_INF = float("-inf")
NEG_BIG = -1e30
V_ROWS = HEAD_DIM + 16


def _round_up(x, m):
    return -(-x // m) * m


def _mm(a, b, dims=None, exact=False):
    if dims is None:
        dims = (((a.ndim - 1,), (0,)), ((), ()))
    if exact:
        return lax.dot_general(a.astype(F32), b.astype(F32), dims, precision=HIGHEST,
                               preferred_element_type=F32)
    return lax.dot_general(a.astype(BF16), b.astype(BF16), dims, preferred_element_type=F32)


def _params(*sem):
    return pltpu.CompilerParams(dimension_semantics=sem, vmem_limit_bytes=VMEM_LIMIT)


def _rms_norm(x, g):
    return x * lax.rsqrt(jnp.mean(x * x, axis=-1, keepdims=True) + NORM_EPS) * g


def _rope_chunk(x, cos, sin):
    lane = lax.broadcasted_iota(jnp.int32, x.shape, 1)
    half = HEAD_DIM // 2
    partner = jnp.where((lane % HEAD_DIM) < half, pltpu.roll(x, LANES - half, 1), pltpu.roll(x, half, 1))
    return x * cos + partner * sin


def _rank_rows(score, blk, ncand):
    rank = jnp.zeros(score.shape, F32)
    for j in range(ncand):
        row = score[..., j:j + 1, :]
        rank = rank + jnp.where(row > score, 1.0, 0.0) + jnp.where(row == score, jnp.where(blk > j, 1.0, 0.0), 0.0)
    return rank


def _proj_kernel(spec, x_ref, g_ref, cos_ref, sin_ref, *refs):
    nseg = len(spec)
    w_refs = refs[:nseg]
    nbias = sum(1 for s in spec if s[2])
    b_refs = list(refs[nseg:nseg + nbias])
    o_refs = refs[nseg + nbias:]
    xn = _rms_norm(x_ref[...], g_ref[...]).astype(BF16)
    for (ncols, rope_chunks, sig), w_ref, o_ref in zip(spec, w_refs, o_refs):
        y = jnp.dot(xn, w_ref[...], preferred_element_type=F32)
        if sig:
            y = jax.nn.sigmoid(y + b_refs.pop(0)[...])
        if rope_chunks:
            cos = cos_ref[...]
            sin = sin_ref[...]
            for c in range(ncols // LANES):
                yc = y[:, c * LANES:(c + 1) * LANES]
                if c in rope_chunks:
                    yc = _rope_chunk(yc, cos, sin)
                o_ref[:, c * LANES:(c + 1) * LANES] = yc
        else:
            o_ref[...] = y


def _project(x, g, cos_tab, sin_tab, weights, biases, spec, name):
    M, D = x.shape
    tm = min(ROW_TILE, M)
    ntab = cos_tab.shape[0] // tm
    in_specs = [pl.BlockSpec((tm, D), lambda i: (i, 0)),
                pl.BlockSpec((1, D), lambda i: (0, 0)),
                pl.BlockSpec((tm, LANES), lambda i: (i % ntab, 0)),
                pl.BlockSpec((tm, LANES), lambda i: (i % ntab, 0))]
    in_specs += [pl.BlockSpec(w.shape, lambda i: (0, 0)) for w in weights]
    in_specs += [pl.BlockSpec(b.shape, lambda i: (0, 0)) for b in biases]
    out_shape = [jax.ShapeDtypeStruct((M, s[0]), F32) for s in spec]
    out_specs = [pl.BlockSpec((tm, s[0]), lambda i: (i, 0)) for s in spec]
    return pl.pallas_call(
        functools.partial(_proj_kernel, spec), grid=(M // tm,), in_specs=in_specs, out_specs=out_specs,
        out_shape=out_shape, compiler_params=_params("parallel"), name=name,
    )(x, g.reshape(1, D), cos_tab, sin_tab, *weights, *biases)


def _rope_tables(pos):
    half = HEAD_DIM // 2
    inv = ROPE_THETA ** (-jnp.arange(half, dtype=F32) / half)
    ang = pos.astype(F32)[:, None] * inv[None, :]
    cos = jnp.cos(ang)
    sin = jnp.sin(ang)
    cos_t = jnp.tile(cos, (1, LANES // half))
    sin_t = jnp.tile(jnp.concatenate([-sin, sin], axis=1), (1, LANES // HEAD_DIM))
    return cos_t, sin_t


def _even_out_kernel(res_ref, y_ref, g_ref, oc_ref, os_ref, ow_ref, gt_ref, ec_ref, es_ref, ew_ref,
                     w1_ref, w2_ref, o_ref):
    a = y_ref[...] * g_ref[...]
    gt = gt_ref[...]
    b = (_mm(gt, ec_ref[...], exact=True) * oc_ref[...] + _mm(gt, es_ref[...], exact=True) * os_ref[...]
         + _mm(gt, ew_ref[...], exact=True) * ow_ref[...])
    o_ref[...] = res_ref[...] + _mm(a, w1_ref[...]) + _mm(b, w2_ref[...])


def _even_out(res, y, g, o_c, o_s, o_w, gates, w_out):
    M, D = res.shape
    n_rw = y.shape[1]
    n_nsa = o_c.shape[1]
    tm = min(ROW_TILE, M)
    heads = n_nsa // HEAD_DIM
    col = jnp.arange(n_nsa)[None, :] // HEAD_DIM
    row = jnp.arange(LANES)[:, None]
    expand = [(row == col * 3 + br).astype(F32) for br in range(3)]
    del heads
    row_spec = lambda n: pl.BlockSpec((tm, n), lambda i: (i, 0))
    full = lambda a: pl.BlockSpec(a.shape, lambda i: (0, 0))
    w1 = w_out[:n_rw].astype(BF16)
    w2 = w_out[n_rw:].astype(BF16)
    return pl.pallas_call(
        _even_out_kernel, grid=(M // tm,),
        in_specs=[row_spec(D), row_spec(n_rw), row_spec(n_rw), row_spec(n_nsa), row_spec(n_nsa), row_spec(n_nsa),
                  row_spec(LANES), full(expand[0]), full(expand[1]), full(expand[2]), full(w1), full(w2)],
        out_specs=row_spec(D), out_shape=jax.ShapeDtypeStruct((M, D), F32),
        compiler_params=_params("parallel"), name="even_out",
    )(res, y, g, o_c, o_s, o_w, gates, *expand, w1, w2)


def _odd_out_kernel(res_ref, a_ref, w_ref, o_ref):
    o_ref[...] = res_ref[...] + _mm(a_ref[...], w_ref[...])


def _odd_out(res, a, w_out):
    M, D = res.shape
    tm = min(ROW_TILE, M)
    w = w_out.astype(BF16)
    return pl.pallas_call(
        _odd_out_kernel, grid=(M // tm,),
        in_specs=[pl.BlockSpec((tm, D), lambda i: (i, 0)), pl.BlockSpec((tm, a.shape[1]), lambda i: (i, 0)),
                  pl.BlockSpec(w.shape, lambda i: (0, 0))],
        out_specs=pl.BlockSpec((tm, D), lambda i: (i, 0)), out_shape=jax.ShapeDtypeStruct((M, D), F32),
        compiler_params=_params("parallel"), name="odd_out",
    )(res, a, w)


def _ffn_kernel(final_norm, x_ref, g_ref, wg_ref, wu_ref, wd_ref, gf_ref, o_ref, xn_scr, acc_scr):
    j = pl.program_id(1)

    @pl.when(j == 0)
    def _():
        xn_scr[...] = _rms_norm(x_ref[...], g_ref[...]).astype(BF16)
        acc_scr[...] = jnp.zeros(acc_scr.shape, F32)

    xn = xn_scr[...]
    h = jax.nn.silu(jnp.dot(xn, wg_ref[...], preferred_element_type=F32)) * jnp.dot(
        xn, wu_ref[...], preferred_element_type=F32)
    acc_scr[...] += _mm(h, wd_ref[...])

    @pl.when(j == pl.num_programs(1) - 1)
    def _():
        y = x_ref[...] + acc_scr[...]
        if final_norm:
            y = _rms_norm(y, gf_ref[...])
        o_ref[...] = y


def _ffn(x, g, wg, wu, wd, g_final=None):
    M, D = x.shape
    F = wg.shape[1]
    tm = min(FFN_ROW_TILE, M)
    tf = FFN_COL_TILE
    final_norm = g_final is not None
    gf = (g_final if final_norm else g).reshape(1, D)
    return pl.pallas_call(
        functools.partial(_ffn_kernel, final_norm), grid=(M // tm, F // tf),
        in_specs=[pl.BlockSpec((tm, D), lambda i, j: (i, 0)), pl.BlockSpec((1, D), lambda i, j: (0, 0)),
                  pl.BlockSpec((D, tf), lambda i, j: (0, j)), pl.BlockSpec((D, tf), lambda i, j: (0, j)),
                  pl.BlockSpec((tf, D), lambda i, j: (j, 0)), pl.BlockSpec((1, D), lambda i, j: (0, 0))],
        out_specs=pl.BlockSpec((tm, D), lambda i, j: (i, 0)), out_shape=jax.ShapeDtypeStruct((M, D), F32),
        scratch_shapes=[pltpu.VMEM((tm, D), BF16), pltpu.VMEM((tm, D), F32)],
        compiler_params=_params("parallel", "arbitrary"), name="ffn",
    )(x, g.reshape(1, D), wg.astype(BF16), wu.astype(BF16), wd.astype(BF16), gf)


def _rwkv_pre_kernel(n_dim, rw_ref, prev_ref, mu_ref, w0_ref, a0_ref, kk_ref, ka_ref, wup_ref, aup_ref, gup_ref,
                     hsum_ref, r_out, lw_out, k_out, v_out, kk_out, b_out, g_out):
    rw = rw_ref[...]
    xm = rw + (prev_ref[...] - rw) * mu_ref[...]
    r = xm[:, :n_dim]
    k = xm[:, n_dim:2 * n_dim]
    v = xm[:, 2 * n_dim:3 * n_dim]
    lora = xm[:, 3 * n_dim:3 * n_dim + DECAY_LORA + AAA_LORA]
    xg = xm[:, 3 * n_dim + DECAY_LORA + AAA_LORA:]
    lw = -math.exp(-0.5) * jax.nn.sigmoid(w0_ref[...] + _mm(jnp.tanh(lora), wup_ref[...], exact=True))
    a = jax.nn.sigmoid(a0_ref[...] + _mm(lora, aup_ref[...], exact=True))
    g = _mm(jax.nn.sigmoid(xg), gup_ref[...])
    kk = k * kk_ref[...]
    norm = jnp.sqrt(_mm(kk * kk, hsum_ref[...], exact=True))
    kk = kk / jnp.maximum(norm, 1e-12)
    g_out[...] = g
    outs = ((r_out, r), (lw_out, lw), (k_out, k * (1.0 + (a - 1.0) * ka_ref[...])), (v_out, v), (kk_out, kk),
            (b_out, kk * a))
    for o_ref, val in outs:
        if len(o_ref.shape) == 2:
            o_ref[...] = val
        else:
            for h in range(o_ref.shape[0]):
                o_ref[h] = val[:, h * HEAD_DIM:(h + 1) * HEAD_DIM]


def _rwkv_pre(rw, prev, mu, w0, w_up, a0, a_up, g_up, k_k, k_a, seq_len):
    M, ncols = rw.shape
    n_dim = w0.shape[0]
    tm = min(ROW_TILE, M)
    H = n_dim // HEAD_DIM
    head_major = seq_len % tm == 0
    nt = seq_len // tm if head_major else 1
    zeros = jnp.zeros((AAA_LORA, n_dim), F32)
    wup_pad = jnp.concatenate([w_up, zeros], axis=0)
    aup_pad = jnp.concatenate([jnp.zeros((DECAY_LORA, n_dim), F32), a_up], axis=0)
    head = jnp.arange(n_dim) // HEAD_DIM
    hsum = (head[:, None] == head[None, :]).astype(F32)
    vec = lambda a: a.reshape(1, -1)
    row = lambda n: pl.BlockSpec((tm, n), lambda i: (i, 0))
    full = lambda a: pl.BlockSpec(a.shape, lambda i: (0, 0))
    ins = [rw, prev, vec(mu), vec(w0), vec(a0), vec(k_k), vec(k_a), wup_pad, aup_pad, g_up, hsum]
    if head_major:
        seq_spec = pl.BlockSpec((None, H, tm, HEAD_DIM), lambda i: (i // nt, 0, i % nt, 0))
        seq_shape = jax.ShapeDtypeStruct((M // seq_len, H, seq_len, HEAD_DIM), F32)
    else:
        seq_spec, seq_shape = row(n_dim), jax.ShapeDtypeStruct((M, n_dim), F32)
    outs = pl.pallas_call(
        functools.partial(_rwkv_pre_kernel, n_dim), grid=(M // tm,),
        in_specs=[row(ncols), row(ncols)] + [full(a) for a in ins[2:]],
        out_specs=[seq_spec] * 6 + [row(n_dim)],
        out_shape=[seq_shape] * 6 + [jax.ShapeDtypeStruct((M, n_dim), F32)],
        compiler_params=_params("parallel"), name="rwkv_pre",
    )(*ins)
    if head_major:
        outs = [t.reshape(-1, seq_len, HEAD_DIM) for t in outs[:6]] + [outs[6]]
    return outs, head_major


def _rwkv_chunk_kernel(n_heads, chunk, r_ref, lw_ref, k_ref, v_ref, kk_ref, b_ref, rk_ref,
                       rw_ref, y0_ref, bonus_ref, a_ref, s1_ref):
    C = chunk
    dh = HEAD_DIM
    ti = lax.broadcasted_iota(jnp.int32, (C, C), 0)
    si = lax.broadcasted_iota(jnp.int32, (C, C), 1)
    incl = jnp.where(si <= ti, 1.0, 0.0)
    eye = jnp.where(lax.broadcasted_iota(jnp.int32, (dh, dh), 0) == lax.broadcasted_iota(jnp.int32, (dh, dh), 1),
                    1.0, 0.0)
    n_double = max(1, math.ceil(math.log2(C)))
    heads = range(n_heads)
    r = [r_ref[h] for h in heads]
    lw = [lw_ref[h] for h in heads]
    k = [k_ref[h] for h in heads]
    v = [v_ref[h] for h in heads]
    b = [b_ref[h] for h in heads]
    cum = [_mm(incl, lw[h], exact=True) for h in heads]
    cum_end = [cum[h][C - 1:C, :] for h in heads]
    kkw = [kk_ref[h] * jnp.exp(cum[h] - lw[h]) for h in heads]
    rwc = [r[h] * jnp.exp(cum[h]) for h in heads]
    w_inv = [jnp.exp(-cum[h]) for h in heads]
    kd = [k[h] * w_inv[h] for h in heads]
    bd = [b[h] * w_inv[h] for h in heads]
    w_end = [jnp.exp(cum_end[h] - cum[h]) for h in heads]
    a_ub = [jnp.where(si < ti, _mm(kkw[h], bd[h], NT), 0.0) for h in heads]
    a_vk = [jnp.where(si < ti, _mm(kkw[h], kd[h], NT), 0.0) for h in heads]
    b_rb = [jnp.where(si <= ti, _mm(rwc[h], bd[h], NT), 0.0) for h in heads]
    b_rk = [jnp.where(si <= ti, _mm(rwc[h], kd[h], NT), 0.0) for h in heads]
    x = [jnp.concatenate([kkw[h], _mm(a_vk[h], v[h])], axis=1) for h in heads]
    p = [-a_ub[h] for h in heads]
    x = [x[h] + _mm(p[h], x[h]) for h in heads]
    for _ in range(n_double - 1):
        p = [_mm(p[h], p[h]) for h in heads]
        x = [x[h] + _mm(p[h], x[h]) for h in heads]
    for h in heads:
        bx = _mm(b_rb[h], x[h])
        rw_ref[h] = rwc[h] - bx[:, :dh]
        y0_ref[h] = _mm(b_rk[h], v[h]) - bx[:, dh:]
        kw = x[h][:, :dh]
        uv = x[h][:, dh:]
        a_ref[h] = eye * jnp.exp(cum_end[h]) - _mm(kw, b[h] * w_end[h], TN)
        s1_ref[h] = _mm(v[h], k[h] * w_end[h], TN) - _mm(uv, b[h] * w_end[h], TN)
        bonus_ref[h] = jnp.sum(r[h] * k[h] * rk_ref[h:h + 1, :], axis=-1, keepdims=True) * v[h]


def _rwkv_state_kernel(n_heads, n_sub, rw_ref, y0_ref, bonus_ref, a_ref, s1_ref, s0_ref, lng_ref, lnb_ref,
                       y_ref, s_out_ref, s_scr):
    c = pl.program_id(1)

    @pl.when(c == 0)
    def _():
        s_scr[...] = s0_ref[...]

    chunk = rw_ref.shape[2]
    for h in range(n_heads):
        s = s_scr[h]
        for j in range(n_sub):
            y = _mm(rw_ref[h, j], s, NT) + y0_ref[h, j]
            s = _mm(s, a_ref[h, j]) + s1_ref[h, j]
            mean = jnp.mean(y, axis=-1, keepdims=True)
            var = jnp.mean(jnp.square(y - mean), axis=-1, keepdims=True)
            yn = (y - mean) * lax.rsqrt(var + RWKV_GN_EPS) * lng_ref[h:h + 1, :] + lnb_ref[h:h + 1, :]
            y_ref[j * chunk:(j + 1) * chunk, h * HEAD_DIM:(h + 1) * HEAD_DIM] = yn + bonus_ref[h, j]
        s_scr[h] = s

    @pl.when(c == pl.num_programs(1) - 1)
    def _():
        s_out_ref[...] = s_scr[...]


def _rwkv_scan(r, lw, k, v, kk, b, s0, r_k, ln_g, ln_b, chunk):
    n_bh, T, dh = r.shape
    H = r_k.shape[0]
    n_chunks = T // chunk
    seq = pl.BlockSpec((H, chunk, dh), lambda i, c: (i, c, 0))
    mat = pl.BlockSpec((H, None, dh, dh), lambda i, c: (i, c, 0, 0))
    par = pl.BlockSpec((H, dh), lambda i, c: (0, 0))
    seq_shape = jax.ShapeDtypeStruct((n_bh, T, dh), F32)
    mat_shape = jax.ShapeDtypeStruct((n_bh, n_chunks, dh, dh), F32)
    rw, y0, bonus, a, s1 = pl.pallas_call(
        functools.partial(_rwkv_chunk_kernel, H, chunk), grid=(n_bh // H, n_chunks),
        in_specs=[seq] * 6 + [par], out_specs=[seq, seq, seq, mat, mat],
        out_shape=[seq_shape, seq_shape, seq_shape, mat_shape, mat_shape],
        compiler_params=_params("parallel", "parallel"), name="rwkv_chunk",
    )(r, lw, k, v, kk, b, r_k)
    n_sub = math.gcd(SCAN_CHUNKS_PER_STEP, n_chunks)
    split = lambda t: t.reshape(n_bh, n_chunks, chunk, dh)
    seq4 = pl.BlockSpec((H, n_sub, chunk, dh), lambda i, c: (i, c, 0, 0))
    mat4 = pl.BlockSpec((H, n_sub, dh, dh), lambda i, c: (i, c, 0, 0))
    state = pl.BlockSpec((H, dh, dh), lambda i, c: (i, 0, 0))
    y, s_new = pl.pallas_call(
        functools.partial(_rwkv_state_kernel, H, n_sub), grid=(n_bh // H, n_chunks // n_sub),
        in_specs=[seq4, seq4, seq4, mat4, mat4, state, par, par],
        out_specs=[pl.BlockSpec((n_sub * chunk, H * dh), lambda i, c: (i * (n_chunks // n_sub) + c, 0)), state],
        out_shape=[jax.ShapeDtypeStruct((n_bh // H * T, H * dh), F32), jax.ShapeDtypeStruct((n_bh, dh, dh), F32)],
        scratch_shapes=[pltpu.VMEM((H, dh, dh), F32)],
        compiler_params=_params("parallel", "arbitrary"), name="rwkv_state",
    )(split(rw), split(y0), split(bonus), a, s1, s0, ln_g.reshape(H, dh), ln_b.reshape(H, dh))
    return y, s_new


def _rwkv_mix(rw, shift_prev, wkv0, p, chunk):
    mu, w0, w_up, a0, a_up, g_up, k_k, k_a, r_k, ln_g, ln_b = p
    B, T, ncols = rw.shape
    H = r_k.shape[0]
    n_dim = H * HEAD_DIM
    prev = jnp.concatenate([shift_prev[:, None], rw[:, :-1]], axis=1)
    outs, head_major = _rwkv_pre(rw.reshape(B * T, ncols), prev.reshape(B * T, ncols), mu, w0, w_up, a0, a_up, g_up,
                                 k_k, k_a, T)
    g = outs[6]
    Tp = _round_up(T, chunk)

    def heads(t):
        if not head_major:
            t = t.reshape(B, T, H, HEAD_DIM).transpose(0, 2, 1, 3).reshape(B * H, T, HEAD_DIM)
        return jnp.pad(t, ((0, 0), (0, Tp - T), (0, 0)))

    y, s_new = _rwkv_scan(*[heads(t) for t in outs[:6]], wkv0.reshape(B * H, HEAD_DIM, HEAD_DIM), r_k, ln_g, ln_b,
                          chunk)
    y = y.reshape(B, Tp, n_dim)[:, :T].reshape(B * T, n_dim)
    return y, g, s_new.reshape(B, H, HEAD_DIM, HEAD_DIM)


def _compress_kernel(x_ref, w1_ref, pe_ref, w2_ref, o_ref):
    nch, dh = o_ref.shape
    w1 = w1_ref[...]
    bias = _mm(pe_ref[...], w1)[0:1, :]
    both = jnp.zeros((nch, 2 * dh), F32)
    for c in range(CMP_STRIDE):
        xc = x_ref[pl.ds(c, nch, stride=CMP_STRIDE), :]
        pair = jnp.concatenate([w1[c * dh:(c + 1) * dh], w1[(CMP_STRIDE + c) * dh:(CMP_STRIDE + c + 1) * dh]],
                               axis=1)
        both = both + _mm(xc, pair)
    h = jax.nn.gelu(both[:, :dh] + pltpu.roll(both[:, dh:], nch - 1, 0) + bias)
    o_ref[...] = _mm(h, w2_ref[...])


def _nsa_compress(kv_cmp, w1, pe, w2):
    _, BG, L, dh = kv_cmp.shape
    nch = L // CMP_STRIDE
    x = kv_cmp[:, :, :nch * CMP_STRIDE]
    kdim = CMP_LEN * dh
    pe8 = jnp.pad(pe.reshape(2, 1, kdim), ((0, 0), (0, SUBLANES - 1), (0, 0)))
    return pl.pallas_call(
        _compress_kernel, grid=(2, BG),
        in_specs=[pl.BlockSpec((None, None, nch * CMP_STRIDE, dh), lambda s, i: (s, i, 0, 0)),
                  pl.BlockSpec((None, kdim, dh), lambda s, i: (s, 0, 0)),
                  pl.BlockSpec((None, SUBLANES, kdim), lambda s, i: (s, 0, 0)),
                  pl.BlockSpec((None, dh, dh), lambda s, i: (s, 0, 0))],
        out_specs=pl.BlockSpec((None, None, nch, dh), lambda s, i: (s, i, 0, 0)),
        out_shape=jax.ShapeDtypeStruct((2, BG, nch, dh), F32),
        compiler_params=_params("parallel", "parallel"), name="nsa_compress",
    )(x, w1.reshape(2, kdim, dh), pe8, w2)


def _load_queries(q_ref, n_rep, tok_major):
    if not tok_major:
        return q_ref[...]
    qt = q_ref[...].T
    return jnp.concatenate([qt[r * HEAD_DIM:(r + 1) * HEAD_DIM] for r in range(n_rep)], axis=1)


def _store_outputs(o_ref, o, n_rep, tok_major):
    if not tok_major:
        o_ref[...] = o
        return
    tq = o.shape[1] // n_rep
    o_ref[...] = jnp.concatenate([o[:, r * tq:(r + 1) * tq] for r in range(n_rep)], axis=0).T


def _query_layout(q, tq, n_groups, group_batch=None):
    if q.ndim == 4:
        BG, nq, dh, rows = q.shape
        return q, False, BG, nq, rows // tq, pl.BlockSpec((group_batch, None, dh, rows), lambda i, j: (i, j, 0, 0))
    assert group_batch in (None, 1)
    B, T, n = q.shape
    G = n_groups
    n_rep = n // HEAD_DIM // G
    nq = T // tq
    spec = pl.BlockSpec((tq, n_rep * HEAD_DIM), lambda i, j: ((i // G) * nq + j, i % G))
    return q.reshape(B * T, n), True, B * G, nq, n_rep, spec


def _cmp_select_kernel(tq, n_rep, q_base, n_sel, tok_major, q_ref, kc_ref, vct_ref, ovt_ref, oc_ref, bias_ref):
    qi = pl.program_id(1)
    rows = n_rep * tq
    ncp = kc_ref.shape[0]
    nr = ovt_ref.shape[0]
    nbp = bias_ref.shape[0]
    q = _load_queries(q_ref, n_rep, tok_major) * HEAD_DIM ** -0.5
    s = _mm(kc_ref[...], q)
    t_row = q_base + qi * tq + lax.broadcasted_iota(jnp.int32, (1, rows), 1) % tq
    c_end = lax.broadcasted_iota(jnp.int32, (ncp, 1), 0) * CMP_STRIDE + (CMP_LEN - 1)
    s = jnp.where(c_end <= t_row, s, NEG_INF)
    m = jnp.max(s, axis=0, keepdims=True)
    e = jnp.exp(s - jnp.where(m == NEG_INF, 0.0, m))
    p = e / jnp.maximum(jnp.sum(e, axis=0, keepdims=True), 1e-30)
    _store_outputs(oc_ref, _mm(vct_ref[...], p), n_rep, tok_major)
    p_sum = p[:, 0:tq]
    for r in range(1, n_rep):
        p_sum = p_sum + p[:, r * tq:(r + 1) * tq]
    imp = _mm(ovt_ref[...], p_sum, exact=True)
    cur = (q_base + qi * tq + lax.broadcasted_iota(jnp.int32, (1, tq), 1)) // SEL_BLOCK
    blk = lax.broadcasted_iota(jnp.int32, (nr, 1), 0)
    forced = jnp.where(blk == cur, 1.0, jnp.where(blk == cur - 1, 1.0, jnp.where(blk == 0, 1.0, 0.0)))
    score = jnp.where(blk <= cur, imp + FORCE_BONUS * forced, NEG_INF)
    rank = _rank_rows(score, blk, n_sel)
    bias = jnp.where(blk <= cur, jnp.where(rank < min(SEL_TOPN, n_sel), 0.0, NEG_BIG), NEG_BIG)
    if nbp > nr:
        bias = jnp.concatenate([bias, jnp.zeros((nbp - nr, tq), F32)], axis=0)
    bias_ref[...] = bias.astype(BF16)


def _cmp_select(q, kc, vct, ovt, tq, q_base, n_sel, nbp):
    qt, tok_major, BG, nq, n_rep, qspec = _query_layout(q, tq, NSA_KV_HEADS)
    ncp, dh = kc.shape[1:]
    return pl.pallas_call(
        functools.partial(_cmp_select_kernel, tq, n_rep, q_base, n_sel, tok_major), grid=(BG, nq),
        in_specs=[qspec, pl.BlockSpec((None, ncp, dh), lambda i, j: (i, 0, 0)),
                  pl.BlockSpec((None, dh, ncp), lambda i, j: (i, 0, 0)),
                  pl.BlockSpec(ovt.shape, lambda i, j: (0, 0))],
        out_specs=[qspec, pl.BlockSpec((None, None, nbp, tq), lambda i, j: (i, j, 0, 0))],
        out_shape=[jax.ShapeDtypeStruct(qt.shape, F32), jax.ShapeDtypeStruct((BG, nq, nbp, tq), BF16)],
        compiler_params=_params("parallel", "parallel"), name="nsa_cmp_select",
    )(qt, kc, vct, ovt)


def _overlap_matrix_t(nc, nsel, ncp, nr):
    i = jnp.arange(nc, dtype=jnp.int32)[None, :]
    j = jnp.arange(nsel, dtype=jnp.int32)[:, None]
    lo = jnp.maximum(i * CMP_STRIDE, j * SEL_BLOCK)
    hi = jnp.minimum(i * CMP_STRIDE + CMP_LEN, (j + 1) * SEL_BLOCK)
    ov = jnp.clip(hi - lo, 0).astype(F32) / CMP_LEN
    return jnp.pad(ov, ((0, nr - nsel), (0, ncp - nc)))


def _means_kernel(n_blk, k_ref, o_ref):
    o_ref[...] = jnp.zeros(o_ref.shape, F32)
    for n in range(n_blk):
        o_ref[n:n + 1, :] = jnp.sum(k_ref[n * MOBA_BLOCK:(n + 1) * MOBA_BLOCK, :], axis=0,
                                    keepdims=True) / MOBA_BLOCK


def _block_means(k, n_blk, nbp):
    BG, Lp, dh = k.shape
    return pl.pallas_call(
        functools.partial(_means_kernel, n_blk), grid=(BG,),
        in_specs=[pl.BlockSpec((None, Lp, dh), lambda i: (i, 0, 0))],
        out_specs=pl.BlockSpec((None, nbp, dh), lambda i: (i, 0, 0)),
        out_shape=jax.ShapeDtypeStruct((BG, nbp, dh), F32), compiler_params=_params("parallel"), name="moba_means",
    )(k)


def _bmm(a, b, exact=False):
    dims = (((2,), (1,)), ((0,), (0,)))
    if exact:
        return lax.dot_general(a.astype(F32), b.astype(F32), dims, precision=HIGHEST, preferred_element_type=F32)
    return lax.dot_general(a.astype(BF16), b.astype(BF16), dims, preferred_element_type=F32)


def _flash_kernel(mode, tq, n_rep, tk, q_base, k_base, n_blk, tok_major, *refs):
    if mode == "sel":
        q_ref, ke_ref, vt_ref, bias_ref, o_ref, lhs_scr, m_scr, acc_scr = refs
    elif mode == "win":
        q_ref, ke_ref, vt_ref, o_ref, lhs_scr, m_scr, acc_scr = refs
    else:
        q_ref, ke_ref, vt_ref, mean_ref, o_ref, lhs_scr, m_scr, acc_scr = refs
    qi = pl.program_id(1)
    rows = n_rep * tq
    q0 = q_base + qi * tq
    t_row = q0 + lax.broadcasted_iota(jnp.int32, (1, 1, rows), 2) % tq
    q = _load_queries(q_ref, n_rep, True)[None] if tok_major else q_ref[...]
    q = q * HEAD_DIM ** -0.5
    lhs_scr[:, 0:HEAD_DIM, :] = q.astype(BF16)
    if mode == "sel":
        lhs_scr[:, HEAD_DIM:, :] = jnp.concatenate([bias_ref[...]] * n_rep, axis=2)
    elif mode == "moba":
        nr = _round_up(n_blk, SUBLANES)
        nbp = mean_ref.shape[1]
        gate = _bmm(mean_ref[:, 0:nr, :], q, exact=True)
        blk = lax.broadcasted_iota(jnp.int32, (1, nr, 1), 1)
        cur = t_row // MOBA_BLOCK
        gate = jnp.where(blk < cur, gate, NEG_INF)
        rank = _rank_rows(gate, blk, n_blk)
        picked = jnp.where(blk < cur, jnp.where(rank < min(MOBA_TOPK, n_blk), 0.0, NEG_BIG), NEG_BIG)
        bias = jnp.where(blk == cur, 0.0, picked)
        if nbp > nr:
            bias = jnp.concatenate([bias, jnp.zeros((bias.shape[0], nbp - nr, rows), F32)], axis=1)
        lhs_scr[:, HEAD_DIM:, :] = bias.astype(BF16)
    m_scr[...] = jnp.full(m_scr.shape, NEG_BIG, F32)
    acc_scr[...] = jnp.zeros(acc_scr.shape, F32)

    def tile(ki, positional):
        k0 = pl.multiple_of(ki * tk, tk)
        s = _bmm(ke_ref[:, pl.ds(k0, tk), :], lhs_scr[...])
        if positional:
            kpos = k_base + k0 + lax.broadcasted_iota(jnp.int32, (1, tk, 1), 1)
            s = jnp.where(kpos <= t_row, s, NEG_BIG)
            if mode == "win":
                s = jnp.where(kpos > t_row - WINDOW, s, NEG_BIG)
        m_prev = m_scr[...]
        m_new = jnp.maximum(m_prev, jnp.max(s, axis=1, keepdims=True))
        p = jnp.exp(s - m_new)
        acc_scr[...] = jnp.exp(m_prev - m_new) * acc_scr[...] + _bmm(vt_ref[:, :, pl.ds(k0, tk)], p)
        m_scr[...] = m_new

    def body(positional):
        def f(ki, carry):
            tile(ki, positional)
            return carry
        return f

    last = (q0 + tq - 1 - k_base) // tk
    if mode == "win":
        lax.fori_loop(jnp.maximum(q0 - (WINDOW - 1) - k_base, 0) // tk, last + 1, body(True), 0)
    else:
        lax.fori_loop(0, last, body(False), 0)
        tile(last, True)
    acc = acc_scr[...]
    o = acc[:, 0:HEAD_DIM] / jnp.maximum(acc[:, HEAD_DIM:HEAD_DIM + 1], 1e-30)
    if tok_major:
        _store_outputs(o_ref, o[0], n_rep, True)
    else:
        o_ref[...] = o


def _flash(mode, q, ke, vt, extra, tq, q_base, k_base, n_blk, n_groups, tk=ATT_K_TILE, group_batch=1):
    gb = group_batch
    qt, tok_major, BG, nq, n_rep, qspec = _query_layout(q, tq, n_groups, gb)
    dh = HEAD_DIM
    rows = n_rep * tq
    Lp, kw = ke.shape[1:]
    assert Lp % tk == 0 and (q_base + nq * tq - 1 - k_base) // tk < Lp // tk
    assert tk % tq == 0 and (q_base - k_base) % tq == 0 and BG % gb == 0
    ins = [qt, ke, vt]
    in_specs = [qspec, pl.BlockSpec((gb, Lp, kw), lambda i, j: (i, 0, 0)),
                pl.BlockSpec((gb, V_ROWS, Lp), lambda i, j: (i, 0, 0))]
    if mode == "sel":
        ins.append(extra)
        in_specs.append(pl.BlockSpec((gb, None, kw - dh, tq), lambda i, j: (i, j, 0, 0)))
    if mode == "moba":
        ins.append(extra)
        in_specs.append(pl.BlockSpec((gb, kw - dh, dh), lambda i, j: (i, 0, 0)))
    scratch = [pltpu.VMEM((gb, kw, rows), BF16), pltpu.VMEM((gb, 1, rows), F32),
               pltpu.VMEM((gb, V_ROWS, rows), F32)]
    return pl.pallas_call(
        functools.partial(_flash_kernel, mode, tq, n_rep, tk, q_base, k_base, n_blk, tok_major),
        grid=(BG // gb, nq), in_specs=in_specs, out_specs=qspec, out_shape=jax.ShapeDtypeStruct(qt.shape, F32),
        scratch_shapes=scratch, compiler_params=_params("parallel", "parallel"), name="flash_" + mode,
    )(*ins)


def _group_queries_t(q, n_groups, tq):
    B, T, n = q.shape
    R = n // HEAD_DIM // n_groups
    q = q.reshape(B, T // tq, tq, n_groups, R, HEAD_DIM).transpose(0, 3, 1, 5, 4, 2)
    return q.reshape(B * n_groups, T // tq, HEAD_DIM, R * tq)


def _ungroup_t(o, B, n_groups, tq):
    BG, nq, dh, rows = o.shape
    R = rows // tq
    o = o.reshape(B, n_groups, nq, dh, R, tq).transpose(0, 2, 5, 1, 4, 3)
    return o.reshape(B, nq * tq, n_groups * R * dh)


def _head_major(x, Lp):
    B, L, G, dh = x.shape
    x = jnp.pad(x.transpose(0, 2, 1, 3), ((0, 0), (0, 0), (0, Lp - L), (0, 0)))
    return x.reshape(B * G, Lp, dh)


def _key_operand(x, Lp, block=None, nbp=0, first_pos=0):
    k = _head_major(x, Lp).astype(BF16)
    if block is None:
        return k
    onehot = ((first_pos + jnp.arange(Lp))[:, None] // block == jnp.arange(nbp)[None, :]).astype(BF16)
    return jnp.concatenate([k, jnp.broadcast_to(onehot[None], (k.shape[0], Lp, nbp))], axis=-1)


def _value_operand(x, Lp):
    B, L, G, dh = x.shape
    v = jnp.pad(x.transpose(0, 2, 3, 1), ((0, 0), (0, 0), (0, 0), (0, Lp - L))).reshape(B * G, dh, Lp)
    return jnp.concatenate([v, jnp.ones((B * G, V_ROWS - dh, Lp), F32)], axis=1).astype(BF16)


def _bias_rows(n):
    return _round_up(HEAD_DIM + n, LANES) - HEAD_DIM


def _sel_blocks(n_pos):
    n_sel = -(-n_pos // SEL_BLOCK)
    return n_sel, _bias_rows(n_sel)


def _nsa_branches(q, n_q, cmp_rows, n_rows, ke_sel, vt_sel, win_kv, cw, tq, q_base, win_base, tk, sel_batch):
    G = NSA_KV_HEADS
    assert cmp_rows.shape[2] // CMP_STRIDE == n_rows // CMP_STRIDE
    kvc = _nsa_compress(cmp_rows, *cw)
    n_sel, nbp = _sel_blocks(max(n_rows, q_base + n_q))
    ovt = _overlap_matrix_t(n_rows // CMP_STRIDE - 1, n_sel, kvc.shape[2], _round_up(n_sel, SUBLANES))
    oc, bias = _cmp_select(q, kvc[0], kvc[1].transpose(0, 2, 1), ovt, tq, q_base, n_sel, nbp)
    o_s = _flash("sel", q, ke_sel, vt_sel, bias, tq, q_base, 0, n_sel, G, tk, sel_batch)
    Lwp = _round_up(max(win_kv.shape[1], q_base + n_q - win_base), ATT_K_TILE)
    o_w = _flash("win", q, _key_operand(win_kv[:, :, 0], Lwp), _value_operand(win_kv[:, :, 1], Lwp), None, tq,
                 q_base, win_base, 0, G)
    return oc, o_s, o_w


def _nsa_prompt(q, kv6, cw, tq):
    B, T = q.shape[:2]
    cmp_rows = jnp.stack([kv6[:, :, 0], kv6[:, :, 1]]).transpose(0, 1, 3, 2, 4).reshape(2, -1, T, HEAD_DIM)
    _, nbp = _sel_blocks(T)
    Lp = _round_up(T, ATT_K_TILE)
    return _nsa_branches(q, T, cmp_rows, T, _key_operand(kv6[:, :, 2], Lp, SEL_BLOCK, nbp),
                         _value_operand(kv6[:, :, 3], Lp), kv6[:, :, 4:6], cw, tq, 0, 0, ATT_K_TILE, 1)


def _nsa_sample(q, kv6, cache, page_table, win_buf, cw, tq):
    Bs, Ts = q.shape[:2]
    page = cache.shape[-1]
    past_len = page_table.shape[1] * page
    step = PAGES_PER_STEP * page
    assert past_len % CMP_STRIDE == 0 and Ts < CMP_STRIDE and tq <= step
    _, nbp = _sel_blocks(past_len + tq)
    tail_ke = _key_operand(kv6[:, :, 2], step, SEL_BLOCK, nbp, past_len)
    tail_vt = _value_operand(kv6[:, :, 3], step)
    rows, ke, vt, _ = _gather_kv(cache, page_table, NSA_KV_HEADS, (0, 1), 2, 3, SEL_BLOCK, tail_ke, tail_vt, False)
    win = jnp.concatenate([win_buf, kv6[:, :, 4:6]], axis=1)
    qt = _group_queries_t(_pad_queries(q, tq), NSA_KV_HEADS, tq)
    outs = _nsa_branches(qt, tq, rows, past_len + Ts, ke, vt, win, cw, tq, past_len, past_len - win_buf.shape[1],
                         step, NSA_KV_HEADS)
    return [_ungroup_t(o, Bs, NSA_KV_HEADS, tq)[:, :Ts] for o in outs], win[:, Ts:]


def _moba_prompt(q, kv, tq):
    B, T, _ = q.shape
    n_blk = -(-T // MOBA_BLOCK)
    nbp = _bias_rows(n_blk)
    Lp = _round_up(T, ATT_K_TILE)
    means = _block_means(_head_major(kv[:, :, 0], Lp), n_blk, nbp)
    return _flash("moba", q, _key_operand(kv[:, :, 0], Lp, MOBA_BLOCK, nbp), _value_operand(kv[:, :, 1], Lp), means,
                  tq, 0, 0, n_blk, MOBA_KV_HEADS)


def _moba_sample(q, kv, cache, page_table, tq):
    Bs, Ts = q.shape[:2]
    page = cache.shape[-1]
    past_len = page_table.shape[1] * page
    step = PAGES_PER_STEP * page
    assert step % MOBA_BLOCK == 0 and tq <= MOBA_BLOCK
    n_blk = -(-(past_len + tq) // MOBA_BLOCK)
    nbp = _bias_rows(n_blk)
    tail_ke = _key_operand(kv[:, :, 0], step, MOBA_BLOCK, nbp, past_len)
    tail_vt = _value_operand(kv[:, :, 1], step)
    _, ke, vt, means = _gather_kv(cache, page_table, MOBA_KV_HEADS, (), 0, 1, MOBA_BLOCK, tail_ke, tail_vt, True)
    means = jnp.pad(means, ((0, 0), (0, max(nbp - means.shape[1], 0)), (0, 0)))[:, :nbp]
    qt = _group_queries_t(_pad_queries(q, tq), MOBA_KV_HEADS, tq)
    o = _flash("moba", qt, ke, vt, means, tq, past_len, 0, n_blk, MOBA_KV_HEADS, step, MOBA_KV_HEADS)
    return _ungroup_t(o, Bs, MOBA_KV_HEADS, tq)[:, :Ts]


def _gather_kv_kernel(n_groups, row_slots, k_slot, v_slot, block, with_means, pt_ref, *refs):
    del pt_ref
    G = n_groups
    dh = HEAD_DIM
    pps = PAGES_PER_STEP
    pages = refs[:pps]
    tail_ke_ref, tail_vt_ref = refs[pps:pps + 2]
    outs = list(refs[pps + 2:])
    rows_ref = outs.pop(0) if row_slots else None
    ke_ref, vt_ref = outs[0], outs[1]
    means_ref = outs[2] if with_means else None
    p = pl.program_id(1)
    last = pl.num_programs(1) - 1
    page = pages[0].shape[-1]
    step, kw = ke_ref.shape[1:]

    @pl.when(p < last)
    def _():
        lane = lax.broadcasted_iota(jnp.int32, (step, kw), 1)
        row = p * step + lax.broadcasted_iota(jnp.int32, (step, kw), 0)
        onehot = jnp.where(lane - dh == row // block, 1.0, 0.0).astype(BF16)
        for g in range(G):
            ke_ref[g] = onehot
        vt_ref[:, dh:, :] = jnp.ones((G, V_ROWS - dh, step), BF16)
        ppb = block // page
        for g in range(G):
            tot = None
            for j, pg in enumerate(pages):
                r0 = j * page
                for si, s in enumerate(row_slots):
                    rows_ref[si, g, r0:r0 + page, :] = pg[s, g].T
                k = pg[k_slot, g].T
                ke_ref[g, r0:r0 + page, 0:dh] = k.astype(BF16)
                vt_ref[g, 0:dh, r0:r0 + page] = pg[v_slot, g].astype(BF16)
                if with_means:
                    ksum = jnp.sum(k, axis=0, keepdims=True)
                    tot = ksum if j % ppb == 0 else tot + ksum
                    if j % ppb == ppb - 1:
                        n = j // ppb
                        means_ref[g, n:n + 1, :] = tot / block

    @pl.when(p == last)
    def _():
        ke_ref[...] = tail_ke_ref[...]
        vt_ref[...] = tail_vt_ref[...]
        if with_means:
            means_ref[...] = jnp.zeros(means_ref.shape, F32)


def _gather_kv(cache, page_table, n_groups, row_slots, k_slot, v_slot, block, tail_ke, tail_vt, with_means):
    _, n_slots, G, dh, page = cache.shape
    Bs, n_pages = page_table.shape
    assert G == n_groups and dh == HEAD_DIM
    pps = PAGES_PER_STEP
    step = pps * page
    assert n_pages % pps == 0
    if with_means:
        assert step % block == 0 and block % page == 0
    n_steps = n_pages // pps
    kw = tail_ke.shape[-1]
    Lp = (n_steps + 1) * step
    bps = step // block

    def page_map(j):
        return lambda b, p, pt: (pt[b, jnp.minimum(p, n_steps - 1) * pps + j], 0, 0, 0, 0)

    in_specs = [pl.BlockSpec((None, n_slots, G, dh, page), page_map(j)) for j in range(pps)]
    in_specs += [pl.BlockSpec((None, G, step, kw), lambda b, p, pt: (b, 0, 0, 0)),
                 pl.BlockSpec((None, G, V_ROWS, step), lambda b, p, pt: (b, 0, 0, 0))]
    out_shape, out_specs = [], []
    if row_slots:
        n_rs = len(row_slots)
        out_shape.append(jax.ShapeDtypeStruct((n_rs, Bs, G, n_steps * step, dh), F32))
        out_specs.append(pl.BlockSpec((n_rs, None, G, step, dh),
                                      lambda b, p, pt: (0, b, 0, jnp.minimum(p, n_steps - 1), 0)))
    out_shape += [jax.ShapeDtypeStruct((Bs, G, Lp, kw), BF16), jax.ShapeDtypeStruct((Bs, G, V_ROWS, Lp), BF16)]
    out_specs += [pl.BlockSpec((None, G, step, kw), lambda b, p, pt: (b, 0, p, 0)),
                  pl.BlockSpec((None, G, V_ROWS, step), lambda b, p, pt: (b, 0, 0, p))]
    if with_means:
        out_shape.append(jax.ShapeDtypeStruct((Bs, G, n_steps + 1, bps, dh), F32))
        out_specs.append(pl.BlockSpec((None, G, None, bps, dh), lambda b, p, pt: (b, 0, p, 0, 0)))
    grid_spec = pltpu.PrefetchScalarGridSpec(num_scalar_prefetch=1, grid=(Bs, n_steps + 1), in_specs=in_specs,
                                             out_specs=out_specs)
    outs = list(pl.pallas_call(
        functools.partial(_gather_kv_kernel, G, tuple(row_slots), k_slot, v_slot, block, with_means),
        grid_spec=grid_spec, out_shape=out_shape, compiler_params=_params("parallel", "arbitrary"),
        name="gather_kv",
    )(page_table, *([cache] * pps), tail_ke.reshape(Bs, G, step, kw), tail_vt.reshape(Bs, G, V_ROWS, step)))
    rows = outs.pop(0).reshape(len(row_slots), Bs * G, n_steps * step, dh) if row_slots else None
    ke = outs[0].reshape(Bs * G, Lp, kw)
    vt = outs[1].reshape(Bs * G, V_ROWS, Lp)
    means = outs[2].reshape(Bs * G, (n_steps + 1) * bps, dh) if with_means else None
    return rows, ke, vt, means


def _pad_queries(x, tq):
    return jnp.pad(x, ((0, 0), (0, tq - x.shape[1]), (0, 0)))


def kernel(x_prompt, x_sample, cache_nsa_kv, cache_moba_kv, state_win_kv, state_wkv, state_shift, page_table, norm_mix, norm_ffn, norm_final, even_w_in, even_w_out, rwkv_mu, rwkv_w0, rwkv_w_up, rwkv_a0, rwkv_a_up, rwkv_g_up, rwkv_k_k, rwkv_k_a, rwkv_r_k, rwkv_ln_g, rwkv_ln_b, nsa_gate_b, nsa_cmp_w1, nsa_cmp_pe, nsa_cmp_w2, odd_w_in, odd_w_out, ffn_w_gate, ffn_w_up, ffn_w_down):
    B, T, D = x_prompt.shape
    Bs, Ts, _ = x_sample.shape
    depth = norm_mix.shape[0]
    page = cache_nsa_kv.shape[2]
    past_len = page_table.shape[1] * page
    rwkv_dim = rwkv_w0.shape[1]
    rwkv_cols = rwkv_mu.shape[1]
    nsa_heads = nsa_gate_b.shape[1] // 3
    nsa_dim = nsa_heads * HEAD_DIM
    nsa_kv_cols = 6 * NSA_KV_HEADS * HEAD_DIM
    moba_kv_cols = 2 * MOBA_KV_HEADS * HEAD_DIM
    moba_dim = odd_w_in.shape[2] - moba_kv_cols
    tq_p = min(ATT_Q_TILE, T)
    tq_s = SAMPLE_Q_PAD

    cos_p, sin_p = _rope_tables(jnp.arange(T, dtype=jnp.int32))
    pos_s = past_len + jnp.arange(Ts, dtype=jnp.int32)
    cos_s, sin_s = _rope_tables(jnp.tile(pos_s, Bs))

    even_spec = ((rwkv_cols, (), False), (nsa_dim, tuple(range(nsa_dim // LANES)), False),
                 (nsa_kv_cols, tuple(range(0, nsa_kv_cols // LANES, 2)), False), (LANES, (), True))
    k_chunks = MOBA_KV_HEADS * HEAD_DIM // LANES
    odd_spec = ((moba_dim, tuple(range(moba_dim // LANES)), False), (moba_kv_cols, tuple(range(k_chunks)), False))

    hp = x_prompt.reshape(B * T, D)
    hs = x_sample.reshape(Bs * Ts, D)
    nsa_p, nsa_s, moba_p, moba_s = [], [], [], []
    win_p, win_s, wkv_p, wkv_s, sh_p, sh_s = [], [], [], [], [], []
    for layer in range(depth):
        i = layer // 2
        if layer % 2 == 0:
            w_in = even_w_in[i].astype(BF16)
            o = rwkv_cols
            n_gate = 3 * nsa_heads
            weights = [w_in[:, :o], w_in[:, o:o + nsa_dim], w_in[:, o + nsa_dim:o + nsa_dim + nsa_kv_cols],
                       jnp.pad(w_in[:, o + nsa_dim + nsa_kv_cols:], ((0, 0), (0, LANES - n_gate)))]
            gate_b = jnp.pad(nsa_gate_b[i], (0, LANES - n_gate)).reshape(1, LANES)
            rp = (rwkv_mu[i], rwkv_w0[i], rwkv_w_up[i], rwkv_a0[i], rwkv_a_up[i], rwkv_g_up[i],
                  rwkv_k_k[i], rwkv_k_a[i], rwkv_r_k[i], rwkv_ln_g[i], rwkv_ln_b[i])
            cw = (nsa_cmp_w1[i], nsa_cmp_pe[i], nsa_cmp_w2[i])

            rw, q, kv, gates = _project(hp, norm_mix[layer], cos_p, sin_p, weights, [gate_b], even_spec, "even_proj")
            rw3 = rw.reshape(B, T, rwkv_cols)
            kv6 = kv.reshape(B, T, 6, NSA_KV_HEADS, HEAD_DIM)
            y, g, wkv_new = _rwkv_mix(rw3, jnp.zeros((B, rwkv_cols), F32),
                                      jnp.zeros((B, rwkv_dim // HEAD_DIM, HEAD_DIM, HEAD_DIM), F32), rp, SCAN_CHUNK)
            o_c, o_s, o_w = _nsa_prompt(q.reshape(B, T, nsa_dim), kv6, cw, tq_p)
            hp = _even_out(hp, y, g, o_c.reshape(B * T, nsa_dim), o_s.reshape(B * T, nsa_dim),
                           o_w.reshape(B * T, nsa_dim), gates, even_w_out[i])
            nsa_p.append(kv6[:, :, :4])
            win_p.append(kv6[:, T - min(WINDOW, T):, 4:6])
            wkv_p.append(wkv_new)
            sh_p.append(rw3[:, -1])

            rw, q, kv, gates = _project(hs, norm_mix[layer], cos_s, sin_s, weights, [gate_b], even_spec, "even_proj_s")
            rw3 = rw.reshape(Bs, Ts, rwkv_cols)
            kv6 = kv.reshape(Bs, Ts, 6, NSA_KV_HEADS, HEAD_DIM)
            y, g, wkv_new = _rwkv_mix(rw3, state_shift[i], state_wkv[i], rp, SUBLANES)
            outs, win_new = _nsa_sample(q.reshape(Bs, Ts, nsa_dim), kv6,
                                        cache_nsa_kv[i].transpose(0, 2, 3, 4, 1), page_table,
                                        state_win_kv[i], cw, tq_s)
            o_c, o_s, o_w = [t.reshape(Bs * Ts, nsa_dim) for t in outs]
            hs = _even_out(hs, y, g, o_c, o_s, o_w, gates, even_w_out[i])
            nsa_s.append(kv6[:, :, :4])
            win_s.append(win_new)
            wkv_s.append(wkv_new)
            sh_s.append(rw3[:, -1])
        else:
            w_in = odd_w_in[i].astype(BF16)
            weights = [w_in[:, :moba_dim], w_in[:, moba_dim:]]
            q, kv = _project(hp, norm_mix[layer], cos_p, sin_p, weights, [], odd_spec, "odd_proj")
            kv2 = kv.reshape(B, T, 2, MOBA_KV_HEADS, HEAD_DIM)
            a = _moba_prompt(q.reshape(B, T, moba_dim), kv2, tq_p)
            hp = _odd_out(hp, a.reshape(B * T, moba_dim), odd_w_out[i])
            moba_p.append(kv2)

            q, kv = _project(hs, norm_mix[layer], cos_s, sin_s, weights, [], odd_spec, "odd_proj_s")
            kv2 = kv.reshape(Bs, Ts, 2, MOBA_KV_HEADS, HEAD_DIM)
            a = _moba_sample(q.reshape(Bs, Ts, moba_dim), kv2, cache_moba_kv[i].transpose(0, 2, 3, 4, 1),
                             page_table, tq_s)
            hs = _odd_out(hs, a.reshape(Bs * Ts, moba_dim), odd_w_out[i])
            moba_s.append(kv2)
        g_final = norm_final if layer == depth - 1 else None
        hp = _ffn(hp, norm_ffn[layer], ffn_w_gate[layer], ffn_w_up[layer], ffn_w_down[layer], g_final)
        hs = _ffn(hs, norm_ffn[layer], ffn_w_gate[layer], ffn_w_up[layer], ffn_w_down[layer], g_final)
    return (hp.reshape(B, T, D), hs.reshape(Bs, Ts, D), jnp.stack(nsa_p), jnp.stack(nsa_s), jnp.stack(moba_p),
            jnp.stack(moba_s), jnp.stack(win_p), jnp.stack(win_s), jnp.stack(wkv_p), jnp.stack(wkv_s),
            jnp.stack(sh_p), jnp.stack(sh_s))
```

```python
import functools
import math

import jax
import jax.numpy as jnp
from jax import lax
from jax.experimental import pallas as pl
from jax.experimental.pallas import tpu as pltpu

F32 = jnp.float32
BF16 = jnp.bfloat16
HIGHEST = lax.Precision.HIGHEST

HEAD_DIM = 64
NORM_EPS = 1e-6
ROPE_THETA = 10000.0
DECAY_LORA = 64
AAA_LORA = 64
GATE_LORA = 128
RWKV_GN_EPS = 64e-5
NSA_KV_HEADS = 2
CMP_STRIDE = 16
CMP_LEN = 2 * CMP_STRIDE
SEL_BLOCK = 64
SEL_TOPN = 16
WINDOW = 512
FORCE_BONUS = 100.0
MOBA_KV_HEADS = 4
MOBA_BLOCK = 256
MOBA_TOPK = 3

LANES = 128
SUBLANES = 8
VMEM_LIMIT = 56 * 1024 * 1024

ROW_TILE = 512
FFN_ROW_TILE = 1024
FFN_COL_TILE = 256
ATT_Q_TILE = 256
ATT_K_TILE = 512
SCAN_CHUNK = 64
SCAN_CHUNKS_PER_STEP = 4
SAMPLE_Q_PAD = 32
PAGES_PER_STEP = 8

NT = (((1,), (1,)), ((), ()))
TN = (((0,), (0,)), ((), ()))
NEG_INF = float("-inf")
NEG_BIG = -1e30
V_ROWS = HEAD_DIM + 16


def _round_up(x, m):
    return -(-x // m) * m


def _mm(a, b, dims=None, exact=False):
    if dims is None:
        dims = (((a.ndim - 1,), (0,)), ((), ()))
    if exact:
        return lax.dot_general(a.astype(F32), b.astype(F32), dims, precision=HIGHEST,
                               preferred_element_type=F32)
    return lax.dot_general(a.astype(BF16), b.astype(BF16), dims, preferred_element_type=F32)


def _params(*sem):
    return pltpu.CompilerParams(dimension_semantics=sem, vmem_limit_bytes=VMEM_LIMIT)


def _rms_norm(x, g):
    return x * lax.rsqrt(jnp.mean(x * x, axis=-1, keepdims=True) + NORM_EPS) * g


def _rope_chunk(x, cos, sin):
    lane = lax.broadcasted_iota(jnp.int32, x.shape, 1)
    half = HEAD_DIM // 2
    partner = jnp.where((lane % HEAD_DIM) < half, pltpu.roll(x, LANES - half, 1), pltpu.roll(x, half, 1))
    return x * cos + partner * sin


def _rank_rows(score, blk, ncand):
    rank = jnp.zeros(score.shape, F32)
    for j in range(ncand):
        row = score[..., j:j + 1, :]
        rank = rank + jnp.where(row > score, 1.0, 0.0) + jnp.where(row == score, jnp.where(blk > j, 1.0, 0.0), 0.0)
    return rank


def _proj_kernel(spec, x_ref, g_ref, cos_ref, sin_ref, *refs):
    nseg = len(spec)
    w_refs = refs[:nseg]
    nbias = sum(1 for s in spec if s[2])
    b_refs = list(refs[nseg:nseg + nbias])
    o_refs = refs[nseg + nbias:]
    xn = _rms_norm(x_ref[...], g_ref[...]).astype(BF16)
    for (ncols, rope_chunks, sig), w_ref, o_ref in zip(spec, w_refs, o_refs):
        y = jnp.dot(xn, w_ref[...], preferred_element_type=F32)
        if sig:
            y = jax.nn.sigmoid(y + b_refs.pop(0)[...])
        if rope_chunks:
            cos = cos_ref[...]
            sin = sin_ref[...]
            for c in range(ncols // LANES):
                yc = y[:, c * LANES:(c + 1) * LANES]
                if c in rope_chunks:
                    yc = _rope_chunk(yc, cos, sin)
                o_ref[:, c * LANES:(c + 1) * LANES] = yc
        else:
            o_ref[...] = y


def _project(x, g, cos_tab, sin_tab, weights, biases, spec, name):
    M, D = x.shape
    tm = min(ROW_TILE, M)
    ntab = cos_tab.shape[0] // tm
    in_specs = [pl.BlockSpec((tm, D), lambda i: (i, 0)),
                pl.BlockSpec((1, D), lambda i: (0, 0)),
                pl.BlockSpec((tm, LANES), lambda i: (i % ntab, 0)),
                pl.BlockSpec((tm, LANES), lambda i: (i % ntab, 0))]
    in_specs += [pl.BlockSpec(w.shape, lambda i: (0, 0)) for w in weights]
    in_specs += [pl.BlockSpec(b.shape, lambda i: (0, 0)) for b in biases]
    out_shape = [jax.ShapeDtypeStruct((M, s[0]), F32) for s in spec]
    out_specs = [pl.BlockSpec((tm, s[0]), lambda i: (i, 0)) for s in spec]
    return pl.pallas_call(
        functools.partial(_proj_kernel, spec), grid=(M // tm,), in_specs=in_specs, out_specs=out_specs,
        out_shape=out_shape, compiler_params=_params("parallel"), name=name,
    )(x, g.reshape(1, D), cos_tab, sin_tab, *weights, *biases)


def _rope_tables(pos):
    half = HEAD_DIM // 2
    inv = ROPE_THETA ** (-jnp.arange(half, dtype=F32) / half)
    ang = pos.astype(F32)[:, None] * inv[None, :]
    cos = jnp.cos(ang)
    sin = jnp.sin(ang)
    cos_t = jnp.tile(cos, (1, LANES // half))
    sin_t = jnp.tile(jnp.concatenate([-sin, sin], axis=1), (1, LANES // HEAD_DIM))
    return cos_t, sin_t


def _even_out_kernel(res_ref, y_ref, g_ref, oc_ref, os_ref, ow_ref, gt_ref, ec_ref, es_ref, ew_ref,
                     w1_ref, w2_ref, o_ref):
    a = y_ref[...] * g_ref[...]
    gt = gt_ref[...]
    b = (_mm(gt, ec_ref[...], exact=True) * oc_ref[...] + _mm(gt, es_ref[...], exact=True) * os_ref[...]
         + _mm(gt, ew_ref[...], exact=True) * ow_ref[...])
    o_ref[...] = res_ref[...] + _mm(a, w1_ref[...]) + _mm(b, w2_ref[...])


def _even_out(res, y, g, o_c, o_s, o_w, gates, w_out):
    M, D = res.shape
    n_rw = y.shape[1]
    n_nsa = o_c.shape[1]
    tm = min(ROW_TILE, M)
    heads = n_nsa // HEAD_DIM
    col = jnp.arange(n_nsa)[None, :] // HEAD_DIM
    row = jnp.arange(LANES)[:, None]
    expand = [(row == col * 3 + br).astype(F32) for br in range(3)]
    del heads
    row_spec = lambda n: pl.BlockSpec((tm, n), lambda i: (i, 0))
    full = lambda a: pl.BlockSpec(a.shape, lambda i: (0, 0))
    w1 = w_out[:n_rw].astype(BF16)
    w2 = w_out[n_rw:].astype(BF16)
    return pl.pallas_call(
        _even_out_kernel, grid=(M // tm,),
        in_specs=[row_spec(D), row_spec(n_rw), row_spec(n_rw), row_spec(n_nsa), row_spec(n_nsa), row_spec(n_nsa),
                  row_spec(LANES), full(expand[0]), full(expand[1]), full(expand[2]), full(w1), full(w2)],
        out_specs=row_spec(D), out_shape=jax.ShapeDtypeStruct((M, D), F32),
        compiler_params=_params("parallel"), name="even_out",
    )(res, y, g, o_c, o_s, o_w, gates, *expand, w1, w2)


def _odd_out_kernel(res_ref, a_ref, w_ref, o_ref):
    o_ref[...] = res_ref[...] + _mm(a_ref[...], w_ref[...])


def _odd_out(res, a, w_out):
    M, D = res.shape
    tm = min(ROW_TILE, M)
    w = w_out.astype(BF16)
    return pl.pallas_call(
        _odd_out_kernel, grid=(M // tm,),
        in_specs=[pl.BlockSpec((tm, D), lambda i: (i, 0)), pl.BlockSpec((tm, a.shape[1]), lambda i: (i, 0)),
                  pl.BlockSpec(w.shape, lambda i: (0, 0))],
        out_specs=pl.BlockSpec((tm, D), lambda i: (i, 0)), out_shape=jax.ShapeDtypeStruct((M, D), F32),
        compiler_params=_params("parallel"), name="odd_out",
    )(res, a, w)


def _ffn_kernel(final_norm, x_ref, g_ref, wg_ref, wu_ref, wd_ref, gf_ref, o_ref, xn_scr, acc_scr):
    j = pl.program_id(1)

    @pl.when(j == 0)
    def _():
        xn_scr[...] = _rms_norm(x_ref[...], g_ref[...]).astype(BF16)
        acc_scr[...] = jnp.zeros(acc_scr.shape, F32)

    xn = xn_scr[...]
    h = jax.nn.silu(jnp.dot(xn, wg_ref[...], preferred_element_type=F32)) * jnp.dot(
        xn, wu_ref[...], preferred_element_type=F32)
    acc_scr[...] += _mm(h, wd_ref[...])

    @pl.when(j == pl.num_programs(1) - 1)
    def _():
        y = x_ref[...] + acc_scr[...]
        if final_norm:
            y = _rms_norm(y, gf_ref[...])
        o_ref[...] = y


def _ffn(x, g, wg, wu, wd, g_final=None):
    M, D = x.shape
    F = wg.shape[1]
    tm = min(FFN_ROW_TILE, M)
    tf = FFN_COL_TILE
    final_norm = g_final is not None
    gf = (g_final if final_norm else g).reshape(1, D)
    return pl.pallas_call(
        functools.partial(_ffn_kernel, final_norm), grid=(M // tm, F // tf),
        in_specs=[pl.BlockSpec((tm, D), lambda i, j: (i, 0)), pl.BlockSpec((1, D), lambda i, j: (0, 0)),
                  pl.BlockSpec((D, tf), lambda i, j: (0, j)), pl.BlockSpec((D, tf), lambda i, j: (0, j)),
                  pl.BlockSpec((tf, D), lambda i, j: (j, 0)), pl.BlockSpec((1, D), lambda i, j: (0, 0))],
        out_specs=pl.BlockSpec((tm, D), lambda i, j: (i, 0)), out_shape=jax.ShapeDtypeStruct((M, D), F32),
        scratch_shapes=[pltpu.VMEM((tm, D), BF16), pltpu.VMEM((tm, D), F32)],
        compiler_params=_params("parallel", "arbitrary"), name="ffn",
    )(x, g.reshape(1, D), wg.astype(BF16), wu.astype(BF16), wd.astype(BF16), gf)


def _rwkv_pre_kernel(n_dim, rw_ref, prev_ref, mu_ref, w0_ref, a0_ref, kk_ref, ka_ref, wup_ref, aup_ref, gup_ref,
                     hsum_ref, r_out, lw_out, k_out, v_out, kk_out, b_out, g_out):
    rw = rw_ref[...]
    xm = rw + (prev_ref[...] - rw) * mu_ref[...]
    r = xm[:, :n_dim]
    k = xm[:, n_dim:2 * n_dim]
    v = xm[:, 2 * n_dim:3 * n_dim]
    lora = xm[:, 3 * n_dim:3 * n_dim + DECAY_LORA + AAA_LORA]
    xg = xm[:, 3 * n_dim + DECAY_LORA + AAA_LORA:]
    lw = -math.exp(-0.5) * jax.nn.sigmoid(w0_ref[...] + _mm(jnp.tanh(lora), wup_ref[...], exact=True))
    a = jax.nn.sigmoid(a0_ref[...] + _mm(lora, aup_ref[...], exact=True))
    g = _mm(jax.nn.sigmoid(xg), gup_ref[...])
    kk = k * kk_ref[...]
    sq = kk * kk
    sq_hi = sq.astype(BF16)
    norm = jnp.sqrt(_mm(sq_hi, hsum_ref[...]) + _mm(sq - sq_hi.astype(F32), hsum_ref[...]))
    kk = kk / jnp.maximum(norm, 1e-12)
    g_out[...] = g
    outs = ((r_out, r), (lw_out, lw), (k_out, k * (1.0 + (a - 1.0) * ka_ref[...])), (v_out, v), (kk_out, kk),
            (b_out, kk * a))
    for o_ref, val in outs:
        if len(o_ref.shape) == 2:
            o_ref[...] = val
        else:
            for h in range(o_ref.shape[0]):
                o_ref[h] = val[:, h * HEAD_DIM:(h + 1) * HEAD_DIM]


def _rwkv_pre(rw, prev, mu, w0, w_up, a0, a_up, g_up, k_k, k_a, seq_len):
    M, ncols = rw.shape
    n_dim = w0.shape[0]
    tm = min(ROW_TILE, M)
    H = n_dim // HEAD_DIM
    head_major = seq_len % tm == 0
    nt = seq_len // tm if head_major else 1
    zeros = jnp.zeros((AAA_LORA, n_dim), F32)
    wup_pad = jnp.concatenate([w_up, zeros], axis=0)
    aup_pad = jnp.concatenate([jnp.zeros((DECAY_LORA, n_dim), F32), a_up], axis=0)
    head = jnp.arange(n_dim) // HEAD_DIM
    hsum = (head[:, None] == head[None, :]).astype(F32)
    vec = lambda a: a.reshape(1, -1)
    row = lambda n: pl.BlockSpec((tm, n), lambda i: (i, 0))
    full = lambda a: pl.BlockSpec(a.shape, lambda i: (0, 0))
    ins = [rw, prev, vec(mu), vec(w0), vec(a0), vec(k_k), vec(k_a), wup_pad, aup_pad, g_up, hsum]
    if head_major:
        seq_spec = pl.BlockSpec((None, H, tm, HEAD_DIM), lambda i: (i // nt, 0, i % nt, 0))
        seq_shape = jax.ShapeDtypeStruct((M // seq_len, H, seq_len, HEAD_DIM), F32)
    else:
        seq_spec, seq_shape = row(n_dim), jax.ShapeDtypeStruct((M, n_dim), F32)
    outs = pl.pallas_call(
        functools.partial(_rwkv_pre_kernel, n_dim), grid=(M // tm,),
        in_specs=[row(ncols), row(ncols)] + [full(a) for a in ins[2:]],
        out_specs=[seq_spec] * 6 + [row(n_dim)],
        out_shape=[seq_shape] * 6 + [jax.ShapeDtypeStruct((M, n_dim), F32)],
        compiler_params=_params("parallel"), name="rwkv_pre",
    )(*ins)
    if head_major:
        outs = [t.reshape(-1, seq_len, HEAD_DIM) for t in outs[:6]] + [outs[6]]
    return outs, head_major


def _rwkv_chunk_kernel(n_heads, chunk, r_ref, lw_ref, k_ref, v_ref, kk_ref, b_ref, rk_ref,
                       rw_ref, y0_ref, bonus_ref, a_ref, s1_ref):
    C = chunk
    dh = HEAD_DIM
    ti = lax.broadcasted_iota(jnp.int32, (C, C), 0)
    si = lax.broadcasted_iota(jnp.int32, (C, C), 1)
    incl = jnp.where(si <= ti, 1.0, 0.0)
    eye = jnp.where(lax.broadcasted_iota(jnp.int32, (dh, dh), 0) == lax.broadcasted_iota(jnp.int32, (dh, dh), 1),
                    1.0, 0.0)
    n_double = max(1, math.ceil(math.log2(C)))
    heads = range(n_heads)
    r = [r_ref[h] for h in heads]
    lw = [lw_ref[h] for h in heads]
    k = [k_ref[h] for h in heads]
    v = [v_ref[h] for h in heads]
    b = [b_ref[h] for h in heads]
    cum = [_mm(incl, lw[h], exact=True) for h in heads]
    cum_end = [cum[h][C - 1:C, :] for h in heads]
    kkw = [kk_ref[h] * jnp.exp(cum[h] - lw[h]) for h in heads]
    rwc = [r[h] * jnp.exp(cum[h]) for h in heads]
    w_inv = [jnp.exp(-cum[h]) for h in heads]
    kd = [k[h] * w_inv[h] for h in heads]
    bd = [b[h] * w_inv[h] for h in heads]
    w_end = [jnp.exp(cum_end[h] - cum[h]) for h in heads]
    a_ub = [jnp.where(si < ti, _mm(kkw[h], bd[h], NT), 0.0) for h in heads]
    a_vk = [jnp.where(si < ti, _mm(kkw[h], kd[h], NT), 0.0) for h in heads]
    b_rb = [jnp.where(si <= ti, _mm(rwc[h], bd[h], NT), 0.0) for h in heads]
    b_rk = [jnp.where(si <= ti, _mm(rwc[h], kd[h], NT), 0.0) for h in heads]
    x = [jnp.concatenate([kkw[h], _mm(a_vk[h], v[h])], axis=1) for h in heads]
    p = [-a_ub[h] for h in heads]
    x = [x[h] + _mm(p[h], x[h]) for h in heads]
    for _ in range(n_double - 1):
        p = [_mm(p[h], p[h]) for h in heads]
        x = [x[h] + _mm(p[h], x[h]) for h in heads]
    for h in heads:
        bx = _mm(b_rb[h], x[h])
        rw_ref[h] = rwc[h] - bx[:, :dh]
        y0_ref[h] = _mm(b_rk[h], v[h]) - bx[:, dh:]
        kw = x[h][:, :dh]
        uv = x[h][:, dh:]
        a_ref[h] = eye * jnp.exp(cum_end[h]) - _mm(kw, b[h] * w_end[h], TN)
        s1_ref[h] = _mm(v[h], k[h] * w_end[h], TN) - _mm(uv, b[h] * w_end[h], TN)
        bonus_ref[h] = jnp.sum(r[h] * k[h] * rk_ref[h:h + 1, :], axis=-1, keepdims=True) * v[h]


def _rwkv_state_kernel(n_heads, n_sub, rw_ref, y0_ref, bonus_ref, a_ref, s1_ref, s0_ref, lng_ref, lnb_ref,
                       y_ref, s_out_ref, s_scr):
    c = pl.program_id(1)

    @pl.when(c == 0)
    def _():
        s_scr[...] = s0_ref[...]

    chunk = rw_ref.shape[2]
    for h in range(n_heads):
        s = s_scr[h]
        for j in range(n_sub):
            y = _mm(rw_ref[h, j], s, NT) + y0_ref[h, j]
            s = _mm(s, a_ref[h, j]) + s1_ref[h, j]
            mean = jnp.mean(y, axis=-1, keepdims=True)
            var = jnp.mean(jnp.square(y - mean), axis=-1, keepdims=True)
            yn = (y - mean) * lax.rsqrt(var + RWKV_GN_EPS) * lng_ref[h:h + 1, :] + lnb_ref[h:h + 1, :]
            y_ref[j * chunk:(j + 1) * chunk, h * HEAD_DIM:(h + 1) * HEAD_DIM] = yn + bonus_ref[h, j]
        s_scr[h] = s

    @pl.when(c == pl.num_programs(1) - 1)
    def _():
        s_out_ref[...] = s_scr[...]


def _rwkv_scan(r, lw, k, v, kk, b, s0, r_k, ln_g, ln_b, chunk):
    n_bh, T, dh = r.shape
    H = r_k.shape[0]
    n_chunks = T // chunk
    seq = pl.BlockSpec((H, chunk, dh), lambda i, c: (i, c, 0))
    mat = pl.BlockSpec((H, None, dh, dh), lambda i, c: (i, c, 0, 0))
    par = pl.BlockSpec((H, dh), lambda i, c: (0, 0))
    seq_shape = jax.ShapeDtypeStruct((n_bh, T, dh), F32)
    mat_shape = jax.ShapeDtypeStruct((n_bh, n_chunks, dh, dh), F32)
    rw, y0, bonus, a, s1 = pl.pallas_call(
        functools.partial(_rwkv_chunk_kernel, H, chunk), grid=(n_bh // H, n_chunks),
        in_specs=[seq] * 6 + [par], out_specs=[seq, seq, seq, mat, mat],
        out_shape=[seq_shape, seq_shape, seq_shape, mat_shape, mat_shape],
        compiler_params=_params("parallel", "parallel"), name="rwkv_chunk",
    )(r, lw, k, v, kk, b, r_k)
    n_sub = math.gcd(SCAN_CHUNKS_PER_STEP, n_chunks)
    split = lambda t: t.reshape(n_bh, n_chunks, chunk, dh)
    seq4 = pl.BlockSpec((H, n_sub, chunk, dh), lambda i, c: (i, c, 0, 0))
    mat4 = pl.BlockSpec((H, n_sub, dh, dh), lambda i, c: (i, c, 0, 0))
    state = pl.BlockSpec((H, dh, dh), lambda i, c: (i, 0, 0))
    y, s_new = pl.pallas_call(
        functools.partial(_rwkv_state_kernel, H, n_sub), grid=(n_bh // H, n_chunks // n_sub),
        in_specs=[seq4, seq4, seq4, mat4, mat4, state, par, par],
        out_specs=[pl.BlockSpec((n_sub * chunk, H * dh), lambda i, c: (i * (n_chunks // n_sub) + c, 0)), state],
        out_shape=[jax.ShapeDtypeStruct((n_bh // H * T, H * dh), F32), jax.ShapeDtypeStruct((n_bh, dh, dh), F32)],
        scratch_shapes=[pltpu.VMEM((H, dh, dh), F32)],
        compiler_params=_params("parallel", "arbitrary"), name="rwkv_state",
    )(split(rw), split(y0), split(bonus), a, s1, s0, ln_g.reshape(H, dh), ln_b.reshape(H, dh))
    return y, s_new


def _rwkv_mix(rw, shift_prev, wkv0, p, chunk):
    mu, w0, w_up, a0, a_up, g_up, k_k, k_a, r_k, ln_g, ln_b = p
    B, T, ncols = rw.shape
    H = r_k.shape[0]
    n_dim = H * HEAD_DIM
    prev = jnp.concatenate([shift_prev[:, None], rw[:, :-1]], axis=1)
    outs, head_major = _rwkv_pre(rw.reshape(B * T, ncols), prev.reshape(B * T, ncols), mu, w0, w_up, a0, a_up, g_up,
                                 k_k, k_a, T)
    g = outs[6]
    Tp = _round_up(T, chunk)

    def heads(t):
        if not head_major:
            t = t.reshape(B, T, H, HEAD_DIM).transpose(0, 2, 1, 3).reshape(B * H, T, HEAD_DIM)
        return jnp.pad(t, ((0, 0), (0, Tp - T), (0, 0)))

    y, s_new = _rwkv_scan(*[heads(t) for t in outs[:6]], wkv0.reshape(B * H, HEAD_DIM, HEAD_DIM), r_k, ln_g, ln_b,
                          chunk)
    y = y.reshape(B, Tp, n_dim)[:, :T].reshape(B * T, n_dim)
    return y, g, s_new.reshape(B, H, HEAD_DIM, HEAD_DIM)


def _compress_kernel(transpose_out, x_ref, w1_ref, pe_ref, w2_ref, o_ref):
    n_groups = o_ref.shape[0]
    width = x_ref.shape[1]
    nch = x_ref.shape[0] // CMP_STRIDE
    both = jnp.zeros((nch, 2 * width), F32)
    bias = jnp.zeros((1, width), F32)
    for c in range(CMP_STRIDE):
        w = w1_ref[c]
        both = both + _mm(x_ref[pl.ds(c, nch, stride=CMP_STRIDE), :], w)
        pw = _mm(pe_ref[c], w, exact=True)
        bias = bias + pw[0:1, :width] + pw[1:2, width:]
    h = jax.nn.gelu(both[:, :width] + pltpu.roll(both[:, width:], nch - 1, 0) + bias)
    if transpose_out:
        res = _mm(w2_ref[...], h, NT)
        for g in range(n_groups):
            o_ref[g] = res[g * HEAD_DIM:(g + 1) * HEAD_DIM]
    else:
        res = _mm(h, w2_ref[...])
        for g in range(n_groups):
            o_ref[g] = res[:, g * HEAD_DIM:(g + 1) * HEAD_DIM]


def _nsa_compress(x, x_spec, n_seq, n_rows, w1, pe, w2, transpose_out):
    G = NSA_KV_HEADS
    dh = HEAD_DIM
    nch = n_rows // CMP_STRIDE
    eye = jnp.eye(G, dtype=F32)
    blockdiag = lambda m: jnp.kron(eye, m)
    w1p = jnp.stack([jnp.concatenate([blockdiag(w1[c]), blockdiag(w1[CMP_STRIDE + c])], axis=1)
                     for c in range(CMP_STRIDE)]).astype(BF16)
    pe2 = jnp.stack([jnp.tile(pe[:CMP_STRIDE], (1, G)), jnp.tile(pe[CMP_STRIDE:], (1, G))], axis=1)
    pe2 = jnp.pad(pe2, ((0, 0), (0, SUBLANES - 2), (0, 0)))
    out_block = (None, G, dh, nch) if transpose_out else (None, G, nch, dh)
    out = pl.pallas_call(
        functools.partial(_compress_kernel, transpose_out), grid=(n_seq,),
        in_specs=[x_spec, pl.BlockSpec(w1p.shape, lambda i: (0, 0, 0)), pl.BlockSpec(pe2.shape, lambda i: (0, 0, 0)),
                  pl.BlockSpec((G * dh, G * dh), lambda i: (0, 0))],
        out_specs=pl.BlockSpec(out_block, lambda i: (i, 0, 0, 0)),
        out_shape=jax.ShapeDtypeStruct((n_seq,) + out_block[1:], F32),
        compiler_params=_params("parallel"), name="nsa_compress",
    )(x, w1p, pe2, blockdiag(w2.T if transpose_out else w2).astype(BF16))
    return out.reshape((n_seq * G,) + out_block[2:])


def _load_queries(q_ref, n_rep, tok_major):
    if not tok_major:
        return q_ref[...]
    qt = q_ref[...].T
    return jnp.concatenate([qt[r * HEAD_DIM:(r + 1) * HEAD_DIM] for r in range(n_rep)], axis=1)


def _store_outputs(o_ref, o, n_rep, tok_major):
    if not tok_major:
        o_ref[...] = o
        return
    tq = o.shape[1] // n_rep
    o_ref[...] = jnp.concatenate([o[:, r * tq:(r + 1) * tq] for r in range(n_rep)], axis=0).T


def _query_layout(q, tq, n_groups, group_batch=None):
    if q.ndim == 4:
        BG, nq, dh, rows = q.shape
        return q, False, BG, nq, rows // tq, pl.BlockSpec((group_batch, None, dh, rows), lambda i, j: (i, j, 0, 0))
    assert group_batch in (None, 1)
    B, T, n = q.shape
    G = n_groups
    n_rep = n // HEAD_DIM // G
    nq = T // tq
    spec = pl.BlockSpec((tq, n_rep * HEAD_DIM), lambda i, j: ((i // G) * nq + j, i % G))
    return q.reshape(B * T, n), True, B * G, nq, n_rep, spec


def _cmp_select_kernel(tq, n_rep, q_base, n_sel, tok_major, q_ref, kc_ref, vct_ref, ovt_ref, oc_ref, bias_ref):
    qi = pl.program_id(1)
    rows = n_rep * tq
    ncp = kc_ref.shape[0]
    nr = ovt_ref.shape[0]
    nbp = bias_ref.shape[0]
    q = _load_queries(q_ref, n_rep, tok_major) * HEAD_DIM ** -0.5
    s = _mm(kc_ref[...], q)
    t_row = q_base + qi * tq + lax.broadcasted_iota(jnp.int32, (1, rows), 1) % tq
    c_end = lax.broadcasted_iota(jnp.int32, (ncp, 1), 0) * CMP_STRIDE + (CMP_LEN - 1)
    s = jnp.where(c_end <= t_row, s, NEG_INF)
    m = jnp.max(s, axis=0, keepdims=True)
    e = jnp.exp(s - jnp.where(m == NEG_INF, 0.0, m))
    p = e / jnp.maximum(jnp.sum(e, axis=0, keepdims=True), 1e-30)
    _store_outputs(oc_ref, _mm(vct_ref[...], p), n_rep, tok_major)
    p_sum = p[:, 0:tq]
    for r in range(1, n_rep):
        p_sum = p_sum + p[:, r * tq:(r + 1) * tq]
    imp = _mm(ovt_ref[...], p_sum, exact=True)
    cur = (q_base + qi * tq + lax.broadcasted_iota(jnp.int32, (1, tq), 1)) // SEL_BLOCK
    blk = lax.broadcasted_iota(jnp.int32, (nr, 1), 0)
    forced = jnp.where(blk == cur, 1.0, jnp.where(blk == cur - 1, 1.0, jnp.where(blk == 0, 1.0, 0.0)))
    score = jnp.where(blk <= cur, imp + FORCE_BONUS * forced, NEG_INF)
    rank = _rank_rows(score, blk, n_sel)
    bias = jnp.where(blk <= cur, jnp.where(rank < min(SEL_TOPN, n_sel), 0.0, NEG_BIG), NEG_BIG)
    if nbp > nr:
        bias = jnp.concatenate([bias, jnp.zeros((nbp - nr, tq), F32)], axis=0)
    bias_ref[...] = bias.astype(BF16)


def _cmp_select(q, kc, vct, ovt, tq, q_base, n_sel, nbp):
    qt, tok_major, BG, nq, n_rep, qspec = _query_layout(q, tq, NSA_KV_HEADS)
    ncp, dh = kc.shape[1:]
    return pl.pallas_call(
        functools.partial(_cmp_select_kernel, tq, n_rep, q_base, n_sel, tok_major), grid=(BG, nq),
        in_specs=[qspec, pl.BlockSpec((None, ncp, dh), lambda i, j: (i, 0, 0)),
                  pl.BlockSpec((None, dh, ncp), lambda i, j: (i, 0, 0)),
                  pl.BlockSpec(ovt.shape, lambda i, j: (0, 0))],
        out_specs=[qspec, pl.BlockSpec((None, None, nbp, tq), lambda i, j: (i, j, 0, 0))],
        out_shape=[jax.ShapeDtypeStruct(qt.shape, F32), jax.ShapeDtypeStruct((BG, nq, nbp, tq), BF16)],
        compiler_params=_params("parallel", "parallel"), name="nsa_cmp_select",
    )(qt, kc, vct, ovt)


def _overlap_matrix_t(nc, nsel, ncp, nr):
    i = jnp.arange(nc, dtype=jnp.int32)[None, :]
    j = jnp.arange(nsel, dtype=jnp.int32)[:, None]
    lo = jnp.maximum(i * CMP_STRIDE, j * SEL_BLOCK)
    hi = jnp.minimum(i * CMP_STRIDE + CMP_LEN, (j + 1) * SEL_BLOCK)
    ov = jnp.clip(hi - lo, 0).astype(F32) / CMP_LEN
    return jnp.pad(ov, ((0, nr - nsel), (0, ncp - nc)))


def _means_kernel(n_blk, k_ref, o_ref):
    o_ref[...] = jnp.zeros(o_ref.shape, F32)
    for n in range(n_blk):
        o_ref[n:n + 1, :] = jnp.sum(k_ref[n * MOBA_BLOCK:(n + 1) * MOBA_BLOCK, :], axis=0,
                                    keepdims=True) / MOBA_BLOCK


def _block_means(k, n_blk, nbp):
    BG, Lp, dh = k.shape
    return pl.pallas_call(
        functools.partial(_means_kernel, n_blk), grid=(BG,),
        in_specs=[pl.BlockSpec((None, Lp, dh), lambda i: (i, 0, 0))],
        out_specs=pl.BlockSpec((None, nbp, dh), lambda i: (i, 0, 0)),
        out_shape=jax.ShapeDtypeStruct((BG, nbp, dh), F32), compiler_params=_params("parallel"), name="moba_means",
    )(k)


def _bmm(a, b, exact=False):
    dims = (((2,), (1,)), ((0,), (0,)))
    if exact:
        return lax.dot_general(a.astype(F32), b.astype(F32), dims, precision=HIGHEST, preferred_element_type=F32)
    return lax.dot_general(a.astype(BF16), b.astype(BF16), dims, preferred_element_type=F32)


def _flash_kernel(mode, tq, n_rep, tk, q_base, k_base, n_blk, tok_major, *refs):
    if mode == "sel":
        q_ref, ke_ref, vt_ref, bias_ref, o_ref, lhs_scr, m_scr, acc_scr = refs
    elif mode == "win":
        q_ref, ke_ref, vt_ref, o_ref, lhs_scr, m_scr, acc_scr = refs
    else:
        q_ref, ke_ref, vt_ref, mean_ref, o_ref, lhs_scr, m_scr, acc_scr = refs
    qi = pl.program_id(1)
    rows = n_rep * tq
    q0 = q_base + qi * tq
    t_row = q0 + lax.broadcasted_iota(jnp.int32, (1, 1, rows), 2) % tq
    q = _load_queries(q_ref, n_rep, True)[None] if tok_major else q_ref[...]
    q = q * HEAD_DIM ** -0.5
    lhs_scr[:, 0:HEAD_DIM, :] = q.astype(BF16)
    if mode == "sel":
        lhs_scr[:, HEAD_DIM:, :] = jnp.concatenate([bias_ref[...]] * n_rep, axis=2)
    elif mode == "moba":
        nr = _round_up(n_blk, SUBLANES)
        nbp = mean_ref.shape[1]
        gate = _bmm(mean_ref[:, 0:nr, :], q, exact=True)
        blk = lax.broadcasted_iota(jnp.int32, (1, nr, 1), 1)
        cur = t_row // MOBA_BLOCK
        gate = jnp.where(blk < cur, gate, NEG_INF)
        rank = _rank_rows(gate, blk, n_blk)
        picked = jnp.where(blk < cur, jnp.where(rank < min(MOBA_TOPK, n_blk), 0.0, NEG_BIG), NEG_BIG)
        bias = jnp.where(blk == cur, 0.0, picked)
        if nbp > nr:
            bias = jnp.concatenate([bias, jnp.zeros((bias.shape[0], nbp - nr, rows), F32)], axis=1)
        lhs_scr[:, HEAD_DIM:, :] = bias.astype(BF16)
    m_scr[...] = jnp.full(m_scr.shape, NEG_BIG, F32)
    acc_scr[...] = jnp.zeros(acc_scr.shape, F32)

    def tile(ki, positional):
        k0 = pl.multiple_of(ki * tk, tk)
        s = _bmm(ke_ref[:, pl.ds(k0, tk), :], lhs_scr[...])
        if positional:
            kpos = k_base + k0 + lax.broadcasted_iota(jnp.int32, (1, tk, 1), 1)
            s = jnp.where(kpos <= t_row, s, NEG_BIG)
            if mode == "win":
                s = jnp.where(kpos > t_row - WINDOW, s, NEG_BIG)
        m_prev = m_scr[...]
        m_new = jnp.maximum(m_prev, jnp.max(s, axis=1, keepdims=True))
        p = jnp.exp(s - m_new)
        acc_scr[...] = jnp.exp(m_prev - m_new) * acc_scr[...] + _bmm(vt_ref[:, :, pl.ds(k0, tk)], p)
        m_scr[...] = m_new

    def body(positional):
        def f(ki, carry):
            tile(ki, positional)
            return carry
        return f

    last = (q0 + tq - 1 - k_base) // tk
    if mode == "win":
        lax.fori_loop(jnp.maximum(q0 - (WINDOW - 1) - k_base, 0) // tk, last + 1, body(True), 0)
    else:
        lax.fori_loop(0, last, body(False), 0)
        tile(last, True)
    acc = acc_scr[...]
    o = acc[:, 0:HEAD_DIM] / jnp.maximum(acc[:, HEAD_DIM:HEAD_DIM + 1], 1e-30)
    if tok_major:
        _store_outputs(o_ref, o[0], n_rep, True)
    else:
        o_ref[...] = o


def _flash(mode, q, ke, vt, extra, tq, q_base, k_base, n_blk, n_groups, tk=ATT_K_TILE, group_batch=1):
    gb = group_batch
    qt, tok_major, BG, nq, n_rep, qspec = _query_layout(q, tq, n_groups, gb)
    dh = HEAD_DIM
    rows = n_rep * tq
    Lp, kw = ke.shape[1:]
    assert Lp % tk == 0 and (q_base + nq * tq - 1 - k_base) // tk < Lp // tk
    assert tk % tq == 0 and (q_base - k_base) % tq == 0 and BG % gb == 0
    ins = [qt, ke, vt]
    in_specs = [qspec, pl.BlockSpec((gb, Lp, kw), lambda i, j: (i, 0, 0)),
                pl.BlockSpec((gb, V_ROWS, Lp), lambda i, j: (i, 0, 0))]
    if mode == "sel":
        ins.append(extra)
        in_specs.append(pl.BlockSpec((gb, None, kw - dh, tq), lambda i, j: (i, j, 0, 0)))
    if mode == "moba":
        ins.append(extra)
        in_specs.append(pl.BlockSpec((gb, kw - dh, dh), lambda i, j: (i, 0, 0)))
    scratch = [pltpu.VMEM((gb, kw, rows), BF16), pltpu.VMEM((gb, 1, rows), F32),
               pltpu.VMEM((gb, V_ROWS, rows), F32)]
    return pl.pallas_call(
        functools.partial(_flash_kernel, mode, tq, n_rep, tk, q_base, k_base, n_blk, tok_major),
        grid=(BG // gb, nq), in_specs=in_specs, out_specs=qspec, out_shape=jax.ShapeDtypeStruct(qt.shape, F32),
        scratch_shapes=scratch, compiler_params=_params("parallel", "parallel"), name="flash_" + mode,
    )(*ins)


def _group_queries_t(q, n_groups, tq):
    B, T, n = q.shape
    R = n // HEAD_DIM // n_groups
    q = q.reshape(B, T // tq, tq, n_groups, R, HEAD_DIM).transpose(0, 3, 1, 5, 4, 2)
    return q.reshape(B * n_groups, T // tq, HEAD_DIM, R * tq)


def _ungroup_t(o, B, n_groups, tq):
    BG, nq, dh, rows = o.shape
    R = rows // tq
    o = o.reshape(B, n_groups, nq, dh, R, tq).transpose(0, 2, 5, 1, 4, 3)
    return o.reshape(B, nq * tq, n_groups * R * dh)


def _head_major(x, Lp):
    B, L, G, dh = x.shape
    x = jnp.pad(x.transpose(0, 2, 1, 3), ((0, 0), (0, 0), (0, Lp - L), (0, 0)))
    return x.reshape(B * G, Lp, dh)


def _key_operand(x, Lp, block=None, nbp=0, first_pos=0):
    k = _head_major(x, Lp).astype(BF16)
    if block is None:
        return k
    onehot = ((first_pos + jnp.arange(Lp))[:, None] // block == jnp.arange(nbp)[None, :]).astype(BF16)
    return jnp.concatenate([k, jnp.broadcast_to(onehot[None], (k.shape[0], Lp, nbp))], axis=-1)


def _value_operand(x, Lp):
    B, L, G, dh = x.shape
    v = jnp.pad(x.transpose(0, 2, 3, 1), ((0, 0), (0, 0), (0, 0), (0, Lp - L))).reshape(B * G, dh, Lp)
    return jnp.concatenate([v, jnp.ones((B * G, V_ROWS - dh, Lp), F32)], axis=1).astype(BF16)


def _bias_rows(n):
    return _round_up(HEAD_DIM + n, LANES) - HEAD_DIM


def _sel_blocks(n_pos):
    n_sel = -(-n_pos // SEL_BLOCK)
    return n_sel, _bias_rows(n_sel)


def _nsa_branches(q, n_q, cmp_x, cmp_specs, n_seq, n_rows, ke_sel, vt_sel, win_kv, cw, tq, q_base, win_base, tk,
                  sel_batch):
    G = NSA_KV_HEADS
    w1, pe, w2 = cw
    n_cmp_rows = n_rows // CMP_STRIDE * CMP_STRIDE
    kc = _nsa_compress(cmp_x, cmp_specs[0], n_seq, n_cmp_rows, w1[0], pe[0], w2[0], False)
    vct = _nsa_compress(cmp_x, cmp_specs[1], n_seq, n_cmp_rows, w1[1], pe[1], w2[1], True)
    n_sel, nbp = _sel_blocks(max(n_rows, q_base + n_q))
    ovt = _overlap_matrix_t(n_rows // CMP_STRIDE - 1, n_sel, kc.shape[1], _round_up(n_sel, SUBLANES))
    oc, bias = _cmp_select(q, kc, vct, ovt, tq, q_base, n_sel, nbp)
    o_s = _flash("sel", q, ke_sel, vt_sel, bias, tq, q_base, 0, n_sel, G, tk, sel_batch)
    Lwp = _round_up(max(win_kv.shape[1], q_base + n_q - win_base), ATT_K_TILE)
    o_w = _flash("win", q, _key_operand(win_kv[:, :, 0], Lwp), _value_operand(win_kv[:, :, 1], Lwp), None, tq,
                 q_base, win_base, 0, G)
    return oc, o_s, o_w


def _nsa_prompt(q, kv6, cw, tq):
    B, T = q.shape[:2]
    width = NSA_KV_HEADS * HEAD_DIM
    assert T % CMP_STRIDE == 0
    cmp_specs = [pl.BlockSpec((T, width), lambda b, s=s: (b, s)) for s in (0, 1)]
    _, nbp = _sel_blocks(T)
    Lp = _round_up(T, ATT_K_TILE)
    return _nsa_branches(q, T, kv6.reshape(B * T, -1), cmp_specs, B, T, _key_operand(kv6[:, :, 2], Lp, SEL_BLOCK, nbp),
                         _value_operand(kv6[:, :, 3], Lp), kv6[:, :, 4:6], cw, tq, 0, 0, ATT_K_TILE, 1)


def _nsa_sample(q, kv6, cache, page_table, win_buf, cw, tq):
    Bs, Ts = q.shape[:2]
    page = cache.shape[-1]
    past_len = page_table.shape[1] * page
    step = PAGES_PER_STEP * page
    assert past_len % CMP_STRIDE == 0 and Ts < CMP_STRIDE and tq <= step
    _, nbp = _sel_blocks(past_len + tq)
    tail_ke = _key_operand(kv6[:, :, 2], step, SEL_BLOCK, nbp, past_len)
    tail_vt = _value_operand(kv6[:, :, 3], step)
    rows, ke, vt, _ = _gather_kv(cache, page_table, NSA_KV_HEADS, (0, 1), 2, 3, SEL_BLOCK, tail_ke, tail_vt, False)
    win = jnp.concatenate([win_buf, kv6[:, :, 4:6]], axis=1)
    qt = _group_queries_t(_pad_queries(q, tq), NSA_KV_HEADS, tq)
    cmp_specs = [pl.BlockSpec((None, None) + rows.shape[2:], lambda b, s=s: (s, b, 0, 0)) for s in (0, 1)]
    outs = _nsa_branches(qt, tq, rows, cmp_specs, Bs, past_len + Ts, ke, vt, win, cw, tq, past_len,
                         past_len - win_buf.shape[1], step, NSA_KV_HEADS)
    return [_ungroup_t(o, Bs, NSA_KV_HEADS, tq)[:, :Ts] for o in outs], win[:, Ts:]


def _moba_prompt(q, kv, tq):
    B, T, _ = q.shape
    n_blk = -(-T // MOBA_BLOCK)
    nbp = _bias_rows(n_blk)
    Lp = _round_up(T, ATT_K_TILE)
    means = _block_means(_head_major(kv[:, :, 0], Lp), n_blk, nbp)
    return _flash("moba", q, _key_operand(kv[:, :, 0], Lp, MOBA_BLOCK, nbp), _value_operand(kv[:, :, 1], Lp), means,
                  tq, 0, 0, n_blk, MOBA_KV_HEADS)


def _moba_sample(q, kv, cache, page_table, tq):
    Bs, Ts = q.shape[:2]
    page = cache.shape[-1]
    past_len = page_table.shape[1] * page
    step = PAGES_PER_STEP * page
    assert step % MOBA_BLOCK == 0 and tq <= MOBA_BLOCK
    n_blk = -(-(past_len + tq) // MOBA_BLOCK)
    nbp = _bias_rows(n_blk)
    tail_ke = _key_operand(kv[:, :, 0], step, MOBA_BLOCK, nbp, past_len)
    tail_vt = _value_operand(kv[:, :, 1], step)
    _, ke, vt, means = _gather_kv(cache, page_table, MOBA_KV_HEADS, (), 0, 1, MOBA_BLOCK, tail_ke, tail_vt, True)
    means = jnp.pad(means, ((0, 0), (0, max(nbp - means.shape[1], 0)), (0, 0)))[:, :nbp]
    qt = _group_queries_t(_pad_queries(q, tq), MOBA_KV_HEADS, tq)
    o = _flash("moba", qt, ke, vt, means, tq, past_len, 0, n_blk, MOBA_KV_HEADS, step, MOBA_KV_HEADS)
    return _ungroup_t(o, Bs, MOBA_KV_HEADS, tq)[:, :Ts]


def _gather_kv_kernel(n_groups, row_slots, k_slot, v_slot, block, with_means, pt_ref, *refs):
    del pt_ref
    G = n_groups
    dh = HEAD_DIM
    pps = PAGES_PER_STEP
    pages = refs[:pps]
    tail_ke_ref, tail_vt_ref = refs[pps:pps + 2]
    outs = list(refs[pps + 2:])
    rows_ref = outs.pop(0) if row_slots else None
    ke_ref, vt_ref = outs[0], outs[1]
    means_ref = outs[2] if with_means else None
    p = pl.program_id(1)
    last = pl.num_programs(1) - 1
    page = pages[0].shape[-1]
    step, kw = ke_ref.shape[1:]

    @pl.when(p < last)
    def _():
        lane = lax.broadcasted_iota(jnp.int32, (step, kw), 1)
        row = p * step + lax.broadcasted_iota(jnp.int32, (step, kw), 0)
        onehot = jnp.where(lane - dh == row // block, 1.0, 0.0).astype(BF16)
        for g in range(G):
            ke_ref[g] = onehot
        vt_ref[:, dh:, :] = jnp.ones((G, V_ROWS - dh, step), BF16)
        ppb = block // page
        tot = None
        for j, pg in enumerate(pages):
            r0 = j * page
            for si, s in enumerate(row_slots):
                rows_ref[si, r0:r0 + page, :] = pg[s].reshape(G * dh, page).T
            k = pg[k_slot].reshape(G * dh, page).T
            for g in range(G):
                ke_ref[g, r0:r0 + page, 0:dh] = k[:, g * dh:(g + 1) * dh].astype(BF16)
                vt_ref[g, 0:dh, r0:r0 + page] = pg[v_slot, g].astype(BF16)
            if with_means:
                ksum = jnp.sum(k, axis=0, keepdims=True)
                tot = ksum if j % ppb == 0 else tot + ksum
                if j % ppb == ppb - 1:
                    n = j // ppb
                    for g in range(G):
                        means_ref[g, n:n + 1, :] = tot[:, g * dh:(g + 1) * dh] / block

    @pl.when(p == last)
    def _():
        ke_ref[...] = tail_ke_ref[...]
        vt_ref[...] = tail_vt_ref[...]
        if with_means:
            means_ref[...] = jnp.zeros(means_ref.shape, F32)


def _gather_kv(cache, page_table, n_groups, row_slots, k_slot, v_slot, block, tail_ke, tail_vt, with_means):
    _, n_slots, G, dh, page = cache.shape
    Bs, n_pages = page_table.shape
    assert G == n_groups and dh == HEAD_DIM
    pps = PAGES_PER_STEP
    step = pps * page
    assert n_pages % pps == 0
    if with_means:
        assert step % block == 0 and block % page == 0
    n_steps = n_pages // pps
    kw = tail_ke.shape[-1]
    Lp = (n_steps + 1) * step
    bps = step // block

    def page_map(j):
        return lambda b, p, pt: (pt[b, jnp.minimum(p, n_steps - 1) * pps + j], 0, 0, 0, 0)

    in_specs = [pl.BlockSpec((None, n_slots, G, dh, page), page_map(j)) for j in range(pps)]
    in_specs += [pl.BlockSpec((None, G, step, kw), lambda b, p, pt: (b, 0, 0, 0)),
                 pl.BlockSpec((None, G, V_ROWS, step), lambda b, p, pt: (b, 0, 0, 0))]
    out_shape, out_specs = [], []
    if row_slots:
        n_rs = len(row_slots)
        out_shape.append(jax.ShapeDtypeStruct((n_rs, Bs, n_steps * step, G * dh), F32))
        out_specs.append(pl.BlockSpec((n_rs, None, step, G * dh),
                                      lambda b, p, pt: (0, b, jnp.minimum(p, n_steps - 1), 0)))
    out_shape += [jax.ShapeDtypeStruct((Bs, G, Lp, kw), BF16), jax.ShapeDtypeStruct((Bs, G, V_ROWS, Lp), BF16)]
    out_specs += [pl.BlockSpec((None, G, step, kw), lambda b, p, pt: (b, 0, p, 0)),
                  pl.BlockSpec((None, G, V_ROWS, step), lambda b, p, pt: (b, 0, 0, p))]
    if with_means:
        out_shape.append(jax.ShapeDtypeStruct((Bs, G, n_steps + 1, bps, dh), F32))
        out_specs.append(pl.BlockSpec((None, G, None, bps, dh), lambda b, p, pt: (b, 0, p, 0, 0)))
    grid_spec = pltpu.PrefetchScalarGridSpec(num_scalar_prefetch=1, grid=(Bs, n_steps + 1), in_specs=in_specs,
                                             out_specs=out_specs)
    outs = list(pl.pallas_call(
        functools.partial(_gather_kv_kernel, G, tuple(row_slots), k_slot, v_slot, block, with_means),
        grid_spec=grid_spec, out_shape=out_shape, compiler_params=_params("parallel", "arbitrary"),
        name="gather_kv",
    )(page_table, *([cache] * pps), tail_ke.reshape(Bs, G, step, kw), tail_vt.reshape(Bs, G, V_ROWS, step)))
    rows = outs.pop(0) if row_slots else None
    ke = outs[0].reshape(Bs * G, Lp, kw)
    vt = outs[1].reshape(Bs * G, V_ROWS, Lp)
    means = outs[2].reshape(Bs * G, (n_steps + 1) * bps, dh) if with_means else None
    return rows, ke, vt, means


def _pad_queries(x, tq):
    return jnp.pad(x, ((0, 0), (0, tq - x.shape[1]), (0, 0)))


def kernel(x_prompt, x_sample, cache_nsa_kv, cache_moba_kv, state_win_kv, state_wkv, state_shift, page_table, norm_mix, norm_ffn, norm_final, even_w_in, even_w_out, rwkv_mu, rwkv_w0, rwkv_w_up, rwkv_a0, rwkv_a_up, rwkv_g_up, rwkv_k_k, rwkv_k_a, rwkv_r_k, rwkv_ln_g, rwkv_ln_b, nsa_gate_b, nsa_cmp_w1, nsa_cmp_pe, nsa_cmp_w2, odd_w_in, odd_w_out, ffn_w_gate, ffn_w_up, ffn_w_down):
    B, T, D = x_prompt.shape
    Bs, Ts, _ = x_sample.shape
    depth = norm_mix.shape[0]
    page = cache_nsa_kv.shape[2]
    past_len = page_table.shape[1] * page
    rwkv_dim = rwkv_w0.shape[1]
    rwkv_cols = rwkv_mu.shape[1]
    nsa_heads = nsa_gate_b.shape[1] // 3
    nsa_dim = nsa_heads * HEAD_DIM
    nsa_kv_cols = 6 * NSA_KV_HEADS * HEAD_DIM
    moba_kv_cols = 2 * MOBA_KV_HEADS * HEAD_DIM
    moba_dim = odd_w_in.shape[2] - moba_kv_cols
    tq_p = min(ATT_Q_TILE, T)
    tq_s = SAMPLE_Q_PAD

    cos_p, sin_p = _rope_tables(jnp.arange(T, dtype=jnp.int32))
    pos_s = past_len + jnp.arange(Ts, dtype=jnp.int32)
    cos_s, sin_s = _rope_tables(jnp.tile(pos_s, Bs))

    even_spec = ((rwkv_cols, (), False), (nsa_dim, tuple(range(nsa_dim // LANES)), False),
                 (nsa_kv_cols, tuple(range(0, nsa_kv_cols // LANES, 2)), False), (LANES, (), True))
    k_chunks = MOBA_KV_HEADS * HEAD_DIM // LANES
    odd_spec = ((moba_dim, tuple(range(moba_dim // LANES)), False), (moba_kv_cols, tuple(range(k_chunks)), False))

    hp = x_prompt.reshape(B * T, D)
    hs = x_sample.reshape(Bs * Ts, D)
    nsa_p, nsa_s, moba_p, moba_s = [], [], [], []
    win_p, win_s, wkv_p, wkv_s, sh_p, sh_s = [], [], [], [], [], []
    for layer in range(depth):
        i = layer // 2
        if layer % 2 == 0:
            w_in = even_w_in[i].astype(BF16)
            o = rwkv_cols
            n_gate = 3 * nsa_heads
            weights = [w_in[:, :o], w_in[:, o:o + nsa_dim], w_in[:, o + nsa_dim:o + nsa_dim + nsa_kv_cols],
                       jnp.pad(w_in[:, o + nsa_dim + nsa_kv_cols:], ((0, 0), (0, LANES - n_gate)))]
            gate_b = jnp.pad(nsa_gate_b[i], (0, LANES - n_gate)).reshape(1, LANES)
            rp = (rwkv_mu[i], rwkv_w0[i], rwkv_w_up[i], rwkv_a0[i], rwkv_a_up[i], rwkv_g_up[i],
                  rwkv_k_k[i], rwkv_k_a[i], rwkv_r_k[i], rwkv_ln_g[i], rwkv_ln_b[i])
            cw = (nsa_cmp_w1[i], nsa_cmp_pe[i], nsa_cmp_w2[i])

            rw, q, kv, gates = _project(hp, norm_mix[layer], cos_p, sin_p, weights, [gate_b], even_spec, "even_proj")
            rw3 = rw.reshape(B, T, rwkv_cols)
            kv6 = kv.reshape(B, T, 6, NSA_KV_HEADS, HEAD_DIM)
            y, g, wkv_new = _rwkv_mix(rw3, jnp.zeros((B, rwkv_cols), F32),
                                      jnp.zeros((B, rwkv_dim // HEAD_DIM, HEAD_DIM, HEAD_DIM), F32), rp, SCAN_CHUNK)
            o_c, o_s, o_w = _nsa_prompt(q.reshape(B, T, nsa_dim), kv6, cw, tq_p)
            hp = _even_out(hp, y, g, o_c.reshape(B * T, nsa_dim), o_s.reshape(B * T, nsa_dim),
                           o_w.reshape(B * T, nsa_dim), gates, even_w_out[i])
            nsa_p.append(kv6[:, :, :4])
            win_p.append(kv6[:, T - min(WINDOW, T):, 4:6])
            wkv_p.append(wkv_new)
            sh_p.append(rw3[:, -1])

            rw, q, kv, gates = _project(hs, norm_mix[layer], cos_s, sin_s, weights, [gate_b], even_spec, "even_proj_s")
            rw3 = rw.reshape(Bs, Ts, rwkv_cols)
            kv6 = kv.reshape(Bs, Ts, 6, NSA_KV_HEADS, HEAD_DIM)
            y, g, wkv_new = _rwkv_mix(rw3, state_shift[i], state_wkv[i], rp, SUBLANES)
            outs, win_new = _nsa_sample(q.reshape(Bs, Ts, nsa_dim), kv6,
                                        cache_nsa_kv[i].transpose(0, 2, 3, 4, 1), page_table,
                                        state_win_kv[i], cw, tq_s)
            o_c, o_s, o_w = [t.reshape(Bs * Ts, nsa_dim) for t in outs]
            hs = _even_out(hs, y, g, o_c, o_s, o_w, gates, even_w_out[i])
            nsa_s.append(kv6[:, :, :4])
            win_s.append(win_new)
            wkv_s.append(wkv_new)
            sh_s.append(rw3[:, -1])
        else:
            w_in = odd_w_in[i].astype(BF16)
            weights = [w_in[:, :moba_dim], w_in[:, moba_dim:]]
            q, kv = _project(hp, norm_mix[layer], cos_p, sin_p, weights, [], odd_spec, "odd_proj")
            kv2 = kv.reshape(B, T, 2, MOBA_KV_HEADS, HEAD_DIM)
            a = _moba_prompt(q.reshape(B, T, moba_dim), kv2, tq_p)
            hp = _odd_out(hp, a.reshape(B * T, moba_dim), odd_w_out[i])
            moba_p.append(kv2)

            q, kv = _project(hs, norm_mix[layer], cos_s, sin_s, weights, [], odd_spec, "odd_proj_s")
            kv2 = kv.reshape(Bs, Ts, 2, MOBA_KV_HEADS, HEAD_DIM)
            a = _moba_sample(q.reshape(Bs, Ts, moba_dim), kv2, cache_moba_kv[i].transpose(0, 2, 3, 4, 1),
                             page_table, tq_s)
            hs = _odd_out(hs, a.reshape(Bs * Ts, moba_dim), odd_w_out[i])
            moba_s.append(kv2)
        g_final = norm_final if layer == depth - 1 else None
        hp = _ffn(hp, norm_ffn[layer], ffn_w_gate[layer], ffn_w_up[layer], ffn_w_down[layer], g_final)
        hs = _ffn(hs, norm_ffn[layer], ffn_w_gate[layer], ffn_w_up[layer], ffn_w_down[layer], g_final)
    return (hp.reshape(B, T, D), hs.reshape(Bs, Ts, D), jnp.stack(nsa_p), jnp.stack(nsa_s), jnp.stack(moba_p),
            jnp.stack(moba_s), jnp.stack(win_p), jnp.stack(win_s), jnp.stack(wkv_p), jnp.stack(wkv_s),
            jnp.stack(sh_p), jnp.stack(sh_s))
```

```python
import functools
import math

import jax
import jax.numpy as jnp
from jax import lax
from jax.experimental import pallas as pl
from jax.experimental.pallas import tpu as pltpu

F32 = jnp.float32
BF16 = jnp.bfloat16
HIGHEST = lax.Precision.HIGHEST

HEAD_DIM = 64
NORM_EPS = 1e-6
ROPE_THETA = 10000.0
DECAY_LORA = 64
AAA_LORA = 64
GATE_LORA = 128
RWKV_GN_EPS = 64e-5
NSA_KV_HEADS = 2
CMP_STRIDE = 16
CMP_LEN = 2 * CMP_STRIDE
SEL_BLOCK = 64
SEL_TOPN = 16
WINDOW = 512
FORCE_BONUS = 100.0
MOBA_KV_HEADS = 4
MOBA_BLOCK = 256
MOBA_TOPK = 3

LANES = 128
SUBLANES = 8
VMEM_LIMIT = 56 * 1024 * 1024

ROW_TILE = 512
FFN_ROW_TILE = 1024
FFN_COL_TILE = 256
ATT_Q_TILE = 512
ATT_K_TILE = 512
SCAN_CHUNK = 64
SCAN_CHUNKS_PER_STEP = 4
SAMPLE_Q_PAD = 32
PAGES_PER_STEP = 8

NT = (((1,), (1,)), ((), ()))
TN = (((0,), (0,)), ((), ()))
NEG_INF = float("-inf")
NEG_BIG = -1e30
V_ROWS = HEAD_DIM + 16


def _round_up(x, m):
    return -(-x // m) * m


def _mm(a, b, dims=None, exact=False):
    if dims is None:
        dims = (((a.ndim - 1,), (0,)), ((), ()))
    if exact:
        return lax.dot_general(a.astype(F32), b.astype(F32), dims, precision=HIGHEST,
                               preferred_element_type=F32)
    return lax.dot_general(a.astype(BF16), b.astype(BF16), dims, preferred_element_type=F32)


def _params(*sem):
    return pltpu.CompilerParams(dimension_semantics=sem, vmem_limit_bytes=VMEM_LIMIT)


def _rms_norm(x, g):
    return x * lax.rsqrt(jnp.mean(x * x, axis=-1, keepdims=True) + NORM_EPS) * g


def _rope_chunk(x, cos, sin):
    lane = lax.broadcasted_iota(jnp.int32, x.shape, 1)
    half = HEAD_DIM // 2
    partner = jnp.where((lane % HEAD_DIM) < half, pltpu.roll(x, LANES - half, 1), pltpu.roll(x, half, 1))
    return x * cos + partner * sin


def _rank_rows(score, blk, ncand):
    rank = jnp.zeros(score.shape, F32)
    for j in range(ncand):
        row = score[..., j:j + 1, :]
        rank = rank + jnp.where(row > score, 1.0, 0.0) + jnp.where(row == score, jnp.where(blk > j, 1.0, 0.0), 0.0)
    return rank


def _proj_kernel(spec, x_ref, g_ref, cos_ref, sin_ref, *refs):
    nseg = len(spec)
    w_refs = refs[:nseg]
    nbias = sum(1 for s in spec if s[2])
    b_refs = list(refs[nseg:nseg + nbias])
    o_refs = refs[nseg + nbias:]
    xn = _rms_norm(x_ref[...], g_ref[...]).astype(BF16)
    for (ncols, rope_chunks, sig), w_ref, o_ref in zip(spec, w_refs, o_refs):
        y = jnp.dot(xn, w_ref[...], preferred_element_type=F32)
        if sig:
            y = jax.nn.sigmoid(y + b_refs.pop(0)[...])
        if rope_chunks:
            cos = cos_ref[...]
            sin = sin_ref[...]
            for c in range(ncols // LANES):
                yc = y[:, c * LANES:(c + 1) * LANES]
                if c in rope_chunks:
                    yc = _rope_chunk(yc, cos, sin)
                o_ref[:, c * LANES:(c + 1) * LANES] = yc
        else:
            o_ref[...] = y


def _project(x, g, cos_tab, sin_tab, weights, biases, spec, name):
    M, D = x.shape
    tm = min(ROW_TILE, M)
    ntab = cos_tab.shape[0] // tm
    in_specs = [pl.BlockSpec((tm, D), lambda i: (i, 0)),
                pl.BlockSpec((1, D), lambda i: (0, 0)),
                pl.BlockSpec((tm, LANES), lambda i: (i % ntab, 0)),
                pl.BlockSpec((tm, LANES), lambda i: (i % ntab, 0))]
    in_specs += [pl.BlockSpec(w.shape, lambda i: (0, 0)) for w in weights]
    in_specs += [pl.BlockSpec(b.shape, lambda i: (0, 0)) for b in biases]
    out_shape = [jax.ShapeDtypeStruct((M, s[0]), F32) for s in spec]
    out_specs = [pl.BlockSpec((tm, s[0]), lambda i: (i, 0)) for s in spec]
    return pl.pallas_call(
        functools.partial(_proj_kernel, spec), grid=(M // tm,), in_specs=in_specs, out_specs=out_specs,
        out_shape=out_shape, compiler_params=_params("parallel"), name=name,
    )(x, g.reshape(1, D), cos_tab, sin_tab, *weights, *biases)


def _rope_tables(pos):
    half = HEAD_DIM // 2
    inv = ROPE_THETA ** (-jnp.arange(half, dtype=F32) / half)
    ang = pos.astype(F32)[:, None] * inv[None, :]
    cos = jnp.cos(ang)
    sin = jnp.sin(ang)
    cos_t = jnp.tile(cos, (1, LANES // half))
    sin_t = jnp.tile(jnp.concatenate([-sin, sin], axis=1), (1, LANES // HEAD_DIM))
    return cos_t, sin_t


def _even_out_kernel(res_ref, y_ref, g_ref, oc_ref, os_ref, ow_ref, gt_ref, ec_ref, es_ref, ew_ref,
                     w1_ref, w2_ref, o_ref):
    a = y_ref[...] * g_ref[...]
    gt = gt_ref[...]
    b = (_mm(gt, ec_ref[...], exact=True) * oc_ref[...] + _mm(gt, es_ref[...], exact=True) * os_ref[...]
         + _mm(gt, ew_ref[...], exact=True) * ow_ref[...])
    o_ref[...] = res_ref[...] + _mm(a, w1_ref[...]) + _mm(b, w2_ref[...])


def _even_out(res, y, g, o_c, o_s, o_w, gates, w_out):
    M, D = res.shape
    n_rw = y.shape[1]
    n_nsa = o_c.shape[1]
    tm = min(ROW_TILE, M)
    heads = n_nsa // HEAD_DIM
    col = jnp.arange(n_nsa)[None, :] // HEAD_DIM
    row = jnp.arange(LANES)[:, None]
    expand = [(row == col * 3 + br).astype(F32) for br in range(3)]
    del heads
    row_spec = lambda n: pl.BlockSpec((tm, n), lambda i: (i, 0))
    full = lambda a: pl.BlockSpec(a.shape, lambda i: (0, 0))
    w1 = w_out[:n_rw].astype(BF16)
    w2 = w_out[n_rw:].astype(BF16)
    return pl.pallas_call(
        _even_out_kernel, grid=(M // tm,),
        in_specs=[row_spec(D), row_spec(n_rw), row_spec(n_rw), row_spec(n_nsa), row_spec(n_nsa), row_spec(n_nsa),
                  row_spec(LANES), full(expand[0]), full(expand[1]), full(expand[2]), full(w1), full(w2)],
        out_specs=row_spec(D), out_shape=jax.ShapeDtypeStruct((M, D), F32),
        compiler_params=_params("parallel"), name="even_out",
    )(res, y, g, o_c, o_s, o_w, gates, *expand, w1, w2)


def _odd_out_kernel(res_ref, a_ref, w_ref, o_ref):
    o_ref[...] = res_ref[...] + _mm(a_ref[...], w_ref[...])


def _odd_out(res, a, w_out):
    M, D = res.shape
    tm = min(ROW_TILE, M)
    w = w_out.astype(BF16)
    return pl.pallas_call(
        _odd_out_kernel, grid=(M // tm,),
        in_specs=[pl.BlockSpec((tm, D), lambda i: (i, 0)), pl.BlockSpec((tm, a.shape[1]), lambda i: (i, 0)),
                  pl.BlockSpec(w.shape, lambda i: (0, 0))],
        out_specs=pl.BlockSpec((tm, D), lambda i: (i, 0)), out_shape=jax.ShapeDtypeStruct((M, D), F32),
        compiler_params=_params("parallel"), name="odd_out",
    )(res, a, w)


def _ffn_kernel(final_norm, x_ref, g_ref, wg_ref, wu_ref, wd_ref, gf_ref, o_ref, xn_scr, acc_scr):
    j = pl.program_id(1)

    @pl.when(j == 0)
    def _():
        xn_scr[...] = _rms_norm(x_ref[...], g_ref[...]).astype(BF16)
        acc_scr[...] = jnp.zeros(acc_scr.shape, F32)

    xn = xn_scr[...]
    h = jax.nn.silu(jnp.dot(xn, wg_ref[...], preferred_element_type=F32)) * jnp.dot(
        xn, wu_ref[...], preferred_element_type=F32)
    acc_scr[...] += _mm(h, wd_ref[...])

    @pl.when(j == pl.num_programs(1) - 1)
    def _():
        y = x_ref[...] + acc_scr[...]
        if final_norm:
            y = _rms_norm(y, gf_ref[...])
        o_ref[...] = y


def _ffn(x, g, wg, wu, wd, g_final=None):
    M, D = x.shape
    F = wg.shape[1]
    tm = min(FFN_ROW_TILE, M)
    tf = FFN_COL_TILE
    final_norm = g_final is not None
    gf = (g_final if final_norm else g).reshape(1, D)
    return pl.pallas_call(
        functools.partial(_ffn_kernel, final_norm), grid=(M // tm, F // tf),
        in_specs=[pl.BlockSpec((tm, D), lambda i, j: (i, 0)), pl.BlockSpec((1, D), lambda i, j: (0, 0)),
                  pl.BlockSpec((D, tf), lambda i, j: (0, j)), pl.BlockSpec((D, tf), lambda i, j: (0, j)),
                  pl.BlockSpec((tf, D), lambda i, j: (j, 0)), pl.BlockSpec((1, D), lambda i, j: (0, 0))],
        out_specs=pl.BlockSpec((tm, D), lambda i, j: (i, 0)), out_shape=jax.ShapeDtypeStruct((M, D), F32),
        scratch_shapes=[pltpu.VMEM((tm, D), BF16), pltpu.VMEM((tm, D), F32)],
        compiler_params=_params("parallel", "arbitrary"), name="ffn",
    )(x, g.reshape(1, D), wg.astype(BF16), wu.astype(BF16), wd.astype(BF16), gf)


def _rwkv_pre_kernel(n_dim, rw_ref, prev_ref, mu_ref, w0_ref, a0_ref, kk_ref, ka_ref, wup_ref, aup_ref, gup_ref,
                     hsum_ref, r_out, lw_out, k_out, v_out, kk_out, b_out, g_out):
    rw = rw_ref[...]
    if prev_ref.shape[0] == rw.shape[0]:
        prev = prev_ref[...]
    else:
        first = lax.broadcasted_iota(jnp.int32, (rw.shape[0], 1), 0) == 0
        prev = jnp.where(first, prev_ref[...], pltpu.roll(rw, 1, 0))
    xm = rw + (prev - rw) * mu_ref[...]
    r = xm[:, :n_dim]
    k = xm[:, n_dim:2 * n_dim]
    v = xm[:, 2 * n_dim:3 * n_dim]
    lora = xm[:, 3 * n_dim:3 * n_dim + DECAY_LORA + AAA_LORA]
    xg = xm[:, 3 * n_dim + DECAY_LORA + AAA_LORA:]
    lw = -math.exp(-0.5) * jax.nn.sigmoid(w0_ref[...] + _mm(jnp.tanh(lora), wup_ref[...], exact=True))
    a = jax.nn.sigmoid(a0_ref[...] + _mm(lora, aup_ref[...], exact=True))
    g = _mm(jax.nn.sigmoid(xg), gup_ref[...])
    kk = k * kk_ref[...]
    sq = kk * kk
    sq_hi = sq.astype(BF16)
    norm = jnp.sqrt(_mm(sq_hi, hsum_ref[...]) + _mm(sq - sq_hi.astype(F32), hsum_ref[...]))
    kk = kk / jnp.maximum(norm, 1e-12)
    g_out[...] = g
    outs = ((r_out, r), (lw_out, lw), (k_out, k * (1.0 + (a - 1.0) * ka_ref[...])), (v_out, v), (kk_out, kk),
            (b_out, kk * a))
    for o_ref, val in outs:
        if len(o_ref.shape) == 2:
            o_ref[...] = val
        else:
            for h in range(o_ref.shape[0]):
                o_ref[h] = val[:, h * HEAD_DIM:(h + 1) * HEAD_DIM]


def _rwkv_pre(rw3, shift_prev, mu, w0, w_up, a0, a_up, g_up, k_k, k_a):
    B, seq_len, ncols = rw3.shape
    M = B * seq_len
    rw = rw3.reshape(M, ncols)
    n_dim = w0.shape[0]
    tm = min(ROW_TILE, M)
    H = n_dim // HEAD_DIM
    head_major = seq_len % tm == 0
    nt = seq_len // tm if head_major else 1
    if head_major:
        prev = jnp.concatenate([shift_prev[:, None], rw3[:, tm - 1:seq_len - 1:tm]], axis=1).reshape(B * nt, 1, ncols)
        prev_spec = pl.BlockSpec((None, 1, ncols), lambda i: (i, 0, 0))
    else:
        prev = jnp.concatenate([shift_prev[:, None], rw3[:, :-1]], axis=1).reshape(M, ncols)
        prev_spec = pl.BlockSpec((tm, ncols), lambda i: (i, 0))
    zeros = jnp.zeros((AAA_LORA, n_dim), F32)
    wup_pad = jnp.concatenate([w_up, zeros], axis=0)
    aup_pad = jnp.concatenate([jnp.zeros((DECAY_LORA, n_dim), F32), a_up], axis=0)
    head = jnp.arange(n_dim) // HEAD_DIM
    hsum = (head[:, None] == head[None, :]).astype(F32)
    vec = lambda a: a.reshape(1, -1)
    row = lambda n: pl.BlockSpec((tm, n), lambda i: (i, 0))
    full = lambda a: pl.BlockSpec(a.shape, lambda i: (0, 0))
    ins = [rw, prev, vec(mu), vec(w0), vec(a0), vec(k_k), vec(k_a), wup_pad, aup_pad, g_up, hsum]
    if head_major:
        seq_spec = pl.BlockSpec((None, H, tm, HEAD_DIM), lambda i: (i // nt, 0, i % nt, 0))
        seq_shape = jax.ShapeDtypeStruct((M // seq_len, H, seq_len, HEAD_DIM), F32)
    else:
        seq_spec, seq_shape = row(n_dim), jax.ShapeDtypeStruct((M, n_dim), F32)
    outs = pl.pallas_call(
        functools.partial(_rwkv_pre_kernel, n_dim), grid=(M // tm,),
        in_specs=[row(ncols), prev_spec] + [full(a) for a in ins[2:]],
        out_specs=[seq_spec] * 6 + [row(n_dim)],
        out_shape=[seq_shape] * 6 + [jax.ShapeDtypeStruct((M, n_dim), F32)],
        compiler_params=_params("parallel"), name="rwkv_pre",
    )(*ins)
    if head_major:
        outs = [t.reshape(-1, seq_len, HEAD_DIM) for t in outs[:6]] + [outs[6]]
    return outs, head_major


def _rwkv_chunk_kernel(n_heads, chunk, r_ref, lw_ref, k_ref, v_ref, kk_ref, b_ref, rk_ref,
                       rw_ref, y0_ref, bonus_ref, a_ref, s1_ref):
    C = chunk
    dh = HEAD_DIM
    ti = lax.broadcasted_iota(jnp.int32, (C, C), 0)
    si = lax.broadcasted_iota(jnp.int32, (C, C), 1)
    incl = jnp.where(si <= ti, 1.0, 0.0)
    eye = jnp.where(lax.broadcasted_iota(jnp.int32, (dh, dh), 0) == lax.broadcasted_iota(jnp.int32, (dh, dh), 1),
                    1.0, 0.0)
    n_double = max(1, math.ceil(math.log2(C)))
    heads = range(n_heads)
    r = [r_ref[h] for h in heads]
    lw = [lw_ref[h] for h in heads]
    k = [k_ref[h] for h in heads]
    v = [v_ref[h] for h in heads]
    b = [b_ref[h] for h in heads]
    cum = [_mm(incl, lw[h], exact=True) for h in heads]
    cum_end = [cum[h][C - 1:C, :] for h in heads]
    kkw = [kk_ref[h] * jnp.exp(cum[h] - lw[h]) for h in heads]
    rwc = [r[h] * jnp.exp(cum[h]) for h in heads]
    w_inv = [jnp.exp(-cum[h]) for h in heads]
    kd = [k[h] * w_inv[h] for h in heads]
    bd = [b[h] * w_inv[h] for h in heads]
    w_end = [jnp.exp(cum_end[h] - cum[h]) for h in heads]
    a_ub = [jnp.where(si < ti, _mm(kkw[h], bd[h], NT), 0.0) for h in heads]
    a_vk = [jnp.where(si < ti, _mm(kkw[h], kd[h], NT), 0.0) for h in heads]
    b_rb = [jnp.where(si <= ti, _mm(rwc[h], bd[h], NT), 0.0) for h in heads]
    b_rk = [jnp.where(si <= ti, _mm(rwc[h], kd[h], NT), 0.0) for h in heads]
    x = [jnp.concatenate([kkw[h], _mm(a_vk[h], v[h])], axis=1) for h in heads]
    p = [-a_ub[h] for h in heads]
    x = [x[h] + _mm(p[h], x[h]) for h in heads]
    for _ in range(n_double - 1):
        p = [_mm(p[h], p[h]) for h in heads]
        x = [x[h] + _mm(p[h], x[h]) for h in heads]
    for h in heads:
        bx = _mm(b_rb[h], x[h])
        rw_ref[h] = rwc[h] - bx[:, :dh]
        y0_ref[h] = _mm(b_rk[h], v[h]) - bx[:, dh:]
        kw = x[h][:, :dh]
        uv = x[h][:, dh:]
        a_ref[h] = eye * jnp.exp(cum_end[h]) - _mm(kw, b[h] * w_end[h], TN)
        s1_ref[h] = _mm(v[h], k[h] * w_end[h], TN) - _mm(uv, b[h] * w_end[h], TN)
        bonus_ref[h] = jnp.sum(r[h] * k[h] * rk_ref[h:h + 1, :], axis=-1, keepdims=True) * v[h]


def _rwkv_state_kernel(n_heads, n_sub, rw_ref, y0_ref, bonus_ref, a_ref, s1_ref, s0_ref, lng_ref, lnb_ref,
                       y_ref, s_out_ref, s_scr):
    c = pl.program_id(1)

    @pl.when(c == 0)
    def _():
        s_scr[...] = s0_ref[...]

    chunk = rw_ref.shape[2]
    for h in range(n_heads):
        s = s_scr[h]
        for j in range(n_sub):
            y = _mm(rw_ref[h, j], s, NT) + y0_ref[h, j]
            s = _mm(s, a_ref[h, j]) + s1_ref[h, j]
            mean = jnp.mean(y, axis=-1, keepdims=True)
            var = jnp.mean(jnp.square(y - mean), axis=-1, keepdims=True)
            yn = (y - mean) * lax.rsqrt(var + RWKV_GN_EPS) * lng_ref[h:h + 1, :] + lnb_ref[h:h + 1, :]
            y_ref[j * chunk:(j + 1) * chunk, h * HEAD_DIM:(h + 1) * HEAD_DIM] = yn + bonus_ref[h, j]
        s_scr[h] = s

    @pl.when(c == pl.num_programs(1) - 1)
    def _():
        s_out_ref[...] = s_scr[...]


def _rwkv_scan(r, lw, k, v, kk, b, s0, r_k, ln_g, ln_b, chunk):
    n_bh, T, dh = r.shape
    H = r_k.shape[0]
    n_chunks = T // chunk
    seq = pl.BlockSpec((H, chunk, dh), lambda i, c: (i, c, 0))
    mat = pl.BlockSpec((H, None, dh, dh), lambda i, c: (i, c, 0, 0))
    par = pl.BlockSpec((H, dh), lambda i, c: (0, 0))
    seq_shape = jax.ShapeDtypeStruct((n_bh, T, dh), F32)
    mat_shape = jax.ShapeDtypeStruct((n_bh, n_chunks, dh, dh), F32)
    rw, y0, bonus, a, s1 = pl.pallas_call(
        functools.partial(_rwkv_chunk_kernel, H, chunk), grid=(n_bh // H, n_chunks),
        in_specs=[seq] * 6 + [par], out_specs=[seq, seq, seq, mat, mat],
        out_shape=[seq_shape, seq_shape, seq_shape, mat_shape, mat_shape],
        compiler_params=_params("parallel", "parallel"), name="rwkv_chunk",
    )(r, lw, k, v, kk, b, r_k)
    n_sub = math.gcd(SCAN_CHUNKS_PER_STEP, n_chunks)
    split = lambda t: t.reshape(n_bh, n_chunks, chunk, dh)
    seq4 = pl.BlockSpec((H, n_sub, chunk, dh), lambda i, c: (i, c, 0, 0))
    mat4 = pl.BlockSpec((H, n_sub, dh, dh), lambda i, c: (i, c, 0, 0))
    state = pl.BlockSpec((H, dh, dh), lambda i, c: (i, 0, 0))
    y, s_new = pl.pallas_call(
        functools.partial(_rwkv_state_kernel, H, n_sub), grid=(n_bh // H, n_chunks // n_sub),
        in_specs=[seq4, seq4, seq4, mat4, mat4, state, par, par],
        out_specs=[pl.BlockSpec((n_sub * chunk, H * dh), lambda i, c: (i * (n_chunks // n_sub) + c, 0)), state],
        out_shape=[jax.ShapeDtypeStruct((n_bh // H * T, H * dh), F32), jax.ShapeDtypeStruct((n_bh, dh, dh), F32)],
        scratch_shapes=[pltpu.VMEM((H, dh, dh), F32)],
        compiler_params=_params("parallel", "arbitrary"), name="rwkv_state",
    )(split(rw), split(y0), split(bonus), a, s1, s0, ln_g.reshape(H, dh), ln_b.reshape(H, dh))
    return y, s_new


def _rwkv_mix(rw, shift_prev, wkv0, p, chunk):
    mu, w0, w_up, a0, a_up, g_up, k_k, k_a, r_k, ln_g, ln_b = p
    B, T, ncols = rw.shape
    H = r_k.shape[0]
    n_dim = H * HEAD_DIM
    outs, head_major = _rwkv_pre(rw, shift_prev, mu, w0, w_up, a0, a_up, g_up, k_k, k_a)
    g = outs[6]
    Tp = _round_up(T, chunk)

    def heads(t):
        if not head_major:
            t = t.reshape(B, T, H, HEAD_DIM).transpose(0, 2, 1, 3).reshape(B * H, T, HEAD_DIM)
        return jnp.pad(t, ((0, 0), (0, Tp - T), (0, 0)))

    y, s_new = _rwkv_scan(*[heads(t) for t in outs[:6]], wkv0.reshape(B * H, HEAD_DIM, HEAD_DIM), r_k, ln_g, ln_b,
                          chunk)
    y = y.reshape(B, Tp, n_dim)[:, :T].reshape(B * T, n_dim)
    return y, g, s_new.reshape(B, H, HEAD_DIM, HEAD_DIM)


def _compress_kernel(transpose_out, x_ref, w1_ref, pe_ref, w2_ref, o_ref):
    n_groups = o_ref.shape[0]
    width = x_ref.shape[1]
    nch = x_ref.shape[0] // CMP_STRIDE
    both = jnp.zeros((nch, 2 * width), F32)
    bias = jnp.zeros((1, width), F32)
    for c in range(CMP_STRIDE):
        w = w1_ref[c]
        both = both + _mm(x_ref[pl.ds(c, nch, stride=CMP_STRIDE), :], w)
        pw = _mm(pe_ref[c], w, exact=True)
        bias = bias + pw[0:1, :width] + pw[1:2, width:]
    h = jax.nn.gelu(both[:, :width] + pltpu.roll(both[:, width:], nch - 1, 0) + bias)
    if transpose_out:
        res = _mm(w2_ref[...], h, NT)
        for g in range(n_groups):
            o_ref[g] = res[g * HEAD_DIM:(g + 1) * HEAD_DIM]
    else:
        res = _mm(h, w2_ref[...])
        for g in range(n_groups):
            o_ref[g] = res[:, g * HEAD_DIM:(g + 1) * HEAD_DIM]


def _nsa_compress(x, x_spec, n_seq, n_rows, w1, pe, w2, transpose_out):
    G = NSA_KV_HEADS
    dh = HEAD_DIM
    nch = n_rows // CMP_STRIDE
    eye = jnp.eye(G, dtype=F32)


    def blockdiag(m):
        out = eye[:, None, :, None] * m[..., None, :, None, :]
        return out.reshape(m.shape[:-2] + (G * dh, G * dh))

    w1p = jnp.concatenate([blockdiag(w1[:CMP_STRIDE]), blockdiag(w1[CMP_STRIDE:])], axis=-1).astype(BF16)
    pe2 = jnp.stack([jnp.tile(pe[:CMP_STRIDE], (1, G)), jnp.tile(pe[CMP_STRIDE:], (1, G))], axis=1)
    pe2 = jnp.pad(pe2, ((0, 0), (0, SUBLANES - 2), (0, 0)))
    out_block = (None, G, dh, nch) if transpose_out else (None, G, nch, dh)
    out = pl.pallas_call(
        functools.partial(_compress_kernel, transpose_out), grid=(n_seq,),
        in_specs=[x_spec, pl.BlockSpec(w1p.shape, lambda i: (0, 0, 0)), pl.BlockSpec(pe2.shape, lambda i: (0, 0, 0)),
                  pl.BlockSpec((G * dh, G * dh), lambda i: (0, 0))],
        out_specs=pl.BlockSpec(out_block, lambda i: (i, 0, 0, 0)),
        out_shape=jax.ShapeDtypeStruct((n_seq,) + out_block[1:], F32),
        compiler_params=_params("parallel"), name="nsa_compress",
    )(x, w1p, pe2, blockdiag(w2.T if transpose_out else w2).astype(BF16))
    return out.reshape((n_seq * G,) + out_block[2:])


def _load_queries(q_ref, n_rep, tok_major):
    if not tok_major:
        return q_ref[...]
    qt = q_ref[...].T
    return jnp.concatenate([qt[r * HEAD_DIM:(r + 1) * HEAD_DIM] for r in range(n_rep)], axis=1)


def _store_outputs(o_ref, o, n_rep, tok_major):
    if not tok_major:
        o_ref[...] = o
        return
    tq = o.shape[1] // n_rep
    o_ref[...] = jnp.concatenate([o[:, r * tq:(r + 1) * tq] for r in range(n_rep)], axis=0).T


def _query_layout(q, tq, n_groups, group_batch=None):
    if q.ndim == 4:
        BG, nq, dh, rows = q.shape
        return q, False, BG, nq, rows // tq, pl.BlockSpec((group_batch, None, dh, rows), lambda i, j: (i, j, 0, 0))
    assert group_batch in (None, 1)
    B, T, n = q.shape
    G = n_groups
    n_rep = n // HEAD_DIM // G
    nq = T // tq
    spec = pl.BlockSpec((tq, n_rep * HEAD_DIM), lambda i, j: ((i // G) * nq + j, i % G))
    return q.reshape(B * T, n), True, B * G, nq, n_rep, spec


def _cmp_select_kernel(tq, n_rep, q_base, n_sel, tok_major, q_ref, kc_ref, vct_ref, ovt_ref, oc_ref, bias_ref):
    qi = pl.program_id(1)
    rows = n_rep * tq
    ncp = kc_ref.shape[0]
    nr = ovt_ref.shape[0]
    nbp = bias_ref.shape[0]
    q = _load_queries(q_ref, n_rep, tok_major) * HEAD_DIM ** -0.5
    s = _mm(kc_ref[...], q)
    t_row = q_base + qi * tq + lax.broadcasted_iota(jnp.int32, (1, rows), 1) % tq
    c_end = lax.broadcasted_iota(jnp.int32, (ncp, 1), 0) * CMP_STRIDE + (CMP_LEN - 1)
    s = jnp.where(c_end <= t_row, s, NEG_INF)
    m = jnp.max(s, axis=0, keepdims=True)
    e = jnp.exp(s - jnp.where(m == NEG_INF, 0.0, m))
    p = e / jnp.maximum(jnp.sum(e, axis=0, keepdims=True), 1e-30)
    _store_outputs(oc_ref, _mm(vct_ref[...], p), n_rep, tok_major)
    p_sum = p[:, 0:tq]
    for r in range(1, n_rep):
        p_sum = p_sum + p[:, r * tq:(r + 1) * tq]
    imp = _mm(ovt_ref[...], p_sum, exact=True)
    cur = (q_base + qi * tq + lax.broadcasted_iota(jnp.int32, (1, tq), 1)) // SEL_BLOCK
    blk = lax.broadcasted_iota(jnp.int32, (nr, 1), 0)
    forced = jnp.where(blk == cur, 1.0, jnp.where(blk == cur - 1, 1.0, jnp.where(blk == 0, 1.0, 0.0)))
    score = jnp.where(blk <= cur, imp + FORCE_BONUS * forced, NEG_INF)
    rank = _rank_rows(score, blk, n_sel)
    bias = jnp.where(blk <= cur, jnp.where(rank < min(SEL_TOPN, n_sel), 0.0, NEG_BIG), NEG_BIG)
    if nbp > nr:
        bias = jnp.concatenate([bias, jnp.zeros((nbp - nr, tq), F32)], axis=0)
    bias_ref[...] = bias.astype(BF16)


def _cmp_select(q, kc, vct, ovt, tq, q_base, n_sel, nbp):
    qt, tok_major, BG, nq, n_rep, qspec = _query_layout(q, tq, NSA_KV_HEADS)
    ncp, dh = kc.shape[1:]
    return pl.pallas_call(
        functools.partial(_cmp_select_kernel, tq, n_rep, q_base, n_sel, tok_major), grid=(BG, nq),
        in_specs=[qspec, pl.BlockSpec((None, ncp, dh), lambda i, j: (i, 0, 0)),
                  pl.BlockSpec((None, dh, ncp), lambda i, j: (i, 0, 0)),
                  pl.BlockSpec(ovt.shape, lambda i, j: (0, 0))],
        out_specs=[qspec, pl.BlockSpec((None, None, nbp, tq), lambda i, j: (i, j, 0, 0))],
        out_shape=[jax.ShapeDtypeStruct(qt.shape, F32), jax.ShapeDtypeStruct((BG, nq, nbp, tq), BF16)],
        compiler_params=_params("parallel", "parallel"), name="nsa_cmp_select",
    )(qt, kc, vct, ovt)


def _overlap_matrix_t(nc, nsel, ncp, nr):
    i = jnp.arange(nc, dtype=jnp.int32)[None, :]
    j = jnp.arange(nsel, dtype=jnp.int32)[:, None]
    lo = jnp.maximum(i * CMP_STRIDE, j * SEL_BLOCK)
    hi = jnp.minimum(i * CMP_STRIDE + CMP_LEN, (j + 1) * SEL_BLOCK)
    ov = jnp.clip(hi - lo, 0).astype(F32) / CMP_LEN
    return jnp.pad(ov, ((0, nr - nsel), (0, ncp - nc)))


def _means_kernel(n_blk, k_ref, o_ref):
    o_ref[...] = jnp.zeros(o_ref.shape, F32)
    for n in range(n_blk):
        o_ref[n:n + 1, :] = jnp.sum(k_ref[n * MOBA_BLOCK:(n + 1) * MOBA_BLOCK, :], axis=0,
                                    keepdims=True) / MOBA_BLOCK


def _block_means(k, n_blk, nbp):
    BG, Lp, dh = k.shape
    return pl.pallas_call(
        functools.partial(_means_kernel, n_blk), grid=(BG,),
        in_specs=[pl.BlockSpec((None, Lp, dh), lambda i: (i, 0, 0))],
        out_specs=pl.BlockSpec((None, nbp, dh), lambda i: (i, 0, 0)),
        out_shape=jax.ShapeDtypeStruct((BG, nbp, dh), F32), compiler_params=_params("parallel"), name="moba_means",
    )(k)


def _bmm(a, b, exact=False):
    dims = (((2,), (1,)), ((0,), (0,)))
    if exact:
        return lax.dot_general(a.astype(F32), b.astype(F32), dims, precision=HIGHEST, preferred_element_type=F32)
    return lax.dot_general(a.astype(BF16), b.astype(BF16), dims, preferred_element_type=F32)


def _flash_kernel(mode, tq, n_rep, tk, q_base, k_base, n_blk, tok_major, *refs):
    if mode == "sel":
        q_ref, ke_ref, vt_ref, bias_ref, o_ref, lhs_scr, m_scr, acc_scr = refs
    elif mode == "win":
        q_ref, ke_ref, vt_ref, o_ref, lhs_scr, m_scr, acc_scr = refs
    else:
        q_ref, ke_ref, vt_ref, mean_ref, o_ref, lhs_scr, m_scr, acc_scr = refs
    qi = pl.program_id(1)
    rows = n_rep * tq
    q0 = q_base + qi * tq
    t_row = q0 + lax.broadcasted_iota(jnp.int32, (1, 1, rows), 2) % tq
    q = _load_queries(q_ref, n_rep, True)[None] if tok_major else q_ref[...]
    q = q * HEAD_DIM ** -0.5
    lhs_scr[:, 0:HEAD_DIM, :] = q.astype(BF16)
    if mode == "sel":
        lhs_scr[:, HEAD_DIM:, :] = jnp.concatenate([bias_ref[...]] * n_rep, axis=2)
    elif mode == "moba":
        nr = _round_up(n_blk, SUBLANES)
        nbp = mean_ref.shape[1]
        gate = _bmm(mean_ref[:, 0:nr, :], q, exact=True)
        blk = lax.broadcasted_iota(jnp.int32, (1, nr, 1), 1)
        cur = t_row // MOBA_BLOCK
        gate = jnp.where(blk < cur, gate, NEG_INF)
        rank = _rank_rows(gate, blk, n_blk)
        picked = jnp.where(blk < cur, jnp.where(rank < min(MOBA_TOPK, n_blk), 0.0, NEG_BIG), NEG_BIG)
        bias = jnp.where(blk == cur, 0.0, picked)
        if nbp > nr:
            bias = jnp.concatenate([bias, jnp.zeros((bias.shape[0], nbp - nr, rows), F32)], axis=1)
        lhs_scr[:, HEAD_DIM:, :] = bias.astype(BF16)
    m_scr[...] = jnp.full(m_scr.shape, NEG_BIG, F32)
    acc_scr[...] = jnp.zeros(acc_scr.shape, F32)

    def tile(ki, positional):
        k0 = pl.multiple_of(ki * tk, tk)
        s = _bmm(ke_ref[:, pl.ds(k0, tk), :], lhs_scr[...])
        if positional:
            kpos = k_base + k0 + lax.broadcasted_iota(jnp.int32, (1, tk, 1), 1)
            s = jnp.where(kpos <= t_row, s, NEG_BIG)
            if mode == "win":
                s = jnp.where(kpos > t_row - WINDOW, s, NEG_BIG)
        m_prev = m_scr[...]
        m_new = jnp.maximum(m_prev, jnp.max(s, axis=1, keepdims=True))
        p = jnp.exp(s - m_new)
        acc_scr[...] = jnp.exp(m_prev - m_new) * acc_scr[...] + _bmm(vt_ref[:, :, pl.ds(k0, tk)], p)
        m_scr[...] = m_new

    def body(positional):
        def f(ki, carry):
            tile(ki, positional)
            return carry
        return f

    last = (q0 + tq - 1 - k_base) // tk
    if mode == "win":
        lax.fori_loop(jnp.maximum(q0 - (WINDOW - 1) - k_base, 0) // tk, last + 1, body(True), 0)
    else:
        lax.fori_loop(0, last, body(False), 0)
        tile(last, True)
    acc = acc_scr[...]
    o = acc[:, 0:HEAD_DIM] / jnp.maximum(acc[:, HEAD_DIM:HEAD_DIM + 1], 1e-30)
    if tok_major:
        _store_outputs(o_ref, o[0], n_rep, True)
    else:
        o_ref[...] = o


def _flash(mode, q, ke, vt, extra, tq, q_base, k_base, n_blk, n_groups, tk=ATT_K_TILE, group_batch=1):
    gb = group_batch
    qt, tok_major, BG, nq, n_rep, qspec = _query_layout(q, tq, n_groups, gb)
    dh = HEAD_DIM
    rows = n_rep * tq
    Lp, kw = ke.shape[1:]
    assert Lp % tk == 0 and (q_base + nq * tq - 1 - k_base) // tk < Lp // tk
    assert tk % tq == 0 and (q_base - k_base) % tq == 0 and BG % gb == 0
    ins = [qt, ke, vt]
    in_specs = [qspec, pl.BlockSpec((gb, Lp, kw), lambda i, j: (i, 0, 0)),
                pl.BlockSpec((gb, V_ROWS, Lp), lambda i, j: (i, 0, 0))]
    if mode == "sel":
        ins.append(extra)
        in_specs.append(pl.BlockSpec((gb, None, kw - dh, tq), lambda i, j: (i, j, 0, 0)))
    if mode == "moba":
        ins.append(extra)
        in_specs.append(pl.BlockSpec((gb, kw - dh, dh), lambda i, j: (i, 0, 0)))
    scratch = [pltpu.VMEM((gb, kw, rows), BF16), pltpu.VMEM((gb, 1, rows), F32),
               pltpu.VMEM((gb, V_ROWS, rows), F32)]
    return pl.pallas_call(
        functools.partial(_flash_kernel, mode, tq, n_rep, tk, q_base, k_base, n_blk, tok_major),
        grid=(BG // gb, nq), in_specs=in_specs, out_specs=qspec, out_shape=jax.ShapeDtypeStruct(qt.shape, F32),
        scratch_shapes=scratch, compiler_params=_params("parallel", "parallel"), name="flash_" + mode,
    )(*ins)


def _group_queries_t(q, n_groups, tq):
    B, T, n = q.shape
    R = n // HEAD_DIM // n_groups
    q = q.reshape(B, T // tq, tq, n_groups, R, HEAD_DIM).transpose(0, 3, 1, 5, 4, 2)
    return q.reshape(B * n_groups, T // tq, HEAD_DIM, R * tq)


def _ungroup_t(o, B, n_groups, tq):
    BG, nq, dh, rows = o.shape
    R = rows // tq
    o = o.reshape(B, n_groups, nq, dh, R, tq).transpose(0, 2, 5, 1, 4, 3)
    return o.reshape(B, nq * tq, n_groups * R * dh)


def _head_major(x, Lp):
    B, L, G, dh = x.shape
    x = jnp.pad(x.transpose(0, 2, 1, 3), ((0, 0), (0, 0), (0, Lp - L), (0, 0)))
    return x.reshape(B * G, Lp, dh)


def _key_operand(x, Lp, block=None, nbp=0, first_pos=0):
    k = _head_major(x, Lp).astype(BF16)
    if block is None:
        return k
    onehot = ((first_pos + jnp.arange(Lp))[:, None] // block == jnp.arange(nbp)[None, :]).astype(BF16)
    return jnp.concatenate([k, jnp.broadcast_to(onehot[None], (k.shape[0], Lp, nbp))], axis=-1)


def _value_operand(x, Lp):
    B, L, G, dh = x.shape
    v = jnp.pad(x.transpose(0, 2, 3, 1), ((0, 0), (0, 0), (0, 0), (0, Lp - L))).reshape(B * G, dh, Lp)
    return jnp.concatenate([v, jnp.ones((B * G, V_ROWS - dh, Lp), F32)], axis=1).astype(BF16)


def _bias_rows(n):
    return _round_up(HEAD_DIM + n, LANES) - HEAD_DIM


def _sel_blocks(n_pos):
    n_sel = -(-n_pos // SEL_BLOCK)
    return n_sel, _bias_rows(n_sel)


def _nsa_branches(q, n_q, cmp_x, cmp_specs, n_seq, n_rows, ke_sel, vt_sel, win_kv, cw, tq, q_base, win_base, tk,
                  sel_batch):
    G = NSA_KV_HEADS
    w1, pe, w2 = cw
    n_cmp_rows = n_rows // CMP_STRIDE * CMP_STRIDE
    kc = _nsa_compress(cmp_x, cmp_specs[0], n_seq, n_cmp_rows, w1[0], pe[0], w2[0], False)
    vct = _nsa_compress(cmp_x, cmp_specs[1], n_seq, n_cmp_rows, w1[1], pe[1], w2[1], True)
    n_sel, nbp = _sel_blocks(max(n_rows, q_base + n_q))
    ovt = _overlap_matrix_t(n_rows // CMP_STRIDE - 1, n_sel, kc.shape[1], _round_up(n_sel, SUBLANES))
    oc, bias = _cmp_select(q, kc, vct, ovt, tq, q_base, n_sel, nbp)
    o_s = _flash("sel", q, ke_sel, vt_sel, bias, tq, q_base, 0, n_sel, G, tk, sel_batch)
    Lwp = _round_up(max(win_kv.shape[1], q_base + n_q - win_base), ATT_K_TILE)
    o_w = _flash("win", q, _key_operand(win_kv[:, :, 0], Lwp), _value_operand(win_kv[:, :, 1], Lwp), None, tq,
                 q_base, win_base, 0, G)
    return oc, o_s, o_w


def _nsa_prompt(q, kv6, cw, tq):
    B, T = q.shape[:2]
    width = NSA_KV_HEADS * HEAD_DIM
    assert T % CMP_STRIDE == 0
    cmp_specs = [pl.BlockSpec((T, width), lambda b, s=s: (b, s)) for s in (0, 1)]
    _, nbp = _sel_blocks(T)
    Lp = _round_up(T, ATT_K_TILE)
    return _nsa_branches(q, T, kv6.reshape(B * T, -1), cmp_specs, B, T, _key_operand(kv6[:, :, 2], Lp, SEL_BLOCK, nbp),
                         _value_operand(kv6[:, :, 3], Lp), kv6[:, :, 4:6], cw, tq, 0, 0, ATT_K_TILE, 1)


def _nsa_sample(q, kv6, cache, page_table, win_buf, cw, tq):
    Bs, Ts = q.shape[:2]
    page = cache.shape[-1]
    past_len = page_table.shape[1] * page
    step = PAGES_PER_STEP * page
    assert past_len % CMP_STRIDE == 0 and Ts < CMP_STRIDE and tq <= step
    _, nbp = _sel_blocks(past_len + tq)
    tail_ke = _key_operand(kv6[:, :, 2], step, SEL_BLOCK, nbp, past_len)
    tail_vt = _value_operand(kv6[:, :, 3], step)
    rows, ke, vt, _ = _gather_kv(cache, page_table, NSA_KV_HEADS, (0, 1), 2, 3, SEL_BLOCK, tail_ke, tail_vt, False)
    win = jnp.concatenate([win_buf, kv6[:, :, 4:6]], axis=1)
    qt = _group_queries_t(_pad_queries(q, tq), NSA_KV_HEADS, tq)
    cmp_specs = [pl.BlockSpec((None, None) + rows.shape[2:], lambda b, s=s: (s, b, 0, 0)) for s in (0, 1)]
    outs = _nsa_branches(qt, tq, rows, cmp_specs, Bs, past_len + Ts, ke, vt, win, cw, tq, past_len,
                         past_len - win_buf.shape[1], step, NSA_KV_HEADS)
    return [_ungroup_t(o, Bs, NSA_KV_HEADS, tq)[:, :Ts] for o in outs], win[:, Ts:]


def _moba_prompt(q, kv, tq):
    B, T, _ = q.shape
    n_blk = -(-T // MOBA_BLOCK)
    nbp = _bias_rows(n_blk)
    Lp = _round_up(T, ATT_K_TILE)
    means = _block_means(_head_major(kv[:, :, 0], Lp), n_blk, nbp)
    return _flash("moba", q, _key_operand(kv[:, :, 0], Lp, MOBA_BLOCK, nbp), _value_operand(kv[:, :, 1], Lp), means,
                  tq, 0, 0, n_blk, MOBA_KV_HEADS)


def _moba_sample(q, kv, cache, page_table, tq):
    Bs, Ts = q.shape[:2]
    page = cache.shape[-1]
    past_len = page_table.shape[1] * page
    step = PAGES_PER_STEP * page
    assert step % MOBA_BLOCK == 0 and tq <= MOBA_BLOCK
    n_blk = -(-(past_len + tq) // MOBA_BLOCK)
    nbp = _bias_rows(n_blk)
    tail_ke = _key_operand(kv[:, :, 0], step, MOBA_BLOCK, nbp, past_len)
    tail_vt = _value_operand(kv[:, :, 1], step)
    _, ke, vt, means = _gather_kv(cache, page_table, MOBA_KV_HEADS, (), 0, 1, MOBA_BLOCK, tail_ke, tail_vt, True)
    means = jnp.pad(means, ((0, 0), (0, max(nbp - means.shape[1], 0)), (0, 0)))[:, :nbp]
    qt = _group_queries_t(_pad_queries(q, tq), MOBA_KV_HEADS, tq)
    o = _flash("moba", qt, ke, vt, means, tq, past_len, 0, n_blk, MOBA_KV_HEADS, step, MOBA_KV_HEADS)
    return _ungroup_t(o, Bs, MOBA_KV_HEADS, tq)[:, :Ts]


def _gather_kv_kernel(n_groups, row_slots, k_slot, v_slot, block, with_means, pt_ref, *refs):
    del pt_ref
    G = n_groups
    dh = HEAD_DIM
    pps = PAGES_PER_STEP
    pages = refs[:pps]
    tail_ke_ref, tail_vt_ref = refs[pps:pps + 2]
    outs = list(refs[pps + 2:])
    rows_ref = outs.pop(0) if row_slots else None
    ke_ref, vt_ref = outs[0], outs[1]
    means_ref = outs[2] if with_means else None
    p = pl.program_id(1)
    last = pl.num_programs(1) - 1
    page = pages[0].shape[-1]
    step, kw = ke_ref.shape[1:]

    @pl.when(p < last)
    def _():
        lane = lax.broadcasted_iota(jnp.int32, (step, kw), 1)
        row = p * step + lax.broadcasted_iota(jnp.int32, (step, kw), 0)
        onehot = jnp.where(lane - dh == row // block, 1.0, 0.0).astype(BF16)
        for g in range(G):
            ke_ref[g] = onehot
        vt_ref[:, dh:, :] = jnp.ones((G, V_ROWS - dh, step), BF16)
        ppb = block // page
        tot = None
        for j, pg in enumerate(pages):
            r0 = j * page
            for si, s in enumerate(row_slots):
                rows_ref[si, r0:r0 + page, :] = pg[s].reshape(G * dh, page).T
            k = pg[k_slot].reshape(G * dh, page).T
            for g in range(G):
                ke_ref[g, r0:r0 + page, 0:dh] = k[:, g * dh:(g + 1) * dh].astype(BF16)
                vt_ref[g, 0:dh, r0:r0 + page] = pg[v_slot, g].astype(BF16)
            if with_means:
                ksum = jnp.sum(k, axis=0, keepdims=True)
                tot = ksum if j % ppb == 0 else tot + ksum
                if j % ppb == ppb - 1:
                    n = j // ppb
                    for g in range(G):
                        means_ref[g, n:n + 1, :] = tot[:, g * dh:(g + 1) * dh] / block

    @pl.when(p == last)
    def _():
        ke_ref[...] = tail_ke_ref[...]
        vt_ref[...] = tail_vt_ref[...]
        if with_means:
            means_ref[...] = jnp.zeros(means_ref.shape, F32)


def _gather_kv(cache, page_table, n_groups, row_slots, k_slot, v_slot, block, tail_ke, tail_vt, with_means):
    _, n_slots, G, dh, page = cache.shape
    Bs, n_pages = page_table.shape
    assert G == n_groups and dh == HEAD_DIM
    pps = PAGES_PER_STEP
    step = pps * page
    assert n_pages % pps == 0
    if with_means:
        assert step % block == 0 and block % page == 0
    n_steps = n_pages // pps
    kw = tail_ke.shape[-1]
    Lp = (n_steps + 1) * step
    bps = step // block

    def page_map(j):
        return lambda b, p, pt: (pt[b, jnp.minimum(p, n_steps - 1) * pps + j], 0, 0, 0, 0)

    in_specs = [pl.BlockSpec((None, n_slots, G, dh, page), page_map(j)) for j in range(pps)]
    in_specs += [pl.BlockSpec((None, G, step, kw), lambda b, p, pt: (b, 0, 0, 0)),
                 pl.BlockSpec((None, G, V_ROWS, step), lambda b, p, pt: (b, 0, 0, 0))]
    out_shape, out_specs = [], []
    if row_slots:
        n_rs = len(row_slots)
        out_shape.append(jax.ShapeDtypeStruct((n_rs, Bs, n_steps * step, G * dh), F32))
        out_specs.append(pl.BlockSpec((n_rs, None, step, G * dh),
                                      lambda b, p, pt: (0, b, jnp.minimum(p, n_steps - 1), 0)))
    out_shape += [jax.ShapeDtypeStruct((Bs, G, Lp, kw), BF16), jax.ShapeDtypeStruct((Bs, G, V_ROWS, Lp), BF16)]
    out_specs += [pl.BlockSpec((None, G, step, kw), lambda b, p, pt: (b, 0, p, 0)),
                  pl.BlockSpec((None, G, V_ROWS, step), lambda b, p, pt: (b, 0, 0, p))]
    if with_means:
        out_shape.append(jax.ShapeDtypeStruct((Bs, G, n_steps + 1, bps, dh), F32))
        out_specs.append(pl.BlockSpec((None, G, None, bps, dh), lambda b, p, pt: (b, 0, p, 0, 0)))
    grid_spec = pltpu.PrefetchScalarGridSpec(num_scalar_prefetch=1, grid=(Bs, n_steps + 1), in_specs=in_specs,
                                             out_specs=out_specs)
    outs = list(pl.pallas_call(
        functools.partial(_gather_kv_kernel, G, tuple(row_slots), k_slot, v_slot, block, with_means),
        grid_spec=grid_spec, out_shape=out_shape, compiler_params=_params("parallel", "arbitrary"),
        name="gather_kv",
    )(page_table, *([cache] * pps), tail_ke.reshape(Bs, G, step, kw), tail_vt.reshape(Bs, G, V_ROWS, step)))
    rows = outs.pop(0) if row_slots else None
    ke = outs[0].reshape(Bs * G, Lp, kw)
    vt = outs[1].reshape(Bs * G, V_ROWS, Lp)
    means = outs[2].reshape(Bs * G, (n_steps + 1) * bps, dh) if with_means else None
    return rows, ke, vt, means


def _pad_queries(x, tq):
    return jnp.pad(x, ((0, 0), (0, tq - x.shape[1]), (0, 0)))


def kernel(x_prompt, x_sample, cache_nsa_kv, cache_moba_kv, state_win_kv, state_wkv, state_shift, page_table, norm_mix, norm_ffn, norm_final, even_w_in, even_w_out, rwkv_mu, rwkv_w0, rwkv_w_up, rwkv_a0, rwkv_a_up, rwkv_g_up, rwkv_k_k, rwkv_k_a, rwkv_r_k, rwkv_ln_g, rwkv_ln_b, nsa_gate_b, nsa_cmp_w1, nsa_cmp_pe, nsa_cmp_w2, odd_w_in, odd_w_out, ffn_w_gate, ffn_w_up, ffn_w_down):
    B, T, D = x_prompt.shape
    Bs, Ts, _ = x_sample.shape
    depth = norm_mix.shape[0]
    page = cache_nsa_kv.shape[2]
    past_len = page_table.shape[1] * page
    rwkv_dim = rwkv_w0.shape[1]
    rwkv_cols = rwkv_mu.shape[1]
    nsa_heads = nsa_gate_b.shape[1] // 3
    nsa_dim = nsa_heads * HEAD_DIM
    nsa_kv_cols = 6 * NSA_KV_HEADS * HEAD_DIM
    moba_kv_cols = 2 * MOBA_KV_HEADS * HEAD_DIM
    moba_dim = odd_w_in.shape[2] - moba_kv_cols
    tq_p = min(ATT_Q_TILE, T)
    tq_s = SAMPLE_Q_PAD

    cos_p, sin_p = _rope_tables(jnp.arange(T, dtype=jnp.int32))
    pos_s = past_len + jnp.arange(Ts, dtype=jnp.int32)
    cos_s, sin_s = _rope_tables(jnp.tile(pos_s, Bs))

    even_spec = ((rwkv_cols, (), False), (nsa_dim, tuple(range(nsa_dim // LANES)), False),
                 (nsa_kv_cols, tuple(range(0, nsa_kv_cols // LANES, 2)), False), (LANES, (), True))
    k_chunks = MOBA_KV_HEADS * HEAD_DIM // LANES
    odd_spec = ((moba_dim, tuple(range(moba_dim // LANES)), False), (moba_kv_cols, tuple(range(k_chunks)), False))

    hp = x_prompt.reshape(B * T, D)
    hs = x_sample.reshape(Bs * Ts, D)
    nsa_p, nsa_s, moba_p, moba_s = [], [], [], []
    win_p, win_s, wkv_p, wkv_s, sh_p, sh_s = [], [], [], [], [], []
    for layer in range(depth):
        i = layer // 2
        if layer % 2 == 0:
            w_in = even_w_in[i].astype(BF16)
            o = rwkv_cols
            n_gate = 3 * nsa_heads
            weights = [w_in[:, :o], w_in[:, o:o + nsa_dim], w_in[:, o + nsa_dim:o + nsa_dim + nsa_kv_cols],
                       jnp.pad(w_in[:, o + nsa_dim + nsa_kv_cols:], ((0, 0), (0, LANES - n_gate)))]
            gate_b = jnp.pad(nsa_gate_b[i], (0, LANES - n_gate)).reshape(1, LANES)
            rp = (rwkv_mu[i], rwkv_w0[i], rwkv_w_up[i], rwkv_a0[i], rwkv_a_up[i], rwkv_g_up[i],
                  rwkv_k_k[i], rwkv_k_a[i], rwkv_r_k[i], rwkv_ln_g[i], rwkv_ln_b[i])
            cw = (nsa_cmp_w1[i], nsa_cmp_pe[i], nsa_cmp_w2[i])

            rw, q, kv, gates = _project(hp, norm_mix[layer], cos_p, sin_p, weights, [gate_b], even_spec, "even_proj")
            rw3 = rw.reshape(B, T, rwkv_cols)
            kv6 = kv.reshape(B, T, 6, NSA_KV_HEADS, HEAD_DIM)
            y, g, wkv_new = _rwkv_mix(rw3, jnp.zeros((B, rwkv_cols), F32),
                                      jnp.zeros((B, rwkv_dim // HEAD_DIM, HEAD_DIM, HEAD_DIM), F32), rp, SCAN_CHUNK)
            o_c, o_s, o_w = _nsa_prompt(q.reshape(B, T, nsa_dim), kv6, cw, tq_p)
            hp = _even_out(hp, y, g, o_c.reshape(B * T, nsa_dim), o_s.reshape(B * T, nsa_dim),
                           o_w.reshape(B * T, nsa_dim), gates, even_w_out[i])
            nsa_p.append(kv6[:, :, :4])
            win_p.append(kv6[:, T - min(WINDOW, T):, 4:6])
            wkv_p.append(wkv_new)
            sh_p.append(rw3[:, -1])

            rw, q, kv, gates = _project(hs, norm_mix[layer], cos_s, sin_s, weights, [gate_b], even_spec, "even_proj_s")
            rw3 = rw.reshape(Bs, Ts, rwkv_cols)
            kv6 = kv.reshape(Bs, Ts, 6, NSA_KV_HEADS, HEAD_DIM)
            y, g, wkv_new = _rwkv_mix(rw3, state_shift[i], state_wkv[i], rp, SUBLANES)
            outs, win_new = _nsa_sample(q.reshape(Bs, Ts, nsa_dim), kv6,
                                        cache_nsa_kv[i].transpose(0, 2, 3, 4, 1), page_table,
                                        state_win_kv[i], cw, tq_s)
            o_c, o_s, o_w = [t.reshape(Bs * Ts, nsa_dim) for t in outs]
            hs = _even_out(hs, y, g, o_c, o_s, o_w, gates, even_w_out[i])
            nsa_s.append(kv6[:, :, :4])
            win_s.append(win_new)
            wkv_s.append(wkv_new)
            sh_s.append(rw3[:, -1])
        else:
            w_in = odd_w_in[i].astype(BF16)
            weights = [w_in[:, :moba_dim], w_in[:, moba_dim:]]
            q, kv = _project(hp, norm_mix[layer], cos_p, sin_p, weights, [], odd_spec, "odd_proj")
            kv2 = kv.reshape(B, T, 2, MOBA_KV_HEADS, HEAD_DIM)
            a = _moba_prompt(q.reshape(B, T, moba_dim), kv2, tq_p)
            hp = _odd_out(hp, a.reshape(B * T, moba_dim), odd_w_out[i])
            moba_p.append(kv2)

            q, kv = _project(hs, norm_mix[layer], cos_s, sin_s, weights, [], odd_spec, "odd_proj_s")
            kv2 = kv.reshape(Bs, Ts, 2, MOBA_KV_HEADS, HEAD_DIM)
            a = _moba_sample(q.reshape(Bs, Ts, moba_dim), kv2, cache_moba_kv[i].transpose(0, 2, 3, 4, 1),
                             page_table, tq_s)
            hs = _odd_out(hs, a.reshape(Bs * Ts, moba_dim), odd_w_out[i])
            moba_s.append(kv2)
        g_final = norm_final if layer == depth - 1 else None
        hp = _ffn(hp, norm_ffn[layer], ffn_w_gate[layer], ffn_w_up[layer], ffn_w_down[layer], g_final)
        hs = _ffn(hs, norm_ffn[layer], ffn_w_gate[layer], ffn_w_up[layer], ffn_w_down[layer], g_final)
    return (hp.reshape(B, T, D), hs.reshape(Bs, Ts, D), jnp.stack(nsa_p), jnp.stack(nsa_s), jnp.stack(moba_p),
            jnp.stack(moba_s), jnp.stack(win_p), jnp.stack(win_s), jnp.stack(wkv_p), jnp.stack(wkv_s),
            jnp.stack(sh_p), jnp.stack(sh_s))
```

```python
import functools
import math

import jax
import jax.numpy as jnp
from jax import lax
from jax.experimental import pallas as pl
from jax.experimental.pallas import tpu as pltpu

F32 = jnp.float32
BF16 = jnp.bfloat16
HIGHEST = lax.Precision.HIGHEST

HEAD_DIM = 64
NORM_EPS = 1e-6
ROPE_THETA = 10000.0
DECAY_LORA = 64
AAA_LORA = 64
GATE_LORA = 128
RWKV_GN_EPS = 64e-5
NSA_KV_HEADS = 2
CMP_STRIDE = 16
CMP_LEN = 2 * CMP_STRIDE
SEL_BLOCK = 64
SEL_TOPN = 16
WINDOW = 512
FORCE_BONUS = 100.0
MOBA_KV_HEADS = 4
MOBA_BLOCK = 256
MOBA_TOPK = 3

LANES = 128
SUBLANES = 8
VMEM_LIMIT = 56 * 1024 * 1024

ROW_TILE = 512
FFN_ROW_TILE = 1024
FFN_COL_TILE = 256
ATT_Q_TILE = 512
ATT_K_TILE = 512
SCAN_CHUNK = 64
SCAN_CHUNKS_PER_STEP = 4
SAMPLE_Q_PAD = 32
PAGES_PER_STEP = 8

NT = (((1,), (1,)), ((), ()))
TN = (((0,), (0,)), ((), ()))
NEG_INF = float("-inf")
NEG_BIG = -1e30
V_ROWS = HEAD_DIM + 16


def _round_up(x, m):
    return -(-x // m) * m


def _mm(a, b, dims=None, exact=False):
    if dims is None:
        dims = (((a.ndim - 1,), (0,)), ((), ()))
    if exact:
        return lax.dot_general(a.astype(F32), b.astype(F32), dims, precision=HIGHEST,
                               preferred_element_type=F32)
    return lax.dot_general(a.astype(BF16), b.astype(BF16), dims, preferred_element_type=F32)


def _split3(x):
    hi = x.astype(BF16)
    rest = x - hi.astype(F32)
    mid = rest.astype(BF16)
    return hi, mid, (rest - mid.astype(F32)).astype(BF16)


def _mm_onehot(a, b, dims=None):
    return sum(_mm(t, b, dims) for t in _split3(a))


def _mm_3pass(a, b, dims=None):
    a_hi, a_lo, _ = _split3(a)
    b_hi, b_lo, _ = _split3(b)
    return _mm(a_hi, b_hi, dims) + _mm(a_lo, b_hi, dims) + _mm(a_hi, b_lo, dims)


def _params(*sem):
    return pltpu.CompilerParams(dimension_semantics=sem, vmem_limit_bytes=VMEM_LIMIT)


def _rms_norm(x, g):
    return x * lax.rsqrt(jnp.mean(x * x, axis=-1, keepdims=True) + NORM_EPS) * g


def _rope_chunk(x, cos, sin):
    lane = lax.broadcasted_iota(jnp.int32, x.shape, 1)
    half = HEAD_DIM // 2
    partner = jnp.where((lane % HEAD_DIM) < half, pltpu.roll(x, LANES - half, 1), pltpu.roll(x, half, 1))
    return x * cos + partner * sin


def _rank_rows(score, blk, ncand):
    rank = jnp.zeros(score.shape, F32)
    for j in range(ncand):
        row = score[..., j:j + 1, :]
        rank = rank + jnp.where(row > score, 1.0, 0.0) + jnp.where(row == score, jnp.where(blk > j, 1.0, 0.0), 0.0)
    return rank


def _proj_kernel(spec, x_ref, g_ref, cos_ref, sin_ref, *refs):
    nseg = len(spec)
    w_refs = refs[:nseg]
    nbias = sum(1 for s in spec if s[2])
    b_refs = list(refs[nseg:nseg + nbias])
    o_refs = refs[nseg + nbias:]
    xn = _rms_norm(x_ref[...], g_ref[...]).astype(BF16)
    for (ncols, rope_chunks, sig), w_ref, o_ref in zip(spec, w_refs, o_refs):
        y = jnp.dot(xn, w_ref[...], preferred_element_type=F32)
        if sig:
            y = jax.nn.sigmoid(y + b_refs.pop(0)[...])
        if rope_chunks:
            cos = cos_ref[...]
            sin = sin_ref[...]
            for c in range(ncols // LANES):
                yc = y[:, c * LANES:(c + 1) * LANES]
                if c in rope_chunks:
                    yc = _rope_chunk(yc, cos, sin)
                o_ref[:, c * LANES:(c + 1) * LANES] = yc
        else:
            o_ref[...] = y


def _project(x, g, cos_tab, sin_tab, weights, biases, spec, name):
    M, D = x.shape
    tm = min(ROW_TILE, M)
    ntab = cos_tab.shape[0] // tm
    in_specs = [pl.BlockSpec((tm, D), lambda i: (i, 0)),
                pl.BlockSpec((1, D), lambda i: (0, 0)),
                pl.BlockSpec((tm, LANES), lambda i: (i % ntab, 0)),
                pl.BlockSpec((tm, LANES), lambda i: (i % ntab, 0))]
    in_specs += [pl.BlockSpec(w.shape, lambda i: (0, 0)) for w in weights]
    in_specs += [pl.BlockSpec(b.shape, lambda i: (0, 0)) for b in biases]
    out_shape = [jax.ShapeDtypeStruct((M, s[0]), F32) for s in spec]
    out_specs = [pl.BlockSpec((tm, s[0]), lambda i: (i, 0)) for s in spec]
    return pl.pallas_call(
        functools.partial(_proj_kernel, spec), grid=(M // tm,), in_specs=in_specs, out_specs=out_specs,
        out_shape=out_shape, compiler_params=_params("parallel"), name=name,
    )(x, g.reshape(1, D), cos_tab, sin_tab, *weights, *biases)


def _rope_tables(pos):
    half = HEAD_DIM // 2
    inv = ROPE_THETA ** (-jnp.arange(half, dtype=F32) / half)
    ang = pos.astype(F32)[:, None] * inv[None, :]
    cos = jnp.cos(ang)
    sin = jnp.sin(ang)
    cos_t = jnp.tile(cos, (1, LANES // half))
    sin_t = jnp.tile(jnp.concatenate([-sin, sin], axis=1), (1, LANES // HEAD_DIM))
    return cos_t, sin_t


def _even_out_kernel(res_ref, y_ref, g_ref, oc_ref, os_ref, ow_ref, gt_ref, ec_ref, es_ref, ew_ref,
                     w1_ref, w2_ref, o_ref):
    a = y_ref[...] * g_ref[...]
    gt = gt_ref[...]
    b = (_mm_onehot(gt, ec_ref[...]) * oc_ref[...] + _mm_onehot(gt, es_ref[...]) * os_ref[...]
         + _mm_onehot(gt, ew_ref[...]) * ow_ref[...])
    o_ref[...] = res_ref[...] + _mm(a, w1_ref[...]) + _mm(b, w2_ref[...])


def _even_out(res, y, g, o_c, o_s, o_w, gates, w_out):
    M, D = res.shape
    n_rw = y.shape[1]
    n_nsa = o_c.shape[1]
    tm = min(ROW_TILE, M)
    heads = n_nsa // HEAD_DIM
    col = jnp.arange(n_nsa)[None, :] // HEAD_DIM
    row = jnp.arange(LANES)[:, None]
    expand = [(row == col * 3 + br).astype(F32) for br in range(3)]
    del heads
    row_spec = lambda n: pl.BlockSpec((tm, n), lambda i: (i, 0))
    full = lambda a: pl.BlockSpec(a.shape, lambda i: (0, 0))
    w1 = w_out[:n_rw].astype(BF16)
    w2 = w_out[n_rw:].astype(BF16)
    return pl.pallas_call(
        _even_out_kernel, grid=(M // tm,),
        in_specs=[row_spec(D), row_spec(n_rw), row_spec(n_rw), row_spec(n_nsa), row_spec(n_nsa), row_spec(n_nsa),
                  row_spec(LANES), full(expand[0]), full(expand[1]), full(expand[2]), full(w1), full(w2)],
        out_specs=row_spec(D), out_shape=jax.ShapeDtypeStruct((M, D), F32),
        compiler_params=_params("parallel"), name="even_out",
    )(res, y, g, o_c, o_s, o_w, gates, *expand, w1, w2)


def _odd_out_kernel(res_ref, a_ref, w_ref, o_ref):
    o_ref[...] = res_ref[...] + _mm(a_ref[...], w_ref[...])


def _odd_out(res, a, w_out):
    M, D = res.shape
    tm = min(ROW_TILE, M)
    w = w_out.astype(BF16)
    return pl.pallas_call(
        _odd_out_kernel, grid=(M // tm,),
        in_specs=[pl.BlockSpec((tm, D), lambda i: (i, 0)), pl.BlockSpec((tm, a.shape[1]), lambda i: (i, 0)),
                  pl.BlockSpec(w.shape, lambda i: (0, 0))],
        out_specs=pl.BlockSpec((tm, D), lambda i: (i, 0)), out_shape=jax.ShapeDtypeStruct((M, D), F32),
        compiler_params=_params("parallel"), name="odd_out",
    )(res, a, w)


def _ffn_kernel(final_norm, x_ref, g_ref, wg_ref, wu_ref, wd_ref, gf_ref, o_ref, xn_scr, acc_scr):
    j = pl.program_id(1)

    @pl.when(j == 0)
    def _():
        xn_scr[...] = _rms_norm(x_ref[...], g_ref[...]).astype(BF16)
        acc_scr[...] = jnp.zeros(acc_scr.shape, F32)

    xn = xn_scr[...]
    h = jax.nn.silu(jnp.dot(xn, wg_ref[...], preferred_element_type=F32)) * jnp.dot(
        xn, wu_ref[...], preferred_element_type=F32)
    acc_scr[...] += _mm(h, wd_ref[...])

    @pl.when(j == pl.num_programs(1) - 1)
    def _():
        y = x_ref[...] + acc_scr[...]
        if final_norm:
            y = _rms_norm(y, gf_ref[...])
        o_ref[...] = y


def _ffn(x, g, wg, wu, wd, g_final=None):
    M, D = x.shape
    F = wg.shape[1]
    tm = min(FFN_ROW_TILE, M)
    tf = FFN_COL_TILE
    final_norm = g_final is not None
    gf = (g_final if final_norm else g).reshape(1, D)
    return pl.pallas_call(
        functools.partial(_ffn_kernel, final_norm), grid=(M // tm, F // tf),
        in_specs=[pl.BlockSpec((tm, D), lambda i, j: (i, 0)), pl.BlockSpec((1, D), lambda i, j: (0, 0)),
                  pl.BlockSpec((D, tf), lambda i, j: (0, j)), pl.BlockSpec((D, tf), lambda i, j: (0, j)),
                  pl.BlockSpec((tf, D), lambda i, j: (j, 0)), pl.BlockSpec((1, D), lambda i, j: (0, 0))],
        out_specs=pl.BlockSpec((tm, D), lambda i, j: (i, 0)), out_shape=jax.ShapeDtypeStruct((M, D), F32),
        scratch_shapes=[pltpu.VMEM((tm, D), BF16), pltpu.VMEM((tm, D), F32)],
        compiler_params=_params("parallel", "arbitrary"), name="ffn",
    )(x, g.reshape(1, D), wg.astype(BF16), wu.astype(BF16), wd.astype(BF16), gf)


def _rwkv_pre_kernel(n_dim, rw_ref, prev_ref, mu_ref, w0_ref, a0_ref, kk_ref, ka_ref, wup_ref, aup_ref, gup_ref,
                     hsum_ref, r_out, lw_out, k_out, v_out, kk_out, b_out, g_out):
    rw = rw_ref[...]
    if prev_ref.shape[0] == rw.shape[0]:
        prev = prev_ref[...]
    else:
        first = lax.broadcasted_iota(jnp.int32, (rw.shape[0], 1), 0) == 0
        prev = jnp.where(first, prev_ref[...], pltpu.roll(rw, 1, 0))
    xm = rw + (prev - rw) * mu_ref[...]
    r = xm[:, :n_dim]
    k = xm[:, n_dim:2 * n_dim]
    v = xm[:, 2 * n_dim:3 * n_dim]
    lora = xm[:, 3 * n_dim:3 * n_dim + DECAY_LORA + AAA_LORA]
    xg = xm[:, 3 * n_dim + DECAY_LORA + AAA_LORA:]
    lw = -math.exp(-0.5) * jax.nn.sigmoid(w0_ref[...] + _mm_3pass(jnp.tanh(lora), wup_ref[...]))
    a = jax.nn.sigmoid(a0_ref[...] + _mm_3pass(lora, aup_ref[...]))
    g = _mm(jax.nn.sigmoid(xg), gup_ref[...])
    kk = k * kk_ref[...]
    sq = kk * kk
    sq_hi = sq.astype(BF16)
    norm = jnp.sqrt(_mm(sq_hi, hsum_ref[...]) + _mm(sq - sq_hi.astype(F32), hsum_ref[...]))
    kk = kk / jnp.maximum(norm, 1e-12)
    g_out[...] = g
    outs = ((r_out, r), (lw_out, lw), (k_out, k * (1.0 + (a - 1.0) * ka_ref[...])), (v_out, v), (kk_out, kk),
            (b_out, kk * a))
    for o_ref, val in outs:
        if len(o_ref.shape) == 2:
            o_ref[...] = val
        else:
            for h in range(o_ref.shape[0]):
                o_ref[h] = val[:, h * HEAD_DIM:(h + 1) * HEAD_DIM]


def _rwkv_pre(rw3, shift_prev, mu, w0, w_up, a0, a_up, g_up, k_k, k_a):
    B, seq_len, ncols = rw3.shape
    M = B * seq_len
    rw = rw3.reshape(M, ncols)
    n_dim = w0.shape[0]
    tm = min(ROW_TILE, M)
    H = n_dim // HEAD_DIM
    head_major = seq_len % tm == 0
    nt = seq_len // tm if head_major else 1
    if head_major:
        prev = jnp.concatenate([shift_prev[:, None], rw3[:, tm - 1:seq_len - 1:tm]], axis=1).reshape(B * nt, 1, ncols)
        prev_spec = pl.BlockSpec((None, 1, ncols), lambda i: (i, 0, 0))
    else:
        prev = jnp.concatenate([shift_prev[:, None], rw3[:, :-1]], axis=1).reshape(M, ncols)
        prev_spec = pl.BlockSpec((tm, ncols), lambda i: (i, 0))
    zeros = jnp.zeros((AAA_LORA, n_dim), F32)
    wup_pad = jnp.concatenate([w_up, zeros], axis=0)
    aup_pad = jnp.concatenate([jnp.zeros((DECAY_LORA, n_dim), F32), a_up], axis=0)
    head = jnp.arange(n_dim) // HEAD_DIM
    hsum = (head[:, None] == head[None, :]).astype(F32)
    vec = lambda a: a.reshape(1, -1)
    row = lambda n: pl.BlockSpec((tm, n), lambda i: (i, 0))
    full = lambda a: pl.BlockSpec(a.shape, lambda i: (0, 0))
    ins = [rw, prev, vec(mu), vec(w0), vec(a0), vec(k_k), vec(k_a), wup_pad, aup_pad, g_up, hsum]
    if head_major:
        seq_spec = pl.BlockSpec((None, H, tm, HEAD_DIM), lambda i: (i // nt, 0, i % nt, 0))
        seq_shape = jax.ShapeDtypeStruct((M // seq_len, H, seq_len, HEAD_DIM), F32)
    else:
        seq_spec, seq_shape = row(n_dim), jax.ShapeDtypeStruct((M, n_dim), F32)
    outs = pl.pallas_call(
        functools.partial(_rwkv_pre_kernel, n_dim), grid=(M // tm,),
        in_specs=[row(ncols), prev_spec] + [full(a) for a in ins[2:]],
        out_specs=[seq_spec] * 6 + [row(n_dim)],
        out_shape=[seq_shape] * 6 + [jax.ShapeDtypeStruct((M, n_dim), F32)],
        compiler_params=_params("parallel"), name="rwkv_pre",
    )(*ins)
    if head_major:
        outs = [t.reshape(-1, seq_len, HEAD_DIM) for t in outs[:6]] + [outs[6]]
    return outs, head_major


def _rwkv_chunk_kernel(n_heads, chunk, r_ref, lw_ref, k_ref, v_ref, kk_ref, b_ref, rk_ref,
                       rw_ref, y0_ref, bonus_ref, a_ref, s1_ref):
    C = chunk
    dh = HEAD_DIM
    ti = lax.broadcasted_iota(jnp.int32, (C, C), 0)
    si = lax.broadcasted_iota(jnp.int32, (C, C), 1)
    incl = jnp.where(si <= ti, 1.0, 0.0)
    eye = jnp.where(lax.broadcasted_iota(jnp.int32, (dh, dh), 0) == lax.broadcasted_iota(jnp.int32, (dh, dh), 1),
                    1.0, 0.0)
    n_double = max(1, math.ceil(math.log2(C)))
    heads = range(n_heads)
    r = [r_ref[h] for h in heads]
    lw = [lw_ref[h] for h in heads]
    k = [k_ref[h] for h in heads]
    v = [v_ref[h] for h in heads]
    b = [b_ref[h] for h in heads]
    cum = [sum(_mm(incl, t) for t in _split3(lw[h])) for h in heads]
    cum_end = [cum[h][C - 1:C, :] for h in heads]
    kkw = [kk_ref[h] * jnp.exp(cum[h] - lw[h]) for h in heads]
    rwc = [r[h] * jnp.exp(cum[h]) for h in heads]
    w_inv = [jnp.exp(-cum[h]) for h in heads]
    kd = [k[h] * w_inv[h] for h in heads]
    bd = [b[h] * w_inv[h] for h in heads]
    w_end = [jnp.exp(cum_end[h] - cum[h]) for h in heads]
    both = [jnp.concatenate([kkw[h], rwc[h]], axis=0) for h in heads]
    g_b = [_mm(both[h], bd[h], NT) for h in heads]
    g_k = [_mm(both[h], kd[h], NT) for h in heads]
    a_ub = [jnp.where(si < ti, g_b[h][:C], 0.0) for h in heads]
    b_rb = [jnp.where(si <= ti, g_b[h][C:], 0.0) for h in heads]
    a_vk = [jnp.where(si < ti, g_k[h][:C], 0.0) for h in heads]
    b_rk = [jnp.where(si <= ti, g_k[h][C:], 0.0) for h in heads]
    gv = [_mm(jnp.concatenate([a_vk[h], b_rk[h]], axis=0), v[h]) for h in heads]
    x = [jnp.concatenate([kkw[h], gv[h][:C]], axis=1) for h in heads]
    p = [-a_ub[h] for h in heads]
    for _ in range(n_double):
        px = [_mm(p[h], jnp.concatenate([x[h], p[h]], axis=1)) for h in heads]
        x = [x[h] + px[h][:, :2 * dh] for h in heads]
        p = [px[h][:, 2 * dh:] for h in heads]
    for h in heads:
        bx = _mm(b_rb[h], x[h])
        rw_ref[h] = rwc[h] - bx[:, :dh]
        y0_ref[h] = gv[h][C:] - bx[:, dh:]
        xb = _mm(x[h], b[h] * w_end[h], TN)
        a_ref[h] = eye * jnp.exp(cum_end[h]) - xb[:dh]
        s1_ref[h] = _mm(v[h], k[h] * w_end[h], TN) - xb[dh:]
        bonus_ref[h] = jnp.sum(r[h] * k[h] * rk_ref[h:h + 1, :], axis=-1, keepdims=True) * v[h]


def _rwkv_state_kernel(n_heads, n_sub, rw_ref, y0_ref, bonus_ref, a_ref, s1_ref, s0_ref, lng_ref, lnb_ref,
                       y_ref, s_out_ref, s_scr):
    c = pl.program_id(1)

    @pl.when(c == 0)
    def _():
        s_scr[...] = s0_ref[...]

    chunk = rw_ref.shape[2]
    for h in range(n_heads):
        s = s_scr[h]
        for j in range(n_sub):
            y = _mm(rw_ref[h, j], s, NT) + y0_ref[h, j]
            s = _mm(s, a_ref[h, j]) + s1_ref[h, j]
            mean = jnp.mean(y, axis=-1, keepdims=True)
            var = jnp.mean(jnp.square(y - mean), axis=-1, keepdims=True)
            yn = (y - mean) * lax.rsqrt(var + RWKV_GN_EPS) * lng_ref[h:h + 1, :] + lnb_ref[h:h + 1, :]
            y_ref[j * chunk:(j + 1) * chunk, h * HEAD_DIM:(h + 1) * HEAD_DIM] = yn + bonus_ref[h, j]
        s_scr[h] = s

    @pl.when(c == pl.num_programs(1) - 1)
    def _():
        s_out_ref[...] = s_scr[...]


def _rwkv_scan(r, lw, k, v, kk, b, s0, r_k, ln_g, ln_b, chunk):
    n_bh, T, dh = r.shape
    H = r_k.shape[0]
    n_chunks = T // chunk
    seq = pl.BlockSpec((H, chunk, dh), lambda i, c: (i, c, 0))
    mat = pl.BlockSpec((H, None, dh, dh), lambda i, c: (i, c, 0, 0))
    par = pl.BlockSpec((H, dh), lambda i, c: (0, 0))
    seq_shape = jax.ShapeDtypeStruct((n_bh, T, dh), F32)
    mat_shape = jax.ShapeDtypeStruct((n_bh, n_chunks, dh, dh), F32)
    rw, y0, bonus, a, s1 = pl.pallas_call(
        functools.partial(_rwkv_chunk_kernel, H, chunk), grid=(n_bh // H, n_chunks),
        in_specs=[seq] * 6 + [par], out_specs=[seq, seq, seq, mat, mat],
        out_shape=[seq_shape, seq_shape, seq_shape, mat_shape, mat_shape],
        compiler_params=_params("parallel", "parallel"), name="rwkv_chunk",
    )(r, lw, k, v, kk, b, r_k)
    n_sub = math.gcd(SCAN_CHUNKS_PER_STEP, n_chunks)
    split = lambda t: t.reshape(n_bh, n_chunks, chunk, dh)
    seq4 = pl.BlockSpec((H, n_sub, chunk, dh), lambda i, c: (i, c, 0, 0))
    mat4 = pl.BlockSpec((H, n_sub, dh, dh), lambda i, c: (i, c, 0, 0))
    state = pl.BlockSpec((H, dh, dh), lambda i, c: (i, 0, 0))
    y, s_new = pl.pallas_call(
        functools.partial(_rwkv_state_kernel, H, n_sub), grid=(n_bh // H, n_chunks // n_sub),
        in_specs=[seq4, seq4, seq4, mat4, mat4, state, par, par],
        out_specs=[pl.BlockSpec((n_sub * chunk, H * dh), lambda i, c: (i * (n_chunks // n_sub) + c, 0)), state],
        out_shape=[jax.ShapeDtypeStruct((n_bh // H * T, H * dh), F32), jax.ShapeDtypeStruct((n_bh, dh, dh), F32)],
        scratch_shapes=[pltpu.VMEM((H, dh, dh), F32)],
        compiler_params=_params("parallel", "arbitrary"), name="rwkv_state",
    )(split(rw), split(y0), split(bonus), a, s1, s0, ln_g.reshape(H, dh), ln_b.reshape(H, dh))
    return y, s_new


def _rwkv_mix(rw, shift_prev, wkv0, p, chunk):
    mu, w0, w_up, a0, a_up, g_up, k_k, k_a, r_k, ln_g, ln_b = p
    B, T, ncols = rw.shape
    H = r_k.shape[0]
    n_dim = H * HEAD_DIM
    outs, head_major = _rwkv_pre(rw, shift_prev, mu, w0, w_up, a0, a_up, g_up, k_k, k_a)
    g = outs[6]
    Tp = _round_up(T, chunk)

    def heads(t):
        if not head_major:
            t = t.reshape(B, T, H, HEAD_DIM).transpose(0, 2, 1, 3).reshape(B * H, T, HEAD_DIM)
        return jnp.pad(t, ((0, 0), (0, Tp - T), (0, 0)))

    y, s_new = _rwkv_scan(*[heads(t) for t in outs[:6]], wkv0.reshape(B * H, HEAD_DIM, HEAD_DIM), r_k, ln_g, ln_b,
                          chunk)
    y = y.reshape(B, Tp, n_dim)[:, :T].reshape(B * T, n_dim)
    return y, g, s_new.reshape(B, H, HEAD_DIM, HEAD_DIM)


def _compress_kernel(transpose_out, x_ref, w1_ref, pe_ref, w2_ref, o_ref):
    n_groups = o_ref.shape[0]
    width = x_ref.shape[1]
    nch = x_ref.shape[0] // CMP_STRIDE
    both = jnp.zeros((nch, 2 * width), F32)
    bias = jnp.zeros((1, width), F32)
    for c in range(CMP_STRIDE):
        w = w1_ref[c]
        both = both + _mm(x_ref[pl.ds(c, nch, stride=CMP_STRIDE), :], w)
        pw = _mm(pe_ref[c], w, exact=True)
        bias = bias + pw[0:1, :width] + pw[1:2, width:]
    h = jax.nn.gelu(both[:, :width] + pltpu.roll(both[:, width:], nch - 1, 0) + bias)
    if transpose_out:
        res = _mm(w2_ref[...], h, NT)
        for g in range(n_groups):
            o_ref[g] = res[g * HEAD_DIM:(g + 1) * HEAD_DIM]
    else:
        res = _mm(h, w2_ref[...])
        for g in range(n_groups):
            o_ref[g] = res[:, g * HEAD_DIM:(g + 1) * HEAD_DIM]


def _nsa_compress(x, x_spec, n_seq, n_rows, w1, pe, w2, transpose_out):
    G = NSA_KV_HEADS
    dh = HEAD_DIM
    nch = n_rows // CMP_STRIDE
    eye = jnp.eye(G, dtype=F32)


    def blockdiag(m):
        out = eye[:, None, :, None] * m[..., None, :, None, :]
        return out.reshape(m.shape[:-2] + (G * dh, G * dh))

    w1p = jnp.concatenate([blockdiag(w1[:CMP_STRIDE]), blockdiag(w1[CMP_STRIDE:])], axis=-1).astype(BF16)
    pe2 = jnp.stack([jnp.tile(pe[:CMP_STRIDE], (1, G)), jnp.tile(pe[CMP_STRIDE:], (1, G))], axis=1)
    pe2 = jnp.pad(pe2, ((0, 0), (0, SUBLANES - 2), (0, 0)))
    out_block = (None, G, dh, nch) if transpose_out else (None, G, nch, dh)
    out = pl.pallas_call(
        functools.partial(_compress_kernel, transpose_out), grid=(n_seq,),
        in_specs=[x_spec, pl.BlockSpec(w1p.shape, lambda i: (0, 0, 0)), pl.BlockSpec(pe2.shape, lambda i: (0, 0, 0)),
                  pl.BlockSpec((G * dh, G * dh), lambda i: (0, 0))],
        out_specs=pl.BlockSpec(out_block, lambda i: (i, 0, 0, 0)),
        out_shape=jax.ShapeDtypeStruct((n_seq,) + out_block[1:], F32),
        compiler_params=_params("parallel"), name="nsa_compress",
    )(x, w1p, pe2, blockdiag(w2.T if transpose_out else w2).astype(BF16))
    return out.reshape((n_seq * G,) + out_block[2:])


def _load_queries(q_ref, n_rep, tok_major):
    if not tok_major:
        return q_ref[...]
    qt = q_ref[...].T
    return jnp.concatenate([qt[r * HEAD_DIM:(r + 1) * HEAD_DIM] for r in range(n_rep)], axis=1)


def _store_outputs(o_ref, o, n_rep, tok_major):
    if not tok_major:
        o_ref[...] = o
        return
    tq = o.shape[1] // n_rep
    o_ref[...] = jnp.concatenate([o[:, r * tq:(r + 1) * tq] for r in range(n_rep)], axis=0).T


def _query_layout(q, tq, n_groups, group_batch=None):
    if q.ndim == 4:
        BG, nq, dh, rows = q.shape
        return q, False, BG, nq, rows // tq, pl.BlockSpec((group_batch, None, dh, rows), lambda i, j: (i, j, 0, 0))
    assert group_batch in (None, 1)
    B, T, n = q.shape
    G = n_groups
    n_rep = n // HEAD_DIM // G
    nq = T // tq
    spec = pl.BlockSpec((tq, n_rep * HEAD_DIM), lambda i, j: ((i // G) * nq + j, i % G))
    return q.reshape(B * T, n), True, B * G, nq, n_rep, spec


def _cmp_select_kernel(tq, n_rep, q_base, n_sel, tok_major, q_ref, kc_ref, vct_ref, ovt_ref, oc_ref, bias_ref):
    qi = pl.program_id(1)
    rows = n_rep * tq
    ncp = kc_ref.shape[0]
    nr = ovt_ref.shape[0]
    nbp = bias_ref.shape[0]
    q = _load_queries(q_ref, n_rep, tok_major) * HEAD_DIM ** -0.5
    s = _mm(kc_ref[...], q)
    t_row = q_base + qi * tq + lax.broadcasted_iota(jnp.int32, (1, rows), 1) % tq
    c_end = lax.broadcasted_iota(jnp.int32, (ncp, 1), 0) * CMP_STRIDE + (CMP_LEN - 1)
    s = jnp.where(c_end <= t_row, s, NEG_INF)
    m = jnp.max(s, axis=0, keepdims=True)
    e = jnp.exp(s - jnp.where(m == NEG_INF, 0.0, m))
    p = e / jnp.maximum(jnp.sum(e, axis=0, keepdims=True), 1e-30)
    _store_outputs(oc_ref, _mm(vct_ref[...], p), n_rep, tok_major)
    p_sum = p[:, 0:tq]
    for r in range(1, n_rep):
        p_sum = p_sum + p[:, r * tq:(r + 1) * tq]
    imp = sum(_mm(ovt_ref[...], t) for t in _split3(p_sum))
    cur = (q_base + qi * tq + lax.broadcasted_iota(jnp.int32, (1, tq), 1)) // SEL_BLOCK
    blk = lax.broadcasted_iota(jnp.int32, (nr, 1), 0)
    forced = jnp.where(blk == cur, 1.0, jnp.where(blk == cur - 1, 1.0, jnp.where(blk == 0, 1.0, 0.0)))
    score = jnp.where(blk <= cur, imp + FORCE_BONUS * forced, NEG_INF)
    rank = _rank_rows(score, blk, n_sel)
    bias = jnp.where(blk <= cur, jnp.where(rank < min(SEL_TOPN, n_sel), 0.0, NEG_BIG), NEG_BIG)
    if nbp > nr:
        bias = jnp.concatenate([bias, jnp.zeros((nbp - nr, tq), F32)], axis=0)
    bias_ref[...] = bias.astype(BF16)


def _cmp_select(q, kc, vct, ovt, tq, q_base, n_sel, nbp):
    qt, tok_major, BG, nq, n_rep, qspec = _query_layout(q, tq, NSA_KV_HEADS)
    ncp, dh = kc.shape[1:]
    return pl.pallas_call(
        functools.partial(_cmp_select_kernel, tq, n_rep, q_base, n_sel, tok_major), grid=(BG, nq),
        in_specs=[qspec, pl.BlockSpec((None, ncp, dh), lambda i, j: (i, 0, 0)),
                  pl.BlockSpec((None, dh, ncp), lambda i, j: (i, 0, 0)),
                  pl.BlockSpec(ovt.shape, lambda i, j: (0, 0))],
        out_specs=[qspec, pl.BlockSpec((None, None, nbp, tq), lambda i, j: (i, j, 0, 0))],
        out_shape=[jax.ShapeDtypeStruct(qt.shape, F32), jax.ShapeDtypeStruct((BG, nq, nbp, tq), BF16)],
        compiler_params=_params("parallel", "parallel"), name="nsa_cmp_select",
    )(qt, kc, vct, ovt)


def _overlap_matrix_t(nc, nsel, ncp, nr):
    i = jnp.arange(nc, dtype=jnp.int32)[None, :]
    j = jnp.arange(nsel, dtype=jnp.int32)[:, None]
    lo = jnp.maximum(i * CMP_STRIDE, j * SEL_BLOCK)
    hi = jnp.minimum(i * CMP_STRIDE + CMP_LEN, (j + 1) * SEL_BLOCK)
    ov = jnp.clip(hi - lo, 0).astype(F32) / CMP_LEN
    return jnp.pad(ov, ((0, nr - nsel), (0, ncp - nc)))


def _means_kernel(n_blk, k_ref, o_ref):
    o_ref[...] = jnp.zeros(o_ref.shape, F32)
    for n in range(n_blk):
        o_ref[n:n + 1, :] = jnp.sum(k_ref[n * MOBA_BLOCK:(n + 1) * MOBA_BLOCK, :], axis=0,
                                    keepdims=True) / MOBA_BLOCK


def _block_means(k, n_blk, nbp):
    BG, Lp, dh = k.shape
    return pl.pallas_call(
        functools.partial(_means_kernel, n_blk), grid=(BG,),
        in_specs=[pl.BlockSpec((None, Lp, dh), lambda i: (i, 0, 0))],
        out_specs=pl.BlockSpec((None, nbp, dh), lambda i: (i, 0, 0)),
        out_shape=jax.ShapeDtypeStruct((BG, nbp, dh), F32), compiler_params=_params("parallel"), name="moba_means",
    )(k)


def _bmm(a, b, exact=False):
    dims = (((2,), (1,)), ((0,), (0,)))
    if exact:
        return lax.dot_general(a.astype(F32), b.astype(F32), dims, precision=HIGHEST, preferred_element_type=F32)
    return lax.dot_general(a.astype(BF16), b.astype(BF16), dims, preferred_element_type=F32)


def _flash_kernel(mode, tq, n_rep, tk, q_base, k_base, n_blk, tok_major, *refs):
    if mode == "sel":
        q_ref, ke_ref, vt_ref, bias_ref, o_ref, lhs_scr, m_scr, acc_scr = refs
    elif mode == "win":
        q_ref, ke_ref, vt_ref, o_ref, lhs_scr, m_scr, acc_scr = refs
    else:
        q_ref, ke_ref, vt_ref, mean_ref, o_ref, lhs_scr, m_scr, acc_scr = refs
    qi = pl.program_id(1)
    rows = n_rep * tq
    q0 = q_base + qi * tq
    t_row = q0 + lax.broadcasted_iota(jnp.int32, (1, 1, rows), 2) % tq
    q = _load_queries(q_ref, n_rep, True)[None] if tok_major else q_ref[...]
    q = q * HEAD_DIM ** -0.5
    lhs_scr[:, 0:HEAD_DIM, :] = q.astype(BF16)
    if mode == "sel":
        lhs_scr[:, HEAD_DIM:, :] = jnp.concatenate([bias_ref[...]] * n_rep, axis=2)
    elif mode == "moba":
        nr = _round_up(n_blk, SUBLANES)
        nbp = mean_ref.shape[1]
        gate = _bmm(mean_ref[:, 0:nr, :], q, exact=True)
        blk = lax.broadcasted_iota(jnp.int32, (1, nr, 1), 1)
        cur = t_row // MOBA_BLOCK
        gate = jnp.where(blk < cur, gate, NEG_INF)
        rank = _rank_rows(gate, blk, n_blk)
        picked = jnp.where(blk < cur, jnp.where(rank < min(MOBA_TOPK, n_blk), 0.0, NEG_BIG), NEG_BIG)
        bias = jnp.where(blk == cur, 0.0, picked)
        if nbp > nr:
            bias = jnp.concatenate([bias, jnp.zeros((bias.shape[0], nbp - nr, rows), F32)], axis=1)
        lhs_scr[:, HEAD_DIM:, :] = bias.astype(BF16)
    m_scr[...] = jnp.full(m_scr.shape, NEG_BIG, F32)
    acc_scr[...] = jnp.zeros(acc_scr.shape, F32)

    def tile(ki, positional):
        k0 = pl.multiple_of(ki * tk, tk)
        s = _bmm(ke_ref[:, pl.ds(k0, tk), :], lhs_scr[...])
        if positional:
            kpos = k_base + k0 + lax.broadcasted_iota(jnp.int32, (1, tk, 1), 1)
            s = jnp.where(kpos <= t_row, s, NEG_BIG)
            if mode == "win":
                s = jnp.where(kpos > t_row - WINDOW, s, NEG_BIG)
        m_prev = m_scr[...]
        m_new = jnp.maximum(m_prev, jnp.max(s, axis=1, keepdims=True))
        p = jnp.exp(s - m_new)
        acc_scr[...] = jnp.exp(m_prev - m_new) * acc_scr[...] + _bmm(vt_ref[:, :, pl.ds(k0, tk)], p)
        m_scr[...] = m_new

    def body(positional):
        def f(ki, carry):
            tile(ki, positional)
            return carry
        return f

    last = (q0 + tq - 1 - k_base) // tk
    if mode == "win":
        lax.fori_loop(jnp.maximum(q0 - (WINDOW - 1) - k_base, 0) // tk, last + 1, body(True), 0)
    else:
        lax.fori_loop(0, last, body(False), 0)
        tile(last, True)
    acc = acc_scr[...]
    o = acc[:, 0:HEAD_DIM] / jnp.maximum(acc[:, HEAD_DIM:HEAD_DIM + 1], 1e-30)
    if tok_major:
        _store_outputs(o_ref, o[0], n_rep, True)
    else:
        o_ref[...] = o


def _flash(mode, q, ke, vt, extra, tq, q_base, k_base, n_blk, n_groups, tk=ATT_K_TILE, group_batch=1):
    gb = group_batch
    qt, tok_major, BG, nq, n_rep, qspec = _query_layout(q, tq, n_groups, gb)
    dh = HEAD_DIM
    rows = n_rep * tq
    Lp, kw = ke.shape[1:]
    assert Lp % tk == 0 and (q_base + nq * tq - 1 - k_base) // tk < Lp // tk
    assert tk % tq == 0 and (q_base - k_base) % tq == 0 and BG % gb == 0
    ins = [qt, ke, vt]
    in_specs = [qspec, pl.BlockSpec((gb, Lp, kw), lambda i, j: (i, 0, 0)),
                pl.BlockSpec((gb, V_ROWS, Lp), lambda i, j: (i, 0, 0))]
    if mode == "sel":
        ins.append(extra)
        in_specs.append(pl.BlockSpec((gb, None, kw - dh, tq), lambda i, j: (i, j, 0, 0)))
    if mode == "moba":
        ins.append(extra)
        in_specs.append(pl.BlockSpec((gb, kw - dh, dh), lambda i, j: (i, 0, 0)))
    scratch = [pltpu.VMEM((gb, kw, rows), BF16), pltpu.VMEM((gb, 1, rows), F32),
               pltpu.VMEM((gb, V_ROWS, rows), F32)]
    return pl.pallas_call(
        functools.partial(_flash_kernel, mode, tq, n_rep, tk, q_base, k_base, n_blk, tok_major),
        grid=(BG // gb, nq), in_specs=in_specs, out_specs=qspec, out_shape=jax.ShapeDtypeStruct(qt.shape, F32),
        scratch_shapes=scratch, compiler_params=_params("parallel", "parallel"), name="flash_" + mode,
    )(*ins)


def _group_queries_t(q, n_groups, tq):
    B, T, n = q.shape
    R = n // HEAD_DIM // n_groups
    q = q.reshape(B, T // tq, tq, n_groups, R, HEAD_DIM).transpose(0, 3, 1, 5, 4, 2)
    return q.reshape(B * n_groups, T // tq, HEAD_DIM, R * tq)


def _ungroup_t(o, B, n_groups, tq):
    BG, nq, dh, rows = o.shape
    R = rows // tq
    o = o.reshape(B, n_groups, nq, dh, R, tq).transpose(0, 2, 5, 1, 4, 3)
    return o.reshape(B, nq * tq, n_groups * R * dh)


def _head_major(x, Lp):
    B, L, G, dh = x.shape
    x = jnp.pad(x.transpose(0, 2, 1, 3), ((0, 0), (0, 0), (0, Lp - L), (0, 0)))
    return x.reshape(B * G, Lp, dh)


def _key_operand(x, Lp, block=None, nbp=0, first_pos=0):
    k = _head_major(x, Lp).astype(BF16)
    if block is None:
        return k
    onehot = ((first_pos + jnp.arange(Lp))[:, None] // block == jnp.arange(nbp)[None, :]).astype(BF16)
    return jnp.concatenate([k, jnp.broadcast_to(onehot[None], (k.shape[0], Lp, nbp))], axis=-1)


def _value_operand(x, Lp):
    B, L, G, dh = x.shape
    v = jnp.pad(x.transpose(0, 2, 3, 1), ((0, 0), (0, 0), (0, 0), (0, Lp - L))).reshape(B * G, dh, Lp)
    return jnp.concatenate([v, jnp.ones((B * G, V_ROWS - dh, Lp), F32)], axis=1).astype(BF16)


def _bias_rows(n):
    return _round_up(HEAD_DIM + n, LANES) - HEAD_DIM


def _sel_blocks(n_pos):
    n_sel = -(-n_pos // SEL_BLOCK)
    return n_sel, _bias_rows(n_sel)


def _nsa_branches(q, n_q, cmp_x, cmp_specs, n_seq, n_rows, ke_sel, vt_sel, win_kv, cw, tq, q_base, win_base, tk,
                  sel_batch):
    G = NSA_KV_HEADS
    w1, pe, w2 = cw
    n_cmp_rows = n_rows // CMP_STRIDE * CMP_STRIDE
    kc = _nsa_compress(cmp_x, cmp_specs[0], n_seq, n_cmp_rows, w1[0], pe[0], w2[0], False)
    vct = _nsa_compress(cmp_x, cmp_specs[1], n_seq, n_cmp_rows, w1[1], pe[1], w2[1], True)
    n_sel, nbp = _sel_blocks(max(n_rows, q_base + n_q))
    ovt = _overlap_matrix_t(n_rows // CMP_STRIDE - 1, n_sel, kc.shape[1], _round_up(n_sel, SUBLANES))
    oc, bias = _cmp_select(q, kc, vct, ovt, tq, q_base, n_sel, nbp)
    o_s = _flash("sel", q, ke_sel, vt_sel, bias, tq, q_base, 0, n_sel, G, tk, sel_batch)
    Lwp = _round_up(max(win_kv.shape[1], q_base + n_q - win_base), ATT_K_TILE)
    o_w = _flash("win", q, _key_operand(win_kv[:, :, 0], Lwp), _value_operand(win_kv[:, :, 1], Lwp), None, tq,
                 q_base, win_base, 0, G)
    return oc, o_s, o_w


def _nsa_prompt(q, kv6, cw, tq):
    B, T = q.shape[:2]
    width = NSA_KV_HEADS * HEAD_DIM
    assert T % CMP_STRIDE == 0
    cmp_specs = [pl.BlockSpec((T, width), lambda b, s=s: (b, s)) for s in (0, 1)]
    _, nbp = _sel_blocks(T)
    Lp = _round_up(T, ATT_K_TILE)
    return _nsa_branches(q, T, kv6.reshape(B * T, -1), cmp_specs, B, T, _key_operand(kv6[:, :, 2], Lp, SEL_BLOCK, nbp),
                         _value_operand(kv6[:, :, 3], Lp), kv6[:, :, 4:6], cw, tq, 0, 0, ATT_K_TILE, 1)


def _nsa_sample(q, kv6, cache, page_table, win_buf, cw, tq):
    Bs, Ts = q.shape[:2]
    page = cache.shape[-1]
    past_len = page_table.shape[1] * page
    step = PAGES_PER_STEP * page
    assert past_len % CMP_STRIDE == 0 and Ts < CMP_STRIDE and tq <= step
    _, nbp = _sel_blocks(past_len + tq)
    tail_ke = _key_operand(kv6[:, :, 2], step, SEL_BLOCK, nbp, past_len)
    tail_vt = _value_operand(kv6[:, :, 3], step)
    rows, ke, vt, _ = _gather_kv(cache, page_table, NSA_KV_HEADS, (0, 1), 2, 3, SEL_BLOCK, tail_ke, tail_vt, False)
    win = jnp.concatenate([win_buf, kv6[:, :, 4:6]], axis=1)
    qt = _group_queries_t(_pad_queries(q, tq), NSA_KV_HEADS, tq)
    cmp_specs = [pl.BlockSpec((None, None) + rows.shape[2:], lambda b, s=s: (s, b, 0, 0)) for s in (0, 1)]
    outs = _nsa_branches(qt, tq, rows, cmp_specs, Bs, past_len + Ts, ke, vt, win, cw, tq, past_len,
                         past_len - win_buf.shape[1], step, NSA_KV_HEADS)
    return [_ungroup_t(o, Bs, NSA_KV_HEADS, tq)[:, :Ts] for o in outs], win[:, Ts:]


def _moba_prompt(q, kv, tq):
    B, T, _ = q.shape
    n_blk = -(-T // MOBA_BLOCK)
    nbp = _bias_rows(n_blk)
    Lp = _round_up(T, ATT_K_TILE)
    means = _block_means(_head_major(kv[:, :, 0], Lp), n_blk, nbp)
    return _flash("moba", q, _key_operand(kv[:, :, 0], Lp, MOBA_BLOCK, nbp), _value_operand(kv[:, :, 1], Lp), means,
                  tq, 0, 0, n_blk, MOBA_KV_HEADS)


def _moba_sample(q, kv, cache, page_table, tq):
    Bs, Ts = q.shape[:2]
    page = cache.shape[-1]
    past_len = page_table.shape[1] * page
    step = PAGES_PER_STEP * page
    assert step % MOBA_BLOCK == 0 and tq <= MOBA_BLOCK
    n_blk = -(-(past_len + tq) // MOBA_BLOCK)
    nbp = _bias_rows(n_blk)
    tail_ke = _key_operand(kv[:, :, 0], step, MOBA_BLOCK, nbp, past_len)
    tail_vt = _value_operand(kv[:, :, 1], step)
    _, ke, vt, means = _gather_kv(cache, page_table, MOBA_KV_HEADS, (), 0, 1, MOBA_BLOCK, tail_ke, tail_vt, True)
    means = jnp.pad(means, ((0, 0), (0, max(nbp - means.shape[1], 0)), (0, 0)))[:, :nbp]
    qt = _group_queries_t(_pad_queries(q, tq), MOBA_KV_HEADS, tq)
    o = _flash("moba", qt, ke, vt, means, tq, past_len, 0, n_blk, MOBA_KV_HEADS, step, MOBA_KV_HEADS)
    return _ungroup_t(o, Bs, MOBA_KV_HEADS, tq)[:, :Ts]


def _gather_kv_kernel(n_groups, row_slots, k_slot, v_slot, block, with_means, pt_ref, *refs):
    del pt_ref
    G = n_groups
    dh = HEAD_DIM
    pps = PAGES_PER_STEP
    pages = refs[:pps]
    tail_ke_ref, tail_vt_ref = refs[pps:pps + 2]
    outs = list(refs[pps + 2:])
    rows_ref = outs.pop(0) if row_slots else None
    ke_ref, vt_ref = outs[0], outs[1]
    means_ref = outs[2] if with_means else None
    p = pl.program_id(1)
    last = pl.num_programs(1) - 1
    page = pages[0].shape[-1]
    step, kw = ke_ref.shape[1:]

    @pl.when(p < last)
    def _():
        lane = lax.broadcasted_iota(jnp.int32, (step, kw), 1)
        row = p * step + lax.broadcasted_iota(jnp.int32, (step, kw), 0)
        onehot = jnp.where(lane - dh == row // block, 1.0, 0.0).astype(BF16)
        for g in range(G):
            ke_ref[g] = onehot
        vt_ref[:, dh:, :] = jnp.ones((G, V_ROWS - dh, step), BF16)
        ppb = block // page
        tot = None
        for j, pg in enumerate(pages):
            r0 = j * page
            for si, s in enumerate(row_slots):
                rows_ref[si, r0:r0 + page, :] = pg[s].reshape(G * dh, page).T
            k = pg[k_slot].reshape(G * dh, page).T
            for g in range(G):
                ke_ref[g, r0:r0 + page, 0:dh] = k[:, g * dh:(g + 1) * dh].astype(BF16)
                vt_ref[g, 0:dh, r0:r0 + page] = pg[v_slot, g].astype(BF16)
            if with_means:
                ksum = jnp.sum(k, axis=0, keepdims=True)
                tot = ksum if j % ppb == 0 else tot + ksum
                if j % ppb == ppb - 1:
                    n = j // ppb
                    for g in range(G):
                        means_ref[g, n:n + 1, :] = tot[:, g * dh:(g + 1) * dh] / block

    @pl.when(p == last)
    def _():
        ke_ref[...] = tail_ke_ref[...]
        vt_ref[...] = tail_vt_ref[...]
        if with_means:
            means_ref[...] = jnp.zeros(means_ref.shape, F32)


def _gather_kv(cache, page_table, n_groups, row_slots, k_slot, v_slot, block, tail_ke, tail_vt, with_means):
    _, n_slots, G, dh, page = cache.shape
    Bs, n_pages = page_table.shape
    assert G == n_groups and dh == HEAD_DIM
    pps = PAGES_PER_STEP
    step = pps * page
    assert n_pages % pps == 0
    if with_means:
        assert step % block == 0 and block % page == 0
    n_steps = n_pages // pps
    kw = tail_ke.shape[-1]
    Lp = (n_steps + 1) * step
    bps = step // block

    def page_map(j):
        return lambda b, p, pt: (pt[b, jnp.minimum(p, n_steps - 1) * pps + j], 0, 0, 0, 0)

    in_specs = [pl.BlockSpec((None, n_slots, G, dh, page), page_map(j)) for j in range(pps)]
    in_specs += [pl.BlockSpec((None, G, step, kw), lambda b, p, pt: (b, 0, 0, 0)),
                 pl.BlockSpec((None, G, V_ROWS, step), lambda b, p, pt: (b, 0, 0, 0))]
    out_shape, out_specs = [], []
    if row_slots:
        n_rs = len(row_slots)
        out_shape.append(jax.ShapeDtypeStruct((n_rs, Bs, n_steps * step, G * dh), F32))
        out_specs.append(pl.BlockSpec((n_rs, None, step, G * dh),
                                      lambda b, p, pt: (0, b, jnp.minimum(p, n_steps - 1), 0)))
    out_shape += [jax.ShapeDtypeStruct((Bs, G, Lp, kw), BF16), jax.ShapeDtypeStruct((Bs, G, V_ROWS, Lp), BF16)]
    out_specs += [pl.BlockSpec((None, G, step, kw), lambda b, p, pt: (b, 0, p, 0)),
                  pl.BlockSpec((None, G, V_ROWS, step), lambda b, p, pt: (b, 0, 0, p))]
    if with_means:
        out_shape.append(jax.ShapeDtypeStruct((Bs, G, n_steps + 1, bps, dh), F32))
        out_specs.append(pl.BlockSpec((None, G, None, bps, dh), lambda b, p, pt: (b, 0, p, 0, 0)))
    grid_spec = pltpu.PrefetchScalarGridSpec(num_scalar_prefetch=1, grid=(Bs, n_steps + 1), in_specs=in_specs,
                                             out_specs=out_specs)
    outs = list(pl.pallas_call(
        functools.partial(_gather_kv_kernel, G, tuple(row_slots), k_slot, v_slot, block, with_means),
        grid_spec=grid_spec, out_shape=out_shape, compiler_params=_params("parallel", "arbitrary"),
        name="gather_kv",
    )(page_table, *([cache] * pps), tail_ke.reshape(Bs, G, step, kw), tail_vt.reshape(Bs, G, V_ROWS, step)))
    rows = outs.pop(0) if row_slots else None
    ke = outs[0].reshape(Bs * G, Lp, kw)
    vt = outs[1].reshape(Bs * G, V_ROWS, Lp)
    means = outs[2].reshape(Bs * G, (n_steps + 1) * bps, dh) if with_means else None
    return rows, ke, vt, means


def _pad_queries(x, tq):
    return jnp.pad(x, ((0, 0), (0, tq - x.shape[1]), (0, 0)))


def kernel(x_prompt, x_sample, cache_nsa_kv, cache_moba_kv, state_win_kv, state_wkv, state_shift, page_table, norm_mix, norm_ffn, norm_final, even_w_in, even_w_out, rwkv_mu, rwkv_w0, rwkv_w_up, rwkv_a0, rwkv_a_up, rwkv_g_up, rwkv_k_k, rwkv_k_a, rwkv_r_k, rwkv_ln_g, rwkv_ln_b, nsa_gate_b, nsa_cmp_w1, nsa_cmp_pe, nsa_cmp_w2, odd_w_in, odd_w_out, ffn_w_gate, ffn_w_up, ffn_w_down):
    B, T, D = x_prompt.shape
    Bs, Ts, _ = x_sample.shape
    depth = norm_mix.shape[0]
    page = cache_nsa_kv.shape[2]
    past_len = page_table.shape[1] * page
    rwkv_dim = rwkv_w0.shape[1]
    rwkv_cols = rwkv_mu.shape[1]
    nsa_heads = nsa_gate_b.shape[1] // 3
    nsa_dim = nsa_heads * HEAD_DIM
    nsa_kv_cols = 6 * NSA_KV_HEADS * HEAD_DIM
    moba_kv_cols = 2 * MOBA_KV_HEADS * HEAD_DIM
    moba_dim = odd_w_in.shape[2] - moba_kv_cols
    tq_p = min(ATT_Q_TILE, T)
    tq_s = SAMPLE_Q_PAD

    cos_p, sin_p = _rope_tables(jnp.arange(T, dtype=jnp.int32))
    pos_s = past_len + jnp.arange(Ts, dtype=jnp.int32)
    cos_s, sin_s = _rope_tables(jnp.tile(pos_s, Bs))

    even_spec = ((rwkv_cols, (), False), (nsa_dim, tuple(range(nsa_dim // LANES)), False),
                 (nsa_kv_cols, tuple(range(0, nsa_kv_cols // LANES, 2)), False), (LANES, (), True))
    k_chunks = MOBA_KV_HEADS * HEAD_DIM // LANES
    odd_spec = ((moba_dim, tuple(range(moba_dim // LANES)), False), (moba_kv_cols, tuple(range(k_chunks)), False))

    hp = x_prompt.reshape(B * T, D)
    hs = x_sample.reshape(Bs * Ts, D)
    nsa_p, nsa_s, moba_p, moba_s = [], [], [], []
    win_p, win_s, wkv_p, wkv_s, sh_p, sh_s = [], [], [], [], [], []
    for layer in range(depth):
        i = layer // 2
        if layer % 2 == 0:
            w_in = even_w_in[i].astype(BF16)
            o = rwkv_cols
            n_gate = 3 * nsa_heads
            weights = [w_in[:, :o], w_in[:, o:o + nsa_dim], w_in[:, o + nsa_dim:o + nsa_dim + nsa_kv_cols],
                       jnp.pad(w_in[:, o + nsa_dim + nsa_kv_cols:], ((0, 0), (0, LANES - n_gate)))]
            gate_b = jnp.pad(nsa_gate_b[i], (0, LANES - n_gate)).reshape(1, LANES)
            rp = (rwkv_mu[i], rwkv_w0[i], rwkv_w_up[i], rwkv_a0[i], rwkv_a_up[i], rwkv_g_up[i],
                  rwkv_k_k[i], rwkv_k_a[i], rwkv_r_k[i], rwkv_ln_g[i], rwkv_ln_b[i])
            cw = (nsa_cmp_w1[i], nsa_cmp_pe[i], nsa_cmp_w2[i])

            rw, q, kv, gates = _project(hp, norm_mix[layer], cos_p, sin_p, weights, [gate_b], even_spec, "even_proj")
            rw3 = rw.reshape(B, T, rwkv_cols)
            kv6 = kv.reshape(B, T, 6, NSA_KV_HEADS, HEAD_DIM)
            y, g, wkv_new = _rwkv_mix(rw3, jnp.zeros((B, rwkv_cols), F32),
                                      jnp.zeros((B, rwkv_dim // HEAD_DIM, HEAD_DIM, HEAD_DIM), F32), rp, SCAN_CHUNK)
            o_c, o_s, o_w = _nsa_prompt(q.reshape(B, T, nsa_dim), kv6, cw, tq_p)
            hp = _even_out(hp, y, g, o_c.reshape(B * T, nsa_dim), o_s.reshape(B * T, nsa_dim),
                           o_w.reshape(B * T, nsa_dim), gates, even_w_out[i])
            nsa_p.append(kv6[:, :, :4])
            win_p.append(kv6[:, T - min(WINDOW, T):, 4:6])
            wkv_p.append(wkv_new)
            sh_p.append(rw3[:, -1])

            rw, q, kv, gates = _project(hs, norm_mix[layer], cos_s, sin_s, weights, [gate_b], even_spec, "even_proj_s")
            rw3 = rw.reshape(Bs, Ts, rwkv_cols)
            kv6 = kv.reshape(Bs, Ts, 6, NSA_KV_HEADS, HEAD_DIM)
            y, g, wkv_new = _rwkv_mix(rw3, state_shift[i], state_wkv[i], rp, SUBLANES)
            outs, win_new = _nsa_sample(q.reshape(Bs, Ts, nsa_dim), kv6,
                                        cache_nsa_kv[i].transpose(0, 2, 3, 4, 1), page_table,
                                        state_win_kv[i], cw, tq_s)
            o_c, o_s, o_w = [t.reshape(Bs * Ts, nsa_dim) for t in outs]
            hs = _even_out(hs, y, g, o_c, o_s, o_w, gates, even_w_out[i])
            nsa_s.append(kv6[:, :, :4])
            win_s.append(win_new)
            wkv_s.append(wkv_new)
            sh_s.append(rw3[:, -1])
        else:
            w_in = odd_w_in[i].astype(BF16)
            weights = [w_in[:, :moba_dim], w_in[:, moba_dim:]]
            q, kv = _project(hp, norm_mix[layer], cos_p, sin_p, weights, [], odd_spec, "odd_proj")
            kv2 = kv.reshape(B, T, 2, MOBA_KV_HEADS, HEAD_DIM)
            a = _moba_prompt(q.reshape(B, T, moba_dim), kv2, tq_p)
            hp = _odd_out(hp, a.reshape(B * T, moba_dim), odd_w_out[i])
            moba_p.append(kv2)

            q, kv = _project(hs, norm_mix[layer], cos_s, sin_s, weights, [], odd_spec, "odd_proj_s")
            kv2 = kv.reshape(Bs, Ts, 2, MOBA_KV_HEADS, HEAD_DIM)
            a = _moba_sample(q.reshape(Bs, Ts, moba_dim), kv2, cache_moba_kv[i].transpose(0, 2, 3, 4, 1),
                             page_table, tq_s)
            hs = _odd_out(hs, a.reshape(Bs * Ts, moba_dim), odd_w_out[i])
            moba_s.append(kv2)
        g_final = norm_final if layer == depth - 1 else None
        hp = _ffn(hp, norm_ffn[layer], ffn_w_gate[layer], ffn_w_up[layer], ffn_w_down[layer], g_final)
        hs = _ffn(hs, norm_ffn[layer], ffn_w_gate[layer], ffn_w_up[layer], ffn_w_down[layer], g_final)
    return (hp.reshape(B, T, D), hs.reshape(Bs, Ts, D), jnp.stack(nsa_p), jnp.stack(nsa_s), jnp.stack(moba_p),
            jnp.stack(moba_s), jnp.stack(win_p), jnp.stack(win_s), jnp.stack(wkv_p), jnp.stack(wkv_s),
            jnp.stack(sh_p), jnp.stack(sh_s))
```

```python
import functools
import math

import jax
import jax.numpy as jnp
from jax import lax
from jax.experimental import pallas as pl
from jax.experimental.pallas import tpu as pltpu

F32 = jnp.float32
BF16 = jnp.bfloat16
HIGHEST = lax.Precision.HIGHEST

HEAD_DIM = 64
NORM_EPS = 1e-6
ROPE_THETA = 10000.0
DECAY_LORA = 64
AAA_LORA = 64
GATE_LORA = 128
RWKV_GN_EPS = 64e-5
NSA_KV_HEADS = 2
CMP_STRIDE = 16
CMP_LEN = 2 * CMP_STRIDE
SEL_BLOCK = 64
SEL_TOPN = 16
WINDOW = 512
FORCE_BONUS = 100.0
MOBA_KV_HEADS = 4
MOBA_BLOCK = 256
MOBA_TOPK = 3

LANES = 128
SUBLANES = 8
VMEM_LIMIT = 56 * 1024 * 1024

ROW_TILE = 512
FFN_ROW_TILE = 512
FFN_COL_TILE = 1408
ATT_Q_TILE = 512
ATT_K_TILE = 512
SCAN_CHUNK = 64
SCAN_CHUNKS_PER_STEP = 4
SAMPLE_Q_PAD = 32
PAGES_PER_STEP = 8

NT = (((1,), (1,)), ((), ()))
TN = (((0,), (0,)), ((), ()))
NEG_INF = float("-inf")
NEG_BIG = -1e30
V_ROWS = HEAD_DIM + 16


def _round_up(x, m):
    return -(-x // m) * m


def _mm(a, b, dims=None, exact=False):
    if dims is None:
        dims = (((a.ndim - 1,), (0,)), ((), ()))
    if exact:
        return lax.dot_general(a.astype(F32), b.astype(F32), dims, precision=HIGHEST,
                               preferred_element_type=F32)
    return lax.dot_general(a.astype(BF16), b.astype(BF16), dims, preferred_element_type=F32)


def _split3(x):
    hi = x.astype(BF16)
    rest = x - hi.astype(F32)
    mid = rest.astype(BF16)
    return hi, mid, (rest - mid.astype(F32)).astype(BF16)


def _mm_onehot(a, b, dims=None):
    return sum(_mm(t, b, dims) for t in _split3(a))


def _mm_3pass(a, b, dims=None):
    a_hi, a_lo, _ = _split3(a)
    b_hi, b_lo, _ = _split3(b)
    return _mm(a_hi, b_hi, dims) + _mm(a_lo, b_hi, dims) + _mm(a_hi, b_lo, dims)


def _params(*sem):
    return pltpu.CompilerParams(dimension_semantics=sem, vmem_limit_bytes=VMEM_LIMIT)


def _rms_norm(x, g):
    return x * lax.rsqrt(jnp.mean(x * x, axis=-1, keepdims=True) + NORM_EPS) * g


def _rope_chunk(x, cos, sin):
    lane = lax.broadcasted_iota(jnp.int32, x.shape, 1)
    half = HEAD_DIM // 2
    partner = jnp.where((lane % HEAD_DIM) < half, pltpu.roll(x, LANES - half, 1), pltpu.roll(x, half, 1))
    return x * cos + partner * sin


def _rank_rows(score, blk, ncand):
    rank = jnp.zeros(score.shape, F32)
    for j in range(ncand):
        row = score[..., j:j + 1, :]
        rank = rank + jnp.where(row > score, 1.0, 0.0) + jnp.where(row == score, jnp.where(blk > j, 1.0, 0.0), 0.0)
    return rank


def _proj_kernel(spec, x_ref, g_ref, cos_ref, sin_ref, *refs):
    nseg = len(spec)
    w_refs = refs[:nseg]
    nbias = sum(1 for s in spec if s[2])
    b_refs = list(refs[nseg:nseg + nbias])
    o_refs = refs[nseg + nbias:]
    xn = _rms_norm(x_ref[...], g_ref[...]).astype(BF16)
    for (ncols, rope_chunks, sig), w_ref, o_ref in zip(spec, w_refs, o_refs):
        y = jnp.dot(xn, w_ref[...], preferred_element_type=F32)
        if sig:
            y = jax.nn.sigmoid(y + b_refs.pop(0)[...])
        if rope_chunks:
            cos = cos_ref[...]
            sin = sin_ref[...]
            for c in range(ncols // LANES):
                yc = y[:, c * LANES:(c + 1) * LANES]
                if c in rope_chunks:
                    yc = _rope_chunk(yc, cos, sin)
                o_ref[:, c * LANES:(c + 1) * LANES] = yc
        else:
            o_ref[...] = y


def _project(x, g, cos_tab, sin_tab, weights, biases, spec, name):
    M, D = x.shape
    tm = min(ROW_TILE, M)
    ntab = cos_tab.shape[0] // tm
    in_specs = [pl.BlockSpec((tm, D), lambda i: (i, 0)),
                pl.BlockSpec((1, D), lambda i: (0, 0)),
                pl.BlockSpec((tm, LANES), lambda i: (i % ntab, 0)),
                pl.BlockSpec((tm, LANES), lambda i: (i % ntab, 0))]
    in_specs += [pl.BlockSpec(w.shape, lambda i: (0, 0)) for w in weights]
    in_specs += [pl.BlockSpec(b.shape, lambda i: (0, 0)) for b in biases]
    out_shape = [jax.ShapeDtypeStruct((M, s[0]), F32) for s in spec]
    out_specs = [pl.BlockSpec((tm, s[0]), lambda i: (i, 0)) for s in spec]
    return pl.pallas_call(
        functools.partial(_proj_kernel, spec), grid=(M // tm,), in_specs=in_specs, out_specs=out_specs,
        out_shape=out_shape, compiler_params=_params("parallel"), name=name,
    )(x, g.reshape(1, D), cos_tab, sin_tab, *weights, *biases)


def _rope_tables(pos):
    half = HEAD_DIM // 2
    inv = ROPE_THETA ** (-jnp.arange(half, dtype=F32) / half)
    ang = pos.astype(F32)[:, None] * inv[None, :]
    cos = jnp.cos(ang)
    sin = jnp.sin(ang)
    cos_t = jnp.tile(cos, (1, LANES // half))
    sin_t = jnp.tile(jnp.concatenate([-sin, sin], axis=1), (1, LANES // HEAD_DIM))
    return cos_t, sin_t


def _even_out_kernel(res_ref, y_ref, g_ref, oc_ref, os_ref, ow_ref, gt_ref, ec_ref, es_ref, ew_ref,
                     w1_ref, w2_ref, o_ref):
    a = y_ref[...] * g_ref[...]
    gt = gt_ref[...]
    b = (_mm_onehot(gt, ec_ref[...]) * oc_ref[...] + _mm_onehot(gt, es_ref[...]) * os_ref[...]
         + _mm_onehot(gt, ew_ref[...]) * ow_ref[...])
    o_ref[...] = res_ref[...] + _mm(a, w1_ref[...]) + _mm(b, w2_ref[...])


def _even_out(res, y, g, o_c, o_s, o_w, gates, w_out):
    M, D = res.shape
    n_rw = y.shape[1]
    n_nsa = o_c.shape[1]
    tm = min(ROW_TILE, M)
    heads = n_nsa // HEAD_DIM
    col = jnp.arange(n_nsa)[None, :] // HEAD_DIM
    row = jnp.arange(LANES)[:, None]
    expand = [(row == col * 3 + br).astype(F32) for br in range(3)]
    del heads
    row_spec = lambda n: pl.BlockSpec((tm, n), lambda i: (i, 0))
    full = lambda a: pl.BlockSpec(a.shape, lambda i: (0, 0))
    w1 = w_out[:n_rw].astype(BF16)
    w2 = w_out[n_rw:].astype(BF16)
    return pl.pallas_call(
        _even_out_kernel, grid=(M // tm,),
        in_specs=[row_spec(D), row_spec(n_rw), row_spec(n_rw), row_spec(n_nsa), row_spec(n_nsa), row_spec(n_nsa),
                  row_spec(LANES), full(expand[0]), full(expand[1]), full(expand[2]), full(w1), full(w2)],
        out_specs=row_spec(D), out_shape=jax.ShapeDtypeStruct((M, D), F32),
        compiler_params=_params("parallel"), name="even_out",
    )(res, y, g, o_c, o_s, o_w, gates, *expand, w1, w2)


def _odd_out_kernel(res_ref, a_ref, w_ref, o_ref):
    o_ref[...] = res_ref[...] + _mm(a_ref[...], w_ref[...])


def _odd_out(res, a, w_out):
    M, D = res.shape
    tm = min(ROW_TILE, M)
    w = w_out.astype(BF16)
    return pl.pallas_call(
        _odd_out_kernel, grid=(M // tm,),
        in_specs=[pl.BlockSpec((tm, D), lambda i: (i, 0)), pl.BlockSpec((tm, a.shape[1]), lambda i: (i, 0)),
                  pl.BlockSpec(w.shape, lambda i: (0, 0))],
        out_specs=pl.BlockSpec((tm, D), lambda i: (i, 0)), out_shape=jax.ShapeDtypeStruct((M, D), F32),
        compiler_params=_params("parallel"), name="odd_out",
    )(res, a, w)


def _ffn_kernel(final_norm, x_ref, g_ref, wg_ref, wu_ref, wd_ref, gf_ref, o_ref, xn_scr, acc_scr):
    j = pl.program_id(1)

    @pl.when(j == 0)
    def _():
        xn_scr[...] = _rms_norm(x_ref[...], g_ref[...]).astype(BF16)
        acc_scr[...] = jnp.zeros(acc_scr.shape, F32)

    xn = xn_scr[...]
    h = jax.nn.silu(jnp.dot(xn, wg_ref[...], preferred_element_type=F32)) * jnp.dot(
        xn, wu_ref[...], preferred_element_type=F32)
    acc_scr[...] += _mm(h, wd_ref[...])

    @pl.when(j == pl.num_programs(1) - 1)
    def _():
        y = x_ref[...] + acc_scr[...]
        if final_norm:
            y = _rms_norm(y, gf_ref[...])
        o_ref[...] = y


def _ffn(x, g, wg, wu, wd, g_final=None):
    M, D = x.shape
    F = wg.shape[1]
    tm = min(FFN_ROW_TILE, M)
    tf = FFN_COL_TILE
    assert M % tm == 0 and F % tf == 0
    final_norm = g_final is not None
    gf = (g_final if final_norm else g).reshape(1, D)
    return pl.pallas_call(
        functools.partial(_ffn_kernel, final_norm), grid=(M // tm, F // tf),
        in_specs=[pl.BlockSpec((tm, D), lambda i, j: (i, 0)), pl.BlockSpec((1, D), lambda i, j: (0, 0)),
                  pl.BlockSpec((D, tf), lambda i, j: (0, j)), pl.BlockSpec((D, tf), lambda i, j: (0, j)),
                  pl.BlockSpec((tf, D), lambda i, j: (j, 0)), pl.BlockSpec((1, D), lambda i, j: (0, 0))],
        out_specs=pl.BlockSpec((tm, D), lambda i, j: (i, 0)), out_shape=jax.ShapeDtypeStruct((M, D), F32),
        scratch_shapes=[pltpu.VMEM((tm, D), BF16), pltpu.VMEM((tm, D), F32)],
        compiler_params=_params("parallel", "arbitrary"), name="ffn",
    )(x, g.reshape(1, D), wg.astype(BF16), wu.astype(BF16), wd.astype(BF16), gf)


def _rwkv_pre_kernel(n_dim, rw_ref, prev_ref, mu_ref, w0_ref, a0_ref, kk_ref, ka_ref, wup_ref, aup_ref, gup_ref,
                     hsum_ref, r_out, lw_out, k_out, v_out, kk_out, b_out, g_out):
    rw = rw_ref[...]
    if prev_ref.shape[0] == rw.shape[0]:
        prev = prev_ref[...]
    else:
        first = lax.broadcasted_iota(jnp.int32, (rw.shape[0], 1), 0) == 0
        prev = jnp.where(first, prev_ref[...], pltpu.roll(rw, 1, 0))
    xm = rw + (prev - rw) * mu_ref[...]
    r = xm[:, :n_dim]
    k = xm[:, n_dim:2 * n_dim]
    v = xm[:, 2 * n_dim:3 * n_dim]
    lora = xm[:, 3 * n_dim:3 * n_dim + DECAY_LORA + AAA_LORA]
    xg = xm[:, 3 * n_dim + DECAY_LORA + AAA_LORA:]
    lw = -math.exp(-0.5) * jax.nn.sigmoid(w0_ref[...] + _mm_3pass(jnp.tanh(lora), wup_ref[...]))
    a = jax.nn.sigmoid(a0_ref[...] + _mm_3pass(lora, aup_ref[...]))
    g = _mm(jax.nn.sigmoid(xg), gup_ref[...])
    kk = k * kk_ref[...]
    sq = kk * kk
    sq_hi = sq.astype(BF16)
    norm = jnp.sqrt(_mm(sq_hi, hsum_ref[...]) + _mm(sq - sq_hi.astype(F32), hsum_ref[...]))
    kk = kk / jnp.maximum(norm, 1e-12)
    g_out[...] = g
    outs = ((r_out, r), (lw_out, lw), (k_out, k * (1.0 + (a - 1.0) * ka_ref[...])), (v_out, v), (kk_out, kk),
            (b_out, kk * a))
    for o_ref, val in outs:
        if len(o_ref.shape) == 2:
            o_ref[...] = val
        else:
            for h in range(o_ref.shape[0]):
                o_ref[h] = val[:, h * HEAD_DIM:(h + 1) * HEAD_DIM]


def _rwkv_pre(rw3, shift_prev, mu, w0, w_up, a0, a_up, g_up, k_k, k_a):
    B, seq_len, ncols = rw3.shape
    M = B * seq_len
    rw = rw3.reshape(M, ncols)
    n_dim = w0.shape[0]
    tm = min(ROW_TILE, M)
    H = n_dim // HEAD_DIM
    head_major = seq_len % tm == 0
    nt = seq_len // tm if head_major else 1
    if head_major:
        prev = jnp.concatenate([shift_prev[:, None], rw3[:, tm - 1:seq_len - 1:tm]], axis=1).reshape(B * nt, 1, ncols)
        prev_spec = pl.BlockSpec((None, 1, ncols), lambda i: (i, 0, 0))
    else:
        prev = jnp.concatenate([shift_prev[:, None], rw3[:, :-1]], axis=1).reshape(M, ncols)
        prev_spec = pl.BlockSpec((tm, ncols), lambda i: (i, 0))
    zeros = jnp.zeros((AAA_LORA, n_dim), F32)
    wup_pad = jnp.concatenate([w_up, zeros], axis=0)
    aup_pad = jnp.concatenate([jnp.zeros((DECAY_LORA, n_dim), F32), a_up], axis=0)
    head = jnp.arange(n_dim) // HEAD_DIM
    hsum = (head[:, None] == head[None, :]).astype(F32)
    vec = lambda a: a.reshape(1, -1)
    row = lambda n: pl.BlockSpec((tm, n), lambda i: (i, 0))
    full = lambda a: pl.BlockSpec(a.shape, lambda i: (0, 0))
    ins = [rw, prev, vec(mu), vec(w0), vec(a0), vec(k_k), vec(k_a), wup_pad, aup_pad, g_up, hsum]
    if head_major:
        seq_spec = pl.BlockSpec((None, H, tm, HEAD_DIM), lambda i: (i // nt, 0, i % nt, 0))
        seq_shape = jax.ShapeDtypeStruct((M // seq_len, H, seq_len, HEAD_DIM), F32)
    else:
        seq_spec, seq_shape = row(n_dim), jax.ShapeDtypeStruct((M, n_dim), F32)
    outs = pl.pallas_call(
        functools.partial(_rwkv_pre_kernel, n_dim), grid=(M // tm,),
        in_specs=[row(ncols), prev_spec] + [full(a) for a in ins[2:]],
        out_specs=[seq_spec] * 6 + [row(n_dim)],
        out_shape=[seq_shape] * 6 + [jax.ShapeDtypeStruct((M, n_dim), F32)],
        compiler_params=_params("parallel"), name="rwkv_pre",
    )(*ins)
    if head_major:
        outs = [t.reshape(-1, seq_len, HEAD_DIM) for t in outs[:6]] + [outs[6]]
    return outs, head_major


def _rwkv_chunk_kernel(n_heads, chunk, r_ref, lw_ref, k_ref, v_ref, kk_ref, b_ref, rk_ref,
                       rw_ref, y0_ref, bonus_ref, a_ref, s1_ref):
    C = chunk
    dh = HEAD_DIM
    ti = lax.broadcasted_iota(jnp.int32, (C, C), 0)
    si = lax.broadcasted_iota(jnp.int32, (C, C), 1)
    incl = jnp.where(si <= ti, 1.0, 0.0)
    eye = jnp.where(lax.broadcasted_iota(jnp.int32, (dh, dh), 0) == lax.broadcasted_iota(jnp.int32, (dh, dh), 1),
                    1.0, 0.0)
    n_double = max(1, math.ceil(math.log2(C)))
    heads = range(n_heads)
    r = [r_ref[h] for h in heads]
    lw = [lw_ref[h] for h in heads]
    k = [k_ref[h] for h in heads]
    v = [v_ref[h] for h in heads]
    b = [b_ref[h] for h in heads]
    cum = [sum(_mm(incl, t) for t in _split3(lw[h])) for h in heads]
    cum_end = [cum[h][C - 1:C, :] for h in heads]
    kkw = [kk_ref[h] * jnp.exp(cum[h] - lw[h]) for h in heads]
    rwc = [r[h] * jnp.exp(cum[h]) for h in heads]
    w_inv = [jnp.exp(-cum[h]) for h in heads]
    kd = [k[h] * w_inv[h] for h in heads]
    bd = [b[h] * w_inv[h] for h in heads]
    w_end = [jnp.exp(cum_end[h] - cum[h]) for h in heads]
    both = [jnp.concatenate([kkw[h], rwc[h]], axis=0) for h in heads]
    g_b = [_mm(both[h], bd[h], NT) for h in heads]
    g_k = [_mm(both[h], kd[h], NT) for h in heads]
    a_ub = [jnp.where(si < ti, g_b[h][:C], 0.0) for h in heads]
    b_rb = [jnp.where(si <= ti, g_b[h][C:], 0.0) for h in heads]
    a_vk = [jnp.where(si < ti, g_k[h][:C], 0.0) for h in heads]
    b_rk = [jnp.where(si <= ti, g_k[h][C:], 0.0) for h in heads]
    gv = [_mm(jnp.concatenate([a_vk[h], b_rk[h]], axis=0), v[h]) for h in heads]
    x = [jnp.concatenate([kkw[h], gv[h][:C]], axis=1) for h in heads]
    p = [-a_ub[h] for h in heads]
    for _ in range(n_double):
        px = [_mm(p[h], jnp.concatenate([x[h], p[h]], axis=1)) for h in heads]
        x = [x[h] + px[h][:, :2 * dh] for h in heads]
        p = [px[h][:, 2 * dh:] for h in heads]
    for h in heads:
        bx = _mm(b_rb[h], x[h])
        rw_ref[h] = rwc[h] - bx[:, :dh]
        y0_ref[h] = gv[h][C:] - bx[:, dh:]
        xb = _mm(x[h], b[h] * w_end[h], TN)
        a_ref[h] = eye * jnp.exp(cum_end[h]) - xb[:dh]
        s1_ref[h] = _mm(v[h], k[h] * w_end[h], TN) - xb[dh:]
        bonus_ref[h] = jnp.sum(r[h] * k[h] * rk_ref[h:h + 1, :], axis=-1, keepdims=True) * v[h]


def _rwkv_state_kernel(n_heads, n_sub, rw_ref, y0_ref, bonus_ref, a_ref, s1_ref, s0_ref, lng_ref, lnb_ref,
                       y_ref, s_out_ref, s_scr):
    c = pl.program_id(1)

    @pl.when(c == 0)
    def _():
        s_scr[...] = s0_ref[...]

    chunk = rw_ref.shape[2]
    for h in range(n_heads):
        s = s_scr[h]
        for j in range(n_sub):
            y = _mm(rw_ref[h, j], s, NT) + y0_ref[h, j]
            s = _mm(s, a_ref[h, j]) + s1_ref[h, j]
            mean = jnp.mean(y, axis=-1, keepdims=True)
            var = jnp.mean(jnp.square(y - mean), axis=-1, keepdims=True)
            yn = (y - mean) * lax.rsqrt(var + RWKV_GN_EPS) * lng_ref[h:h + 1, :] + lnb_ref[h:h + 1, :]
            y_ref[j * chunk:(j + 1) * chunk, h * HEAD_DIM:(h + 1) * HEAD_DIM] = yn + bonus_ref[h, j]
        s_scr[h] = s

    @pl.when(c == pl.num_programs(1) - 1)
    def _():
        s_out_ref[...] = s_scr[...]


def _rwkv_scan(r, lw, k, v, kk, b, s0, r_k, ln_g, ln_b, chunk):
    n_bh, T, dh = r.shape
    H = r_k.shape[0]
    n_chunks = T // chunk
    seq = pl.BlockSpec((H, chunk, dh), lambda i, c: (i, c, 0))
    mat = pl.BlockSpec((H, None, dh, dh), lambda i, c: (i, c, 0, 0))
    par = pl.BlockSpec((H, dh), lambda i, c: (0, 0))
    seq_shape = jax.ShapeDtypeStruct((n_bh, T, dh), F32)
    mat_shape = jax.ShapeDtypeStruct((n_bh, n_chunks, dh, dh), F32)
    rw, y0, bonus, a, s1 = pl.pallas_call(
        functools.partial(_rwkv_chunk_kernel, H, chunk), grid=(n_bh // H, n_chunks),
        in_specs=[seq] * 6 + [par], out_specs=[seq, seq, seq, mat, mat],
        out_shape=[seq_shape, seq_shape, seq_shape, mat_shape, mat_shape],
        compiler_params=_params("parallel", "parallel"), name="rwkv_chunk",
    )(r, lw, k, v, kk, b, r_k)
    n_sub = math.gcd(SCAN_CHUNKS_PER_STEP, n_chunks)
    split = lambda t: t.reshape(n_bh, n_chunks, chunk, dh)
    seq4 = pl.BlockSpec((H, n_sub, chunk, dh), lambda i, c: (i, c, 0, 0))
    mat4 = pl.BlockSpec((H, n_sub, dh, dh), lambda i, c: (i, c, 0, 0))
    state = pl.BlockSpec((H, dh, dh), lambda i, c: (i, 0, 0))
    y, s_new = pl.pallas_call(
        functools.partial(_rwkv_state_kernel, H, n_sub), grid=(n_bh // H, n_chunks // n_sub),
        in_specs=[seq4, seq4, seq4, mat4, mat4, state, par, par],
        out_specs=[pl.BlockSpec((n_sub * chunk, H * dh), lambda i, c: (i * (n_chunks // n_sub) + c, 0)), state],
        out_shape=[jax.ShapeDtypeStruct((n_bh // H * T, H * dh), F32), jax.ShapeDtypeStruct((n_bh, dh, dh), F32)],
        scratch_shapes=[pltpu.VMEM((H, dh, dh), F32)],
        compiler_params=_params("parallel", "arbitrary"), name="rwkv_state",
    )(split(rw), split(y0), split(bonus), a, s1, s0, ln_g.reshape(H, dh), ln_b.reshape(H, dh))
    return y, s_new


def _rwkv_mix(rw, shift_prev, wkv0, p, chunk):
    mu, w0, w_up, a0, a_up, g_up, k_k, k_a, r_k, ln_g, ln_b = p
    B, T, ncols = rw.shape
    H = r_k.shape[0]
    n_dim = H * HEAD_DIM
    outs, head_major = _rwkv_pre(rw, shift_prev, mu, w0, w_up, a0, a_up, g_up, k_k, k_a)
    g = outs[6]
    Tp = _round_up(T, chunk)

    def heads(t):
        if not head_major:
            t = t.reshape(B, T, H, HEAD_DIM).transpose(0, 2, 1, 3).reshape(B * H, T, HEAD_DIM)
        return jnp.pad(t, ((0, 0), (0, Tp - T), (0, 0)))

    y, s_new = _rwkv_scan(*[heads(t) for t in outs[:6]], wkv0.reshape(B * H, HEAD_DIM, HEAD_DIM), r_k, ln_g, ln_b,
                          chunk)
    y = y.reshape(B, Tp, n_dim)[:, :T].reshape(B * T, n_dim)
    return y, g, s_new.reshape(B, H, HEAD_DIM, HEAD_DIM)


def _compress_kernel(transpose_out, x_ref, w1_ref, pe_ref, w2_ref, o_ref):
    n_groups = o_ref.shape[0]
    width = x_ref.shape[1]
    nch = x_ref.shape[0] // CMP_STRIDE
    both = jnp.zeros((nch, 2 * width), F32)
    bias = jnp.zeros((1, width), F32)
    for c in range(CMP_STRIDE):
        w = w1_ref[c]
        both = both + _mm(x_ref[pl.ds(c, nch, stride=CMP_STRIDE), :], w)
        pw = _mm(pe_ref[c], w, exact=True)
        bias = bias + pw[0:1, :width] + pw[1:2, width:]
    h = jax.nn.gelu(both[:, :width] + pltpu.roll(both[:, width:], nch - 1, 0) + bias)
    if transpose_out:
        res = _mm(w2_ref[...], h, NT)
        for g in range(n_groups):
            o_ref[g] = res[g * HEAD_DIM:(g + 1) * HEAD_DIM]
    else:
        res = _mm(h, w2_ref[...])
        for g in range(n_groups):
            o_ref[g] = res[:, g * HEAD_DIM:(g + 1) * HEAD_DIM]


def _nsa_compress(x, x_spec, n_seq, n_rows, w1, pe, w2, transpose_out):
    G = NSA_KV_HEADS
    dh = HEAD_DIM
    nch = n_rows // CMP_STRIDE
    eye = jnp.eye(G, dtype=F32)


    def blockdiag(m):
        out = eye[:, None, :, None] * m[..., None, :, None, :]
        return out.reshape(m.shape[:-2] + (G * dh, G * dh))

    w1p = jnp.concatenate([blockdiag(w1[:CMP_STRIDE]), blockdiag(w1[CMP_STRIDE:])], axis=-1).astype(BF16)
    pe2 = jnp.stack([jnp.tile(pe[:CMP_STRIDE], (1, G)), jnp.tile(pe[CMP_STRIDE:], (1, G))], axis=1)
    pe2 = jnp.pad(pe2, ((0, 0), (0, SUBLANES - 2), (0, 0)))
    out_block = (None, G, dh, nch) if transpose_out else (None, G, nch, dh)
    out = pl.pallas_call(
        functools.partial(_compress_kernel, transpose_out), grid=(n_seq,),
        in_specs=[x_spec, pl.BlockSpec(w1p.shape, lambda i: (0, 0, 0)), pl.BlockSpec(pe2.shape, lambda i: (0, 0, 0)),
                  pl.BlockSpec((G * dh, G * dh), lambda i: (0, 0))],
        out_specs=pl.BlockSpec(out_block, lambda i: (i, 0, 0, 0)),
        out_shape=jax.ShapeDtypeStruct((n_seq,) + out_block[1:], F32),
        compiler_params=_params("parallel"), name="nsa_compress",
    )(x, w1p, pe2, blockdiag(w2.T if transpose_out else w2).astype(BF16))
    return out.reshape((n_seq * G,) + out_block[2:])


def _load_queries(q_ref, n_rep, tok_major):
    if not tok_major:
        return q_ref[...]
    qt = q_ref[...].T
    return jnp.concatenate([qt[r * HEAD_DIM:(r + 1) * HEAD_DIM] for r in range(n_rep)], axis=1)


def _store_outputs(o_ref, o, n_rep, tok_major):
    if not tok_major:
        o_ref[...] = o
        return
    tq = o.shape[1] // n_rep
    o_ref[...] = jnp.concatenate([o[:, r * tq:(r + 1) * tq] for r in range(n_rep)], axis=0).T


def _query_layout(q, tq, n_groups, group_batch=None):
    if q.ndim == 4:
        BG, nq, dh, rows = q.shape
        return q, False, BG, nq, rows // tq, pl.BlockSpec((group_batch, None, dh, rows), lambda i, j: (i, j, 0, 0))
    assert group_batch in (None, 1)
    B, T, n = q.shape
    G = n_groups
    n_rep = n // HEAD_DIM // G
    nq = T // tq
    spec = pl.BlockSpec((tq, n_rep * HEAD_DIM), lambda i, j: ((i // G) * nq + j, i % G))
    return q.reshape(B * T, n), True, B * G, nq, n_rep, spec


def _cmp_select_kernel(tq, n_rep, q_base, n_sel, tok_major, q_ref, kc_ref, vct_ref, ovt_ref, oc_ref, bias_ref):
    qi = pl.program_id(1)
    rows = n_rep * tq
    ncp = kc_ref.shape[0]
    nr = ovt_ref.shape[0]
    nbp = bias_ref.shape[0]
    q = _load_queries(q_ref, n_rep, tok_major) * HEAD_DIM ** -0.5
    s = _mm(kc_ref[...], q)
    t_row = q_base + qi * tq + lax.broadcasted_iota(jnp.int32, (1, rows), 1) % tq
    c_end = lax.broadcasted_iota(jnp.int32, (ncp, 1), 0) * CMP_STRIDE + (CMP_LEN - 1)
    s = jnp.where(c_end <= t_row, s, NEG_INF)
    m = jnp.max(s, axis=0, keepdims=True)
    e = jnp.exp(s - jnp.where(m == NEG_INF, 0.0, m))
    p = e / jnp.maximum(jnp.sum(e, axis=0, keepdims=True), 1e-30)
    _store_outputs(oc_ref, _mm(vct_ref[...], p), n_rep, tok_major)
    p_sum = p[:, 0:tq]
    for r in range(1, n_rep):
        p_sum = p_sum + p[:, r * tq:(r + 1) * tq]
    imp = sum(_mm(ovt_ref[...], t) for t in _split3(p_sum))
    cur = (q_base + qi * tq + lax.broadcasted_iota(jnp.int32, (1, tq), 1)) // SEL_BLOCK
    blk = lax.broadcasted_iota(jnp.int32, (nr, 1), 0)
    forced = jnp.where(blk == cur, 1.0, jnp.where(blk == cur - 1, 1.0, jnp.where(blk == 0, 1.0, 0.0)))
    score = jnp.where(blk <= cur, imp + FORCE_BONUS * forced, NEG_INF)
    rank = _rank_rows(score, blk, n_sel)
    bias = jnp.where(blk <= cur, jnp.where(rank < min(SEL_TOPN, n_sel), 0.0, NEG_BIG), NEG_BIG)
    if nbp > nr:
        bias = jnp.concatenate([bias, jnp.zeros((nbp - nr, tq), F32)], axis=0)
    bias_ref[...] = bias.astype(BF16)


def _cmp_select(q, kc, vct, ovt, tq, q_base, n_sel, nbp):
    qt, tok_major, BG, nq, n_rep, qspec = _query_layout(q, tq, NSA_KV_HEADS)
    ncp, dh = kc.shape[1:]
    return pl.pallas_call(
        functools.partial(_cmp_select_kernel, tq, n_rep, q_base, n_sel, tok_major), grid=(BG, nq),
        in_specs=[qspec, pl.BlockSpec((None, ncp, dh), lambda i, j: (i, 0, 0)),
                  pl.BlockSpec((None, dh, ncp), lambda i, j: (i, 0, 0)),
                  pl.BlockSpec(ovt.shape, lambda i, j: (0, 0))],
        out_specs=[qspec, pl.BlockSpec((None, None, nbp, tq), lambda i, j: (i, j, 0, 0))],
        out_shape=[jax.ShapeDtypeStruct(qt.shape, F32), jax.ShapeDtypeStruct((BG, nq, nbp, tq), BF16)],
        compiler_params=_params("parallel", "parallel"), name="nsa_cmp_select",
    )(qt, kc, vct, ovt)


def _overlap_matrix_t(nc, nsel, ncp, nr):
    i = jnp.arange(nc, dtype=jnp.int32)[None, :]
    j = jnp.arange(nsel, dtype=jnp.int32)[:, None]
    lo = jnp.maximum(i * CMP_STRIDE, j * SEL_BLOCK)
    hi = jnp.minimum(i * CMP_STRIDE + CMP_LEN, (j + 1) * SEL_BLOCK)
    ov = jnp.clip(hi - lo, 0).astype(F32) / CMP_LEN
    return jnp.pad(ov, ((0, nr - nsel), (0, ncp - nc)))


def _means_kernel(n_blk, k_ref, o_ref):
    o_ref[...] = jnp.zeros(o_ref.shape, F32)
    for n in range(n_blk):
        o_ref[n:n + 1, :] = jnp.sum(k_ref[n * MOBA_BLOCK:(n + 1) * MOBA_BLOCK, :], axis=0,
                                    keepdims=True) / MOBA_BLOCK


def _block_means(k, n_blk, nbp):
    BG, Lp, dh = k.shape
    return pl.pallas_call(
        functools.partial(_means_kernel, n_blk), grid=(BG,),
        in_specs=[pl.BlockSpec((None, Lp, dh), lambda i: (i, 0, 0))],
        out_specs=pl.BlockSpec((None, nbp, dh), lambda i: (i, 0, 0)),
        out_shape=jax.ShapeDtypeStruct((BG, nbp, dh), F32), compiler_params=_params("parallel"), name="moba_means",
    )(k)


def _bmm(a, b, exact=False):
    dims = (((2,), (1,)), ((0,), (0,)))
    if exact:
        return lax.dot_general(a.astype(F32), b.astype(F32), dims, precision=HIGHEST, preferred_element_type=F32)
    return lax.dot_general(a.astype(BF16), b.astype(BF16), dims, preferred_element_type=F32)


def _flash_kernel(mode, tq, n_rep, tk, q_base, k_base, n_blk, tok_major, *refs):
    if mode == "sel":
        q_ref, ke_ref, vt_ref, bias_ref, o_ref, lhs_scr, m_scr, acc_scr = refs
    elif mode == "win":
        q_ref, ke_ref, vt_ref, o_ref, lhs_scr, m_scr, acc_scr = refs
    else:
        q_ref, ke_ref, vt_ref, mean_ref, o_ref, lhs_scr, m_scr, acc_scr = refs
    qi = pl.program_id(1)
    rows = n_rep * tq
    q0 = q_base + qi * tq
    t_row = q0 + lax.broadcasted_iota(jnp.int32, (1, 1, rows), 2) % tq
    q = _load_queries(q_ref, n_rep, True)[None] if tok_major else q_ref[...]
    q = q * HEAD_DIM ** -0.5
    lhs_scr[:, 0:HEAD_DIM, :] = q.astype(BF16)
    if mode == "sel":
        lhs_scr[:, HEAD_DIM:, :] = jnp.concatenate([bias_ref[...]] * n_rep, axis=2)
    elif mode == "moba":
        nr = _round_up(n_blk, SUBLANES)
        nbp = mean_ref.shape[1]
        gate = _bmm(mean_ref[:, 0:nr, :], q, exact=True)
        blk = lax.broadcasted_iota(jnp.int32, (1, nr, 1), 1)
        cur = t_row // MOBA_BLOCK
        gate = jnp.where(blk < cur, gate, NEG_INF)
        rank = _rank_rows(gate, blk, n_blk)
        picked = jnp.where(blk < cur, jnp.where(rank < min(MOBA_TOPK, n_blk), 0.0, NEG_BIG), NEG_BIG)
        bias = jnp.where(blk == cur, 0.0, picked)
        if nbp > nr:
            bias = jnp.concatenate([bias, jnp.zeros((bias.shape[0], nbp - nr, rows), F32)], axis=1)
        lhs_scr[:, HEAD_DIM:, :] = bias.astype(BF16)
    m_scr[...] = jnp.full(m_scr.shape, NEG_BIG, F32)
    acc_scr[...] = jnp.zeros(acc_scr.shape, F32)

    def tile(ki, positional):
        k0 = pl.multiple_of(ki * tk, tk)
        s = _bmm(ke_ref[:, pl.ds(k0, tk), :], lhs_scr[...])
        if positional:
            kpos = k_base + k0 + lax.broadcasted_iota(jnp.int32, (1, tk, 1), 1)
            s = jnp.where(kpos <= t_row, s, NEG_BIG)
            if mode == "win":
                s = jnp.where(kpos > t_row - WINDOW, s, NEG_BIG)
        m_prev = m_scr[...]
        m_new = jnp.maximum(m_prev, jnp.max(s, axis=1, keepdims=True))
        p = jnp.exp(s - m_new)
        acc_scr[...] = jnp.exp(m_prev - m_new) * acc_scr[...] + _bmm(vt_ref[:, :, pl.ds(k0, tk)], p)
        m_scr[...] = m_new

    def body(positional):
        def f(ki, carry):
            tile(ki, positional)
            return carry
        return f

    last = (q0 + tq - 1 - k_base) // tk
    if mode == "win":
        lax.fori_loop(jnp.maximum(q0 - (WINDOW - 1) - k_base, 0) // tk, last + 1, body(True), 0)
    else:
        lax.fori_loop(0, last, body(False), 0)
        tile(last, True)
    acc = acc_scr[...]
    o = acc[:, 0:HEAD_DIM] / jnp.maximum(acc[:, HEAD_DIM:HEAD_DIM + 1], 1e-30)
    if tok_major:
        _store_outputs(o_ref, o[0], n_rep, True)
    else:
        o_ref[...] = o


def _flash(mode, q, ke, vt, extra, tq, q_base, k_base, n_blk, n_groups, tk=ATT_K_TILE, group_batch=1):
    gb = group_batch
    qt, tok_major, BG, nq, n_rep, qspec = _query_layout(q, tq, n_groups, gb)
    dh = HEAD_DIM
    rows = n_rep * tq
    Lp, kw = ke.shape[1:]
    assert Lp % tk == 0 and (q_base + nq * tq - 1 - k_base) // tk < Lp // tk
    assert tk % tq == 0 and (q_base - k_base) % tq == 0 and BG % gb == 0
    ins = [qt, ke, vt]
    in_specs = [qspec, pl.BlockSpec((gb, Lp, kw), lambda i, j: (i, 0, 0)),
                pl.BlockSpec((gb, V_ROWS, Lp), lambda i, j: (i, 0, 0))]
    if mode == "sel":
        ins.append(extra)
        in_specs.append(pl.BlockSpec((gb, None, kw - dh, tq), lambda i, j: (i, j, 0, 0)))
    if mode == "moba":
        ins.append(extra)
        in_specs.append(pl.BlockSpec((gb, kw - dh, dh), lambda i, j: (i, 0, 0)))
    scratch = [pltpu.VMEM((gb, kw, rows), BF16), pltpu.VMEM((gb, 1, rows), F32),
               pltpu.VMEM((gb, V_ROWS, rows), F32)]
    return pl.pallas_call(
        functools.partial(_flash_kernel, mode, tq, n_rep, tk, q_base, k_base, n_blk, tok_major),
        grid=(BG // gb, nq), in_specs=in_specs, out_specs=qspec, out_shape=jax.ShapeDtypeStruct(qt.shape, F32),
        scratch_shapes=scratch, compiler_params=_params("parallel", "parallel"), name="flash_" + mode,
    )(*ins)


def _group_queries_t(q, n_groups, tq):
    B, T, n = q.shape
    R = n // HEAD_DIM // n_groups
    q = q.reshape(B, T // tq, tq, n_groups, R, HEAD_DIM).transpose(0, 3, 1, 5, 4, 2)
    return q.reshape(B * n_groups, T // tq, HEAD_DIM, R * tq)


def _ungroup_t(o, B, n_groups, tq):
    BG, nq, dh, rows = o.shape
    R = rows // tq
    o = o.reshape(B, n_groups, nq, dh, R, tq).transpose(0, 2, 5, 1, 4, 3)
    return o.reshape(B, nq * tq, n_groups * R * dh)


def _head_major(x, Lp):
    B, L, G, dh = x.shape
    x = jnp.pad(x.transpose(0, 2, 1, 3), ((0, 0), (0, 0), (0, Lp - L), (0, 0)))
    return x.reshape(B * G, Lp, dh)


def _key_operand(x, Lp, block=None, nbp=0, first_pos=0):
    k = _head_major(x, Lp).astype(BF16)
    if block is None:
        return k
    onehot = ((first_pos + jnp.arange(Lp))[:, None] // block == jnp.arange(nbp)[None, :]).astype(BF16)
    return jnp.concatenate([k, jnp.broadcast_to(onehot[None], (k.shape[0], Lp, nbp))], axis=-1)


def _value_operand(x, Lp):
    B, L, G, dh = x.shape
    v = jnp.pad(x.transpose(0, 2, 3, 1), ((0, 0), (0, 0), (0, 0), (0, Lp - L))).reshape(B * G, dh, Lp)
    return jnp.concatenate([v, jnp.ones((B * G, V_ROWS - dh, Lp), F32)], axis=1).astype(BF16)


def _bias_rows(n):
    return _round_up(HEAD_DIM + n, LANES) - HEAD_DIM


def _sel_blocks(n_pos):
    n_sel = -(-n_pos // SEL_BLOCK)
    return n_sel, _bias_rows(n_sel)


def _nsa_branches(q, n_q, cmp_x, cmp_specs, n_seq, n_rows, ke_sel, vt_sel, win_kv, cw, tq, q_base, win_base, tk,
                  sel_batch):
    G = NSA_KV_HEADS
    w1, pe, w2 = cw
    n_cmp_rows = n_rows // CMP_STRIDE * CMP_STRIDE
    kc = _nsa_compress(cmp_x, cmp_specs[0], n_seq, n_cmp_rows, w1[0], pe[0], w2[0], False)
    vct = _nsa_compress(cmp_x, cmp_specs[1], n_seq, n_cmp_rows, w1[1], pe[1], w2[1], True)
    n_sel, nbp = _sel_blocks(max(n_rows, q_base + n_q))
    ovt = _overlap_matrix_t(n_rows // CMP_STRIDE - 1, n_sel, kc.shape[1], _round_up(n_sel, SUBLANES))
    oc, bias = _cmp_select(q, kc, vct, ovt, tq, q_base, n_sel, nbp)
    o_s = _flash("sel", q, ke_sel, vt_sel, bias, tq, q_base, 0, n_sel, G, tk, sel_batch)
    Lwp = _round_up(max(win_kv.shape[1], q_base + n_q - win_base), ATT_K_TILE)
    o_w = _flash("win", q, _key_operand(win_kv[:, :, 0], Lwp), _value_operand(win_kv[:, :, 1], Lwp), None, tq,
                 q_base, win_base, 0, G)
    return oc, o_s, o_w


def _nsa_prompt(q, kv6, cw, tq):
    B, T = q.shape[:2]
    width = NSA_KV_HEADS * HEAD_DIM
    assert T % CMP_STRIDE == 0
    cmp_specs = [pl.BlockSpec((T, width), lambda b, s=s: (b, s)) for s in (0, 1)]
    _, nbp = _sel_blocks(T)
    Lp = _round_up(T, ATT_K_TILE)
    return _nsa_branches(q, T, kv6.reshape(B * T, -1), cmp_specs, B, T, _key_operand(kv6[:, :, 2], Lp, SEL_BLOCK, nbp),
                         _value_operand(kv6[:, :, 3], Lp), kv6[:, :, 4:6], cw, tq, 0, 0, ATT_K_TILE, 1)


def _nsa_sample(q, kv6, cache, page_table, win_buf, cw, tq):
    Bs, Ts = q.shape[:2]
    page = cache.shape[-1]
    past_len = page_table.shape[1] * page
    step = PAGES_PER_STEP * page
    assert past_len % CMP_STRIDE == 0 and Ts < CMP_STRIDE and tq <= step
    _, nbp = _sel_blocks(past_len + tq)
    tail_ke = _key_operand(kv6[:, :, 2], step, SEL_BLOCK, nbp, past_len)
    tail_vt = _value_operand(kv6[:, :, 3], step)
    rows, ke, vt, _ = _gather_kv(cache, page_table, NSA_KV_HEADS, (0, 1), 2, 3, SEL_BLOCK, tail_ke, tail_vt, False)
    win = jnp.concatenate([win_buf, kv6[:, :, 4:6]], axis=1)
    qt = _group_queries_t(_pad_queries(q, tq), NSA_KV_HEADS, tq)
    cmp_specs = [pl.BlockSpec((None, None) + rows.shape[2:], lambda b, s=s: (s, b, 0, 0)) for s in (0, 1)]
    outs = _nsa_branches(qt, tq, rows, cmp_specs, Bs, past_len + Ts, ke, vt, win, cw, tq, past_len,
                         past_len - win_buf.shape[1], step, NSA_KV_HEADS)
    return [_ungroup_t(o, Bs, NSA_KV_HEADS, tq)[:, :Ts] for o in outs], win[:, Ts:]


def _moba_prompt(q, kv, tq):
    B, T, _ = q.shape
    n_blk = -(-T // MOBA_BLOCK)
    nbp = _bias_rows(n_blk)
    Lp = _round_up(T, ATT_K_TILE)
    means = _block_means(_head_major(kv[:, :, 0], Lp), n_blk, nbp)
    return _flash("moba", q, _key_operand(kv[:, :, 0], Lp, MOBA_BLOCK, nbp), _value_operand(kv[:, :, 1], Lp), means,
                  tq, 0, 0, n_blk, MOBA_KV_HEADS)


def _moba_sample(q, kv, cache, page_table, tq):
    Bs, Ts = q.shape[:2]
    page = cache.shape[-1]
    past_len = page_table.shape[1] * page
    step = PAGES_PER_STEP * page
    assert step % MOBA_BLOCK == 0 and tq <= MOBA_BLOCK
    n_blk = -(-(past_len + tq) // MOBA_BLOCK)
    nbp = _bias_rows(n_blk)
    tail_ke = _key_operand(kv[:, :, 0], step, MOBA_BLOCK, nbp, past_len)
    tail_vt = _value_operand(kv[:, :, 1], step)
    _, ke, vt, means = _gather_kv(cache, page_table, MOBA_KV_HEADS, (), 0, 1, MOBA_BLOCK, tail_ke, tail_vt, True)
    means = jnp.pad(means, ((0, 0), (0, max(nbp - means.shape[1], 0)), (0, 0)))[:, :nbp]
    qt = _group_queries_t(_pad_queries(q, tq), MOBA_KV_HEADS, tq)
    o = _flash("moba", qt, ke, vt, means, tq, past_len, 0, n_blk, MOBA_KV_HEADS, step, MOBA_KV_HEADS)
    return _ungroup_t(o, Bs, MOBA_KV_HEADS, tq)[:, :Ts]


def _gather_kv_kernel(n_groups, row_slots, k_slot, v_slot, block, with_means, pt_ref, *refs):
    del pt_ref
    G = n_groups
    dh = HEAD_DIM
    pps = PAGES_PER_STEP
    pages = refs[:pps]
    tail_ke_ref, tail_vt_ref = refs[pps:pps + 2]
    outs = list(refs[pps + 2:])
    rows_ref = outs.pop(0) if row_slots else None
    ke_ref, vt_ref = outs[0], outs[1]
    means_ref = outs[2] if with_means else None
    p = pl.program_id(1)
    last = pl.num_programs(1) - 1
    page = pages[0].shape[-1]
    step, kw = ke_ref.shape[1:]

    @pl.when(p < last)
    def _():
        lane = lax.broadcasted_iota(jnp.int32, (step, kw), 1)
        row = p * step + lax.broadcasted_iota(jnp.int32, (step, kw), 0)
        onehot = jnp.where(lane - dh == row // block, 1.0, 0.0).astype(BF16)
        for g in range(G):
            ke_ref[g] = onehot
        vt_ref[:, dh:, :] = jnp.ones((G, V_ROWS - dh, step), BF16)
        ppb = block // page
        tot = None
        for j, pg in enumerate(pages):
            r0 = j * page
            for si, s in enumerate(row_slots):
                rows_ref[si, r0:r0 + page, :] = pg[s].reshape(G * dh, page).T
            k = pg[k_slot].reshape(G * dh, page).T
            for g in range(G):
                ke_ref[g, r0:r0 + page, 0:dh] = k[:, g * dh:(g + 1) * dh].astype(BF16)
                vt_ref[g, 0:dh, r0:r0 + page] = pg[v_slot, g].astype(BF16)
            if with_means:
                ksum = jnp.sum(k, axis=0, keepdims=True)
                tot = ksum if j % ppb == 0 else tot + ksum
                if j % ppb == ppb - 1:
                    n = j // ppb
                    for g in range(G):
                        means_ref[g, n:n + 1, :] = tot[:, g * dh:(g + 1) * dh] / block

    @pl.when(p == last)
    def _():
        ke_ref[...] = tail_ke_ref[...]
        vt_ref[...] = tail_vt_ref[...]
        if with_means:
            means_ref[...] = jnp.zeros(means_ref.shape, F32)


def _gather_kv(cache, page_table, n_groups, row_slots, k_slot, v_slot, block, tail_ke, tail_vt, with_means):
    _, n_slots, G, dh, page = cache.shape
    Bs, n_pages = page_table.shape
    assert G == n_groups and dh == HEAD_DIM
    pps = PAGES_PER_STEP
    step = pps * page
    assert n_pages % pps == 0
    if with_means:
        assert step % block == 0 and block % page == 0
    n_steps = n_pages // pps
    kw = tail_ke.shape[-1]
    Lp = (n_steps + 1) * step
    bps = step // block

    def page_map(j):
        return lambda b, p, pt: (pt[b, jnp.minimum(p, n_steps - 1) * pps + j], 0, 0, 0, 0)

    in_specs = [pl.BlockSpec((None, n_slots, G, dh, page), page_map(j)) for j in range(pps)]
    in_specs += [pl.BlockSpec((None, G, step, kw), lambda b, p, pt: (b, 0, 0, 0)),
                 pl.BlockSpec((None, G, V_ROWS, step), lambda b, p, pt: (b, 0, 0, 0))]
    out_shape, out_specs = [], []
    if row_slots:
        n_rs = len(row_slots)
        out_shape.append(jax.ShapeDtypeStruct((n_rs, Bs, n_steps * step, G * dh), F32))
        out_specs.append(pl.BlockSpec((n_rs, None, step, G * dh),
                                      lambda b, p, pt: (0, b, jnp.minimum(p, n_steps - 1), 0)))
    out_shape += [jax.ShapeDtypeStruct((Bs, G, Lp, kw), BF16), jax.ShapeDtypeStruct((Bs, G, V_ROWS, Lp), BF16)]
    out_specs += [pl.BlockSpec((None, G, step, kw), lambda b, p, pt: (b, 0, p, 0)),
                  pl.BlockSpec((None, G, V_ROWS, step), lambda b, p, pt: (b, 0, 0, p))]
    if with_means:
        out_shape.append(jax.ShapeDtypeStruct((Bs, G, n_steps + 1, bps, dh), F32))
        out_specs.append(pl.BlockSpec((None, G, None, bps, dh), lambda b, p, pt: (b, 0, p, 0, 0)))
    grid_spec = pltpu.PrefetchScalarGridSpec(num_scalar_prefetch=1, grid=(Bs, n_steps + 1), in_specs=in_specs,
                                             out_specs=out_specs)
    outs = list(pl.pallas_call(
        functools.partial(_gather_kv_kernel, G, tuple(row_slots), k_slot, v_slot, block, with_means),
        grid_spec=grid_spec, out_shape=out_shape, compiler_params=_params("parallel", "arbitrary"),
        name="gather_kv",
    )(page_table, *([cache] * pps), tail_ke.reshape(Bs, G, step, kw), tail_vt.reshape(Bs, G, V_ROWS, step)))
    rows = outs.pop(0) if row_slots else None
    ke = outs[0].reshape(Bs * G, Lp, kw)
    vt = outs[1].reshape(Bs * G, V_ROWS, Lp)
    means = outs[2].reshape(Bs * G, (n_steps + 1) * bps, dh) if with_means else None
    return rows, ke, vt, means


def _pad_queries(x, tq):
    return jnp.pad(x, ((0, 0), (0, tq - x.shape[1]), (0, 0)))


def kernel(x_prompt, x_sample, cache_nsa_kv, cache_moba_kv, state_win_kv, state_wkv, state_shift, page_table, norm_mix, norm_ffn, norm_final, even_w_in, even_w_out, rwkv_mu, rwkv_w0, rwkv_w_up, rwkv_a0, rwkv_a_up, rwkv_g_up, rwkv_k_k, rwkv_k_a, rwkv_r_k, rwkv_ln_g, rwkv_ln_b, nsa_gate_b, nsa_cmp_w1, nsa_cmp_pe, nsa_cmp_w2, odd_w_in, odd_w_out, ffn_w_gate, ffn_w_up, ffn_w_down):
    B, T, D = x_prompt.shape
    Bs, Ts, _ = x_sample.shape
    depth = norm_mix.shape[0]
    page = cache_nsa_kv.shape[2]
    past_len = page_table.shape[1] * page
    rwkv_dim = rwkv_w0.shape[1]
    rwkv_cols = rwkv_mu.shape[1]
    nsa_heads = nsa_gate_b.shape[1] // 3
    nsa_dim = nsa_heads * HEAD_DIM
    nsa_kv_cols = 6 * NSA_KV_HEADS * HEAD_DIM
    moba_kv_cols = 2 * MOBA_KV_HEADS * HEAD_DIM
    moba_dim = odd_w_in.shape[2] - moba_kv_cols
    tq_p = min(ATT_Q_TILE, T)
    tq_s = SAMPLE_Q_PAD

    cos_p, sin_p = _rope_tables(jnp.arange(T, dtype=jnp.int32))
    pos_s = past_len + jnp.arange(Ts, dtype=jnp.int32)
    cos_s, sin_s = _rope_tables(jnp.tile(pos_s, Bs))

    even_spec = ((rwkv_cols, (), False), (nsa_dim, tuple(range(nsa_dim // LANES)), False),
                 (nsa_kv_cols, tuple(range(0, nsa_kv_cols // LANES, 2)), False), (LANES, (), True))
    k_chunks = MOBA_KV_HEADS * HEAD_DIM // LANES
    odd_spec = ((moba_dim, tuple(range(moba_dim // LANES)), False), (moba_kv_cols, tuple(range(k_chunks)), False))

    hp = x_prompt.reshape(B * T, D)
    hs = x_sample.reshape(Bs * Ts, D)
    nsa_p, nsa_s, moba_p, moba_s = [], [], [], []
    win_p, win_s, wkv_p, wkv_s, sh_p, sh_s = [], [], [], [], [], []
    for layer in range(depth):
        i = layer // 2
        if layer % 2 == 0:
            w_in = even_w_in[i].astype(BF16)
            o = rwkv_cols
            n_gate = 3 * nsa_heads
            weights = [w_in[:, :o], w_in[:, o:o + nsa_dim], w_in[:, o + nsa_dim:o + nsa_dim + nsa_kv_cols],
                       jnp.pad(w_in[:, o + nsa_dim + nsa_kv_cols:], ((0, 0), (0, LANES - n_gate)))]
            gate_b = jnp.pad(nsa_gate_b[i], (0, LANES - n_gate)).reshape(1, LANES)
            rp = (rwkv_mu[i], rwkv_w0[i], rwkv_w_up[i], rwkv_a0[i], rwkv_a_up[i], rwkv_g_up[i],
                  rwkv_k_k[i], rwkv_k_a[i], rwkv_r_k[i], rwkv_ln_g[i], rwkv_ln_b[i])
            cw = (nsa_cmp_w1[i], nsa_cmp_pe[i], nsa_cmp_w2[i])

            rw, q, kv, gates = _project(hp, norm_mix[layer], cos_p, sin_p, weights, [gate_b], even_spec, "even_proj")
            rw3 = rw.reshape(B, T, rwkv_cols)
            kv6 = kv.reshape(B, T, 6, NSA_KV_HEADS, HEAD_DIM)
            y, g, wkv_new = _rwkv_mix(rw3, jnp.zeros((B, rwkv_cols), F32),
                                      jnp.zeros((B, rwkv_dim // HEAD_DIM, HEAD_DIM, HEAD_DIM), F32), rp, SCAN_CHUNK)
            o_c, o_s, o_w = _nsa_prompt(q.reshape(B, T, nsa_dim), kv6, cw, tq_p)
            hp = _even_out(hp, y, g, o_c.reshape(B * T, nsa_dim), o_s.reshape(B * T, nsa_dim),
                           o_w.reshape(B * T, nsa_dim), gates, even_w_out[i])
            nsa_p.append(kv6[:, :, :4])
            win_p.append(kv6[:, T - min(WINDOW, T):, 4:6])
            wkv_p.append(wkv_new)
            sh_p.append(rw3[:, -1])

            rw, q, kv, gates = _project(hs, norm_mix[layer], cos_s, sin_s, weights, [gate_b], even_spec, "even_proj_s")
            rw3 = rw.reshape(Bs, Ts, rwkv_cols)
            kv6 = kv.reshape(Bs, Ts, 6, NSA_KV_HEADS, HEAD_DIM)
            y, g, wkv_new = _rwkv_mix(rw3, state_shift[i], state_wkv[i], rp, SUBLANES)
            outs, win_new = _nsa_sample(q.reshape(Bs, Ts, nsa_dim), kv6,
                                        cache_nsa_kv[i].transpose(0, 2, 3, 4, 1), page_table,
                                        state_win_kv[i], cw, tq_s)
            o_c, o_s, o_w = [t.reshape(Bs * Ts, nsa_dim) for t in outs]
            hs = _even_out(hs, y, g, o_c, o_s, o_w, gates, even_w_out[i])
            nsa_s.append(kv6[:, :, :4])
            win_s.append(win_new)
            wkv_s.append(wkv_new)
            sh_s.append(rw3[:, -1])
        else:
            w_in = odd_w_in[i].astype(BF16)
            weights = [w_in[:, :moba_dim], w_in[:, moba_dim:]]
            q, kv = _project(hp, norm_mix[layer], cos_p, sin_p, weights, [], odd_spec, "odd_proj")
            kv2 = kv.reshape(B, T, 2, MOBA_KV_HEADS, HEAD_DIM)
            a = _moba_prompt(q.reshape(B, T, moba_dim), kv2, tq_p)
            hp = _odd_out(hp, a.reshape(B * T, moba_dim), odd_w_out[i])
            moba_p.append(kv2)

            q, kv = _project(hs, norm_mix[layer], cos_s, sin_s, weights, [], odd_spec, "odd_proj_s")
            kv2 = kv.reshape(Bs, Ts, 2, MOBA_KV_HEADS, HEAD_DIM)
            a = _moba_sample(q.reshape(Bs, Ts, moba_dim), kv2, cache_moba_kv[i].transpose(0, 2, 3, 4, 1),
                             page_table, tq_s)
            hs = _odd_out(hs, a.reshape(Bs * Ts, moba_dim), odd_w_out[i])
            moba_s.append(kv2)
        g_final = norm_final if layer == depth - 1 else None
        hp = _ffn(hp, norm_ffn[layer], ffn_w_gate[layer], ffn_w_up[layer], ffn_w_down[layer], g_final)
        hs = _ffn(hs, norm_ffn[layer], ffn_w_gate[layer], ffn_w_up[layer], ffn_w_down[layer], g_final)
    return (hp.reshape(B, T, D), hs.reshape(Bs, Ts, D), jnp.stack(nsa_p), jnp.stack(nsa_s), jnp.stack(moba_p),
            jnp.stack(moba_s), jnp.stack(win_p), jnp.stack(win_s), jnp.stack(wkv_p), jnp.stack(wkv_s),
            jnp.stack(sh_p), jnp.stack(sh_s))
```

```python
import functools
import math

import jax
import jax.numpy as jnp
from jax import lax
from jax.experimental import pallas as pl
from jax.experimental.pallas import tpu as pltpu

F32 = jnp.float32
BF16 = jnp.bfloat16
HIGHEST = lax.Precision.HIGHEST

HEAD_DIM = 64
NORM_EPS = 1e-6
ROPE_THETA = 10000.0
DECAY_LORA = 64
AAA_LORA = 64
GATE_LORA = 128
RWKV_GN_EPS = 64e-5
NSA_KV_HEADS = 2
CMP_STRIDE = 16
CMP_LEN = 2 * CMP_STRIDE
SEL_BLOCK = 64
SEL_TOPN = 16
WINDOW = 512
FORCE_BONUS = 100.0
MOBA_KV_HEADS = 4
MOBA_BLOCK = 256
MOBA_TOPK = 3

LANES = 128
SUBLANES = 8
VMEM_LIMIT = 56 * 1024 * 1024

ROW_TILE = 512
FFN_ROW_TILE = 512
FFN_COL_TILE = 1408
ATT_Q_TILE = 512
ATT_K_TILE = 512
SCAN_CHUNK = 64
SCAN_CHUNKS_PER_STEP = 4
SAMPLE_Q_PAD = 32
PAGES_PER_STEP = 8

NT = (((1,), (1,)), ((), ()))
TN = (((0,), (0,)), ((), ()))
NEG_INF = float("-inf")
NEG_BIG = -1e30
V_ROWS = HEAD_DIM + 16


def _round_up(x, m):
    return -(-x // m) * m


def _mm(a, b, dims=None, exact=False):
    if dims is None:
        dims = (((a.ndim - 1,), (0,)), ((), ()))
    if exact:
        return lax.dot_general(a.astype(F32), b.astype(F32), dims, precision=HIGHEST,
                               preferred_element_type=F32)
    return lax.dot_general(a.astype(BF16), b.astype(BF16), dims, preferred_element_type=F32)


def _split3(x):
    hi = x.astype(BF16)
    rest = x - hi.astype(F32)
    mid = rest.astype(BF16)
    return hi, mid, (rest - mid.astype(F32)).astype(BF16)


def _mm_onehot(a, b, dims=None):
    return sum(_mm(t, b, dims) for t in _split3(a))


def _mm_3pass(a, b, dims=None):
    a_hi, a_lo, _ = _split3(a)
    b_hi, b_lo, _ = _split3(b)
    return _mm(a_hi, b_hi, dims) + _mm(a_lo, b_hi, dims) + _mm(a_hi, b_lo, dims)


def _params(*sem):
    return pltpu.CompilerParams(dimension_semantics=sem, vmem_limit_bytes=VMEM_LIMIT)


def _rms_norm(x, g):
    return x * lax.rsqrt(jnp.mean(x * x, axis=-1, keepdims=True) + NORM_EPS) * g


def _rope_chunk(x, cos, sin):
    lane = lax.broadcasted_iota(jnp.int32, x.shape, 1)
    half = HEAD_DIM // 2
    partner = jnp.where((lane % HEAD_DIM) < half, pltpu.roll(x, LANES - half, 1), pltpu.roll(x, half, 1))
    return x * cos + partner * sin


def _rank_rows(score, blk, ncand):
    rank = jnp.zeros(score.shape, F32)
    for j in range(ncand):
        row = score[..., j:j + 1, :]
        rank = rank + jnp.where(row > score, 1.0, 0.0) + jnp.where(row == score, jnp.where(blk > j, 1.0, 0.0), 0.0)
    return rank


def _proj_kernel(spec, x_ref, g_ref, cos_ref, sin_ref, *refs):
    nseg = len(spec)
    w_refs = refs[:nseg]
    nbias = sum(1 for s in spec if s[2])
    b_refs = list(refs[nseg:nseg + nbias])
    o_refs = refs[nseg + nbias:]
    xn = _rms_norm(x_ref[...], g_ref[...]).astype(BF16)
    for (ncols, rope_chunks, sig), w_ref, o_ref in zip(spec, w_refs, o_refs):
        y = jnp.dot(xn, w_ref[...], preferred_element_type=F32)
        if sig:
            y = jax.nn.sigmoid(y + b_refs.pop(0)[...])
        if rope_chunks:
            cos = cos_ref[...]
            sin = sin_ref[...]
            for c in range(ncols // LANES):
                yc = y[:, c * LANES:(c + 1) * LANES]
                if c in rope_chunks:
                    yc = _rope_chunk(yc, cos, sin)
                o_ref[:, c * LANES:(c + 1) * LANES] = yc
        else:
            o_ref[...] = y


def _project(x, g, cos_tab, sin_tab, weights, biases, spec, name):
    M, D = x.shape
    tm = min(ROW_TILE, M)
    ntab = cos_tab.shape[0] // tm
    in_specs = [pl.BlockSpec((tm, D), lambda i: (i, 0)),
                pl.BlockSpec((1, D), lambda i: (0, 0)),
                pl.BlockSpec((tm, LANES), lambda i: (i % ntab, 0)),
                pl.BlockSpec((tm, LANES), lambda i: (i % ntab, 0))]
    in_specs += [pl.BlockSpec(w.shape, lambda i: (0, 0)) for w in weights]
    in_specs += [pl.BlockSpec(b.shape, lambda i: (0, 0)) for b in biases]
    out_shape = [jax.ShapeDtypeStruct((M, s[0]), F32) for s in spec]
    out_specs = [pl.BlockSpec((tm, s[0]), lambda i: (i, 0)) for s in spec]
    return pl.pallas_call(
        functools.partial(_proj_kernel, spec), grid=(M // tm,), in_specs=in_specs, out_specs=out_specs,
        out_shape=out_shape, compiler_params=_params("parallel"), name=name,
    )(x, g.reshape(1, D), cos_tab, sin_tab, *weights, *biases)


def _rope_tables(pos):
    half = HEAD_DIM // 2
    inv = ROPE_THETA ** (-jnp.arange(half, dtype=F32) / half)
    ang = pos.astype(F32)[:, None] * inv[None, :]
    cos = jnp.cos(ang)
    sin = jnp.sin(ang)
    cos_t = jnp.tile(cos, (1, LANES // half))
    sin_t = jnp.tile(jnp.concatenate([-sin, sin], axis=1), (1, LANES // HEAD_DIM))
    return cos_t, sin_t


def _even_out_kernel(res_ref, y_ref, g_ref, oc_ref, os_ref, ow_ref, gt_ref, ec_ref, es_ref, ew_ref,
                     w1_ref, w2_ref, o_ref):
    a = y_ref[...] * g_ref[...]
    gt = gt_ref[...]
    b = (_mm_onehot(gt, ec_ref[...]) * oc_ref[...] + _mm_onehot(gt, es_ref[...]) * os_ref[...]
         + _mm_onehot(gt, ew_ref[...]) * ow_ref[...])
    o_ref[...] = res_ref[...] + _mm(a, w1_ref[...]) + _mm(b, w2_ref[...])


def _even_out(res, y, g, o_c, o_s, o_w, gates, w_out):
    M, D = res.shape
    n_rw = y.shape[1]
    n_nsa = o_c.shape[1]
    tm = min(ROW_TILE, M)
    heads = n_nsa // HEAD_DIM
    col = jnp.arange(n_nsa)[None, :] // HEAD_DIM
    row = jnp.arange(LANES)[:, None]
    expand = [(row == col * 3 + br).astype(F32) for br in range(3)]
    del heads
    row_spec = lambda n: pl.BlockSpec((tm, n), lambda i: (i, 0))
    full = lambda a: pl.BlockSpec(a.shape, lambda i: (0, 0))
    w1 = w_out[:n_rw].astype(BF16)
    w2 = w_out[n_rw:].astype(BF16)
    return pl.pallas_call(
        _even_out_kernel, grid=(M // tm,),
        in_specs=[row_spec(D), row_spec(n_rw), row_spec(n_rw), row_spec(n_nsa), row_spec(n_nsa), row_spec(n_nsa),
                  row_spec(LANES), full(expand[0]), full(expand[1]), full(expand[2]), full(w1), full(w2)],
        out_specs=row_spec(D), out_shape=jax.ShapeDtypeStruct((M, D), F32),
        compiler_params=_params("parallel"), name="even_out",
    )(res, y, g, o_c, o_s, o_w, gates, *expand, w1, w2)


def _odd_out_kernel(res_ref, a_ref, w_ref, o_ref):
    o_ref[...] = res_ref[...] + _mm(a_ref[...], w_ref[...])


def _odd_out(res, a, w_out):
    M, D = res.shape
    tm = min(ROW_TILE, M)
    w = w_out.astype(BF16)
    return pl.pallas_call(
        _odd_out_kernel, grid=(M // tm,),
        in_specs=[pl.BlockSpec((tm, D), lambda i: (i, 0)), pl.BlockSpec((tm, a.shape[1]), lambda i: (i, 0)),
                  pl.BlockSpec(w.shape, lambda i: (0, 0))],
        out_specs=pl.BlockSpec((tm, D), lambda i: (i, 0)), out_shape=jax.ShapeDtypeStruct((M, D), F32),
        compiler_params=_params("parallel"), name="odd_out",
    )(res, a, w)


def _ffn_kernel(final_norm, x_ref, g_ref, wg_ref, wu_ref, wd_ref, gf_ref, o_ref, xn_scr, acc_scr):
    j = pl.program_id(1)

    @pl.when(j == 0)
    def _():
        xn_scr[...] = _rms_norm(x_ref[...], g_ref[...]).astype(BF16)
        acc_scr[...] = jnp.zeros(acc_scr.shape, F32)

    xn = xn_scr[...]
    h = jax.nn.silu(jnp.dot(xn, wg_ref[...], preferred_element_type=F32)) * jnp.dot(
        xn, wu_ref[...], preferred_element_type=F32)
    acc_scr[...] += _mm(h, wd_ref[...])

    @pl.when(j == pl.num_programs(1) - 1)
    def _():
        y = x_ref[...] + acc_scr[...]
        if final_norm:
            y = _rms_norm(y, gf_ref[...])
        o_ref[...] = y


def _ffn(x, g, wg, wu, wd, g_final=None):
    M, D = x.shape
    F = wg.shape[1]
    tm = min(FFN_ROW_TILE, M)
    tf = FFN_COL_TILE
    assert M % tm == 0 and F % tf == 0
    final_norm = g_final is not None
    gf = (g_final if final_norm else g).reshape(1, D)
    return pl.pallas_call(
        functools.partial(_ffn_kernel, final_norm), grid=(M // tm, F // tf),
        in_specs=[pl.BlockSpec((tm, D), lambda i, j: (i, 0)), pl.BlockSpec((1, D), lambda i, j: (0, 0)),
                  pl.BlockSpec((D, tf), lambda i, j: (0, j)), pl.BlockSpec((D, tf), lambda i, j: (0, j)),
                  pl.BlockSpec((tf, D), lambda i, j: (j, 0)), pl.BlockSpec((1, D), lambda i, j: (0, 0))],
        out_specs=pl.BlockSpec((tm, D), lambda i, j: (i, 0)), out_shape=jax.ShapeDtypeStruct((M, D), F32),
        scratch_shapes=[pltpu.VMEM((tm, D), BF16), pltpu.VMEM((tm, D), F32)],
        compiler_params=_params("parallel", "arbitrary"), name="ffn",
    )(x, g.reshape(1, D), wg.astype(BF16), wu.astype(BF16), wd.astype(BF16), gf)


def _rwkv_pre_kernel(n_dim, rw_ref, prev_ref, mu_ref, w0_ref, a0_ref, kk_ref, ka_ref, wup_ref, aup_ref, gup_ref,
                     hsum_ref, r_out, lw_out, k_out, v_out, kk_out, b_out, g_out):
    rw = rw_ref[...]
    if prev_ref.shape[0] == rw.shape[0]:
        prev = prev_ref[...]
    else:
        first = lax.broadcasted_iota(jnp.int32, (rw.shape[0], 1), 0) == 0
        prev = jnp.where(first, prev_ref[...], pltpu.roll(rw, 1, 0))
    xm = rw + (prev - rw) * mu_ref[...]
    r = xm[:, :n_dim]
    k = xm[:, n_dim:2 * n_dim]
    v = xm[:, 2 * n_dim:3 * n_dim]
    lora = xm[:, 3 * n_dim:3 * n_dim + DECAY_LORA + AAA_LORA]
    xg = xm[:, 3 * n_dim + DECAY_LORA + AAA_LORA:]
    lw = -math.exp(-0.5) * jax.nn.sigmoid(w0_ref[...] + _mm_3pass(jnp.tanh(lora), wup_ref[...]))
    a = jax.nn.sigmoid(a0_ref[...] + _mm_3pass(lora, aup_ref[...]))
    g = _mm(jax.nn.sigmoid(xg), gup_ref[...])
    kk = k * kk_ref[...]
    sq = kk * kk
    sq_hi = sq.astype(BF16)
    norm = jnp.sqrt(_mm(sq_hi, hsum_ref[...]) + _mm(sq - sq_hi.astype(F32), hsum_ref[...]))
    kk = kk / jnp.maximum(norm, 1e-12)
    g_out[...] = g
    outs = ((r_out, r), (lw_out, lw), (k_out, k * (1.0 + (a - 1.0) * ka_ref[...])), (v_out, v), (kk_out, kk),
            (b_out, kk * a))
    for o_ref, val in outs:
        if len(o_ref.shape) == 2:
            o_ref[...] = val
        else:
            for h in range(o_ref.shape[0]):
                o_ref[h] = val[:, h * HEAD_DIM:(h + 1) * HEAD_DIM]


def _rwkv_pre(rw3, shift_prev, mu, w0, w_up, a0, a_up, g_up, k_k, k_a):
    B, seq_len, ncols = rw3.shape
    M = B * seq_len
    rw = rw3.reshape(M, ncols)
    n_dim = w0.shape[0]
    tm = min(ROW_TILE, M)
    H = n_dim // HEAD_DIM
    head_major = seq_len % tm == 0
    nt = seq_len // tm if head_major else 1
    if head_major:
        prev = jnp.concatenate([shift_prev[:, None], rw3[:, tm - 1:seq_len - 1:tm]], axis=1).reshape(B * nt, 1, ncols)
        prev_spec = pl.BlockSpec((None, 1, ncols), lambda i: (i, 0, 0))
    else:
        prev = jnp.concatenate([shift_prev[:, None], rw3[:, :-1]], axis=1).reshape(M, ncols)
        prev_spec = pl.BlockSpec((tm, ncols), lambda i: (i, 0))
    zeros = jnp.zeros((AAA_LORA, n_dim), F32)
    wup_pad = jnp.concatenate([w_up, zeros], axis=0)
    aup_pad = jnp.concatenate([jnp.zeros((DECAY_LORA, n_dim), F32), a_up], axis=0)
    head = jnp.arange(n_dim) // HEAD_DIM
    hsum = (head[:, None] == head[None, :]).astype(F32)
    vec = lambda a: a.reshape(1, -1)
    row = lambda n: pl.BlockSpec((tm, n), lambda i: (i, 0))
    full = lambda a: pl.BlockSpec(a.shape, lambda i: (0, 0))
    ins = [rw, prev, vec(mu), vec(w0), vec(a0), vec(k_k), vec(k_a), wup_pad, aup_pad, g_up, hsum]
    if head_major:
        seq_spec = pl.BlockSpec((None, H, tm, HEAD_DIM), lambda i: (i // nt, 0, i % nt, 0))
        seq_shape = jax.ShapeDtypeStruct((M // seq_len, H, seq_len, HEAD_DIM), F32)
    else:
        seq_spec, seq_shape = row(n_dim), jax.ShapeDtypeStruct((M, n_dim), F32)
    outs = pl.pallas_call(
        functools.partial(_rwkv_pre_kernel, n_dim), grid=(M // tm,),
        in_specs=[row(ncols), prev_spec] + [full(a) for a in ins[2:]],
        out_specs=[seq_spec] * 6 + [row(n_dim)],
        out_shape=[seq_shape] * 6 + [jax.ShapeDtypeStruct((M, n_dim), F32)],
        compiler_params=_params("parallel"), name="rwkv_pre",
    )(*ins)
    if head_major:
        outs = [t.reshape(-1, seq_len, HEAD_DIM) for t in outs[:6]] + [outs[6]]
    return outs, head_major


def _rwkv_chunk_kernel(n_heads, chunk, r_ref, lw_ref, k_ref, v_ref, kk_ref, b_ref, rk_ref,
                       rw_ref, y0_ref, bonus_ref, a_ref, s1_ref):
    C = chunk
    dh = HEAD_DIM
    ti = lax.broadcasted_iota(jnp.int32, (C, C), 0)
    si = lax.broadcasted_iota(jnp.int32, (C, C), 1)
    incl = jnp.where(si <= ti, 1.0, 0.0)
    eye = jnp.where(lax.broadcasted_iota(jnp.int32, (dh, dh), 0) == lax.broadcasted_iota(jnp.int32, (dh, dh), 1),
                    1.0, 0.0)
    n_double = max(1, math.ceil(math.log2(C)))
    heads = range(n_heads)
    r = [r_ref[h] for h in heads]
    lw = [lw_ref[h] for h in heads]
    k = [k_ref[h] for h in heads]
    v = [v_ref[h] for h in heads]
    b = [b_ref[h] for h in heads]
    cum = [sum(_mm(incl, t) for t in _split3(lw[h])) for h in heads]
    cum_end = [cum[h][C - 1:C, :] for h in heads]
    kkw = [kk_ref[h] * jnp.exp(cum[h] - lw[h]) for h in heads]
    rwc = [r[h] * jnp.exp(cum[h]) for h in heads]
    w_inv = [jnp.exp(-cum[h]) for h in heads]
    kd = [k[h] * w_inv[h] for h in heads]
    bd = [b[h] * w_inv[h] for h in heads]
    w_end = [jnp.exp(cum_end[h] - cum[h]) for h in heads]
    both = [jnp.concatenate([kkw[h], rwc[h]], axis=0) for h in heads]
    g_b = [_mm(both[h], bd[h], NT) for h in heads]
    g_k = [_mm(both[h], kd[h], NT) for h in heads]
    a_ub = [jnp.where(si < ti, g_b[h][:C], 0.0) for h in heads]
    b_rb = [jnp.where(si <= ti, g_b[h][C:], 0.0) for h in heads]
    a_vk = [jnp.where(si < ti, g_k[h][:C], 0.0) for h in heads]
    b_rk = [jnp.where(si <= ti, g_k[h][C:], 0.0) for h in heads]
    gv = [_mm(jnp.concatenate([a_vk[h], b_rk[h]], axis=0), v[h]) for h in heads]
    x = [jnp.concatenate([kkw[h], gv[h][:C]], axis=1) for h in heads]
    p = [-a_ub[h] for h in heads]
    for _ in range(n_double):
        px = [_mm(p[h], jnp.concatenate([x[h], p[h]], axis=1)) for h in heads]
        x = [x[h] + px[h][:, :2 * dh] for h in heads]
        p = [px[h][:, 2 * dh:] for h in heads]
    for h in heads:
        bx = _mm(b_rb[h], x[h])
        rw_ref[h] = rwc[h] - bx[:, :dh]
        y0_ref[h] = gv[h][C:] - bx[:, dh:]
        xb = _mm(x[h], b[h] * w_end[h], TN)
        a_ref[h] = eye * jnp.exp(cum_end[h]) - xb[:dh]
        s1_ref[h] = _mm(v[h], k[h] * w_end[h], TN) - xb[dh:]
        bonus_ref[h] = jnp.sum(r[h] * k[h] * rk_ref[h:h + 1, :], axis=-1, keepdims=True) * v[h]


def _rwkv_state_kernel(n_heads, n_sub, rw_ref, y0_ref, bonus_ref, a_ref, s1_ref, s0_ref, lng_ref, lnb_ref,
                       y_ref, s_out_ref, s_scr):
    c = pl.program_id(1)

    @pl.when(c == 0)
    def _():
        s_scr[...] = s0_ref[...]

    chunk = rw_ref.shape[2]
    heads = range(n_heads)
    s = [s_scr[h] for h in heads]
    for j in range(n_sub):
        y = [_mm(rw_ref[h, j], s[h], NT) + y0_ref[h, j] for h in heads]
        s = [_mm(s[h], a_ref[h, j]) + s1_ref[h, j] for h in heads]
        for h in heads:
            mean = jnp.mean(y[h], axis=-1, keepdims=True)
            var = jnp.mean(jnp.square(y[h] - mean), axis=-1, keepdims=True)
            yn = (y[h] - mean) * lax.rsqrt(var + RWKV_GN_EPS) * lng_ref[h:h + 1, :] + lnb_ref[h:h + 1, :]
            y_ref[j * chunk:(j + 1) * chunk, h * HEAD_DIM:(h + 1) * HEAD_DIM] = yn + bonus_ref[h, j]
    for h in heads:
        s_scr[h] = s[h]

    @pl.when(c == pl.num_programs(1) - 1)
    def _():
        s_out_ref[...] = s_scr[...]


def _rwkv_scan(r, lw, k, v, kk, b, s0, r_k, ln_g, ln_b, chunk):
    n_bh, T, dh = r.shape
    H = r_k.shape[0]
    n_chunks = T // chunk
    seq = pl.BlockSpec((H, chunk, dh), lambda i, c: (i, c, 0))
    mat = pl.BlockSpec((H, None, dh, dh), lambda i, c: (i, c, 0, 0))
    par = pl.BlockSpec((H, dh), lambda i, c: (0, 0))
    seq_shape = jax.ShapeDtypeStruct((n_bh, T, dh), F32)
    mat_shape = jax.ShapeDtypeStruct((n_bh, n_chunks, dh, dh), F32)
    rw, y0, bonus, a, s1 = pl.pallas_call(
        functools.partial(_rwkv_chunk_kernel, H, chunk), grid=(n_bh // H, n_chunks),
        in_specs=[seq] * 6 + [par], out_specs=[seq, seq, seq, mat, mat],
        out_shape=[seq_shape, seq_shape, seq_shape, mat_shape, mat_shape],
        compiler_params=_params("parallel", "parallel"), name="rwkv_chunk",
    )(r, lw, k, v, kk, b, r_k)
    n_sub = math.gcd(SCAN_CHUNKS_PER_STEP, n_chunks)
    split = lambda t: t.reshape(n_bh, n_chunks, chunk, dh)
    seq4 = pl.BlockSpec((H, n_sub, chunk, dh), lambda i, c: (i, c, 0, 0))
    mat4 = pl.BlockSpec((H, n_sub, dh, dh), lambda i, c: (i, c, 0, 0))
    state = pl.BlockSpec((H, dh, dh), lambda i, c: (i, 0, 0))
    y, s_new = pl.pallas_call(
        functools.partial(_rwkv_state_kernel, H, n_sub), grid=(n_bh // H, n_chunks // n_sub),
        in_specs=[seq4, seq4, seq4, mat4, mat4, state, par, par],
        out_specs=[pl.BlockSpec((n_sub * chunk, H * dh), lambda i, c: (i * (n_chunks // n_sub) + c, 0)), state],
        out_shape=[jax.ShapeDtypeStruct((n_bh // H * T, H * dh), F32), jax.ShapeDtypeStruct((n_bh, dh, dh), F32)],
        scratch_shapes=[pltpu.VMEM((H, dh, dh), F32)],
        compiler_params=_params("parallel", "arbitrary"), name="rwkv_state",
    )(split(rw), split(y0), split(bonus), a, s1, s0, ln_g.reshape(H, dh), ln_b.reshape(H, dh))
    return y, s_new


def _rwkv_mix(rw, shift_prev, wkv0, p, chunk):
    mu, w0, w_up, a0, a_up, g_up, k_k, k_a, r_k, ln_g, ln_b = p
    B, T, ncols = rw.shape
    H = r_k.shape[0]
    n_dim = H * HEAD_DIM
    outs, head_major = _rwkv_pre(rw, shift_prev, mu, w0, w_up, a0, a_up, g_up, k_k, k_a)
    g = outs[6]
    Tp = _round_up(T, chunk)

    def heads(t):
        if not head_major:
            t = t.reshape(B, T, H, HEAD_DIM).transpose(0, 2, 1, 3).reshape(B * H, T, HEAD_DIM)
        return jnp.pad(t, ((0, 0), (0, Tp - T), (0, 0)))

    y, s_new = _rwkv_scan(*[heads(t) for t in outs[:6]], wkv0.reshape(B * H, HEAD_DIM, HEAD_DIM), r_k, ln_g, ln_b,
                          chunk)
    y = y.reshape(B, Tp, n_dim)[:, :T].reshape(B * T, n_dim)
    return y, g, s_new.reshape(B, H, HEAD_DIM, HEAD_DIM)


def _compress_kernel(transpose_out, x_ref, w1_ref, pe_ref, w2_ref, o_ref):
    n_groups = o_ref.shape[0]
    width = x_ref.shape[1]
    nch = x_ref.shape[0] // CMP_STRIDE
    both = jnp.zeros((nch, 2 * width), F32)
    bias = jnp.zeros((1, width), F32)
    for c in range(CMP_STRIDE):
        w = w1_ref[c]
        both = both + _mm(x_ref[pl.ds(c, nch, stride=CMP_STRIDE), :], w)
        pw = _mm(pe_ref[c], w, exact=True)
        bias = bias + pw[0:1, :width] + pw[1:2, width:]
    h = jax.nn.gelu(both[:, :width] + pltpu.roll(both[:, width:], nch - 1, 0) + bias)
    if transpose_out:
        res = _mm(w2_ref[...], h, NT)
        for g in range(n_groups):
            o_ref[g] = res[g * HEAD_DIM:(g + 1) * HEAD_DIM]
    else:
        res = _mm(h, w2_ref[...])
        for g in range(n_groups):
            o_ref[g] = res[:, g * HEAD_DIM:(g + 1) * HEAD_DIM]


def _nsa_compress(x, x_spec, n_seq, n_rows, w1, pe, w2, transpose_out):
    G = NSA_KV_HEADS
    dh = HEAD_DIM
    nch = n_rows // CMP_STRIDE
    eye = jnp.eye(G, dtype=F32)


    def blockdiag(m):
        out = eye[:, None, :, None] * m[..., None, :, None, :]
        return out.reshape(m.shape[:-2] + (G * dh, G * dh))

    w1p = jnp.concatenate([blockdiag(w1[:CMP_STRIDE]), blockdiag(w1[CMP_STRIDE:])], axis=-1).astype(BF16)
    pe2 = jnp.stack([jnp.tile(pe[:CMP_STRIDE], (1, G)), jnp.tile(pe[CMP_STRIDE:], (1, G))], axis=1)
    pe2 = jnp.pad(pe2, ((0, 0), (0, SUBLANES - 2), (0, 0)))
    out_block = (None, G, dh, nch) if transpose_out else (None, G, nch, dh)
    out = pl.pallas_call(
        functools.partial(_compress_kernel, transpose_out), grid=(n_seq,),
        in_specs=[x_spec, pl.BlockSpec(w1p.shape, lambda i: (0, 0, 0)), pl.BlockSpec(pe2.shape, lambda i: (0, 0, 0)),
                  pl.BlockSpec((G * dh, G * dh), lambda i: (0, 0))],
        out_specs=pl.BlockSpec(out_block, lambda i: (i, 0, 0, 0)),
        out_shape=jax.ShapeDtypeStruct((n_seq,) + out_block[1:], F32),
        compiler_params=_params("parallel"), name="nsa_compress",
    )(x, w1p, pe2, blockdiag(w2.T if transpose_out else w2).astype(BF16))
    return out.reshape((n_seq * G,) + out_block[2:])


def _load_queries(q_ref, n_rep, tok_major):
    if not tok_major:
        return q_ref[...]
    qt = q_ref[...].T
    return jnp.concatenate([qt[r * HEAD_DIM:(r + 1) * HEAD_DIM] for r in range(n_rep)], axis=1)


def _store_outputs(o_ref, o, n_rep, tok_major):
    if not tok_major:
        o_ref[...] = o
        return
    tq = o.shape[1] // n_rep
    o_ref[...] = jnp.concatenate([o[:, r * tq:(r + 1) * tq] for r in range(n_rep)], axis=0).T


def _query_layout(q, tq, n_groups, group_batch=None):
    if q.ndim == 4:
        BG, nq, dh, rows = q.shape
        return q, False, BG, nq, rows // tq, pl.BlockSpec((group_batch, None, dh, rows), lambda i, j: (i, j, 0, 0))
    assert group_batch in (None, 1)
    B, T, n = q.shape
    G = n_groups
    n_rep = n // HEAD_DIM // G
    nq = T // tq
    spec = pl.BlockSpec((tq, n_rep * HEAD_DIM), lambda i, j: ((i // G) * nq + j, i % G))
    return q.reshape(B * T, n), True, B * G, nq, n_rep, spec


def _cmp_select_kernel(tq, n_rep, q_base, n_sel, tok_major, q_ref, kc_ref, vct_ref, ovt_ref, oc_ref, bias_ref):
    qi = pl.program_id(1)
    rows = n_rep * tq
    ncp = kc_ref.shape[0]
    nr = ovt_ref.shape[0]
    nbp = bias_ref.shape[0]
    q = _load_queries(q_ref, n_rep, tok_major) * HEAD_DIM ** -0.5
    s = _mm(kc_ref[...], q)
    t_row = q_base + qi * tq + lax.broadcasted_iota(jnp.int32, (1, rows), 1) % tq
    c_end = lax.broadcasted_iota(jnp.int32, (ncp, 1), 0) * CMP_STRIDE + (CMP_LEN - 1)
    s = jnp.where(c_end <= t_row, s, NEG_INF)
    m = jnp.max(s, axis=0, keepdims=True)
    e = jnp.exp(s - jnp.where(m == NEG_INF, 0.0, m))
    p = e / jnp.maximum(jnp.sum(e, axis=0, keepdims=True), 1e-30)
    _store_outputs(oc_ref, _mm(vct_ref[...], p), n_rep, tok_major)
    p_sum = p[:, 0:tq]
    for r in range(1, n_rep):
        p_sum = p_sum + p[:, r * tq:(r + 1) * tq]
    imp = sum(_mm(ovt_ref[...], t) for t in _split3(p_sum))
    cur = (q_base + qi * tq + lax.broadcasted_iota(jnp.int32, (1, tq), 1)) // SEL_BLOCK
    blk = lax.broadcasted_iota(jnp.int32, (nr, 1), 0)
    forced = jnp.where(blk == cur, 1.0, jnp.where(blk == cur - 1, 1.0, jnp.where(blk == 0, 1.0, 0.0)))
    score = jnp.where(blk <= cur, imp + FORCE_BONUS * forced, NEG_INF)
    rank = _rank_rows(score, blk, n_sel)
    bias = jnp.where(blk <= cur, jnp.where(rank < min(SEL_TOPN, n_sel), 0.0, NEG_BIG), NEG_BIG)
    if nbp > nr:
        bias = jnp.concatenate([bias, jnp.zeros((nbp - nr, tq), F32)], axis=0)
    bias_ref[...] = bias.astype(BF16)


def _cmp_select(q, kc, vct, ovt, tq, q_base, n_sel, nbp):
    qt, tok_major, BG, nq, n_rep, qspec = _query_layout(q, tq, NSA_KV_HEADS)
    ncp, dh = kc.shape[1:]
    return pl.pallas_call(
        functools.partial(_cmp_select_kernel, tq, n_rep, q_base, n_sel, tok_major), grid=(BG, nq),
        in_specs=[qspec, pl.BlockSpec((None, ncp, dh), lambda i, j: (i, 0, 0)),
                  pl.BlockSpec((None, dh, ncp), lambda i, j: (i, 0, 0)),
                  pl.BlockSpec(ovt.shape, lambda i, j: (0, 0))],
        out_specs=[qspec, pl.BlockSpec((None, None, nbp, tq), lambda i, j: (i, j, 0, 0))],
        out_shape=[jax.ShapeDtypeStruct(qt.shape, F32), jax.ShapeDtypeStruct((BG, nq, nbp, tq), BF16)],
        compiler_params=_params("parallel", "parallel"), name="nsa_cmp_select",
    )(qt, kc, vct, ovt)


def _overlap_matrix_t(nc, nsel, ncp, nr):
    i = jnp.arange(nc, dtype=jnp.int32)[None, :]
    j = jnp.arange(nsel, dtype=jnp.int32)[:, None]
    lo = jnp.maximum(i * CMP_STRIDE, j * SEL_BLOCK)
    hi = jnp.minimum(i * CMP_STRIDE + CMP_LEN, (j + 1) * SEL_BLOCK)
    ov = jnp.clip(hi - lo, 0).astype(F32) / CMP_LEN
    return jnp.pad(ov, ((0, nr - nsel), (0, ncp - nc)))


def _means_kernel(n_blk, k_ref, o_ref):
    o_ref[...] = jnp.zeros(o_ref.shape, F32)
    for n in range(n_blk):
        o_ref[n:n + 1, :] = jnp.sum(k_ref[n * MOBA_BLOCK:(n + 1) * MOBA_BLOCK, :], axis=0,
                                    keepdims=True) / MOBA_BLOCK


def _block_means(k, n_blk, nbp):
    BG, Lp, dh = k.shape
    return pl.pallas_call(
        functools.partial(_means_kernel, n_blk), grid=(BG,),
        in_specs=[pl.BlockSpec((None, Lp, dh), lambda i: (i, 0, 0))],
        out_specs=pl.BlockSpec((None, nbp, dh), lambda i: (i, 0, 0)),
        out_shape=jax.ShapeDtypeStruct((BG, nbp, dh), F32), compiler_params=_params("parallel"), name="moba_means",
    )(k)


def _bmm(a, b, exact=False):
    dims = (((2,), (1,)), ((0,), (0,)))
    if exact:
        return lax.dot_general(a.astype(F32), b.astype(F32), dims, precision=HIGHEST, preferred_element_type=F32)
    return lax.dot_general(a.astype(BF16), b.astype(BF16), dims, preferred_element_type=F32)


def _flash_kernel(mode, tq, n_rep, tk, q_base, k_base, n_blk, tok_major, *refs):
    if mode == "sel":
        q_ref, ke_ref, vt_ref, bias_ref, o_ref, lhs_scr, m_scr, acc_scr = refs
    elif mode == "win":
        q_ref, ke_ref, vt_ref, o_ref, lhs_scr, m_scr, acc_scr = refs
    else:
        q_ref, ke_ref, vt_ref, mean_ref, o_ref, lhs_scr, m_scr, acc_scr = refs
    qi = pl.program_id(1)
    rows = n_rep * tq
    q0 = q_base + qi * tq
    t_row = q0 + lax.broadcasted_iota(jnp.int32, (1, 1, rows), 2) % tq
    q = _load_queries(q_ref, n_rep, True)[None] if tok_major else q_ref[...]
    q = q * HEAD_DIM ** -0.5
    lhs_scr[:, 0:HEAD_DIM, :] = q.astype(BF16)
    if mode == "sel":
        lhs_scr[:, HEAD_DIM:, :] = jnp.concatenate([bias_ref[...]] * n_rep, axis=2)
    elif mode == "moba":
        nr = _round_up(n_blk, SUBLANES)
        nbp = mean_ref.shape[1]
        gate = _bmm(mean_ref[:, 0:nr, :], q, exact=True)
        blk = lax.broadcasted_iota(jnp.int32, (1, nr, 1), 1)
        cur = t_row // MOBA_BLOCK
        gate = jnp.where(blk < cur, gate, NEG_INF)
        rank = _rank_rows(gate, blk, n_blk)
        picked = jnp.where(blk < cur, jnp.where(rank < min(MOBA_TOPK, n_blk), 0.0, NEG_BIG), NEG_BIG)
        bias = jnp.where(blk == cur, 0.0, picked)
        if nbp > nr:
            bias = jnp.concatenate([bias, jnp.zeros((bias.shape[0], nbp - nr, rows), F32)], axis=1)
        lhs_scr[:, HEAD_DIM:, :] = bias.astype(BF16)
    m_scr[...] = jnp.full(m_scr.shape, NEG_BIG, F32)
    acc_scr[...] = jnp.zeros(acc_scr.shape, F32)

    def tile(ki, positional):
        k0 = pl.multiple_of(ki * tk, tk)
        s = _bmm(ke_ref[:, pl.ds(k0, tk), :], lhs_scr[...])
        if positional:
            kpos = k_base + k0 + lax.broadcasted_iota(jnp.int32, (1, tk, 1), 1)
            s = jnp.where(kpos <= t_row, s, NEG_BIG)
            if mode == "win":
                s = jnp.where(kpos > t_row - WINDOW, s, NEG_BIG)
        m_prev = m_scr[...]
        m_new = jnp.maximum(m_prev, jnp.max(s, axis=1, keepdims=True))
        p = jnp.exp(s - m_new)
        acc_scr[...] = jnp.exp(m_prev - m_new) * acc_scr[...] + _bmm(vt_ref[:, :, pl.ds(k0, tk)], p)
        m_scr[...] = m_new

    def body(positional):
        def f(ki, carry):
            tile(ki, positional)
            return carry
        return f

    last = (q0 + tq - 1 - k_base) // tk
    if mode == "win":
        lax.fori_loop(jnp.maximum(q0 - (WINDOW - 1) - k_base, 0) // tk, last + 1, body(True), 0)
    else:
        lax.fori_loop(0, last, body(False), 0)
        tile(last, True)
    acc = acc_scr[...]
    o = acc[:, 0:HEAD_DIM] / jnp.maximum(acc[:, HEAD_DIM:HEAD_DIM + 1], 1e-30)
    if tok_major:
        _store_outputs(o_ref, o[0], n_rep, True)
    else:
        o_ref[...] = o


def _flash(mode, q, ke, vt, extra, tq, q_base, k_base, n_blk, n_groups, tk=ATT_K_TILE, group_batch=1):
    gb = group_batch
    qt, tok_major, BG, nq, n_rep, qspec = _query_layout(q, tq, n_groups, gb)
    dh = HEAD_DIM
    rows = n_rep * tq
    Lp, kw = ke.shape[1:]
    assert Lp % tk == 0 and (q_base + nq * tq - 1 - k_base) // tk < Lp // tk
    assert tk % tq == 0 and (q_base - k_base) % tq == 0 and BG % gb == 0
    ins = [qt, ke, vt]
    in_specs = [qspec, pl.BlockSpec((gb, Lp, kw), lambda i, j: (i, 0, 0)),
                pl.BlockSpec((gb, V_ROWS, Lp), lambda i, j: (i, 0, 0))]
    if mode == "sel":
        ins.append(extra)
        in_specs.append(pl.BlockSpec((gb, None, kw - dh, tq), lambda i, j: (i, j, 0, 0)))
    if mode == "moba":
        ins.append(extra)
        in_specs.append(pl.BlockSpec((gb, kw - dh, dh), lambda i, j: (i, 0, 0)))
    scratch = [pltpu.VMEM((gb, kw, rows), BF16), pltpu.VMEM((gb, 1, rows), F32),
               pltpu.VMEM((gb, V_ROWS, rows), F32)]
    return pl.pallas_call(
        functools.partial(_flash_kernel, mode, tq, n_rep, tk, q_base, k_base, n_blk, tok_major),
        grid=(BG // gb, nq), in_specs=in_specs, out_specs=qspec, out_shape=jax.ShapeDtypeStruct(qt.shape, F32),
        scratch_shapes=scratch, compiler_params=_params("parallel", "parallel"), name="flash_" + mode,
    )(*ins)


def _group_queries_t(q, n_groups, tq):
    B, T, n = q.shape
    R = n // HEAD_DIM // n_groups
    q = q.reshape(B, T // tq, tq, n_groups, R, HEAD_DIM).transpose(0, 3, 1, 5, 4, 2)
    return q.reshape(B * n_groups, T // tq, HEAD_DIM, R * tq)


def _ungroup_t(o, B, n_groups, tq):
    BG, nq, dh, rows = o.shape
    R = rows // tq
    o = o.reshape(B, n_groups, nq, dh, R, tq).transpose(0, 2, 5, 1, 4, 3)
    return o.reshape(B, nq * tq, n_groups * R * dh)


def _head_major(x, Lp):
    B, L, G, dh = x.shape
    x = jnp.pad(x.transpose(0, 2, 1, 3), ((0, 0), (0, 0), (0, Lp - L), (0, 0)))
    return x.reshape(B * G, Lp, dh)


def _key_operand(x, Lp, block=None, nbp=0, first_pos=0):
    k = _head_major(x, Lp).astype(BF16)
    if block is None:
        return k
    onehot = ((first_pos + jnp.arange(Lp))[:, None] // block == jnp.arange(nbp)[None, :]).astype(BF16)
    return jnp.concatenate([k, jnp.broadcast_to(onehot[None], (k.shape[0], Lp, nbp))], axis=-1)


def _value_operand(x, Lp):
    B, L, G, dh = x.shape
    v = jnp.pad(x.transpose(0, 2, 3, 1), ((0, 0), (0, 0), (0, 0), (0, Lp - L))).reshape(B * G, dh, Lp)
    return jnp.concatenate([v, jnp.ones((B * G, V_ROWS - dh, Lp), F32)], axis=1).astype(BF16)


def _bias_rows(n):
    return _round_up(HEAD_DIM + n, LANES) - HEAD_DIM


def _sel_blocks(n_pos):
    n_sel = -(-n_pos // SEL_BLOCK)
    return n_sel, _bias_rows(n_sel)


def _nsa_branches(q, n_q, cmp_x, cmp_specs, n_seq, n_rows, ke_sel, vt_sel, win_kv, cw, tq, q_base, win_base, tk,
                  sel_batch):
    G = NSA_KV_HEADS
    w1, pe, w2 = cw
    n_cmp_rows = n_rows // CMP_STRIDE * CMP_STRIDE
    kc = _nsa_compress(cmp_x, cmp_specs[0], n_seq, n_cmp_rows, w1[0], pe[0], w2[0], False)
    vct = _nsa_compress(cmp_x, cmp_specs[1], n_seq, n_cmp_rows, w1[1], pe[1], w2[1], True)
    n_sel, nbp = _sel_blocks(max(n_rows, q_base + n_q))
    ovt = _overlap_matrix_t(n_rows // CMP_STRIDE - 1, n_sel, kc.shape[1], _round_up(n_sel, SUBLANES))
    oc, bias = _cmp_select(q, kc, vct, ovt, tq, q_base, n_sel, nbp)
    o_s = _flash("sel", q, ke_sel, vt_sel, bias, tq, q_base, 0, n_sel, G, tk, sel_batch)
    Lwp = _round_up(max(win_kv.shape[1], q_base + n_q - win_base), ATT_K_TILE)
    o_w = _flash("win", q, _key_operand(win_kv[:, :, 0], Lwp), _value_operand(win_kv[:, :, 1], Lwp), None, tq,
                 q_base, win_base, 0, G)
    return oc, o_s, o_w


def _nsa_prompt(q, kv6, cw, tq):
    B, T = q.shape[:2]
    width = NSA_KV_HEADS * HEAD_DIM
    assert T % CMP_STRIDE == 0
    cmp_specs = [pl.BlockSpec((T, width), lambda b, s=s: (b, s)) for s in (0, 1)]
    _, nbp = _sel_blocks(T)
    Lp = _round_up(T, ATT_K_TILE)
    return _nsa_branches(q, T, kv6.reshape(B * T, -1), cmp_specs, B, T, _key_operand(kv6[:, :, 2], Lp, SEL_BLOCK, nbp),
                         _value_operand(kv6[:, :, 3], Lp), kv6[:, :, 4:6], cw, tq, 0, 0, ATT_K_TILE, 1)


def _nsa_sample(q, kv6, cache, page_table, win_buf, cw, tq):
    Bs, Ts = q.shape[:2]
    page = cache.shape[-1]
    past_len = page_table.shape[1] * page
    step = PAGES_PER_STEP * page
    assert past_len % CMP_STRIDE == 0 and Ts < CMP_STRIDE and tq <= step
    _, nbp = _sel_blocks(past_len + tq)
    tail_ke = _key_operand(kv6[:, :, 2], step, SEL_BLOCK, nbp, past_len)
    tail_vt = _value_operand(kv6[:, :, 3], step)
    rows, ke, vt, _ = _gather_kv(cache, page_table, NSA_KV_HEADS, (0, 1), 2, 3, SEL_BLOCK, tail_ke, tail_vt, False)
    win = jnp.concatenate([win_buf, kv6[:, :, 4:6]], axis=1)
    qt = _group_queries_t(_pad_queries(q, tq), NSA_KV_HEADS, tq)
    cmp_specs = [pl.BlockSpec((None, None) + rows.shape[2:], lambda b, s=s: (s, b, 0, 0)) for s in (0, 1)]
    outs = _nsa_branches(qt, tq, rows, cmp_specs, Bs, past_len + Ts, ke, vt, win, cw, tq, past_len,
                         past_len - win_buf.shape[1], step, NSA_KV_HEADS)
    return [_ungroup_t(o, Bs, NSA_KV_HEADS, tq)[:, :Ts] for o in outs], win[:, Ts:]


def _moba_prompt(q, kv, tq):
    B, T, _ = q.shape
    n_blk = -(-T // MOBA_BLOCK)
    nbp = _bias_rows(n_blk)
    Lp = _round_up(T, ATT_K_TILE)
    means = _block_means(_head_major(kv[:, :, 0], Lp), n_blk, nbp)
    return _flash("moba", q, _key_operand(kv[:, :, 0], Lp, MOBA_BLOCK, nbp), _value_operand(kv[:, :, 1], Lp), means,
                  tq, 0, 0, n_blk, MOBA_KV_HEADS)


def _moba_sample(q, kv, cache, page_table, tq):
    Bs, Ts = q.shape[:2]
    page = cache.shape[-1]
    past_len = page_table.shape[1] * page
    step = PAGES_PER_STEP * page
    assert step % MOBA_BLOCK == 0 and tq <= MOBA_BLOCK
    n_blk = -(-(past_len + tq) // MOBA_BLOCK)
    nbp = _bias_rows(n_blk)
    tail_ke = _key_operand(kv[:, :, 0], step, MOBA_BLOCK, nbp, past_len)
    tail_vt = _value_operand(kv[:, :, 1], step)
    _, ke, vt, means = _gather_kv(cache, page_table, MOBA_KV_HEADS, (), 0, 1, MOBA_BLOCK, tail_ke, tail_vt, True)
    means = jnp.pad(means, ((0, 0), (0, max(nbp - means.shape[1], 0)), (0, 0)))[:, :nbp]
    qt = _group_queries_t(_pad_queries(q, tq), MOBA_KV_HEADS, tq)
    o = _flash("moba", qt, ke, vt, means, tq, past_len, 0, n_blk, MOBA_KV_HEADS, step, MOBA_KV_HEADS)
    return _ungroup_t(o, Bs, MOBA_KV_HEADS, tq)[:, :Ts]


def _gather_kv_kernel(n_groups, row_slots, k_slot, v_slot, block, with_means, pt_ref, *refs):
    del pt_ref
    G = n_groups
    dh = HEAD_DIM
    pps = PAGES_PER_STEP
    pages = refs[:pps]
    tail_ke_ref, tail_vt_ref = refs[pps:pps + 2]
    outs = list(refs[pps + 2:])
    rows_ref = outs.pop(0) if row_slots else None
    ke_ref, vt_ref = outs[0], outs[1]
    means_ref = outs[2] if with_means else None
    p = pl.program_id(1)
    last = pl.num_programs(1) - 1
    page = pages[0].shape[-1]
    step, kw = ke_ref.shape[1:]

    @pl.when(p < last)
    def _():
        lane = lax.broadcasted_iota(jnp.int32, (step, kw), 1)
        row = p * step + lax.broadcasted_iota(jnp.int32, (step, kw), 0)
        onehot = jnp.where(lane - dh == row // block, 1.0, 0.0).astype(BF16)
        for g in range(G):
            ke_ref[g] = onehot
        vt_ref[:, dh:, :] = jnp.ones((G, V_ROWS - dh, step), BF16)
        ppb = block // page
        tot = None
        for j, pg in enumerate(pages):
            r0 = j * page
            for si, s in enumerate(row_slots):
                rows_ref[si, r0:r0 + page, :] = pg[s].reshape(G * dh, page).T
            k = pg[k_slot].reshape(G * dh, page).T
            for g in range(G):
                ke_ref[g, r0:r0 + page, 0:dh] = k[:, g * dh:(g + 1) * dh].astype(BF16)
                vt_ref[g, 0:dh, r0:r0 + page] = pg[v_slot, g].astype(BF16)
            if with_means:
                ksum = jnp.sum(k, axis=0, keepdims=True)
                tot = ksum if j % ppb == 0 else tot + ksum
                if j % ppb == ppb - 1:
                    n = j // ppb
                    for g in range(G):
                        means_ref[g, n:n + 1, :] = tot[:, g * dh:(g + 1) * dh] / block

    @pl.when(p == last)
    def _():
        ke_ref[...] = tail_ke_ref[...]
        vt_ref[...] = tail_vt_ref[...]
        if with_means:
            means_ref[...] = jnp.zeros(means_ref.shape, F32)


def _gather_kv(cache, page_table, n_groups, row_slots, k_slot, v_slot, block, tail_ke, tail_vt, with_means):
    _, n_slots, G, dh, page = cache.shape
    Bs, n_pages = page_table.shape
    assert G == n_groups and dh == HEAD_DIM
    pps = PAGES_PER_STEP
    step = pps * page
    assert n_pages % pps == 0
    if with_means:
        assert step % block == 0 and block % page == 0
    n_steps = n_pages // pps
    kw = tail_ke.shape[-1]
    Lp = (n_steps + 1) * step
    bps = step // block

    def page_map(j):
        return lambda b, p, pt: (pt[b, jnp.minimum(p, n_steps - 1) * pps + j], 0, 0, 0, 0)

    in_specs = [pl.BlockSpec((None, n_slots, G, dh, page), page_map(j)) for j in range(pps)]
    in_specs += [pl.BlockSpec((None, G, step, kw), lambda b, p, pt: (b, 0, 0, 0)),
                 pl.BlockSpec((None, G, V_ROWS, step), lambda b, p, pt: (b, 0, 0, 0))]
    out_shape, out_specs = [], []
    if row_slots:
        n_rs = len(row_slots)
        out_shape.append(jax.ShapeDtypeStruct((n_rs, Bs, n_steps * step, G * dh), F32))
        out_specs.append(pl.BlockSpec((n_rs, None, step, G * dh),
                                      lambda b, p, pt: (0, b, jnp.minimum(p, n_steps - 1), 0)))
    out_shape += [jax.ShapeDtypeStruct((Bs, G, Lp, kw), BF16), jax.ShapeDtypeStruct((Bs, G, V_ROWS, Lp), BF16)]
    out_specs += [pl.BlockSpec((None, G, step, kw), lambda b, p, pt: (b, 0, p, 0)),
                  pl.BlockSpec((None, G, V_ROWS, step), lambda b, p, pt: (b, 0, 0, p))]
    if with_means:
        out_shape.append(jax.ShapeDtypeStruct((Bs, G, n_steps + 1, bps, dh), F32))
        out_specs.append(pl.BlockSpec((None, G, None, bps, dh), lambda b, p, pt: (b, 0, p, 0, 0)))
    grid_spec = pltpu.PrefetchScalarGridSpec(num_scalar_prefetch=1, grid=(Bs, n_steps + 1), in_specs=in_specs,
                                             out_specs=out_specs)
    outs = list(pl.pallas_call(
        functools.partial(_gather_kv_kernel, G, tuple(row_slots), k_slot, v_slot, block, with_means),
        grid_spec=grid_spec, out_shape=out_shape, compiler_params=_params("parallel", "arbitrary"),
        name="gather_kv",
    )(page_table, *([cache] * pps), tail_ke.reshape(Bs, G, step, kw), tail_vt.reshape(Bs, G, V_ROWS, step)))
    rows = outs.pop(0) if row_slots else None
    ke = outs[0].reshape(Bs * G, Lp, kw)
    vt = outs[1].reshape(Bs * G, V_ROWS, Lp)
    means = outs[2].reshape(Bs * G, (n_steps + 1) * bps, dh) if with_means else None
    return rows, ke, vt, means


def _pad_queries(x, tq):
    return jnp.pad(x, ((0, 0), (0, tq - x.shape[1]), (0, 0)))


def kernel(x_prompt, x_sample, cache_nsa_kv, cache_moba_kv, state_win_kv, state_wkv, state_shift, page_table, norm_mix, norm_ffn, norm_final, even_w_in, even_w_out, rwkv_mu, rwkv_w0, rwkv_w_up, rwkv_a0, rwkv_a_up, rwkv_g_up, rwkv_k_k, rwkv_k_a, rwkv_r_k, rwkv_ln_g, rwkv_ln_b, nsa_gate_b, nsa_cmp_w1, nsa_cmp_pe, nsa_cmp_w2, odd_w_in, odd_w_out, ffn_w_gate, ffn_w_up, ffn_w_down):
    B, T, D = x_prompt.shape
    Bs, Ts, _ = x_sample.shape
    depth = norm_mix.shape[0]
    page = cache_nsa_kv.shape[2]
    past_len = page_table.shape[1] * page
    rwkv_dim = rwkv_w0.shape[1]
    rwkv_cols = rwkv_mu.shape[1]
    nsa_heads = nsa_gate_b.shape[1] // 3
    nsa_dim = nsa_heads * HEAD_DIM
    nsa_kv_cols = 6 * NSA_KV_HEADS * HEAD_DIM
    moba_kv_cols = 2 * MOBA_KV_HEADS * HEAD_DIM
    moba_dim = odd_w_in.shape[2] - moba_kv_cols
    tq_p = min(ATT_Q_TILE, T)
    tq_s = SAMPLE_Q_PAD

    cos_p, sin_p = _rope_tables(jnp.arange(T, dtype=jnp.int32))
    pos_s = past_len + jnp.arange(Ts, dtype=jnp.int32)
    cos_s, sin_s = _rope_tables(jnp.tile(pos_s, Bs))

    even_spec = ((rwkv_cols, (), False), (nsa_dim, tuple(range(nsa_dim // LANES)), False),
                 (nsa_kv_cols, tuple(range(0, nsa_kv_cols // LANES, 2)), False), (LANES, (), True))
    k_chunks = MOBA_KV_HEADS * HEAD_DIM // LANES
    odd_spec = ((moba_dim, tuple(range(moba_dim // LANES)), False), (moba_kv_cols, tuple(range(k_chunks)), False))

    hp = x_prompt.reshape(B * T, D)
    hs = x_sample.reshape(Bs * Ts, D)
    nsa_p, nsa_s, moba_p, moba_s = [], [], [], []
    win_p, win_s, wkv_p, wkv_s, sh_p, sh_s = [], [], [], [], [], []
    for layer in range(depth):
        i = layer // 2
        if layer % 2 == 0:
            w_in = even_w_in[i].astype(BF16)
            o = rwkv_cols
            n_gate = 3 * nsa_heads
            weights = [w_in[:, :o], w_in[:, o:o + nsa_dim], w_in[:, o + nsa_dim:o + nsa_dim + nsa_kv_cols],
                       jnp.pad(w_in[:, o + nsa_dim + nsa_kv_cols:], ((0, 0), (0, LANES - n_gate)))]
            gate_b = jnp.pad(nsa_gate_b[i], (0, LANES - n_gate)).reshape(1, LANES)
            rp = (rwkv_mu[i], rwkv_w0[i], rwkv_w_up[i], rwkv_a0[i], rwkv_a_up[i], rwkv_g_up[i],
                  rwkv_k_k[i], rwkv_k_a[i], rwkv_r_k[i], rwkv_ln_g[i], rwkv_ln_b[i])
            cw = (nsa_cmp_w1[i], nsa_cmp_pe[i], nsa_cmp_w2[i])

            rw, q, kv, gates = _project(hp, norm_mix[layer], cos_p, sin_p, weights, [gate_b], even_spec, "even_proj")
            rw3 = rw.reshape(B, T, rwkv_cols)
            kv6 = kv.reshape(B, T, 6, NSA_KV_HEADS, HEAD_DIM)
            y, g, wkv_new = _rwkv_mix(rw3, jnp.zeros((B, rwkv_cols), F32),
                                      jnp.zeros((B, rwkv_dim // HEAD_DIM, HEAD_DIM, HEAD_DIM), F32), rp, SCAN_CHUNK)
            o_c, o_s, o_w = _nsa_prompt(q.reshape(B, T, nsa_dim), kv6, cw, tq_p)
            hp = _even_out(hp, y, g, o_c.reshape(B * T, nsa_dim), o_s.reshape(B * T, nsa_dim),
                           o_w.reshape(B * T, nsa_dim), gates, even_w_out[i])
            nsa_p.append(kv6[:, :, :4])
            win_p.append(kv6[:, T - min(WINDOW, T):, 4:6])
            wkv_p.append(wkv_new)
            sh_p.append(rw3[:, -1])

            rw, q, kv, gates = _project(hs, norm_mix[layer], cos_s, sin_s, weights, [gate_b], even_spec, "even_proj_s")
            rw3 = rw.reshape(Bs, Ts, rwkv_cols)
            kv6 = kv.reshape(Bs, Ts, 6, NSA_KV_HEADS, HEAD_DIM)
            y, g, wkv_new = _rwkv_mix(rw3, state_shift[i], state_wkv[i], rp, SUBLANES)
            outs, win_new = _nsa_sample(q.reshape(Bs, Ts, nsa_dim), kv6,
                                        cache_nsa_kv[i].transpose(0, 2, 3, 4, 1), page_table,
                                        state_win_kv[i], cw, tq_s)
            o_c, o_s, o_w = [t.reshape(Bs * Ts, nsa_dim) for t in outs]
            hs = _even_out(hs, y, g, o_c, o_s, o_w, gates, even_w_out[i])
            nsa_s.append(kv6[:, :, :4])
            win_s.append(win_new)
            wkv_s.append(wkv_new)
            sh_s.append(rw3[:, -1])
        else:
            w_in = odd_w_in[i].astype(BF16)
            weights = [w_in[:, :moba_dim], w_in[:, moba_dim:]]
            q, kv = _project(hp, norm_mix[layer], cos_p, sin_p, weights, [], odd_spec, "odd_proj")
            kv2 = kv.reshape(B, T, 2, MOBA_KV_HEADS, HEAD_DIM)
            a = _moba_prompt(q.reshape(B, T, moba_dim), kv2, tq_p)
            hp = _odd_out(hp, a.reshape(B * T, moba_dim), odd_w_out[i])
            moba_p.append(kv2)

            q, kv = _project(hs, norm_mix[layer], cos_s, sin_s, weights, [], odd_spec, "odd_proj_s")
            kv2 = kv.reshape(Bs, Ts, 2, MOBA_KV_HEADS, HEAD_DIM)
            a = _moba_sample(q.reshape(Bs, Ts, moba_dim), kv2, cache_moba_kv[i].transpose(0, 2, 3, 4, 1),
                             page_table, tq_s)
            hs = _odd_out(hs, a.reshape(Bs * Ts, moba_dim), odd_w_out[i])
            moba_s.append(kv2)
        g_final = norm_final if layer == depth - 1 else None
        hp = _ffn(hp, norm_ffn[layer], ffn_w_gate[layer], ffn_w_up[layer], ffn_w_down[layer], g_final)
        hs = _ffn(hs, norm_ffn[layer], ffn_w_gate[layer], ffn_w_up[layer], ffn_w_down[layer], g_final)
    return (hp.reshape(B, T, D), hs.reshape(Bs, Ts, D), jnp.stack(nsa_p), jnp.stack(nsa_s), jnp.stack(moba_p),
            jnp.stack(moba_s), jnp.stack(win_p), jnp.stack(win_s), jnp.stack(wkv_p), jnp.stack(wkv_s),
            jnp.stack(sh_p), jnp.stack(sh_s))
```
